```python
import math
import jax, jax.numpy as jnp
from jax import lax
import numpy as np

D_MODEL = 4096
BATCH = 4
SEQ = 2048
DEPTH = 1
DEC_BATCH = 128
DEC_SEQ = 4
PAST_LEN = 16384
PAGE_SIZE = 128

D_MIX = 2 * D_MODEL
D_M = D_MIX // 2
D_S = D_MIX - D_M
M_HEADS = 8
M_HEAD_IN = D_M // M_HEADS
M_V_DIM = D_M // M_HEADS
M_QK_DIM = M_V_DIM // 2
S_HEAD_DIM = 64
S_HEADS = D_S // S_HEAD_DIM
S_STATE = 128
S_GROUPS = 8
S_HPG = S_HEADS // S_GROUPS
CONV_DIM = D_S + 2 * S_GROUPS * S_STATE
CONV_K = 4
CHUNK = 64
PLE_DIM = 256
GATE_CAP = 15.0
EPS = 1e-6
IN_SIZES = (D_M, D_M, D_M, D_M, M_HEADS, M_HEADS, D_S, CONV_DIM, S_HEADS)
IN_COLS = 4 * D_M + 2 * M_HEADS + D_S + CONV_DIM + S_HEADS

kernel_name = "hybrid_mlstm_ssd_decode_step"


def _rmsnorm(x, g):
    xf = x.astype(jnp.float32)
    y = xf * lax.rsqrt(jnp.mean(xf * xf, axis=-1, keepdims=True) + EPS)
    return (y * g.astype(jnp.float32)).astype(x.dtype)


def _softcap(x):
    return GATE_CAP * jnp.tanh(x / GATE_CAP)


def _causal_conv(u, buf, w, b):
    L = u.shape[1]
    full = jnp.concatenate([buf.astype(u.dtype), u], axis=1)
    y = b
    for j in range(CONV_K):
        y = y + full[:, j:j + L] * w[j]
    return y, full[:, full.shape[1] - (CONV_K - 1):]


def _to_chunks(t, nc, c):
    return jnp.moveaxis(t.reshape(t.shape[0], nc, c, *t.shape[2:]), 1, 0)


def _mlstm(q, k, v, ig, lf, C0, n0, m0):
    bsz, L = q.shape[:2]
    c = math.gcd(L, CHUNK)
    nc = L // c
    causal = jnp.tril(jnp.ones((c, c), dtype=bool))

    def step(carry, inp):
        C, n, m = carry
        qc, kc, vc, igc, lfc = inp
        b = jnp.cumsum(lfc, axis=1)
        dmat = jnp.where(causal[None, :, :, None],
                         b[:, :, None, :] - b[:, None, :, :] + igc[:, None, :, :], -jnp.inf)
        inter = b + m[:, None, :]
        m_t = jnp.maximum(inter, jnp.max(dmat, axis=2))
        w = jnp.exp(dmat - m_t[:, :, None, :])
        a_inter = jnp.exp(inter - m_t)
        s = jnp.einsum('bthd,bshd->btsh', qc, kc) * w
        num = jnp.einsum('btsh,bshv->bthv', s, vc) + a_inter[..., None] * jnp.einsum('bthd,bhdv->bthv', qc, C)
        den = jnp.sum(s, axis=2) + a_inter * jnp.einsum('bthd,bhd->bth', qc, n)
        h = num / jnp.maximum(jnp.abs(den), jnp.exp(-m_t))[..., None]
        b_end = b[:, -1]
        lw = b_end[:, None, :] - b + igc
        m_new = jnp.maximum(b_end + m, jnp.max(lw, axis=1))
        wk = jnp.exp(lw - m_new[:, None, :])
        decay = jnp.exp(b_end + m - m_new)
        C_new = decay[..., None, None] * C + jnp.einsum('bsh,bshd,bshv->bhdv', wk, kc, vc)
        n_new = decay[..., None] * n + jnp.einsum('bsh,bshd->bhd', wk, kc)
        return (C_new, n_new, m_new), h

    f32 = jnp.float32
    xs = tuple(_to_chunks(t.astype(f32), nc, c) for t in (q, k, v, ig, lf))
    (C, n, m), h = lax.scan(step, (C0.astype(f32), n0.astype(f32), m0.astype(f32)), xs)
    return jnp.moveaxis(h, 0, 1).reshape(bsz, L, M_HEADS, M_V_DIM), C, n, m


def _ssd(x, dt, la, bm, cm, h0):
    bsz, L = x.shape[:2]
    c = math.gcd(L, CHUNK)
    nc = L // c
    causal = jnp.tril(jnp.ones((c, c), dtype=bool))

    def step(h, inp):
        xc, dtc, lac, bc, cc = inp
        a = jnp.cumsum(lac, axis=1)
        decay = jnp.exp(jnp.where(causal[None, :, :, None, None], a[:, :, None] - a[:, None], -jnp.inf))
        xdt = xc * dtc[..., None]
        cb = jnp.einsum('btgn,bsgn->btsg', cc, bc)
        y = (jnp.einsum('btsg,btsgr,bsgrp->btgrp', cb, decay, xdt)
             + jnp.exp(a)[..., None] * jnp.einsum('btgn,bgrpn->btgrp', cc, h))
        a_end = a[:, -1]
        h_new = (jnp.exp(a_end)[..., None, None] * h
                 + jnp.einsum('bsgr,bsgrp,bsgn->bgrpn', jnp.exp(a_end[:, None] - a), xdt, bc))
        return h_new, y

    f32 = jnp.float32
    xs = tuple(_to_chunks(t.astype(f32), nc, c) for t in (x, dt, la, bm, cm))
    h, y = lax.scan(step, h0.astype(f32), xs)
    return jnp.moveaxis(y, 0, 1).reshape(bsz, L, S_GROUPS, S_HPG, S_HEAD_DIM), h


def _layer(x, p, m_conv0, m_C0, m_n0, m_m0, s_conv0, s_h0, w):
    (norm_in, w_in, b_ig, b_fg, m_conv_w, m_conv_b, w_q, w_k, m_norm,
     s_conv_w, s_conv_b, dt_bias, a_log, d_skip, s_norm, w_out, ple_proj, ple_gate, ple_norm) = w
    bsz, L, _ = x.shape
    f32 = jnp.float32
    xn = _rmsnorm(x, norm_in)
    proj = xn @ w_in
    u_m, v_m, o_m, z_m, ig, fg, z_s, xbc, dt = jnp.split(proj, np.cumsum(IN_SIZES)[:-1].tolist(), axis=-1)

    uc, m_conv1 = _causal_conv(u_m, m_conv0, m_conv_w, m_conv_b)
    uc = jax.nn.silu(uc).reshape(bsz, L, M_HEADS, M_HEAD_IN)
    q = jnp.einsum('blhe,hed->blhd', uc, w_q)
    k = jnp.einsum('blhe,hed->blhd', uc, w_k) * (M_QK_DIM ** -0.5)
    v = v_m.reshape(bsz, L, M_HEADS, M_V_DIM)
    ig = _softcap(ig.astype(f32) + b_ig)
    lf = jax.nn.log_sigmoid(_softcap(fg.astype(f32) + b_fg))
    hm, m_C1, m_n1, m_m1 = _mlstm(q, k, v, ig, lf, m_C0, m_n0, m_m0)
    hm = jax.nn.sigmoid(o_m.astype(f32)).reshape(bsz, L, M_HEADS, M_V_DIM) * hm
    hm = _rmsnorm(hm, m_norm.reshape(M_HEADS, M_V_DIM)).reshape(bsz, L, D_M).astype(x.dtype)
    hm = hm * jax.nn.silu(z_m)

    xc, s_conv1 = _causal_conv(xbc, s_conv0, s_conv_w, s_conv_b)
    xc = jax.nn.silu(xc)
    xs, bm, cm = jnp.split(xc, [D_S, D_S + S_GROUPS * S_STATE], axis=-1)
    xs = xs.reshape(bsz, L, S_GROUPS, S_HPG, S_HEAD_DIM).astype(f32)
    bm = bm.reshape(bsz, L, S_GROUPS, S_STATE)
    cm = cm.reshape(bsz, L, S_GROUPS, S_STATE)
    dt = jax.nn.softplus(dt.astype(f32) + dt_bias).reshape(bsz, L, S_GROUPS, S_HPG)
    a = -jnp.exp(a_log.astype(f32)).reshape(S_GROUPS, S_HPG)
    ys, s_h1 = _ssd(xs, dt, dt * a, bm, cm, s_h0.reshape(bsz, S_GROUPS, S_HPG, S_HEAD_DIM, S_STATE))
    ys = ys + d_skip.astype(f32).reshape(S_GROUPS, S_HPG, 1) * xs
    ys = ys.reshape(bsz, L, D_S).astype(x.dtype) * jax.nn.silu(z_s)
    ys = _rmsnorm(ys.reshape(bsz, L, S_GROUPS, D_S // S_GROUPS),
                  s_norm.reshape(S_GROUPS, D_S // S_GROUPS)).reshape(bsz, L, D_S)

    x = x + jnp.concatenate([hm, ys], axis=-1) @ w_out
    gate = jax.nn.sigmoid(_rmsnorm(x, ple_norm) @ ple_gate)
    x = x + gate * (p.astype(x.dtype) @ ple_proj)
    return x, (m_C1, m_n1, m_m1, m_conv1, s_h1.reshape(bsz, S_HEADS, S_HEAD_DIM, S_STATE), s_conv1)


def setup_inputs(seed: int = 0) -> dict:
    key = jax.random.key(seed)
    ks = jax.random.split(key, 32)
    nrm = jax.random.normal
    f32 = jnp.float32
    dt0 = jnp.exp(jax.random.uniform(ks[20], (DEPTH, S_HEADS), f32, math.log(1e-3), math.log(1e-1)))
    return {
        "x_prompt": nrm(ks[0], (BATCH, SEQ, D_MODEL), f32),
        "x_sample": nrm(ks[1], (DEC_BATCH, DEC_SEQ, D_MODEL), f32),
        "p_prompt": nrm(ks[2], (DEPTH, BATCH, SEQ, PLE_DIM), f32),
        "p_sample": nrm(ks[3], (DEPTH, DEC_BATCH, DEC_SEQ, PLE_DIM), f32),
        "state_m_C": 0.05 * nrm(ks[4], (DEPTH, DEC_BATCH, M_HEADS, M_QK_DIM, M_V_DIM), f32),
        "state_m_n": 0.05 * nrm(ks[5], (DEPTH, DEC_BATCH, M_HEADS, M_QK_DIM), f32),
        "state_m_m": nrm(ks[6], (DEPTH, DEC_BATCH, M_HEADS), f32),
        "state_m_conv": nrm(ks[7], (DEPTH, DEC_BATCH, CONV_K - 1, D_M), f32),
        "state_s_ssm": 0.1 * nrm(ks[8], (DEPTH, DEC_BATCH, S_HEADS, S_HEAD_DIM, S_STATE), f32),
        "state_s_conv": nrm(ks[9], (DEPTH, DEC_BATCH, CONV_K - 1, CONV_DIM), f32),
        "norm_in": 1.0 + 0.02 * nrm(ks[10], (DEPTH, D_MODEL), f32),
        "w_in": nrm(ks[11], (DEPTH, D_MODEL, IN_COLS), f32) * D_MODEL ** -0.5,
        "b_ig": -1.0 + 0.5 * nrm(ks[12], (DEPTH, M_HEADS), f32),
        "b_fg": 3.0 + 0.5 * nrm(ks[13], (DEPTH, M_HEADS), f32),
        "m_conv_w": 0.5 * nrm(ks[14], (DEPTH, CONV_K, D_M), f32),
        "m_conv_b": 0.02 * nrm(ks[15], (DEPTH, D_M), f32),
        "w_q": nrm(ks[16], (DEPTH, M_HEADS, M_HEAD_IN, M_QK_DIM), f32) * M_HEAD_IN ** -0.5,
        "w_k": nrm(ks[17], (DEPTH, M_HEADS, M_HEAD_IN, M_QK_DIM), f32) * M_HEAD_IN ** -0.5,
        "m_norm": 1.0 + 0.02 * nrm(ks[18], (DEPTH, D_M), f32),
        "s_conv_w": 0.5 * nrm(ks[19], (DEPTH, CONV_K, CONV_DIM), f32),
        "s_conv_b": 0.02 * nrm(ks[21], (DEPTH, CONV_DIM), f32),
        "dt_bias": dt0 + jnp.log(-jnp.expm1(-dt0)),
        "a_log": jnp.log(jax.random.uniform(ks[22], (DEPTH, S_HEADS), f32, 1.0, 16.0)),
        "d_skip": 1.0 + 0.1 * nrm(ks[23], (DEPTH, S_HEADS), f32),
        "s_norm": 1.0 + 0.02 * nrm(ks[24], (DEPTH, D_S), f32),
        "w_out": nrm(ks[25], (DEPTH, D_MIX, D_MODEL), f32) * D_MIX ** -0.5,
        "ple_proj": nrm(ks[26], (DEPTH, PLE_DIM, D_MODEL), f32) * PLE_DIM ** -0.5,
        "ple_gate": nrm(ks[27], (DEPTH, D_MODEL, D_MODEL), f32) * D_MODEL ** -0.5,
        "ple_norm": 1.0 + 0.02 * nrm(ks[28], (DEPTH, D_MODEL), f32),
        "final_norm": 1.0 + 0.02 * nrm(ks[29], (D_MODEL,), f32),
    }


def reference(x_prompt, x_sample, p_prompt, p_sample, state_m_C, state_m_n, state_m_m, state_m_conv,
              state_s_ssm, state_s_conv, norm_in, w_in, b_ig, b_fg, m_conv_w, m_conv_b, w_q, w_k, m_norm,
              s_conv_w, s_conv_b, dt_bias, a_log, d_skip, s_norm, w_out, ple_proj, ple_gate, ple_norm,
              final_norm):
    f32 = jnp.float32
    bp = x_prompt.shape[0]
    yp, ysm = x_prompt, x_sample
    new_p = [[] for _ in range(6)]
    new_s = [[] for _ in range(6)]
    for i in range(DEPTH):
        w = (norm_in[i], w_in[i], b_ig[i], b_fg[i], m_conv_w[i], m_conv_b[i], w_q[i], w_k[i], m_norm[i],
             s_conv_w[i], s_conv_b[i], dt_bias[i], a_log[i], d_skip[i], s_norm[i], w_out[i],
             ple_proj[i], ple_gate[i], ple_norm[i])
        yp, sp = _layer(yp, p_prompt[i],
                        jnp.zeros((bp, CONV_K - 1, D_M), x_prompt.dtype),
                        jnp.zeros((bp, M_HEADS, M_QK_DIM, M_V_DIM), f32),
                        jnp.zeros((bp, M_HEADS, M_QK_DIM), f32),
                        jnp.full((bp, M_HEADS), -jnp.inf, f32),
                        jnp.zeros((bp, CONV_K - 1, CONV_DIM), x_prompt.dtype),
                        jnp.zeros((bp, S_HEADS, S_HEAD_DIM, S_STATE), f32), w)
        ysm, ss = _layer(ysm, p_sample[i], state_m_conv[i], state_m_C[i], state_m_n[i], state_m_m[i],
                         state_s_conv[i], state_s_ssm[i], w)
        for lst, st in zip(new_p, sp):
            lst.append(st)
        for lst, st in zip(new_s, ss):
            lst.append(st)
    y_prompt = _rmsnorm(yp, final_norm)
    y_sample = _rmsnorm(ysm, final_norm)
    p_C, p_n, p_m, p_mconv, p_ssm, p_sconv = (jnp.stack(l) for l in new_p)
    s_C, s_n, s_m, s_mconv, s_ssm, s_sconv = (jnp.stack(l) for l in new_s)
    return (y_prompt, y_sample, p_C, p_n, p_m, p_mconv, p_ssm, p_sconv, s_C, s_n, s_m, s_mconv, s_ssm, s_sconv)
```

```python
import functools

import jax
import jax.numpy as jnp
from jax import lax
from jax.experimental import pallas as pl
from jax.experimental.pallas import tpu as pltpu

F32 = jnp.float32
BF16 = jnp.bfloat16
HIGHEST = lax.Precision.HIGHEST

D_MODEL = 4096
D_M = 4096
D_S = 4096
M_HEADS = 8
M_V = 512
M_QK = 256
S_HEADS = 64
S_HEAD_DIM = 64
S_STATE = 128
S_GROUPS = 8
S_HPG = 8
S_GW = S_HPG * S_HEAD_DIM
CONV_DIM = D_S + 2 * S_GROUPS * S_STATE
CONV_K = 4
PLE_DIM = 256
GATE_CAP = 15.0
EPS = 1e-6

COL_U, COL_V, COL_O, COL_ZM, COL_ZS, COL_XS = 0, 4096, 8192, 12288, 16384, 20480
COL_BM = COL_XS + D_S
COL_CM = COL_BM + S_GROUPS * S_STATE
N_MAIN = COL_XS + CONV_DIM
LANES = 128
GL_IG, GL_FG, GL_DT = 0, M_HEADS, 2 * M_HEADS

PROMPT_CHUNK = 256
SAMPLE_BB = 8
VMEM_LIMIT = 56 * 1024 * 1024

NT_DIMS = (((1,), (1,)), ((), ()))


def _dot(a, b):
    return jnp.dot(a, b, preferred_element_type=F32)


def _dot_nt(a, b):
    return lax.dot_general(a, b, NT_DIMS, preferred_element_type=F32)


def _dot_exact(a, b):
    return jnp.dot(a, b, preferred_element_type=F32, precision=HIGHEST)


def _sigmoid(x):
    return jax.nn.sigmoid(x)


def _silu(x):
    return x * _sigmoid(x)


def _softcap(x):
    return GATE_CAP * jnp.tanh(x / GATE_CAP)


def _softplus(x):
    return jnp.maximum(x, 0.0) + jnp.log1p(jnp.exp(-jnp.abs(x)))


def _log_sigmoid(x):
    return -_softplus(-x)


def _idiv(x, d):
    return lax.shift_right_logical(x, d.bit_length() - 1)


def _imod(x, d):
    return lax.bitwise_and(x, d - 1)


def _lane_pick(x, lane_idx):
    lane = lax.broadcasted_iota(jnp.int32, (1, x.shape[1]), 1)
    return jnp.sum(jnp.where(lane == lane_idx, x, 0.0), axis=1, keepdims=True)


def _col_to_row(col, eye):
    return jnp.sum(jnp.where(eye, col, 0.0), axis=0, keepdims=True)


def _bf16_transpose(x_bf):
    n = x_bf.shape[1]
    r = lax.broadcasted_iota(jnp.int32, (n, n), 0)
    c = lax.broadcasted_iota(jnp.int32, (n, n), 1)
    ident = jnp.where(r == c, 1.0, 0.0).astype(BF16)
    return _dot_nt(ident, x_bf).astype(BF16)


NORM_ROWS = 32


def _rmsnorm_rows_to(x_ref, g, dst_ref):
    def body(r, carry):
        sl = pl.ds(pl.multiple_of(r * NORM_ROWS, NORM_ROWS), NORM_ROWS)
        x = x_ref[sl, :]
        ms = jnp.mean(x * x, axis=-1, keepdims=True)
        dst_ref[sl, :] = (x * lax.rsqrt(ms + EPS) * g).astype(BF16)
        return carry
    lax.fori_loop(0, x_ref.shape[0] // NORM_ROWS, body, 0)


def _in_proj_body(x_ref, g_ref, w_ref, wg_ref, out_ref, gates_ref, xn_ref):
    @pl.when(pl.program_id(1) == 0)
    def _():
        _rmsnorm_rows_to(x_ref, g_ref[...], xn_ref)
        gates_ref[...] = _dot(xn_ref[...], wg_ref[...])
    out_ref[...] = _dot(xn_ref[...], w_ref[...])


def _in_proj(x, norm_w, w_main, w_gate):
    m = x.shape[0]
    tm = min(512, m)
    tn = 1024
    grid = (m // tm, N_MAIN // tn)
    return pl.pallas_call(
        _in_proj_body,
        grid=grid,
        in_specs=[
            pl.BlockSpec((tm, D_MODEL), lambda i, j: (i, 0)),
            pl.BlockSpec((1, D_MODEL), lambda i, j: (0, 0)),
            pl.BlockSpec((D_MODEL, tn), lambda i, j: (0, j)),
            pl.BlockSpec((D_MODEL, LANES), lambda i, j: (0, 0)),
        ],
        out_specs=[
            pl.BlockSpec((tm, tn), lambda i, j: (i, j)),
            pl.BlockSpec((tm, LANES), lambda i, j: (i, 0)),
        ],
        out_shape=[jax.ShapeDtypeStruct((m, N_MAIN), F32), jax.ShapeDtypeStruct((m, LANES), F32)],
        scratch_shapes=[pltpu.VMEM((tm, D_MODEL), BF16)],
        compiler_params=pltpu.CompilerParams(
            dimension_semantics=("parallel", "arbitrary"), vmem_limit_bytes=VMEM_LIMIT),
        name="in_proj",
    )(x, norm_w, w_main, w_gate)


TAIL = 8


def _conv_carry(u, ext_ref, cw, cb):
    c = u.shape[0]
    ext_ref[TAIL:TAIL + c, :] = u
    acc = cb + u * cw[CONV_K - 1:CONV_K, :]
    for j in range(1, CONV_K):
        acc = acc + ext_ref[TAIL - j:TAIL - j + c, :] * cw[CONV_K - 1 - j:CONV_K - j, :]
    ext_ref[0:TAIL, :] = u[c - TAIL:c, :]
    return acc


def _conv_rows(u, buf, cw, cb, t):
    rows = u.shape[0]
    acc = cb + u * cw[CONV_K - 1:CONV_K, :]
    for j in range(1, CONV_K):
        prev = buf if j == CONV_K - 1 else pltpu.roll(buf, rows + j - (CONV_K - 1), 0)
        acc = acc + jnp.where(t >= j, pltpu.roll(u, j, 0), prev) * cw[CONV_K - 1 - j:CONV_K - j, :]
    return acc


def _head_out(hv, o, z, nw):
    og = _sigmoid(o) * hv
    ms = jnp.mean(og * og, axis=-1, keepdims=True)
    return (og * lax.rsqrt(ms + EPS) * nw) * _silu(z)


def _group_out(y, xs, dsk, z, nw):
    gated = (y + dsk * xs) * _silu(z)
    ms = jnp.mean(gated * gated, axis=-1, keepdims=True)
    return gated * lax.rsqrt(ms + EPS) * nw


def _head_expand(g):
    k = lax.broadcasted_iota(jnp.int32, (LANES, S_GW), 0)
    n = lax.broadcasted_iota(jnp.int32, (LANES, S_GW), 1)
    return jnp.where(k == GL_DT + g * S_HPG + _idiv(n, S_HEAD_DIM), 1.0, 0.0)


def _mlstm_prompt_body(u_ref, v_ref, o_ref, z_ref, gt_ref, gb_ref, cw_ref, cb_ref, wq_ref, wkt_ref, nw_ref,
                       hm_ref, c_ref, n_ref, m_ref, ext_ref, mst_ref):
    h = pl.program_id(1)
    c = u_ref.shape[0]

    @pl.when(pl.program_id(2) == 0)
    def _():
        c_ref[...] = jnp.zeros_like(c_ref)
        n_ref[...] = jnp.zeros_like(n_ref)
        mst_ref[...] = jnp.full_like(mst_ref, -jnp.inf)
        ext_ref[0:TAIL, :] = jnp.zeros((TAIL, ext_ref.shape[1]), F32)

    ucb = _silu(_conv_carry(u_ref[...], ext_ref, cw_ref[...], cb_ref[...])).astype(BF16)
    q = _dot(ucb, wq_ref[0])
    kt = _dot_nt(wkt_ref[0], ucb) * (M_QK ** -0.5)
    qb = q.astype(BF16)
    vb = v_ref[...].astype(BF16)

    gates = gt_ref[...] + gb_ref[...]
    capped = _softcap(gates)
    ig_col = _lane_pick(capped, GL_IG + h)

    rowi = lax.broadcasted_iota(jnp.int32, (c, c), 0)
    coli = lax.broadcasted_iota(jnp.int32, (c, c), 1)
    tri = coli <= rowi
    eye = coli == rowi
    b_col = _lane_pick(_dot_exact(jnp.where(tri, 1.0, 0.0), _log_sigmoid(capped)), GL_FG + h)
    b_row = _col_to_row(b_col, eye)
    ig_row = _col_to_row(ig_col, eye)
    m_prev = mst_ref[0:1, :][:, 0:1]

    dmat = jnp.where(tri, b_col - b_row + ig_row, -jnp.inf)
    inter = b_col + m_prev
    m_t = jnp.maximum(inter, jnp.max(dmat, axis=1, keepdims=True))
    a_inter = jnp.exp(inter - m_t)
    s = _dot(qb, kt.astype(BF16)) * jnp.exp(dmat - m_t)
    c_old = c_ref[0, 0]
    n_old = n_ref[0, 0]
    num = _dot(s.astype(BF16), vb) + a_inter * _dot(qb, c_old.astype(BF16))
    den = jnp.sum(s, axis=1, keepdims=True) + a_inter * jnp.sum(q * n_old, axis=1, keepdims=True)
    hv = num / jnp.maximum(jnp.abs(den), jnp.exp(-m_t))
    hm_ref[...] = _head_out(hv, o_ref[...], z_ref[...], nw_ref[...]).astype(BF16)

    b_end = b_col[c - 1:c, :]
    lw = b_end - b_col + ig_col
    m_new = jnp.maximum(b_end + m_prev, jnp.max(lw, axis=0, keepdims=True))
    decay = jnp.exp(b_end + m_prev - m_new)
    kwt = kt * _col_to_row(jnp.exp(lw - m_new), eye)
    c_ref[0, 0] = decay * c_old + _dot(kwt.astype(BF16), vb)
    eye_k = (lax.broadcasted_iota(jnp.int32, (M_QK, M_QK), 0) == lax.broadcasted_iota(jnp.int32, (M_QK, M_QK), 1))
    n_ref[0, 0] = decay * n_old + _col_to_row(jnp.sum(kwt, axis=1, keepdims=True), eye_k)
    mst_ref[...] = jnp.broadcast_to(m_new, mst_ref.shape)
    m_ref[0, 0] = jnp.broadcast_to(m_new, (1, LANES))


def _mlstm_prompt(proj, gates, bsz, seqlen, gate_bias, conv_w, conv_b, wq, wkt, norm_w):
    c = min(PROMPT_CHUNK, seqlen)
    nch = seqlen // c
    def tok(col0):
        return pl.BlockSpec((c, M_V), lambda b, h, k: (b * nch + k, col0 // M_V + h))

    head_vec = lambda rows: pl.BlockSpec((rows, M_V), lambda b, h, k: (0, h))
    return pl.pallas_call(
        _mlstm_prompt_body,
        grid=(bsz, M_HEADS, nch),
        in_specs=[
            tok(COL_U), tok(COL_V), tok(COL_O), tok(COL_ZM),
            pl.BlockSpec((c, LANES), lambda b, h, k: (b * nch + k, 0)),
            pl.BlockSpec((1, LANES), lambda b, h, k: (0, 0)),
            head_vec(CONV_K), head_vec(1),
            pl.BlockSpec((1, M_V, M_QK), lambda b, h, k: (h, 0, 0)),
            pl.BlockSpec((1, M_QK, M_V), lambda b, h, k: (h, 0, 0)),
            head_vec(1),
        ],
        out_specs=[
            pl.BlockSpec((c, M_V), lambda b, h, k: (b * nch + k, h)),
            pl.BlockSpec((1, 1, M_QK, M_V), lambda b, h, k: (b, h, 0, 0)),
            pl.BlockSpec((1, 1, 1, M_QK), lambda b, h, k: (b, h, 0, 0)),
            pl.BlockSpec((1, 1, 1, LANES), lambda b, h, k: (b, h, 0, 0)),
        ],
        out_shape=[
            jax.ShapeDtypeStruct((bsz * seqlen, D_M), BF16),
            jax.ShapeDtypeStruct((bsz, M_HEADS, M_QK, M_V), F32),
            jax.ShapeDtypeStruct((bsz, M_HEADS, 1, M_QK), F32),
            jax.ShapeDtypeStruct((bsz, M_HEADS, 1, LANES), F32),
        ],
        scratch_shapes=[pltpu.VMEM((TAIL + c, M_V), F32), pltpu.VMEM((8, LANES), F32)],
        compiler_params=pltpu.CompilerParams(
            dimension_semantics=("parallel", "parallel", "arbitrary"), vmem_limit_bytes=VMEM_LIMIT),
        name="mlstm_prompt",
    )(proj, proj, proj, proj, gates, gate_bias, conv_w, conv_b, wq, wkt, norm_w)


def _ssd_prompt_body(xs_ref, bm_ref, cm_ref, zs_ref, gt_ref, gb_ref, av_ref, dv_ref,
                     cwx_ref, cbx_ref, cwb_ref, cbb_ref, cwc_ref, cbc_ref, nw_ref,
                     ys_ref, h_ref, extx_ref, extb_ref, extc_ref):
    g = pl.program_id(1)
    c = xs_ref.shape[0]

    @pl.when(pl.program_id(2) == 0)
    def _():
        h_ref[...] = jnp.zeros_like(h_ref)
        extx_ref[0:TAIL, :] = jnp.zeros((TAIL, extx_ref.shape[1]), F32)
        extb_ref[0:TAIL, :] = jnp.zeros((TAIL, extb_ref.shape[1]), F32)
        extc_ref[0:TAIL, :] = jnp.zeros((TAIL, extc_ref.shape[1]), F32)

    xs = _silu(_conv_carry(xs_ref[...], extx_ref, cwx_ref[...], cbx_ref[...]))
    bmb = _silu(_conv_carry(bm_ref[...], extb_ref, cwb_ref[...], cbb_ref[...])).astype(BF16)
    cmb = _silu(_conv_carry(cm_ref[...], extc_ref, cwc_ref[...], cbc_ref[...])).astype(BF16)

    lane = lax.broadcasted_iota(jnp.int32, (1, LANES), 1)
    lane0 = GL_DT + g * S_HPG
    in_group = (lane >= lane0) & (lane < lane0 + S_HPG)
    dt = jnp.where(in_group, _softplus(gt_ref[...] + gb_ref[...]), 0.0)
    la = dt * (-jnp.exp(av_ref[...]))

    rowi = lax.broadcasted_iota(jnp.int32, (c, c), 0)
    coli = lax.broadcasted_iota(jnp.int32, (c, c), 1)
    tri = coli <= rowi
    eye = coli == rowi
    cum = _dot_exact(jnp.where(tri, 1.0, 0.0), la)
    a_end = cum[c - 1:c, :]
    expand = _head_expand(g)
    xdt = xs * _dot_exact(dt, expand)
    ea_x = _dot_exact(jnp.exp(cum), expand)
    w_x = _dot_exact(jnp.exp(a_end - cum), expand)
    dsk = _dot_exact(jnp.broadcast_to(dv_ref[...], (8, LANES)), expand)[0:1, :]

    cb = _dot_nt(cmb, bmb)
    half = lax.broadcasted_iota(jnp.int32, (1, LANES), 1) < S_HEAD_DIM
    pairs = []
    for p in range(S_HPG // 2):
        xp = xdt[:, p * LANES:(p + 1) * LANES]
        acc = None
        for r in (2 * p, 2 * p + 1):
            a_col = _lane_pick(cum, lane0 + r)
            a_row = _col_to_row(a_col, eye)
            mr = (cb * jnp.exp(jnp.where(tri, a_col - a_row, -jnp.inf))).astype(BF16)
            xr = jnp.where(half if r % 2 == 0 else jnp.logical_not(half), xp, 0.0).astype(BF16)
            part = _dot(mr, xr)
            acc = part if acc is None else acc + part
        pairs.append(acc)
    h_old = h_ref[0, 0]
    y = jnp.concatenate(pairs, axis=1) + ea_x * _dot_nt(cmb, h_old.astype(BF16))
    ys_ref[...] = _group_out(y, xs, dsk, zs_ref[...], nw_ref[...]).astype(BF16)

    upd = _dot(_bf16_transpose((xdt * w_x).astype(BF16)), bmb)
    ea_end = jnp.exp(a_end)
    for r in range(S_HPG):
        rows = slice(r * S_HEAD_DIM, (r + 1) * S_HEAD_DIM)
        h_ref[0, 0, rows, :] = _lane_pick(ea_end, lane0 + r) * h_old[rows, :] + upd[rows, :]


def _ssd_prompt(proj, gates, bsz, seqlen, gate_bias, a_vec, d_vec, conv_w, conv_b, norm_w):
    c = min(PROMPT_CHUNK, seqlen)
    nch = seqlen // c
    tok = lambda width, col0: pl.BlockSpec((c, width), lambda b, g, k: (b * nch + k, col0 // width + g))
    vec = lambda rows, width, col0: pl.BlockSpec((rows, width), lambda b, g, k: (0, col0 // width + g))
    const = pl.BlockSpec((1, LANES), lambda b, g, k: (0, 0))
    return pl.pallas_call(
        _ssd_prompt_body,
        grid=(bsz, S_GROUPS, nch),
        in_specs=[
            tok(S_GW, COL_XS), tok(S_STATE, COL_BM), tok(S_STATE, COL_CM), tok(S_GW, COL_ZS),
            pl.BlockSpec((c, LANES), lambda b, g, k: (b * nch + k, 0)),
            const, const, const,
            vec(CONV_K, S_GW, 0), vec(1, S_GW, 0),
            vec(CONV_K, S_STATE, D_S), vec(1, S_STATE, D_S),
            vec(CONV_K, S_STATE, D_S + S_GROUPS * S_STATE), vec(1, S_STATE, D_S + S_GROUPS * S_STATE),
            vec(1, S_GW, 0),
        ],
        out_specs=[
            pl.BlockSpec((c, S_GW), lambda b, g, k: (b * nch + k, g)),
            pl.BlockSpec((1, 1, S_GW, S_STATE), lambda b, g, k: (b, g, 0, 0)),
        ],
        out_shape=[
            jax.ShapeDtypeStruct((bsz * seqlen, D_S), BF16),
            jax.ShapeDtypeStruct((bsz, S_GROUPS, S_GW, S_STATE), F32),
        ],
        scratch_shapes=[pltpu.VMEM((TAIL + c, S_GW), F32), pltpu.VMEM((TAIL + c, S_STATE), F32),
                        pltpu.VMEM((TAIL + c, S_STATE), F32)],
        compiler_params=pltpu.CompilerParams(
            dimension_semantics=("parallel", "parallel", "arbitrary"), vmem_limit_bytes=VMEM_LIMIT),
        name="ssd_prompt",
    )(proj, proj, proj, proj, gates, gate_bias, a_vec, d_vec,
      conv_w, conv_b, conv_w, conv_b, conv_w, conv_b, norm_w)


SEQ4 = 4


def _row_masks(rows):
    rowi = lax.broadcasted_iota(jnp.int32, (rows, rows), 0)
    coli = lax.broadcasted_iota(jnp.int32, (rows, rows), 1)
    same = _idiv(rowi, SEQ4) == _idiv(coli, SEQ4)
    return same, same & (coli <= rowi), coli == rowi, same & (_imod(coli, SEQ4) == SEQ4 - 1)


def _seg_cumsum(x, t):
    out = x
    for j in range(1, SEQ4):
        out = out + jnp.where(t >= j, pltpu.roll(x, j, 0), 0.0)
    return out


def _mlstm_sample_body(u_ref, v_ref, o_ref, z_ref, gt_ref, gb_ref, buf_ref, mrow_ref, c0_ref, n0_ref,
                       cw_ref, cb_ref, wq_ref, wkt_ref, nw_ref,
                       hm_ref, c_ref, n_ref, m_ref):
    h = pl.program_id(0)
    rows = u_ref.shape[0]
    nseq = rows // SEQ4
    rid = lax.broadcasted_iota(jnp.int32, (rows, 1), 0)
    t = _imod(rid, SEQ4)
    seq = _idiv(rid, SEQ4)
    seq_lane = _idiv(lax.broadcasted_iota(jnp.int32, (1, rows), 1), SEQ4)

    ucb = _silu(_conv_rows(u_ref[...], buf_ref[...], cw_ref[...], cb_ref[...], t)).astype(BF16)
    q = _dot(ucb, wq_ref[0])
    kt = _dot_nt(wkt_ref[0], ucb) * (M_QK ** -0.5)
    qb = q.astype(BF16)
    vb = v_ref[...].astype(BF16)

    capped = _softcap(gt_ref[...] + gb_ref[...])
    ig_col = _lane_pick(capped, GL_IG + h)
    b_col = _lane_pick(_seg_cumsum(_log_sigmoid(capped), t), GL_FG + h)
    m_prev = _lane_pick(mrow_ref[...], h)

    same, tri, eye, last = _row_masks(rows)
    b_row = _col_to_row(b_col, eye)
    ig_row = _col_to_row(ig_col, eye)
    dmat = jnp.where(tri, b_col - b_row + ig_row, -jnp.inf)
    inter = b_col + m_prev
    m_t = jnp.maximum(inter, jnp.max(dmat, axis=1, keepdims=True))
    a_inter = jnp.exp(inter - m_t)
    s = _dot(qb, kt.astype(BF16)) * jnp.exp(dmat - m_t)

    b_end = jnp.sum(jnp.where(last, b_row, 0.0), axis=1, keepdims=True)
    lw = b_end - b_col + ig_col
    m_new = jnp.maximum(b_end + m_prev, jnp.max(jnp.where(same, _col_to_row(lw, eye), -jnp.inf), axis=1, keepdims=True))
    decay = jnp.exp(b_end + m_prev - m_new)
    kwt = kt * _col_to_row(jnp.exp(lw - m_new), eye)
    eye_k = (lax.broadcasted_iota(jnp.int32, (M_QK, M_QK), 0) == lax.broadcasted_iota(jnp.int32, (M_QK, M_QK), 1))

    num_inter = jnp.zeros((rows, M_V), F32)
    qn = jnp.zeros((rows, 1), F32)
    for bi in range(nseq):
        own = seq == bi
        c_old = c0_ref[bi, 0]
        n_old = n0_ref[bi, 0]
        num_inter = jnp.where(own, _dot(qb, c_old.astype(BF16)), num_inter)
        qn = jnp.where(own, jnp.sum(q * n_old, axis=1, keepdims=True), qn)
        kw_b = jnp.where(seq_lane == bi, kwt, 0.0)
        d_b = jnp.sum(jnp.where(own & (t == SEQ4 - 1), decay, 0.0), axis=0, keepdims=True)
        c_ref[bi, 0] = d_b * c_old + _dot(kw_b.astype(BF16), vb)
        n_ref[bi, 0] = d_b * n_old + _col_to_row(jnp.sum(kw_b, axis=1, keepdims=True), eye_k)

    num = _dot(s.astype(BF16), vb) + a_inter * num_inter
    den = jnp.sum(s, axis=1, keepdims=True) + a_inter * qn
    hv = num / jnp.maximum(jnp.abs(den), jnp.exp(-m_t))
    hm_ref[...] = _head_out(hv, o_ref[...], z_ref[...], nw_ref[...]).astype(BF16)
    m_ref[0] = jnp.broadcast_to(m_new, (rows, LANES))


def _mlstm_sample(proj, gates, nseq_total, gate_bias, buf_rows, m_rows, c0, n0, conv_w, conv_b, wq, wkt, norm_w):
    bb = min(SAMPLE_BB, nseq_total)
    rows = bb * SEQ4
    nblk = nseq_total // bb
    tok = lambda col0: pl.BlockSpec((rows, M_V), lambda h, i: (i, col0 // M_V + h))
    head_vec = lambda r: pl.BlockSpec((r, M_V), lambda h, i: (0, h))
    c_spec = pl.BlockSpec((bb, 1, M_QK, M_V), lambda h, i: (i, h, 0, 0))
    n_spec = pl.BlockSpec((bb, 1, 1, M_QK), lambda h, i: (i, h, 0, 0))
    return pl.pallas_call(
        _mlstm_sample_body,
        grid=(M_HEADS, nblk),
        in_specs=[
            tok(COL_U), tok(COL_V), tok(COL_O), tok(COL_ZM),
            pl.BlockSpec((rows, LANES), lambda h, i: (i, 0)),
            pl.BlockSpec((1, LANES), lambda h, i: (0, 0)),
            pl.BlockSpec((rows, M_V), lambda h, i: (i, h)),
            pl.BlockSpec((rows, LANES), lambda h, i: (i, 0)),
            c_spec, n_spec,
            head_vec(CONV_K), head_vec(1),
            pl.BlockSpec((1, M_V, M_QK), lambda h, i: (h, 0, 0)),
            pl.BlockSpec((1, M_QK, M_V), lambda h, i: (h, 0, 0)),
            head_vec(1),
        ],
        out_specs=[
            pl.BlockSpec((rows, M_V), lambda h, i: (i, h)),
            c_spec, n_spec,
            pl.BlockSpec((1, rows, LANES), lambda h, i: (h, i, 0)),
        ],
        out_shape=[
            jax.ShapeDtypeStruct((nseq_total * SEQ4, D_M), BF16),
            jax.ShapeDtypeStruct((nseq_total, M_HEADS, M_QK, M_V), F32),
            jax.ShapeDtypeStruct((nseq_total, M_HEADS, 1, M_QK), F32),
            jax.ShapeDtypeStruct((M_HEADS, nseq_total * SEQ4, LANES), F32),
        ],
        compiler_params=pltpu.CompilerParams(
            dimension_semantics=("parallel", "parallel"), vmem_limit_bytes=VMEM_LIMIT),
        name="mlstm_sample",
    )(proj, proj, proj, proj, gates, gate_bias, buf_rows, m_rows, c0, n0, conv_w, conv_b, wq, wkt, norm_w)


def _ssd_sample_body(xs_ref, bm_ref, cm_ref, zs_ref, gt_ref, gb_ref, av_ref, dv_ref,
                     bufx_ref, bufb_ref, bufc_ref, h0_ref,
                     cwx_ref, cbx_ref, cwb_ref, cbb_ref, cwc_ref, cbc_ref, nw_ref,
                     ys_ref, h_ref):
    g = pl.program_id(0)
    rows = xs_ref.shape[0]
    nseq = rows // SEQ4
    rid = lax.broadcasted_iota(jnp.int32, (rows, 1), 0)
    t = _imod(rid, SEQ4)
    seq = _idiv(rid, SEQ4)

    xs = _silu(_conv_rows(xs_ref[...], bufx_ref[...], cwx_ref[...], cbx_ref[...], t))
    bmb = _silu(_conv_rows(bm_ref[...], bufb_ref[...], cwb_ref[...], cbb_ref[...], t)).astype(BF16)
    cmb = _silu(_conv_rows(cm_ref[...], bufc_ref[...], cwc_ref[...], cbc_ref[...], t)).astype(BF16)

    lane = lax.broadcasted_iota(jnp.int32, (1, LANES), 1)
    lane0 = GL_DT + g * S_HPG
    in_group = (lane >= lane0) & (lane < lane0 + S_HPG)
    dt = jnp.where(in_group, _softplus(gt_ref[...] + gb_ref[...]), 0.0)
    cum = _seg_cumsum(dt * (-jnp.exp(av_ref[...])), t)

    same, tri, eye, last = _row_masks(rows)
    a_end = _dot_exact(jnp.where(last, 1.0, 0.0), cum)
    expand = _head_expand(g)
    xdt = xs * _dot_exact(dt, expand)
    ea_x = _dot_exact(jnp.exp(cum), expand)
    w_x = _dot_exact(jnp.exp(a_end - cum), expand)
    dsk = _dot_exact(jnp.broadcast_to(dv_ref[...], (8, LANES)), expand)[0:1, :]
    ea_end = jnp.exp(a_end)

    cb = _dot_nt(cmb, bmb)
    half = lax.broadcasted_iota(jnp.int32, (1, LANES), 1) < S_HEAD_DIM
    pairs = []
    for p in range(S_HPG // 2):
        xp = xdt[:, p * LANES:(p + 1) * LANES]
        acc = None
        for r in (2 * p, 2 * p + 1):
            a_col = _lane_pick(cum, lane0 + r)
            a_row = _col_to_row(a_col, eye)
            mr = (cb * jnp.exp(jnp.where(tri, a_col - a_row, -jnp.inf))).astype(BF16)
            xr = jnp.where(half if r % 2 == 0 else jnp.logical_not(half), xp, 0.0).astype(BF16)
            part = _dot(mr, xr)
            acc = part if acc is None else acc + part
        pairs.append(acc)
    y_intra = jnp.concatenate(pairs, axis=1)

    xw = xdt * w_x
    y_inter = jnp.zeros((rows, S_GW), F32)
    for bi in range(nseq):
        own = seq == bi
        h_old = h0_ref[bi, 0]
        y_inter = jnp.where(own, _dot_nt(cmb, h_old.astype(BF16)), y_inter)
        upd = _dot(_bf16_transpose(jnp.where(own, xw, 0.0).astype(BF16)), bmb)
        d_row = jnp.sum(jnp.where(own & (t == SEQ4 - 1), ea_end, 0.0), axis=0, keepdims=True)
        for r in range(S_HPG):
            hr = slice(r * S_HEAD_DIM, (r + 1) * S_HEAD_DIM)
            h_ref[bi, 0, hr, :] = _lane_pick(d_row, lane0 + r) * h_old[hr, :] + upd[hr, :]

    y = y_intra + ea_x * y_inter
    ys_ref[...] = _group_out(y, xs, dsk, zs_ref[...], nw_ref[...]).astype(BF16)


def _ssd_sample(proj, gates, nseq_total, gate_bias, a_vec, d_vec, buf_rows, h0, conv_w, conv_b, norm_w):
    bb = min(SAMPLE_BB, nseq_total)
    rows = bb * SEQ4
    nblk = nseq_total // bb
    tok = lambda width, col0: pl.BlockSpec((rows, width), lambda g, i: (i, col0 // width + g))
    vec = lambda r, width, col0: pl.BlockSpec((r, width), lambda g, i: (0, col0 // width + g))
    const = pl.BlockSpec((1, LANES), lambda g, i: (0, 0))
    h_spec = pl.BlockSpec((bb, 1, S_GW, S_STATE), lambda g, i: (i, g, 0, 0))
    off_b = D_S
    off_c = D_S + S_GROUPS * S_STATE
    return pl.pallas_call(
        _ssd_sample_body,
        grid=(S_GROUPS, nblk),
        in_specs=[
            tok(S_GW, COL_XS), tok(S_STATE, COL_BM), tok(S_STATE, COL_CM), tok(S_GW, COL_ZS),
            pl.BlockSpec((rows, LANES), lambda g, i: (i, 0)),
            const, const, const,
            tok(S_GW, 0), tok(S_STATE, off_b), tok(S_STATE, off_c),
            h_spec,
            vec(CONV_K, S_GW, 0), vec(1, S_GW, 0),
            vec(CONV_K, S_STATE, off_b), vec(1, S_STATE, off_b),
            vec(CONV_K, S_STATE, off_c), vec(1, S_STATE, off_c),
            vec(1, S_GW, 0),
        ],
        out_specs=[pl.BlockSpec((rows, S_GW), lambda g, i: (i, g)), h_spec],
        out_shape=[
            jax.ShapeDtypeStruct((nseq_total * SEQ4, D_S), BF16),
            jax.ShapeDtypeStruct((nseq_total, S_GROUPS, S_GW, S_STATE), F32),
        ],
        compiler_params=pltpu.CompilerParams(
            dimension_semantics=("parallel", "parallel"), vmem_limit_bytes=VMEM_LIMIT),
        name="ssd_sample",
    )(proj, proj, proj, proj, gates, gate_bias, a_vec, d_vec, buf_rows, buf_rows, buf_rows, h0,
      conv_w, conv_b, conv_w, conv_b, conv_w, conv_b, norm_w)


def _out_proj_body(x_ref, hm_ref, ys_ref, w_ref, out_ref):
    k = pl.program_id(2)

    @pl.when(k == 0)
    def _():
        out_ref[...] = x_ref[...] + _dot(hm_ref[...], w_ref[...])

    @pl.when(k == 1)
    def _():
        out_ref[...] += _dot(ys_ref[...], w_ref[...])


def _out_proj(x, hm, ys, w_out):
    m = x.shape[0]
    tm = min(512, m)
    tn = 1024
    return pl.pallas_call(
        _out_proj_body,
        grid=(m // tm, D_MODEL // tn, 2),
        in_specs=[
            pl.BlockSpec((tm, tn), lambda i, j, k: (i, j)),
            pl.BlockSpec((tm, D_M), lambda i, j, k: (i, 0)),
            pl.BlockSpec((tm, D_S), lambda i, j, k: (i, 0)),
            pl.BlockSpec((D_M, tn), lambda i, j, k: (k, j)),
        ],
        out_specs=pl.BlockSpec((tm, tn), lambda i, j, k: (i, j)),
        out_shape=jax.ShapeDtypeStruct((m, D_MODEL), F32),
        compiler_params=pltpu.CompilerParams(
            dimension_semantics=("parallel", "parallel", "arbitrary"), vmem_limit_bytes=VMEM_LIMIT),
        name="out_proj",
    )(x, hm, ys, w_out)


def _ple_body(x1_ref, x1t_ref, p_ref, gn_ref, wg_ref, wp_ref, fn_ref, out_ref, xn_ref, ss_ref):
    j = pl.program_id(1)
    nj = pl.num_programs(1)
    tn = wg_ref.shape[1]

    @pl.when(j == 0)
    def _():
        _rmsnorm_rows_to(x1_ref, gn_ref[...], xn_ref)
        ss_ref[...] = jnp.zeros_like(ss_ref)

    gate = _sigmoid(_dot(xn_ref[...], wg_ref[...]))
    x2 = x1t_ref[...] + gate * _dot(p_ref[...].astype(BF16), wp_ref[...])
    ss_ref[...] += jnp.sum(x2 * x2, axis=1, keepdims=True)
    for jj in range(out_ref.shape[1] // tn):
        @pl.when(j == jj)
        def _(jj=jj):
            out_ref[:, jj * tn:(jj + 1) * tn] = x2

    @pl.when(j == nj - 1)
    def _():
        fn = fn_ref[...]

        def body(r, carry):
            sl = pl.ds(pl.multiple_of(r * NORM_ROWS, NORM_ROWS), NORM_ROWS)
            scale = lax.rsqrt(ss_ref[sl, :] * (1.0 / D_MODEL) + EPS)
            out_ref[sl, :] = out_ref[sl, :] * scale * fn
            return carry
        lax.fori_loop(0, out_ref.shape[0] // NORM_ROWS, body, 0)


def _ple(x1, p, ple_norm, w_gate, w_proj, final_norm):
    m = x1.shape[0]
    tm = min(512, m)
    tn = 512
    return pl.pallas_call(
        _ple_body,
        grid=(m // tm, D_MODEL // tn),
        in_specs=[
            pl.BlockSpec((tm, D_MODEL), lambda i, j: (i, 0)),
            pl.BlockSpec((tm, tn), lambda i, j: (i, j)),
            pl.BlockSpec((tm, PLE_DIM), lambda i, j: (i, 0)),
            pl.BlockSpec((1, D_MODEL), lambda i, j: (0, 0)),
            pl.BlockSpec((D_MODEL, tn), lambda i, j: (0, j)),
            pl.BlockSpec((PLE_DIM, tn), lambda i, j: (0, j)),
            pl.BlockSpec((1, D_MODEL), lambda i, j: (0, 0)),
        ],
        out_specs=pl.BlockSpec((tm, D_MODEL), lambda i, j: (i, 0)),
        out_shape=jax.ShapeDtypeStruct((m, D_MODEL), F32),
        scratch_shapes=[pltpu.VMEM((tm, D_MODEL), BF16), pltpu.VMEM((tm, 1), F32)],
        compiler_params=pltpu.CompilerParams(
            dimension_semantics=("parallel", "arbitrary"), vmem_limit_bytes=VMEM_LIMIT),
        name="ple",
    )(x1, x1, p, ple_norm, w_gate, w_proj, final_norm)


def _pad_lanes(v, offset):
    out = jnp.zeros((1, LANES), F32)
    return lax.dynamic_update_slice(out, v.reshape(1, -1).astype(F32), (0, offset))


def kernel(x_prompt, x_sample, p_prompt, p_sample, state_m_C, state_m_n, state_m_m, state_m_conv, state_s_ssm, state_s_conv, norm_in, w_in, b_ig, b_fg, m_conv_w, m_conv_b, w_q, w_k, m_norm, s_conv_w, s_conv_b, dt_bias, a_log, d_skip, s_norm, w_out, ple_proj, ple_gate, ple_norm, final_norm):
    assert w_in.shape[0] == 1, "single layer"
    bsz, seqlen, _ = x_prompt.shape
    nseq, dec_seq, _ = x_sample.shape
    assert dec_seq == SEQ4

    w = w_in[0]
    c_ig = 4 * D_M
    c_zs = c_ig + 2 * M_HEADS
    c_xbc = c_zs + D_S
    c_dt = c_xbc + CONV_DIM
    w_main = jnp.concatenate([w[:, :c_ig], w[:, c_zs:c_dt]], axis=1).astype(BF16)
    w_gate = jnp.concatenate(
        [w[:, c_ig:c_zs], w[:, c_dt:], jnp.zeros((D_MODEL, LANES - 2 * M_HEADS - S_HEADS), F32)], axis=1).astype(BF16)
    gate_bias = jnp.concatenate(
        [b_ig[0], b_fg[0], dt_bias[0], jnp.zeros((LANES - 2 * M_HEADS - S_HEADS,), F32)]).reshape(1, LANES)
    a_vec = _pad_lanes(a_log[0], GL_DT)
    d_vec = _pad_lanes(d_skip[0], GL_DT)
    wq = w_q[0].astype(BF16)
    wkt = jnp.swapaxes(w_k[0], 1, 2).astype(BF16)
    w_out_b = w_out[0].astype(BF16)
    ple_gate_b = ple_gate[0].astype(BF16)
    ple_proj_b = ple_proj[0].astype(BF16)
    norm_in_r = norm_in[0].reshape(1, D_MODEL)
    ple_norm_r = ple_norm[0].reshape(1, D_MODEL)
    final_norm_r = final_norm.reshape(1, D_MODEL)
    m_conv_b_r = m_conv_b[0].reshape(1, D_M)
    s_conv_b_r = s_conv_b[0].reshape(1, CONV_DIM)
    m_norm_r = m_norm[0].reshape(1, D_M)
    s_norm_r = s_norm[0].reshape(1, D_S)

    xp = x_prompt.reshape(bsz * seqlen, D_MODEL)
    proj_p, gates_p = _in_proj(xp, norm_in_r, w_main, w_gate)
    hm_p, pc, pn, pm = _mlstm_prompt(proj_p, gates_p, bsz, seqlen, gate_bias, m_conv_w[0], m_conv_b_r, wq, wkt, m_norm_r)
    ys_p, ph = _ssd_prompt(proj_p, gates_p, bsz, seqlen, gate_bias, a_vec, d_vec, s_conv_w[0], s_conv_b_r, s_norm_r)
    x1_p = _out_proj(xp, hm_p, ys_p, w_out_b)
    y_p = _ple(x1_p, p_prompt[0].reshape(bsz * seqlen, PLE_DIM), ple_norm_r, ple_gate_b, ple_proj_b, final_norm_r)

    proj_p3 = proj_p.reshape(bsz, seqlen, N_MAIN)
    tail = seqlen - (CONV_K - 1)
    p_mconv = proj_p3[:, tail:, COL_U:COL_U + D_M]
    p_sconv = proj_p3[:, tail:, COL_XS:COL_XS + CONV_DIM]

    xs_ = x_sample.reshape(nseq * SEQ4, D_MODEL)
    proj_s, gates_s = _in_proj(xs_, norm_in_r, w_main, w_gate)
    pad_rows = lambda st: jnp.pad(st, ((0, 0), (0, SEQ4 - (CONV_K - 1)), (0, 0))).reshape(nseq * SEQ4, st.shape[-1])
    m_rows = jnp.pad(jnp.repeat(state_m_m[0], SEQ4, axis=0), ((0, 0), (0, LANES - M_HEADS)))
    hm_s, sc, sn, sm = _mlstm_sample(
        proj_s, gates_s, nseq, gate_bias, pad_rows(state_m_conv[0]), m_rows,
        state_m_C[0], state_m_n[0].reshape(nseq, M_HEADS, 1, M_QK), m_conv_w[0], m_conv_b_r, wq, wkt, m_norm_r)
    ys_s, sh = _ssd_sample(
        proj_s, gates_s, nseq, gate_bias, a_vec, d_vec, pad_rows(state_s_conv[0]),
        state_s_ssm[0].reshape(nseq, S_GROUPS, S_GW, S_STATE), s_conv_w[0], s_conv_b_r, s_norm_r)
    x1_s = _out_proj(xs_, hm_s, ys_s, w_out_b)
    y_s = _ple(x1_s, p_sample[0].reshape(nseq * SEQ4, PLE_DIM), ple_norm_r, ple_gate_b, ple_proj_b, final_norm_r)

    proj_s3 = proj_s.reshape(nseq, SEQ4, N_MAIN)
    s_mconv = proj_s3[:, SEQ4 - (CONV_K - 1):, COL_U:COL_U + D_M]
    s_sconv = proj_s3[:, SEQ4 - (CONV_K - 1):, COL_XS:COL_XS + CONV_DIM]
    s_m = jnp.transpose(sm[:, SEQ4 - 1::SEQ4, 0])

    return (
        y_p.reshape(bsz, seqlen, D_MODEL),
        y_s.reshape(nseq, SEQ4, D_MODEL),
        pc[None], pn.reshape(1, bsz, M_HEADS, M_QK), pm[:, :, 0, 0][None], p_mconv[None],
        ph.reshape(1, bsz, S_HEADS, S_HEAD_DIM, S_STATE), p_sconv[None],
        sc[None], sn.reshape(1, nseq, M_HEADS, M_QK), s_m[None], s_mconv[None],
        sh.reshape(1, nseq, S_HEADS, S_HEAD_DIM, S_STATE), s_sconv[None],
    )
```

```python
import functools

import jax
import jax.numpy as jnp
from jax import lax
from jax.experimental import pallas as pl
from jax.experimental.pallas import tpu as pltpu

F32 = jnp.float32
BF16 = jnp.bfloat16
HIGHEST = lax.Precision.HIGHEST

D_MODEL = 4096
D_M = 4096
D_S = 4096
M_HEADS = 8
M_V = 512
M_QK = 256
S_HEADS = 64
S_HEAD_DIM = 64
S_STATE = 128
S_GROUPS = 8
S_HPG = 8
S_GW = S_HPG * S_HEAD_DIM
CONV_DIM = D_S + 2 * S_GROUPS * S_STATE
CONV_K = 4
PLE_DIM = 256
GATE_CAP = 15.0
EPS = 1e-6

COL_U, COL_V, COL_O, COL_ZM, COL_ZS, COL_XS = 0, 4096, 8192, 12288, 16384, 20480
COL_BM = COL_XS + D_S
COL_CM = COL_BM + S_GROUPS * S_STATE
N_MAIN = COL_XS + CONV_DIM
LANES = 128
GL_IG, GL_FG, GL_DT = 0, M_HEADS, 2 * M_HEADS

PROMPT_CHUNK = 256
SAMPLE_BB = 8
VMEM_LIMIT = 56 * 1024 * 1024

NT_DIMS = (((1,), (1,)), ((), ()))


def _dot(a, b):
    return jnp.dot(a, b, preferred_element_type=F32)


def _dot_nt(a, b):
    return lax.dot_general(a, b, NT_DIMS, preferred_element_type=F32)


def _dot_exact(a, b):
    return jnp.dot(a, b, preferred_element_type=F32, precision=HIGHEST)


def _sigmoid(x):
    return 0.5 * jnp.tanh(0.5 * x) + 0.5


def _silu(x):
    return x * _sigmoid(x)


def _softcap(x):
    return GATE_CAP * jnp.tanh(x / GATE_CAP)


def _softplus(x):
    return jnp.maximum(x, 0.0) + jnp.log1p(jnp.exp(-jnp.abs(x)))


def _log_sigmoid(x):
    return -_softplus(-x)


def _idiv(x, d):
    return lax.shift_right_logical(x, d.bit_length() - 1)


def _imod(x, d):
    return lax.bitwise_and(x, d - 1)


def _lane_pick(x, lane_idx):
    lane = lax.broadcasted_iota(jnp.int32, (1, x.shape[1]), 1)
    return jnp.sum(jnp.where(lane == lane_idx, x, 0.0), axis=1, keepdims=True)


def _col_to_row(col, eye):
    return jnp.sum(jnp.where(eye, col, 0.0), axis=0, keepdims=True)


def _row_to_col(row, eye):
    return jnp.sum(jnp.where(eye, row, 0.0), axis=1, keepdims=True)


NORM_ROWS = 32


def _rmsnorm_rows_to(x_ref, g, dst_ref):
    def body(r, carry):
        sl = pl.ds(pl.multiple_of(r * NORM_ROWS, NORM_ROWS), NORM_ROWS)
        x = x_ref[sl, :]
        ms = jnp.mean(x * x, axis=-1, keepdims=True)
        dst_ref[sl, :] = (x * lax.rsqrt(ms + EPS) * g).astype(BF16)
        return carry
    lax.fori_loop(0, x_ref.shape[0] // NORM_ROWS, body, 0)


def _in_proj_body(x_ref, g_ref, w_ref, wg_ref, out_ref, gates_ref, gatest_ref, xn_ref):
    @pl.when(pl.program_id(1) == 0)
    def _():
        _rmsnorm_rows_to(x_ref, g_ref[...], xn_ref)
        gates = _dot(xn_ref[...], wg_ref[...])
        gates_ref[...] = gates
        gatest_ref[...] = gates.T
    out_ref[...] = _dot(xn_ref[...], w_ref[...])


def _in_proj(x, norm_w, w_main, w_gate):
    m = x.shape[0]
    tm = min(512, m)
    tn = 1024
    grid = (m // tm, N_MAIN // tn)
    return pl.pallas_call(
        _in_proj_body,
        grid=grid,
        in_specs=[
            pl.BlockSpec((tm, D_MODEL), lambda i, j: (i, 0)),
            pl.BlockSpec((1, D_MODEL), lambda i, j: (0, 0)),
            pl.BlockSpec((D_MODEL, tn), lambda i, j: (0, j)),
            pl.BlockSpec((D_MODEL, LANES), lambda i, j: (0, 0)),
        ],
        out_specs=[
            pl.BlockSpec((tm, tn), lambda i, j: (i, j)),
            pl.BlockSpec((tm, LANES), lambda i, j: (i, 0)),
            pl.BlockSpec((LANES, tm), lambda i, j: (0, i)),
        ],
        out_shape=[jax.ShapeDtypeStruct((m, N_MAIN), F32), jax.ShapeDtypeStruct((m, LANES), F32),
                   jax.ShapeDtypeStruct((LANES, m), F32)],
        scratch_shapes=[pltpu.VMEM((tm, D_MODEL), BF16)],
        compiler_params=pltpu.CompilerParams(
            dimension_semantics=("parallel", "arbitrary"), vmem_limit_bytes=VMEM_LIMIT),
        name="in_proj",
    )(x, norm_w, w_main, w_gate)


GATE_COLS = 2 * M_HEADS
N_ALIGNED = 4 * D_M


def _w_prep_body(a_ref, b_ref, out_ref):
    tc = a_ref.shape[2]

    @pl.when(pl.program_id(1) * tc < N_ALIGNED)
    def _():
        out_ref[...] = a_ref[0].astype(BF16)

    @pl.when(pl.program_id(1) * tc >= N_ALIGNED)
    def _():
        lane = lax.broadcasted_iota(jnp.int32, (1, LANES), 1)
        keep = lane < LANES - GATE_COLS
        nxt = pltpu.roll(a_ref[0, :, 0:LANES], LANES - GATE_COLS, 1)
        for k in range(tc // LANES):
            cur = nxt
            hi = b_ref[0] if k == tc // LANES - 1 else a_ref[0, :, (k + 1) * LANES:(k + 2) * LANES]
            nxt = pltpu.roll(hi, LANES - GATE_COLS, 1)
            out_ref[:, k * LANES:(k + 1) * LANES] = jnp.where(keep, cur, nxt).astype(BF16)


def _w_prep(w_in):
    rb, tc = 512, 1024
    return pl.pallas_call(
        _w_prep_body,
        grid=(D_MODEL // rb, N_MAIN // tc),
        in_specs=[
            pl.BlockSpec((1, rb, tc), lambda i, j: (0, i, j)),
            pl.BlockSpec((1, rb, LANES), lambda i, j: (0, i, (j + 1) * (tc // LANES))),
        ],
        out_specs=pl.BlockSpec((rb, tc), lambda i, j: (i, j)),
        out_shape=jax.ShapeDtypeStruct((D_MODEL, N_MAIN), BF16),
        compiler_params=pltpu.CompilerParams(
            dimension_semantics=("parallel", "parallel"), vmem_limit_bytes=VMEM_LIMIT),
        name="w_prep",
    )(w_in, w_in)


TAIL = 8


def _conv_carry(u, ext_ref, cw, cb):
    c = u.shape[0]
    ext_ref[TAIL:TAIL + c, :] = u
    acc = cb + u * cw[CONV_K - 1:CONV_K, :]
    for j in range(1, CONV_K):
        acc = acc + ext_ref[TAIL - j:TAIL - j + c, :] * cw[CONV_K - 1 - j:CONV_K - j, :]
    ext_ref[0:TAIL, :] = u[c - TAIL:c, :]
    return acc


def _conv_rows(u, buf, cw, cb, t):
    rows = u.shape[0]
    acc = cb + u * cw[CONV_K - 1:CONV_K, :]
    for j in range(1, CONV_K):
        prev = buf if j == CONV_K - 1 else pltpu.roll(buf, rows + j - (CONV_K - 1), 0)
        acc = acc + jnp.where(t >= j, pltpu.roll(u, j, 0), prev) * cw[CONV_K - 1 - j:CONV_K - j, :]
    return acc


def _head_out(hv, o, z, nw):
    og = _sigmoid(o) * hv
    ms = jnp.mean(og * og, axis=-1, keepdims=True)
    return (og * lax.rsqrt(ms + EPS) * nw) * _silu(z)


def _group_out(y, xs, dsk, z, nw):
    gated = (y + dsk * xs) * _silu(z)
    ms = jnp.mean(gated * gated, axis=-1, keepdims=True)
    return gated * lax.rsqrt(ms + EPS) * nw


def _mlstm_prompt_body(u_ref, v_ref, o_ref, z_ref, gt_ref, gb_ref, cw_ref, cb_ref, wq_ref, wkt_ref, nw_ref,
                       hm_ref, c_ref, n_ref, m_ref, ext_ref, mst_ref):
    c = u_ref.shape[0]
    hps = wq_ref.shape[0]
    h0 = pl.program_id(1) * hps

    @pl.when(pl.program_id(2) == 0)
    def _():
        c_ref[...] = jnp.zeros_like(c_ref)
        n_ref[...] = jnp.zeros_like(n_ref)
        mst_ref[...] = jnp.full_like(mst_ref, -jnp.inf)
        ext_ref[0:TAIL, :] = jnp.zeros((TAIL, ext_ref.shape[1]), F32)

    ucb = _silu(_conv_carry(u_ref[...], ext_ref, cw_ref[...], cb_ref[...])).astype(BF16)
    vb = v_ref[...].astype(BF16)

    rowi = lax.broadcasted_iota(jnp.int32, (c, c), 0)
    coli = lax.broadcasted_iota(jnp.int32, (c, c), 1)
    tri = coli <= rowi
    eye = coli == rowi
    ig8 = _softcap(gt_ref[GL_IG:GL_IG + M_HEADS, :] + gb_ref[GL_IG:GL_IG + M_HEADS, :])
    lf8 = _log_sigmoid(_softcap(gt_ref[GL_FG:GL_FG + M_HEADS, :] + gb_ref[GL_FG:GL_FG + M_HEADS, :]))
    b8 = _dot_exact(lf8, jnp.where(rowi <= coli, 1.0, 0.0))
    head_row = lax.broadcasted_iota(jnp.int32, (M_HEADS, 1), 0)

    for hh in range(hps):
        cols = slice(hh * M_V, (hh + 1) * M_V)
        pick = head_row == h0 + hh
        ig_row = jnp.sum(jnp.where(pick, ig8, 0.0), axis=0, keepdims=True)
        b_row = jnp.sum(jnp.where(pick, b8, 0.0), axis=0, keepdims=True)
        b_col = _row_to_col(b_row, eye)
        m_prev = mst_ref[hh:hh + 1, :][:, 0:1]

        q = _dot(ucb[:, cols], wq_ref[hh])
        kt = _dot_nt(wkt_ref[hh], ucb[:, cols])
        qb = q.astype(BF16)
        ktb = kt.astype(BF16)

        r_row = ig_row - b_row
        dmat = jnp.where(tri, b_col + r_row, -jnp.inf)
        inter = b_col + m_prev
        m_t = jnp.maximum(inter, jnp.max(dmat, axis=1, keepdims=True))
        a_inter = jnp.exp(inter - m_t)
        s = _dot(qb, ktb) * jnp.exp(dmat - m_t)
        c_old = c_ref[0, hh]
        n_old = n_ref[0, hh]
        num = _dot(s.astype(BF16), vb[:, cols]) + a_inter * _dot(qb, c_old.astype(BF16))
        qn = _dot_nt(qb, jnp.broadcast_to(n_old, (8, M_QK)).astype(BF16))[:, 0:1]
        den = jnp.sum(s, axis=1, keepdims=True) + a_inter * qn
        hv = num / jnp.maximum(jnp.abs(den), jnp.exp(-m_t))
        hm_ref[:, cols] = _head_out(hv, o_ref[:, cols], z_ref[:, cols], nw_ref[:, cols]).astype(BF16)

        b_end = b_row[:, c - 1:c]
        lw_row = b_end + r_row
        m_new = jnp.maximum(b_end + m_prev, jnp.max(lw_row, axis=1, keepdims=True))
        decay = jnp.exp(b_end + m_prev - m_new)
        wk_row = jnp.exp(lw_row - m_new)
        c_ref[0, hh] = decay * c_old + _dot((kt * wk_row).astype(BF16), vb[:, cols])
        n_ref[0, hh] = decay * n_old + _dot_nt(jnp.broadcast_to(wk_row, (8, c)).astype(BF16), ktb)[0:1, :]
        mst_ref[hh:hh + 1, :] = jnp.broadcast_to(m_new, (1, LANES))
        m_ref[0, hh] = jnp.broadcast_to(m_new, (1, LANES))


MLSTM_HPS = 2


def _mlstm_prompt(proj, gates_t, bsz, seqlen, gate_bias_col, conv_w, conv_b, wq, wkt, norm_w):
    c = min(PROMPT_CHUNK, seqlen)
    nch = seqlen // c
    hps = MLSTM_HPS
    width = hps * M_V

    def tok(col0):
        return pl.BlockSpec((c, width), lambda b, h, k: (b * nch + k, col0 // width + h))

    head_vec = lambda rows: pl.BlockSpec((rows, width), lambda b, h, k: (0, h))
    return pl.pallas_call(
        _mlstm_prompt_body,
        grid=(bsz, M_HEADS // hps, nch),
        in_specs=[
            tok(COL_U), tok(COL_V), tok(COL_O), tok(COL_ZM),
            pl.BlockSpec((LANES, c), lambda b, h, k: (0, b * nch + k)),
            pl.BlockSpec((LANES, 1), lambda b, h, k: (0, 0)),
            head_vec(CONV_K), head_vec(1),
            pl.BlockSpec((hps, M_V, M_QK), lambda b, h, k: (h, 0, 0)),
            pl.BlockSpec((hps, M_QK, M_V), lambda b, h, k: (h, 0, 0)),
            head_vec(1),
        ],
        out_specs=[
            pl.BlockSpec((c, width), lambda b, h, k: (b * nch + k, h)),
            pl.BlockSpec((1, hps, M_QK, M_V), lambda b, h, k: (b, h, 0, 0)),
            pl.BlockSpec((1, hps, 1, M_QK), lambda b, h, k: (b, h, 0, 0)),
            pl.BlockSpec((1, hps, 1, LANES), lambda b, h, k: (b, h, 0, 0)),
        ],
        out_shape=[
            jax.ShapeDtypeStruct((bsz * seqlen, D_M), BF16),
            jax.ShapeDtypeStruct((bsz, M_HEADS, M_QK, M_V), F32),
            jax.ShapeDtypeStruct((bsz, M_HEADS, 1, M_QK), F32),
            jax.ShapeDtypeStruct((bsz, M_HEADS, 1, LANES), F32),
        ],
        scratch_shapes=[pltpu.VMEM((TAIL + c, width), F32), pltpu.VMEM((8, LANES), F32)],
        compiler_params=pltpu.CompilerParams(
            dimension_semantics=("parallel", "parallel", "arbitrary"), vmem_limit_bytes=VMEM_LIMIT),
        name="mlstm_prompt",
    )(proj, proj, proj, proj, gates_t, gate_bias_col, conv_w, conv_b, wq, wkt, norm_w)


LOG2E = 1.4426950408889634
HEAD_ROWS = [slice(r * S_HEAD_DIM, (r + 1) * S_HEAD_DIM) for r in range(S_HPG)]


def _ssd_gate_rows(gt_ref, gbc_ref, avc_ref, lane0):
    rows = pl.ds(lane0, S_HPG)
    dt8 = _softplus(gt_ref[rows, :] + gbc_ref[rows, :])
    return dt8, dt8 * (-jnp.exp(avc_ref[rows, :]))


def _ssd_intra(xst, dt8, cum8, cbt):
    c = xst.shape[1]
    a2r = cum8 * LOG2E
    a2c = jnp.concatenate([a2r, jnp.zeros((LANES - S_HPG, c), F32)], axis=0).T
    ys, xdts = [], []
    for r in range(S_HPG):
        xdt = xst[HEAD_ROWS[r], :] * dt8[r:r + 1, :]
        decay = jnp.exp2(jnp.minimum(a2r[r:r + 1, :] - a2c[:, r:r + 1], 0.0))
        ys.append(_dot(xdt.astype(BF16), (cbt * decay).astype(BF16)))
        xdts.append(xdt)
    return ys, xdts


def _ssd_prompt_body(xs_ref, bm_ref, cm_ref, zs_ref, gt_ref, gbc_ref, avc_ref, dsk_ref,
                     cwx_ref, cbx_ref, cwb_ref, cbb_ref, cwc_ref, cbc_ref, nw_ref,
                     ys_ref, h_ref, extx_ref, extb_ref, extc_ref):
    c = xs_ref.shape[0]
    gps = h_ref.shape[1]
    g0 = pl.program_id(1) * gps

    @pl.when(pl.program_id(2) == 0)
    def _():
        h_ref[...] = jnp.zeros_like(h_ref)
        extx_ref[0:TAIL, :] = jnp.zeros((TAIL, extx_ref.shape[1]), F32)
        extb_ref[0:TAIL, :] = jnp.zeros((TAIL, extb_ref.shape[1]), F32)
        extc_ref[0:TAIL, :] = jnp.zeros((TAIL, extc_ref.shape[1]), F32)

    xs_all = _silu(_conv_carry(xs_ref[...], extx_ref, cwx_ref[...], cbx_ref[...]))
    bmb_all = _silu(_conv_carry(bm_ref[...], extb_ref, cwb_ref[...], cbb_ref[...])).astype(BF16)
    cmb_all = _silu(_conv_carry(cm_ref[...], extc_ref, cwc_ref[...], cbc_ref[...])).astype(BF16)

    src = lax.broadcasted_iota(jnp.int32, (c, c), 0)
    tgt = lax.broadcasted_iota(jnp.int32, (c, c), 1)
    causal = src <= tgt
    upper = jnp.where(causal, 1.0, 0.0)

    for gg in range(gps):
        cols = slice(gg * S_GW, (gg + 1) * S_GW)
        xs = xs_all[:, cols]
        bmb = bmb_all[:, gg * S_STATE:(gg + 1) * S_STATE]
        cmb = cmb_all[:, gg * S_STATE:(gg + 1) * S_STATE]
        lane0 = pl.multiple_of(GL_DT + (g0 + gg) * S_HPG, S_HPG)
        dt8, la8 = _ssd_gate_rows(gt_ref, gbc_ref, avc_ref, lane0)
        cum8 = _dot_exact(la8, upper)
        a_end = cum8[:, c - 1:c]
        w8 = jnp.exp(a_end - cum8)
        ea8 = jnp.exp(cum8)

        cbt = jnp.where(causal, _dot_nt(bmb, cmb), 0.0)
        ys, xdts = _ssd_intra(xs.T, dt8, cum8, cbt)
        h_old = h_ref[0, gg]
        yit = _dot_nt(h_old.astype(BF16), cmb)
        yt = jnp.concatenate([ys[r] + ea8[r:r + 1, :] * yit[HEAD_ROWS[r], :] for r in range(S_HPG)], axis=0)
        ys_ref[:, cols] = _group_out(yt.T, xs, dsk_ref[:, cols], zs_ref[:, cols], nw_ref[:, cols]).astype(BF16)

        xwt = jnp.concatenate([(xdts[r] * w8[r:r + 1, :]).astype(BF16) for r in range(S_HPG)], axis=0)
        upd = _dot(xwt, bmb)
        ea_end = jnp.exp(a_end)
        for r in range(S_HPG):
            h_ref[0, gg, HEAD_ROWS[r], :] = ea_end[r:r + 1, :] * h_old[HEAD_ROWS[r], :] + upd[HEAD_ROWS[r], :]


SSD_GPS = 2


def _ssd_prompt(proj, gates_t, bsz, seqlen, gate_bias_col, a_col, dsk_row, conv_w, conv_b, norm_w):
    c = min(PROMPT_CHUNK, seqlen)
    nch = seqlen // c
    gps = SSD_GPS
    tok = lambda width, col0: pl.BlockSpec((c, gps * width), lambda b, g, k: (b * nch + k, col0 // (gps * width) + g))
    vec = lambda rows, width, col0: pl.BlockSpec((rows, gps * width), lambda b, g, k: (0, col0 // (gps * width) + g))
    col = pl.BlockSpec((LANES, 1), lambda b, g, k: (0, 0))
    return pl.pallas_call(
        _ssd_prompt_body,
        grid=(bsz, S_GROUPS // gps, nch),
        in_specs=[
            tok(S_GW, COL_XS), tok(S_STATE, COL_BM), tok(S_STATE, COL_CM), tok(S_GW, COL_ZS),
            pl.BlockSpec((LANES, c), lambda b, g, k: (0, b * nch + k)),
            col, col, vec(1, S_GW, 0),
            vec(CONV_K, S_GW, 0), vec(1, S_GW, 0),
            vec(CONV_K, S_STATE, D_S), vec(1, S_STATE, D_S),
            vec(CONV_K, S_STATE, D_S + S_GROUPS * S_STATE), vec(1, S_STATE, D_S + S_GROUPS * S_STATE),
            vec(1, S_GW, 0),
        ],
        out_specs=[
            pl.BlockSpec((c, gps * S_GW), lambda b, g, k: (b * nch + k, g)),
            pl.BlockSpec((1, gps, S_GW, S_STATE), lambda b, g, k: (b, g, 0, 0)),
        ],
        out_shape=[
            jax.ShapeDtypeStruct((bsz * seqlen, D_S), BF16),
            jax.ShapeDtypeStruct((bsz, S_GROUPS, S_GW, S_STATE), F32),
        ],
        scratch_shapes=[pltpu.VMEM((TAIL + c, gps * S_GW), F32), pltpu.VMEM((TAIL + c, gps * S_STATE), F32),
                        pltpu.VMEM((TAIL + c, gps * S_STATE), F32)],
        compiler_params=pltpu.CompilerParams(
            dimension_semantics=("parallel", "parallel", "arbitrary"), vmem_limit_bytes=VMEM_LIMIT),
        name="ssd_prompt",
    )(proj, proj, proj, proj, gates_t, gate_bias_col, a_col, dsk_row,
      conv_w, conv_b, conv_w, conv_b, conv_w, conv_b, norm_w)


SEQ4 = 4


def _row_masks(rows):
    rowi = lax.broadcasted_iota(jnp.int32, (rows, rows), 0)
    coli = lax.broadcasted_iota(jnp.int32, (rows, rows), 1)
    same = _idiv(rowi, SEQ4) == _idiv(coli, SEQ4)
    return same, same & (coli <= rowi), coli == rowi, same & (_imod(coli, SEQ4) == SEQ4 - 1)


def _seg_cumsum(x, t):
    out = x
    for j in range(1, SEQ4):
        out = out + jnp.where(t >= j, pltpu.roll(x, j, 0), 0.0)
    return out


def _mlstm_sample_body(u_ref, v_ref, o_ref, z_ref, gt_ref, gb_ref, buf_ref, mrow_ref, c0_ref, n0_ref,
                       cw_ref, cb_ref, wq_ref, wkt_ref, nw_ref,
                       hm_ref, c_ref, n_ref, m_ref):
    h = pl.program_id(0)
    rows = u_ref.shape[0]
    nseq = rows // SEQ4
    rid = lax.broadcasted_iota(jnp.int32, (rows, 1), 0)
    t = _imod(rid, SEQ4)
    seq = _idiv(rid, SEQ4)
    seq_lane = _idiv(lax.broadcasted_iota(jnp.int32, (1, rows), 1), SEQ4)

    ucb = _silu(_conv_rows(u_ref[...], buf_ref[...], cw_ref[...], cb_ref[...], t)).astype(BF16)
    q = _dot(ucb, wq_ref[0])
    kt = _dot_nt(wkt_ref[0], ucb)
    qb = q.astype(BF16)
    vb = v_ref[...].astype(BF16)

    capped = _softcap(gt_ref[...] + gb_ref[...])
    ig_col = _lane_pick(capped, GL_IG + h)
    b_col = _lane_pick(_seg_cumsum(_log_sigmoid(capped), t), GL_FG + h)
    m_prev = _lane_pick(mrow_ref[...], h)

    same, tri, eye, last = _row_masks(rows)
    b_row = _col_to_row(b_col, eye)
    ig_row = _col_to_row(ig_col, eye)
    dmat = jnp.where(tri, b_col - b_row + ig_row, -jnp.inf)
    inter = b_col + m_prev
    m_t = jnp.maximum(inter, jnp.max(dmat, axis=1, keepdims=True))
    a_inter = jnp.exp(inter - m_t)
    s = _dot(qb, kt.astype(BF16)) * jnp.exp(dmat - m_t)

    b_end = jnp.sum(jnp.where(last, b_row, 0.0), axis=1, keepdims=True)
    lw = b_end - b_col + ig_col
    m_new = jnp.maximum(b_end + m_prev, jnp.max(jnp.where(same, _col_to_row(lw, eye), -jnp.inf), axis=1, keepdims=True))
    decay = jnp.exp(b_end + m_prev - m_new)
    kwt = kt * _col_to_row(jnp.exp(lw - m_new), eye)
    eye_k = (lax.broadcasted_iota(jnp.int32, (M_QK, M_QK), 0) == lax.broadcasted_iota(jnp.int32, (M_QK, M_QK), 1))

    num_inter = jnp.zeros((rows, M_V), F32)
    qn = jnp.zeros((rows, 1), F32)
    for bi in range(nseq):
        own = seq == bi
        c_old = c0_ref[bi, 0]
        n_old = n0_ref[bi, 0]
        num_inter = jnp.where(own, _dot(qb, c_old.astype(BF16)), num_inter)
        qn = jnp.where(own, jnp.sum(q * n_old, axis=1, keepdims=True), qn)
        kw_b = jnp.where(seq_lane == bi, kwt, 0.0)
        d_b = jnp.sum(jnp.where(own & (t == SEQ4 - 1), decay, 0.0), axis=0, keepdims=True)
        c_ref[bi, 0] = d_b * c_old + _dot(kw_b.astype(BF16), vb)
        n_ref[bi, 0] = d_b * n_old + _col_to_row(jnp.sum(kw_b, axis=1, keepdims=True), eye_k)

    num = _dot(s.astype(BF16), vb) + a_inter * num_inter
    den = jnp.sum(s, axis=1, keepdims=True) + a_inter * qn
    hv = num / jnp.maximum(jnp.abs(den), jnp.exp(-m_t))
    hm_ref[...] = _head_out(hv, o_ref[...], z_ref[...], nw_ref[...]).astype(BF16)
    m_ref[0] = jnp.broadcast_to(m_new, (rows, LANES))


def _mlstm_sample(proj, gates, nseq_total, gate_bias, buf_rows, m_rows, c0, n0, conv_w, conv_b, wq, wkt, norm_w):
    bb = min(SAMPLE_BB, nseq_total)
    rows = bb * SEQ4
    nblk = nseq_total // bb
    tok = lambda col0: pl.BlockSpec((rows, M_V), lambda h, i: (i, col0 // M_V + h))
    head_vec = lambda r: pl.BlockSpec((r, M_V), lambda h, i: (0, h))
    c_spec = pl.BlockSpec((bb, 1, M_QK, M_V), lambda h, i: (i, h, 0, 0))
    n_spec = pl.BlockSpec((bb, 1, 1, M_QK), lambda h, i: (i, h, 0, 0))
    return pl.pallas_call(
        _mlstm_sample_body,
        grid=(M_HEADS, nblk),
        in_specs=[
            tok(COL_U), tok(COL_V), tok(COL_O), tok(COL_ZM),
            pl.BlockSpec((rows, LANES), lambda h, i: (i, 0)),
            pl.BlockSpec((1, LANES), lambda h, i: (0, 0)),
            pl.BlockSpec((rows, M_V), lambda h, i: (i, h)),
            pl.BlockSpec((rows, LANES), lambda h, i: (i, 0)),
            c_spec, n_spec,
            head_vec(CONV_K), head_vec(1),
            pl.BlockSpec((1, M_V, M_QK), lambda h, i: (h, 0, 0)),
            pl.BlockSpec((1, M_QK, M_V), lambda h, i: (h, 0, 0)),
            head_vec(1),
        ],
        out_specs=[
            pl.BlockSpec((rows, M_V), lambda h, i: (i, h)),
            c_spec, n_spec,
            pl.BlockSpec((1, rows, LANES), lambda h, i: (h, i, 0)),
        ],
        out_shape=[
            jax.ShapeDtypeStruct((nseq_total * SEQ4, D_M), BF16),
            jax.ShapeDtypeStruct((nseq_total, M_HEADS, M_QK, M_V), F32),
            jax.ShapeDtypeStruct((nseq_total, M_HEADS, 1, M_QK), F32),
            jax.ShapeDtypeStruct((M_HEADS, nseq_total * SEQ4, LANES), F32),
        ],
        compiler_params=pltpu.CompilerParams(
            dimension_semantics=("parallel", "parallel"), vmem_limit_bytes=VMEM_LIMIT),
        name="mlstm_sample",
    )(proj, proj, proj, proj, gates, gate_bias, buf_rows, m_rows, c0, n0, conv_w, conv_b, wq, wkt, norm_w)


def _ssd_sample_body(xs_ref, bm_ref, cm_ref, zs_ref, gt_ref, gbc_ref, avc_ref, dsk_ref,
                     bufx_ref, bufb_ref, bufc_ref, h0_ref,
                     cwx_ref, cbx_ref, cwb_ref, cbb_ref, cwc_ref, cbc_ref, nw_ref,
                     ys_ref, h_ref, yit_ref, xwt_ref):
    g = pl.program_id(0)
    rows = xs_ref.shape[0]
    nseq = rows // SEQ4
    rid = lax.broadcasted_iota(jnp.int32, (rows, 1), 0)
    t = _imod(rid, SEQ4)
    seq = _idiv(rid, SEQ4)
    tok_lane = lax.broadcasted_iota(jnp.int32, (1, rows), 1)

    xs = _silu(_conv_rows(xs_ref[...], bufx_ref[...], cwx_ref[...], cbx_ref[...], t))
    bmb = _silu(_conv_rows(bm_ref[...], bufb_ref[...], cwb_ref[...], cbb_ref[...], t)).astype(BF16)
    cmb = _silu(_conv_rows(cm_ref[...], bufc_ref[...], cwc_ref[...], cbc_ref[...], t)).astype(BF16)

    lane0 = pl.multiple_of(GL_DT + g * S_HPG, S_HPG)
    dt8, la8 = _ssd_gate_rows(gt_ref, gbc_ref, avc_ref, lane0)
    src = lax.broadcasted_iota(jnp.int32, (rows, rows), 0)
    tgt = lax.broadcasted_iota(jnp.int32, (rows, rows), 1)
    same = _idiv(src, SEQ4) == _idiv(tgt, SEQ4)
    causal = same & (src <= tgt)
    cum8 = _dot_exact(la8, jnp.where(causal, 1.0, 0.0))
    a_end = _dot_exact(cum8, jnp.where(same & (_imod(src, SEQ4) == SEQ4 - 1), 1.0, 0.0))
    w8 = jnp.exp(a_end - cum8)
    ea8 = jnp.exp(cum8)
    ea_end = jnp.exp(a_end)

    cbt = jnp.where(causal, _dot_nt(bmb, cmb), 0.0)
    ys, xdts = _ssd_intra(xs.T, dt8, cum8, cbt)
    xwt_ref[...] = jnp.concatenate([(xdts[r] * w8[r:r + 1, :]).astype(BF16) for r in range(S_HPG)], axis=0)
    yit_ref[...] = jnp.zeros_like(yit_ref)

    def per_sequence(bi, carry):
        h_old = h0_ref[bi, 0]
        own = _idiv(tok_lane, SEQ4) == bi
        yit_ref[...] += _dot_nt(h_old.astype(BF16), jnp.where(seq == bi, cmb, jnp.zeros_like(cmb)))
        upd = _dot(jnp.where(own, xwt_ref[...], jnp.zeros_like(xwt_ref)), bmb)
        d8 = jnp.sum(jnp.where(tok_lane == bi * SEQ4 + SEQ4 - 1, ea_end, 0.0), axis=1, keepdims=True)
        for r in range(S_HPG):
            h_ref[bi, 0, HEAD_ROWS[r], :] = d8[r:r + 1, :] * h_old[HEAD_ROWS[r], :] + upd[HEAD_ROWS[r], :]
        return carry

    lax.fori_loop(0, nseq, per_sequence, 0)
    yit = yit_ref[...]
    yt = jnp.concatenate([ys[r] + ea8[r:r + 1, :] * yit[HEAD_ROWS[r], :] for r in range(S_HPG)], axis=0)
    ys_ref[...] = _group_out(yt.T, xs, dsk_ref[...], zs_ref[...], nw_ref[...]).astype(BF16)


SSD_SAMPLE_BB = 32


def _ssd_sample(proj, gates_t, nseq_total, gate_bias_col, a_col, dsk_row, buf_rows, h0, conv_w, conv_b, norm_w):
    bb = min(SSD_SAMPLE_BB, nseq_total)
    rows = bb * SEQ4
    nblk = nseq_total // bb
    tok = lambda width, col0: pl.BlockSpec((rows, width), lambda g, i: (i, col0 // width + g))
    vec = lambda r, width, col0: pl.BlockSpec((r, width), lambda g, i: (0, col0 // width + g))
    col = pl.BlockSpec((LANES, 1), lambda g, i: (0, 0))
    h_spec = pl.BlockSpec((bb, 1, S_GW, S_STATE), lambda g, i: (i, g, 0, 0))
    off_b = D_S
    off_c = D_S + S_GROUPS * S_STATE
    return pl.pallas_call(
        _ssd_sample_body,
        grid=(S_GROUPS, nblk),
        in_specs=[
            tok(S_GW, COL_XS), tok(S_STATE, COL_BM), tok(S_STATE, COL_CM), tok(S_GW, COL_ZS),
            pl.BlockSpec((LANES, rows), lambda g, i: (0, i)),
            col, col, vec(1, S_GW, 0),
            tok(S_GW, 0), tok(S_STATE, off_b), tok(S_STATE, off_c),
            h_spec,
            vec(CONV_K, S_GW, 0), vec(1, S_GW, 0),
            vec(CONV_K, S_STATE, off_b), vec(1, S_STATE, off_b),
            vec(CONV_K, S_STATE, off_c), vec(1, S_STATE, off_c),
            vec(1, S_GW, 0),
        ],
        out_specs=[pl.BlockSpec((rows, S_GW), lambda g, i: (i, g)), h_spec],
        out_shape=[
            jax.ShapeDtypeStruct((nseq_total * SEQ4, D_S), BF16),
            jax.ShapeDtypeStruct((nseq_total, S_GROUPS, S_GW, S_STATE), F32),
        ],
        scratch_shapes=[pltpu.VMEM((S_GW, rows), F32), pltpu.VMEM((S_GW, rows), BF16)],
        compiler_params=pltpu.CompilerParams(
            dimension_semantics=("parallel", "parallel"), vmem_limit_bytes=VMEM_LIMIT),
        name="ssd_sample",
    )(proj, proj, proj, proj, gates_t, gate_bias_col, a_col, dsk_row, buf_rows, buf_rows, buf_rows, h0,
      conv_w, conv_b, conv_w, conv_b, conv_w, conv_b, norm_w)


def _out_proj_body(x_ref, hm_ref, ys_ref, w_ref, out_ref):
    k = pl.program_id(2)

    @pl.when(k == 0)
    def _():
        out_ref[...] = x_ref[...] + _dot(hm_ref[...], w_ref[...])

    @pl.when(k == 1)
    def _():
        out_ref[...] += _dot(ys_ref[...], w_ref[...])


def _out_proj(x, hm, ys, w_out):
    m = x.shape[0]
    tm = min(512, m)
    tn = 1024
    return pl.pallas_call(
        _out_proj_body,
        grid=(m // tm, D_MODEL // tn, 2),
        in_specs=[
            pl.BlockSpec((tm, tn), lambda i, j, k: (i, j)),
            pl.BlockSpec((tm, D_M), lambda i, j, k: (i, 0)),
            pl.BlockSpec((tm, D_S), lambda i, j, k: (i, 0)),
            pl.BlockSpec((D_M, tn), lambda i, j, k: (k, j)),
        ],
        out_specs=pl.BlockSpec((tm, tn), lambda i, j, k: (i, j)),
        out_shape=jax.ShapeDtypeStruct((m, D_MODEL), F32),
        compiler_params=pltpu.CompilerParams(
            dimension_semantics=("parallel", "parallel", "arbitrary"), vmem_limit_bytes=VMEM_LIMIT),
        name="out_proj",
    )(x, hm, ys, w_out)


def _ple_body(x1_ref, x1t_ref, p_ref, gn_ref, wg_ref, wp_ref, fn_ref, out_ref, xn_ref, ss_ref):
    j = pl.program_id(1)
    nj = pl.num_programs(1)
    tn = wg_ref.shape[1]

    @pl.when(j == 0)
    def _():
        _rmsnorm_rows_to(x1_ref, gn_ref[...], xn_ref)
        ss_ref[...] = jnp.zeros_like(ss_ref)

    gate = _sigmoid(_dot(xn_ref[...], wg_ref[...]))
    x2 = x1t_ref[...] + gate * _dot(p_ref[...].astype(BF16), wp_ref[...])
    ss_ref[...] += jnp.sum(x2 * x2, axis=1, keepdims=True)
    for jj in range(out_ref.shape[1] // tn):
        @pl.when(j == jj)
        def _(jj=jj):
            out_ref[:, jj * tn:(jj + 1) * tn] = x2

    @pl.when(j == nj - 1)
    def _():
        fn = fn_ref[...]

        def body(r, carry):
            sl = pl.ds(pl.multiple_of(r * NORM_ROWS, NORM_ROWS), NORM_ROWS)
            scale = lax.rsqrt(ss_ref[sl, :] * (1.0 / D_MODEL) + EPS)
            out_ref[sl, :] = out_ref[sl, :] * scale * fn
            return carry
        lax.fori_loop(0, out_ref.shape[0] // NORM_ROWS, body, 0)


def _ple(x1, p, ple_norm, w_gate, w_proj, final_norm):
    m = x1.shape[0]
    tm = min(512, m)
    tn = 512
    return pl.pallas_call(
        _ple_body,
        grid=(m // tm, D_MODEL // tn),
        in_specs=[
            pl.BlockSpec((tm, D_MODEL), lambda i, j: (i, 0)),
            pl.BlockSpec((tm, tn), lambda i, j: (i, j)),
            pl.BlockSpec((tm, PLE_DIM), lambda i, j: (i, 0)),
            pl.BlockSpec((1, D_MODEL), lambda i, j: (0, 0)),
            pl.BlockSpec((D_MODEL, tn), lambda i, j: (0, j)),
            pl.BlockSpec((PLE_DIM, tn), lambda i, j: (0, j)),
            pl.BlockSpec((1, D_MODEL), lambda i, j: (0, 0)),
        ],
        out_specs=pl.BlockSpec((tm, D_MODEL), lambda i, j: (i, 0)),
        out_shape=jax.ShapeDtypeStruct((m, D_MODEL), F32),
        scratch_shapes=[pltpu.VMEM((tm, D_MODEL), BF16), pltpu.VMEM((tm, 1), F32)],
        compiler_params=pltpu.CompilerParams(
            dimension_semantics=("parallel", "arbitrary"), vmem_limit_bytes=VMEM_LIMIT),
        name="ple",
    )(x1, x1, p, ple_norm, w_gate, w_proj, final_norm)


def _pad_lanes(v, offset):
    out = jnp.zeros((1, LANES), F32)
    return lax.dynamic_update_slice(out, v.reshape(1, -1).astype(F32), (0, offset))


def kernel(x_prompt, x_sample, p_prompt, p_sample, state_m_C, state_m_n, state_m_m, state_m_conv, state_s_ssm, state_s_conv, norm_in, w_in, b_ig, b_fg, m_conv_w, m_conv_b, w_q, w_k, m_norm, s_conv_w, s_conv_b, dt_bias, a_log, d_skip, s_norm, w_out, ple_proj, ple_gate, ple_norm, final_norm):
    assert w_in.shape[0] == 1, "single layer"
    bsz, seqlen, _ = x_prompt.shape
    nseq, dec_seq, _ = x_sample.shape
    assert dec_seq == SEQ4

    w = w_in[0]
    c_ig = 4 * D_M
    c_zs = c_ig + 2 * M_HEADS
    c_xbc = c_zs + D_S
    c_dt = c_xbc + CONV_DIM
    assert c_ig == N_ALIGNED and c_zs - c_ig == GATE_COLS and c_dt - GATE_COLS == N_MAIN
    w_main = _w_prep(w_in)
    w_gate = jnp.concatenate(
        [w[:, c_ig:c_zs], w[:, c_dt:], jnp.zeros((D_MODEL, LANES - 2 * M_HEADS - S_HEADS), F32)], axis=1).astype(BF16)
    gate_bias = jnp.concatenate(
        [b_ig[0], b_fg[0], dt_bias[0], jnp.zeros((LANES - 2 * M_HEADS - S_HEADS,), F32)]).reshape(1, LANES)
    gate_bias_col = gate_bias.reshape(LANES, 1)
    a_col = _pad_lanes(a_log[0], GL_DT).reshape(LANES, 1)
    dsk_row = jnp.repeat(d_skip[0], S_HEAD_DIM).reshape(1, D_S)
    wq = w_q[0].astype(BF16)
    wkt = (jnp.swapaxes(w_k[0], 1, 2) * (M_QK ** -0.5)).astype(BF16)
    w_out_b = w_out[0].astype(BF16)
    ple_gate_b = ple_gate[0].astype(BF16)
    ple_proj_b = ple_proj[0].astype(BF16)
    norm_in_r = norm_in[0].reshape(1, D_MODEL)
    ple_norm_r = ple_norm[0].reshape(1, D_MODEL)
    final_norm_r = final_norm.reshape(1, D_MODEL)
    m_conv_b_r = m_conv_b[0].reshape(1, D_M)
    s_conv_b_r = s_conv_b[0].reshape(1, CONV_DIM)
    m_norm_r = m_norm[0].reshape(1, D_M)
    s_norm_r = s_norm[0].reshape(1, D_S)

    xp = x_prompt.reshape(bsz * seqlen, D_MODEL)
    proj_p, _, gates_pt = _in_proj(xp, norm_in_r, w_main, w_gate)
    hm_p, pc, pn, pm = _mlstm_prompt(proj_p, gates_pt, bsz, seqlen, gate_bias_col, m_conv_w[0], m_conv_b_r, wq, wkt, m_norm_r)
    ys_p, ph = _ssd_prompt(proj_p, gates_pt, bsz, seqlen, gate_bias_col, a_col, dsk_row, s_conv_w[0], s_conv_b_r, s_norm_r)
    x1_p = _out_proj(xp, hm_p, ys_p, w_out_b)
    y_p = _ple(x1_p, p_prompt[0].reshape(bsz * seqlen, PLE_DIM), ple_norm_r, ple_gate_b, ple_proj_b, final_norm_r)

    proj_p3 = proj_p.reshape(bsz, seqlen, N_MAIN)
    tail = seqlen - (CONV_K - 1)
    p_mconv = proj_p3[:, tail:, COL_U:COL_U + D_M]
    p_sconv = proj_p3[:, tail:, COL_XS:COL_XS + CONV_DIM]

    xs_ = x_sample.reshape(nseq * SEQ4, D_MODEL)
    proj_s, gates_s, gates_st = _in_proj(xs_, norm_in_r, w_main, w_gate)
    pad_rows = lambda st: jnp.pad(st, ((0, 0), (0, SEQ4 - (CONV_K - 1)), (0, 0))).reshape(nseq * SEQ4, st.shape[-1])
    m_rows = jnp.pad(jnp.repeat(state_m_m[0], SEQ4, axis=0), ((0, 0), (0, LANES - M_HEADS)))
    hm_s, sc, sn, sm = _mlstm_sample(
        proj_s, gates_s, nseq, gate_bias, pad_rows(state_m_conv[0]), m_rows,
        state_m_C[0], state_m_n[0].reshape(nseq, M_HEADS, 1, M_QK), m_conv_w[0], m_conv_b_r, wq, wkt, m_norm_r)
    ys_s, sh = _ssd_sample(
        proj_s, gates_st, nseq, gate_bias_col, a_col, dsk_row, pad_rows(state_s_conv[0]),
        state_s_ssm[0].reshape(nseq, S_GROUPS, S_GW, S_STATE), s_conv_w[0], s_conv_b_r, s_norm_r)
    x1_s = _out_proj(xs_, hm_s, ys_s, w_out_b)
    y_s = _ple(x1_s, p_sample[0].reshape(nseq * SEQ4, PLE_DIM), ple_norm_r, ple_gate_b, ple_proj_b, final_norm_r)

    keep = SEQ4 - (CONV_K - 1)
    s_mconv = proj_s[:, COL_U:COL_U + D_M].reshape(nseq, SEQ4, D_M)[:, keep:]
    s_sconv = proj_s[:, COL_XS:COL_XS + CONV_DIM].reshape(nseq, SEQ4, CONV_DIM)[:, keep:]
    s_m = jnp.transpose(sm[:, SEQ4 - 1::SEQ4, 0])

    return (
        y_p.reshape(bsz, seqlen, D_MODEL),
        y_s.reshape(nseq, SEQ4, D_MODEL),
        pc[None], pn.reshape(1, bsz, M_HEADS, M_QK), pm[:, :, 0, 0][None], p_mconv[None],
        ph.reshape(1, bsz, S_HEADS, S_HEAD_DIM, S_STATE), p_sconv[None],
        sc[None], sn.reshape(1, nseq, M_HEADS, M_QK), s_m[None], s_mconv[None],
        sh.reshape(1, nseq, S_HEADS, S_HEAD_DIM, S_STATE), s_sconv[None],
    )
```

```python
import functools

import jax
import jax.numpy as jnp
from jax import lax
from jax.experimental import pallas as pl
from jax.experimental.pallas import tpu as pltpu

F32 = jnp.float32
BF16 = jnp.bfloat16
HIGHEST = lax.Precision.HIGHEST

D_MODEL = 4096
D_M = 4096
D_S = 4096
M_HEADS = 8
M_V = 512
M_QK = 256
S_HEADS = 64
S_HEAD_DIM = 64
S_STATE = 128
S_GROUPS = 8
S_HPG = 8
S_GW = S_HPG * S_HEAD_DIM
CONV_DIM = D_S + 2 * S_GROUPS * S_STATE
CONV_K = 4
PLE_DIM = 256
GATE_CAP = 15.0
EPS = 1e-6

COL_U, COL_V, COL_O, COL_ZM, COL_ZS, COL_XS = 0, 4096, 8192, 12288, 16384, 20480
COL_BM = COL_XS + D_S
COL_CM = COL_BM + S_GROUPS * S_STATE
N_MAIN = COL_XS + CONV_DIM
LANES = 128
GL_IG, GL_FG, GL_DT = 0, M_HEADS, 2 * M_HEADS

PROMPT_CHUNK = 256
SAMPLE_BB = 8
VMEM_LIMIT = 56 * 1024 * 1024

NT_DIMS = (((1,), (1,)), ((), ()))


def _dot(a, b):
    return jnp.dot(a, b, preferred_element_type=F32)


def _dot_nt(a, b):
    return lax.dot_general(a, b, NT_DIMS, preferred_element_type=F32)


def _dot_exact(a, b):
    return jnp.dot(a, b, preferred_element_type=F32, precision=HIGHEST)


def _sigmoid(x):
    return 0.5 * jnp.tanh(0.5 * x) + 0.5


def _silu(x):
    return x * _sigmoid(x)


def _softcap(x):
    return GATE_CAP * jnp.tanh(x / GATE_CAP)


def _softplus(x):
    return jnp.maximum(x, 0.0) + jnp.log1p(jnp.exp(-jnp.abs(x)))


def _log_sigmoid(x):
    return -_softplus(-x)


def _idiv(x, d):
    return lax.shift_right_logical(x, d.bit_length() - 1)


def _imod(x, d):
    return lax.bitwise_and(x, d - 1)


def _lane_pick(x, lane_idx):
    lane = lax.broadcasted_iota(jnp.int32, (1, x.shape[1]), 1)
    return jnp.sum(jnp.where(lane == lane_idx, x, 0.0), axis=1, keepdims=True)


def _col_to_row(col, eye):
    return jnp.sum(jnp.where(eye, col, 0.0), axis=0, keepdims=True)


def _row_to_col(row, eye):
    return jnp.sum(jnp.where(eye, row, 0.0), axis=1, keepdims=True)


NORM_ROWS = 32


def _rmsnorm_rows_to(x_ref, g, dst_ref):
    def body(r, carry):
        sl = pl.ds(pl.multiple_of(r * NORM_ROWS, NORM_ROWS), NORM_ROWS)
        x = x_ref[sl, :]
        ms = jnp.mean(x * x, axis=-1, keepdims=True)
        dst_ref[sl, :] = (x * lax.rsqrt(ms + EPS) * g).astype(BF16)
        return carry
    lax.fori_loop(0, x_ref.shape[0] // NORM_ROWS, body, 0)


def _in_proj_body(x_ref, g_ref, w_ref, wg_ref, out_ref, gates_ref, gatest_ref, xn_ref):
    @pl.when(pl.program_id(1) == 0)
    def _():
        _rmsnorm_rows_to(x_ref, g_ref[...], xn_ref)
        gates = _dot_nt(xn_ref[...], wg_ref[...])
        gates_ref[...] = gates
        gatest_ref[...] = gates.T
    out_ref[...] = _dot_nt(xn_ref[...], w_ref[...])


def _in_proj(x, norm_w, w_main, w_gate):
    m = x.shape[0]
    tm = min(512, m)
    tn = 1024
    grid = (m // tm, N_MAIN // tn)
    return pl.pallas_call(
        _in_proj_body,
        grid=grid,
        in_specs=[
            pl.BlockSpec((tm, D_MODEL), lambda i, j: (i, 0)),
            pl.BlockSpec((1, D_MODEL), lambda i, j: (0, 0)),
            pl.BlockSpec((tn, D_MODEL), lambda i, j: (j, 0)),
            pl.BlockSpec((LANES, D_MODEL), lambda i, j: (0, 0)),
        ],
        out_specs=[
            pl.BlockSpec((tm, tn), lambda i, j: (i, j)),
            pl.BlockSpec((tm, LANES), lambda i, j: (i, 0)),
            pl.BlockSpec((LANES, tm), lambda i, j: (0, i)),
        ],
        out_shape=[jax.ShapeDtypeStruct((m, N_MAIN), F32), jax.ShapeDtypeStruct((m, LANES), F32),
                   jax.ShapeDtypeStruct((LANES, m), F32)],
        scratch_shapes=[pltpu.VMEM((tm, D_MODEL), BF16)],
        compiler_params=pltpu.CompilerParams(
            dimension_semantics=("parallel", "arbitrary"), vmem_limit_bytes=VMEM_LIMIT),
        name="in_proj",
    )(x, norm_w, w_main, w_gate)


GATE_COLS = 2 * M_HEADS
N_ALIGNED = 4 * D_M


def _w_prep_body(src_ref, ga_ref, gb_ref, out_ref, wg_ref):
    out_ref[...] = src_ref[0].astype(BF16)

    @pl.when(pl.program_id(0) == 0)
    def _():
        wg_ref[0:GL_DT, :] = ga_ref[0].astype(BF16)
        wg_ref[GL_DT:GL_DT + S_HEADS, :] = gb_ref[0].astype(BF16)
        wg_ref[GL_DT + S_HEADS:LANES, :] = jnp.zeros((LANES - GL_DT - S_HEADS, D_MODEL), BF16)


def _w_prep(w_in_t):
    rb = 512

    def src_row(i):
        return pl.multiple_of(i * rb + (i // (N_ALIGNED // rb)) * GATE_COLS, GATE_COLS)

    return pl.pallas_call(
        _w_prep_body,
        grid=(N_MAIN // rb,),
        in_specs=[
            pl.BlockSpec((pl.Element(1), pl.Element(rb), pl.Element(D_MODEL)), lambda i: (0, src_row(i), 0)),
            pl.BlockSpec((pl.Element(1), pl.Element(GATE_COLS), pl.Element(D_MODEL)), lambda i: (0, N_ALIGNED, 0)),
            pl.BlockSpec((pl.Element(1), pl.Element(S_HEADS), pl.Element(D_MODEL)),
                         lambda i: (0, N_MAIN + GATE_COLS, 0)),
        ],
        out_specs=[pl.BlockSpec((rb, D_MODEL), lambda i: (i, 0)), pl.BlockSpec((LANES, D_MODEL), lambda i: (0, 0))],
        out_shape=[jax.ShapeDtypeStruct((N_MAIN, D_MODEL), BF16), jax.ShapeDtypeStruct((LANES, D_MODEL), BF16)],
        compiler_params=pltpu.CompilerParams(dimension_semantics=("arbitrary",), vmem_limit_bytes=VMEM_LIMIT),
        name="w_prep",
    )(w_in_t, w_in_t, w_in_t)


TAIL = 8


def _conv_carry(u, ext_ref, cw, cb):
    c = u.shape[0]
    ext_ref[TAIL:TAIL + c, :] = u
    acc = cb + u * cw[CONV_K - 1:CONV_K, :]
    for j in range(1, CONV_K):
        acc = acc + ext_ref[TAIL - j:TAIL - j + c, :] * cw[CONV_K - 1 - j:CONV_K - j, :]
    ext_ref[0:TAIL, :] = u[c - TAIL:c, :]
    return acc


def _conv_rows(u, buf, cw, cb, t):
    rows = u.shape[0]
    acc = cb + u * cw[CONV_K - 1:CONV_K, :]
    for j in range(1, CONV_K):
        prev = buf if j == CONV_K - 1 else pltpu.roll(buf, rows + j - (CONV_K - 1), 0)
        acc = acc + jnp.where(t >= j, pltpu.roll(u, j, 0), prev) * cw[CONV_K - 1 - j:CONV_K - j, :]
    return acc


def _head_out(hv, o, z, nw):
    og = _sigmoid(o) * hv
    ms = jnp.mean(og * og, axis=-1, keepdims=True)
    return (og * lax.rsqrt(ms + EPS) * nw) * _silu(z)


def _group_out(y, xs, dsk, z, nw):
    gated = (y + dsk * xs) * _silu(z)
    ms = jnp.mean(gated * gated, axis=-1, keepdims=True)
    return gated * lax.rsqrt(ms + EPS) * nw


def _mlstm_prompt_body(u_ref, v_ref, o_ref, z_ref, gt_ref, gb_ref, cw_ref, cb_ref, wq_ref, wkt_ref, nw_ref,
                       hm_ref, c_ref, n_ref, m_ref, ext_ref, mst_ref):
    c = u_ref.shape[0]
    hps = wq_ref.shape[0]
    h0 = pl.program_id(1) * hps

    @pl.when(pl.program_id(2) == 0)
    def _():
        c_ref[...] = jnp.zeros_like(c_ref)
        n_ref[...] = jnp.zeros_like(n_ref)
        mst_ref[...] = jnp.full_like(mst_ref, -jnp.inf)
        ext_ref[0:TAIL, :] = jnp.zeros((TAIL, ext_ref.shape[1]), F32)

    ucb = _silu(_conv_carry(u_ref[...], ext_ref, cw_ref[...], cb_ref[...])).astype(BF16)
    vb = v_ref[...].astype(BF16)

    rowi = lax.broadcasted_iota(jnp.int32, (c, c), 0)
    coli = lax.broadcasted_iota(jnp.int32, (c, c), 1)
    tri = coli <= rowi
    eye = coli == rowi
    ig8 = _softcap(gt_ref[GL_IG:GL_IG + M_HEADS, :] + gb_ref[GL_IG:GL_IG + M_HEADS, :])
    lf8 = _log_sigmoid(_softcap(gt_ref[GL_FG:GL_FG + M_HEADS, :] + gb_ref[GL_FG:GL_FG + M_HEADS, :]))
    b8 = _dot_exact(lf8, jnp.where(rowi <= coli, 1.0, 0.0))
    head_row = lax.broadcasted_iota(jnp.int32, (M_HEADS, 1), 0)

    for hh in range(hps):
        cols = slice(hh * M_V, (hh + 1) * M_V)
        pick = head_row == h0 + hh
        ig_row = jnp.sum(jnp.where(pick, ig8, 0.0), axis=0, keepdims=True)
        b_row = jnp.sum(jnp.where(pick, b8, 0.0), axis=0, keepdims=True)
        b_col = _row_to_col(b_row, eye)
        m_prev = mst_ref[hh:hh + 1, :][:, 0:1]

        q = _dot(ucb[:, cols], wq_ref[hh])
        kt = _dot_nt(wkt_ref[hh], ucb[:, cols])
        qb = q.astype(BF16)
        ktb = kt.astype(BF16)

        r_row = ig_row - b_row
        dmat = jnp.where(tri, b_col + r_row, -jnp.inf)
        inter = b_col + m_prev
        m_t = jnp.maximum(inter, jnp.max(dmat, axis=1, keepdims=True))
        a_inter = jnp.exp(inter - m_t)
        s = _dot(qb, ktb) * jnp.exp(dmat - m_t)
        c_old = c_ref[0, hh]
        n_old = n_ref[0, hh]
        num = _dot(s.astype(BF16), vb[:, cols]) + a_inter * _dot(qb, c_old.astype(BF16))
        qn = _dot_nt(qb, jnp.broadcast_to(n_old, (8, M_QK)).astype(BF16))[:, 0:1]
        den = jnp.sum(s, axis=1, keepdims=True) + a_inter * qn
        hv = num / jnp.maximum(jnp.abs(den), jnp.exp(-m_t))
        hm_ref[:, cols] = _head_out(hv, o_ref[:, cols], z_ref[:, cols], nw_ref[:, cols]).astype(BF16)

        b_end = b_row[:, c - 1:c]
        lw_row = b_end + r_row
        m_new = jnp.maximum(b_end + m_prev, jnp.max(lw_row, axis=1, keepdims=True))
        decay = jnp.exp(b_end + m_prev - m_new)
        wk_row = jnp.exp(lw_row - m_new)
        c_ref[0, hh] = decay * c_old + _dot((kt * wk_row).astype(BF16), vb[:, cols])
        n_ref[0, hh] = decay * n_old + _dot_nt(jnp.broadcast_to(wk_row, (8, c)).astype(BF16), ktb)[0:1, :]
        mst_ref[hh:hh + 1, :] = jnp.broadcast_to(m_new, (1, LANES))
        m_ref[0, hh] = jnp.broadcast_to(m_new, (1, LANES))


MLSTM_HPS = 2


def _mlstm_prompt(proj, gates_t, bsz, seqlen, gate_bias_col, conv_w, conv_b, wq, wkt, norm_w):
    c = min(PROMPT_CHUNK, seqlen)
    nch = seqlen // c
    hps = MLSTM_HPS
    width = hps * M_V

    def tok(col0):
        return pl.BlockSpec((c, width), lambda b, h, k: (b * nch + k, col0 // width + h))

    head_vec = lambda rows: pl.BlockSpec((rows, width), lambda b, h, k: (0, h))
    return pl.pallas_call(
        _mlstm_prompt_body,
        grid=(bsz, M_HEADS // hps, nch),
        in_specs=[
            tok(COL_U), tok(COL_V), tok(COL_O), tok(COL_ZM),
            pl.BlockSpec((LANES, c), lambda b, h, k: (0, b * nch + k)),
            pl.BlockSpec((LANES, 1), lambda b, h, k: (0, 0)),
            head_vec(CONV_K), head_vec(1),
            pl.BlockSpec((hps, M_V, M_QK), lambda b, h, k: (h, 0, 0)),
            pl.BlockSpec((hps, M_QK, M_V), lambda b, h, k: (h, 0, 0)),
            head_vec(1),
        ],
        out_specs=[
            pl.BlockSpec((c, width), lambda b, h, k: (b * nch + k, h)),
            pl.BlockSpec((1, hps, M_QK, M_V), lambda b, h, k: (b, h, 0, 0)),
            pl.BlockSpec((1, hps, 1, M_QK), lambda b, h, k: (b, h, 0, 0)),
            pl.BlockSpec((1, hps, 1, LANES), lambda b, h, k: (b, h, 0, 0)),
        ],
        out_shape=[
            jax.ShapeDtypeStruct((bsz * seqlen, D_M), BF16),
            jax.ShapeDtypeStruct((bsz, M_HEADS, M_QK, M_V), F32),
            jax.ShapeDtypeStruct((bsz, M_HEADS, 1, M_QK), F32),
            jax.ShapeDtypeStruct((bsz, M_HEADS, 1, LANES), F32),
        ],
        scratch_shapes=[pltpu.VMEM((TAIL + c, width), F32), pltpu.VMEM((8, LANES), F32)],
        compiler_params=pltpu.CompilerParams(
            dimension_semantics=("parallel", "parallel", "arbitrary"), vmem_limit_bytes=VMEM_LIMIT),
        name="mlstm_prompt",
    )(proj, proj, proj, proj, gates_t, gate_bias_col, conv_w, conv_b, wq, wkt, norm_w)


LOG2E = 1.4426950408889634
HEAD_ROWS = [slice(r * S_HEAD_DIM, (r + 1) * S_HEAD_DIM) for r in range(S_HPG)]


def _ssd_gate_rows(gt_ref, gbc_ref, avc_ref, lane0):
    rows = pl.ds(lane0, S_HPG)
    dt8 = _softplus(gt_ref[rows, :] + gbc_ref[rows, :])
    return dt8, dt8 * (-jnp.exp(avc_ref[rows, :]))


def _ssd_intra(xst, dt8, cum8, cbt):
    c = xst.shape[1]
    a2r = cum8 * LOG2E
    a2c = jnp.concatenate([a2r, jnp.zeros((LANES - S_HPG, c), F32)], axis=0).T
    ys, xdts = [], []
    for r in range(S_HPG):
        xdt = xst[HEAD_ROWS[r], :] * dt8[r:r + 1, :]
        decay = jnp.exp2(jnp.minimum(a2r[r:r + 1, :] - a2c[:, r:r + 1], 0.0))
        ys.append(_dot(xdt.astype(BF16), (cbt * decay).astype(BF16)))
        xdts.append(xdt)
    return ys, xdts


def _ssd_prompt_body(xs_ref, bm_ref, cm_ref, zs_ref, gt_ref, gbc_ref, avc_ref, dsk_ref,
                     cwx_ref, cbx_ref, cwb_ref, cbb_ref, cwc_ref, cbc_ref, nw_ref,
                     ys_ref, h_ref, extx_ref, extb_ref, extc_ref):
    c = xs_ref.shape[0]
    gps = h_ref.shape[1]
    g0 = pl.program_id(1) * gps

    @pl.when(pl.program_id(2) == 0)
    def _():
        h_ref[...] = jnp.zeros_like(h_ref)
        extx_ref[0:TAIL, :] = jnp.zeros((TAIL, extx_ref.shape[1]), F32)
        extb_ref[0:TAIL, :] = jnp.zeros((TAIL, extb_ref.shape[1]), F32)
        extc_ref[0:TAIL, :] = jnp.zeros((TAIL, extc_ref.shape[1]), F32)

    xs_all = _silu(_conv_carry(xs_ref[...], extx_ref, cwx_ref[...], cbx_ref[...]))
    bmb_all = _silu(_conv_carry(bm_ref[...], extb_ref, cwb_ref[...], cbb_ref[...])).astype(BF16)
    cmb_all = _silu(_conv_carry(cm_ref[...], extc_ref, cwc_ref[...], cbc_ref[...])).astype(BF16)

    src = lax.broadcasted_iota(jnp.int32, (c, c), 0)
    tgt = lax.broadcasted_iota(jnp.int32, (c, c), 1)
    causal = src <= tgt
    upper = jnp.where(causal, 1.0, 0.0)

    for gg in range(gps):
        cols = slice(gg * S_GW, (gg + 1) * S_GW)
        xs = xs_all[:, cols]
        bmb = bmb_all[:, gg * S_STATE:(gg + 1) * S_STATE]
        cmb = cmb_all[:, gg * S_STATE:(gg + 1) * S_STATE]
        lane0 = pl.multiple_of(GL_DT + (g0 + gg) * S_HPG, S_HPG)
        dt8, la8 = _ssd_gate_rows(gt_ref, gbc_ref, avc_ref, lane0)
        cum8 = _dot_exact(la8, upper)
        a_end = cum8[:, c - 1:c]
        w8 = jnp.exp(a_end - cum8)
        ea8 = jnp.exp(cum8)

        cbt = jnp.where(causal, _dot_nt(bmb, cmb), 0.0)
        ys, xdts = _ssd_intra(xs.T, dt8, cum8, cbt)
        h_old = h_ref[0, gg]
        yit = _dot_nt(h_old.astype(BF16), cmb)
        yt = jnp.concatenate([ys[r] + ea8[r:r + 1, :] * yit[HEAD_ROWS[r], :] for r in range(S_HPG)], axis=0)
        ys_ref[:, cols] = _group_out(yt.T, xs, dsk_ref[:, cols], zs_ref[:, cols], nw_ref[:, cols]).astype(BF16)

        xwt = jnp.concatenate([(xdts[r] * w8[r:r + 1, :]).astype(BF16) for r in range(S_HPG)], axis=0)
        upd = _dot(xwt, bmb)
        ea_end = jnp.exp(a_end)
        for r in range(S_HPG):
            h_ref[0, gg, HEAD_ROWS[r], :] = ea_end[r:r + 1, :] * h_old[HEAD_ROWS[r], :] + upd[HEAD_ROWS[r], :]


SSD_GPS = 2


def _ssd_prompt(proj, gates_t, bsz, seqlen, gate_bias_col, a_col, dsk_row, conv_w, conv_b, norm_w):
    c = min(PROMPT_CHUNK, seqlen)
    nch = seqlen // c
    gps = SSD_GPS
    tok = lambda width, col0: pl.BlockSpec((c, gps * width), lambda b, g, k: (b * nch + k, col0 // (gps * width) + g))
    vec = lambda rows, width, col0: pl.BlockSpec((rows, gps * width), lambda b, g, k: (0, col0 // (gps * width) + g))
    col = pl.BlockSpec((LANES, 1), lambda b, g, k: (0, 0))
    return pl.pallas_call(
        _ssd_prompt_body,
        grid=(bsz, S_GROUPS // gps, nch),
        in_specs=[
            tok(S_GW, COL_XS), tok(S_STATE, COL_BM), tok(S_STATE, COL_CM), tok(S_GW, COL_ZS),
            pl.BlockSpec((LANES, c), lambda b, g, k: (0, b * nch + k)),
            col, col, vec(1, S_GW, 0),
            vec(CONV_K, S_GW, 0), vec(1, S_GW, 0),
            vec(CONV_K, S_STATE, D_S), vec(1, S_STATE, D_S),
            vec(CONV_K, S_STATE, D_S + S_GROUPS * S_STATE), vec(1, S_STATE, D_S + S_GROUPS * S_STATE),
            vec(1, S_GW, 0),
        ],
        out_specs=[
            pl.BlockSpec((c, gps * S_GW), lambda b, g, k: (b * nch + k, g)),
            pl.BlockSpec((1, gps, S_GW, S_STATE), lambda b, g, k: (b, g, 0, 0)),
        ],
        out_shape=[
            jax.ShapeDtypeStruct((bsz * seqlen, D_S), BF16),
            jax.ShapeDtypeStruct((bsz, S_GROUPS, S_GW, S_STATE), F32),
        ],
        scratch_shapes=[pltpu.VMEM((TAIL + c, gps * S_GW), F32), pltpu.VMEM((TAIL + c, gps * S_STATE), F32),
                        pltpu.VMEM((TAIL + c, gps * S_STATE), F32)],
        compiler_params=pltpu.CompilerParams(
            dimension_semantics=("parallel", "parallel", "arbitrary"), vmem_limit_bytes=VMEM_LIMIT),
        name="ssd_prompt",
    )(proj, proj, proj, proj, gates_t, gate_bias_col, a_col, dsk_row,
      conv_w, conv_b, conv_w, conv_b, conv_w, conv_b, norm_w)


SEQ4 = 4


def _row_masks(rows):
    rowi = lax.broadcasted_iota(jnp.int32, (rows, rows), 0)
    coli = lax.broadcasted_iota(jnp.int32, (rows, rows), 1)
    same = _idiv(rowi, SEQ4) == _idiv(coli, SEQ4)
    return same, same & (coli <= rowi), coli == rowi, same & (_imod(coli, SEQ4) == SEQ4 - 1)


def _seg_cumsum(x, t):
    out = x
    for j in range(1, SEQ4):
        out = out + jnp.where(t >= j, pltpu.roll(x, j, 0), 0.0)
    return out


def _mlstm_sample_body(u_ref, v_ref, o_ref, z_ref, gt_ref, gb_ref, buf_ref, mrow_ref, c0_ref, n0_ref,
                       cw_ref, cb_ref, wq_ref, wkt_ref, nw_ref,
                       hm_ref, c_ref, n_ref, m_ref):
    h = pl.program_id(0)
    rows = u_ref.shape[0]
    nseq = rows // SEQ4
    rid = lax.broadcasted_iota(jnp.int32, (rows, 1), 0)
    t = _imod(rid, SEQ4)
    seq = _idiv(rid, SEQ4)
    seq_lane = _idiv(lax.broadcasted_iota(jnp.int32, (1, rows), 1), SEQ4)

    ucb = _silu(_conv_rows(u_ref[...], buf_ref[...], cw_ref[...], cb_ref[...], t)).astype(BF16)
    q = _dot(ucb, wq_ref[0])
    kt = _dot_nt(wkt_ref[0], ucb)
    qb = q.astype(BF16)
    vb = v_ref[...].astype(BF16)

    capped = _softcap(gt_ref[...] + gb_ref[...])
    ig_col = _lane_pick(capped, GL_IG + h)
    b_col = _lane_pick(_seg_cumsum(_log_sigmoid(capped), t), GL_FG + h)
    m_prev = _lane_pick(mrow_ref[...], h)

    same, tri, eye, last = _row_masks(rows)
    b_row = _col_to_row(b_col, eye)
    ig_row = _col_to_row(ig_col, eye)
    dmat = jnp.where(tri, b_col - b_row + ig_row, -jnp.inf)
    inter = b_col + m_prev
    m_t = jnp.maximum(inter, jnp.max(dmat, axis=1, keepdims=True))
    a_inter = jnp.exp(inter - m_t)
    s = _dot(qb, kt.astype(BF16)) * jnp.exp(dmat - m_t)

    b_end = jnp.sum(jnp.where(last, b_row, 0.0), axis=1, keepdims=True)
    lw = b_end - b_col + ig_col
    m_new = jnp.maximum(b_end + m_prev, jnp.max(jnp.where(same, _col_to_row(lw, eye), -jnp.inf), axis=1, keepdims=True))
    decay = jnp.exp(b_end + m_prev - m_new)
    kwt = kt * _col_to_row(jnp.exp(lw - m_new), eye)
    eye_k = (lax.broadcasted_iota(jnp.int32, (M_QK, M_QK), 0) == lax.broadcasted_iota(jnp.int32, (M_QK, M_QK), 1))

    num_inter = jnp.zeros((rows, M_V), F32)
    qn = jnp.zeros((rows, 1), F32)
    for bi in range(nseq):
        own = seq == bi
        c_old = c0_ref[bi, 0]
        n_old = n0_ref[bi, 0]
        num_inter = jnp.where(own, _dot(qb, c_old.astype(BF16)), num_inter)
        qn = jnp.where(own, jnp.sum(q * n_old, axis=1, keepdims=True), qn)
        kw_b = jnp.where(seq_lane == bi, kwt, 0.0)
        d_b = jnp.sum(jnp.where(own & (t == SEQ4 - 1), decay, 0.0), axis=0, keepdims=True)
        c_ref[bi, 0] = d_b * c_old + _dot(kw_b.astype(BF16), vb)
        n_ref[bi, 0] = d_b * n_old + _col_to_row(jnp.sum(kw_b, axis=1, keepdims=True), eye_k)

    num = _dot(s.astype(BF16), vb) + a_inter * num_inter
    den = jnp.sum(s, axis=1, keepdims=True) + a_inter * qn
    hv = num / jnp.maximum(jnp.abs(den), jnp.exp(-m_t))
    hm_ref[...] = _head_out(hv, o_ref[...], z_ref[...], nw_ref[...]).astype(BF16)
    m_ref[0] = jnp.broadcast_to(m_new, (rows, LANES))


def _mlstm_sample(proj, gates, nseq_total, gate_bias, buf_rows, m_rows, c0, n0, conv_w, conv_b, wq, wkt, norm_w):
    bb = min(SAMPLE_BB, nseq_total)
    rows = bb * SEQ4
    nblk = nseq_total // bb
    tok = lambda col0: pl.BlockSpec((rows, M_V), lambda h, i: (i, col0 // M_V + h))
    head_vec = lambda r: pl.BlockSpec((r, M_V), lambda h, i: (0, h))
    c_spec = pl.BlockSpec((bb, 1, M_QK, M_V), lambda h, i: (i, h, 0, 0))
    n_spec = pl.BlockSpec((bb, 1, 1, M_QK), lambda h, i: (i, h, 0, 0))
    return pl.pallas_call(
        _mlstm_sample_body,
        grid=(M_HEADS, nblk),
        in_specs=[
            tok(COL_U), tok(COL_V), tok(COL_O), tok(COL_ZM),
            pl.BlockSpec((rows, LANES), lambda h, i: (i, 0)),
            pl.BlockSpec((1, LANES), lambda h, i: (0, 0)),
            pl.BlockSpec((rows, M_V), lambda h, i: (i, h)),
            pl.BlockSpec((rows, LANES), lambda h, i: (i, 0)),
            c_spec, n_spec,
            head_vec(CONV_K), head_vec(1),
            pl.BlockSpec((1, M_V, M_QK), lambda h, i: (h, 0, 0)),
            pl.BlockSpec((1, M_QK, M_V), lambda h, i: (h, 0, 0)),
            head_vec(1),
        ],
        out_specs=[
            pl.BlockSpec((rows, M_V), lambda h, i: (i, h)),
            c_spec, n_spec,
            pl.BlockSpec((1, rows, LANES), lambda h, i: (h, i, 0)),
        ],
        out_shape=[
            jax.ShapeDtypeStruct((nseq_total * SEQ4, D_M), BF16),
            jax.ShapeDtypeStruct((nseq_total, M_HEADS, M_QK, M_V), F32),
            jax.ShapeDtypeStruct((nseq_total, M_HEADS, 1, M_QK), F32),
            jax.ShapeDtypeStruct((M_HEADS, nseq_total * SEQ4, LANES), F32),
        ],
        compiler_params=pltpu.CompilerParams(
            dimension_semantics=("parallel", "parallel"), vmem_limit_bytes=VMEM_LIMIT),
        name="mlstm_sample",
    )(proj, proj, proj, proj, gates, gate_bias, buf_rows, m_rows, c0, n0, conv_w, conv_b, wq, wkt, norm_w)


def _ssd_sample_body(xs_ref, bm_ref, cm_ref, zs_ref, gt_ref, gbc_ref, avc_ref, dsk_ref,
                     bufx_ref, bufb_ref, bufc_ref, h0_ref,
                     cwx_ref, cbx_ref, cwb_ref, cbb_ref, cwc_ref, cbc_ref, nw_ref,
                     ys_ref, h_ref, yit_ref, xwt_ref):
    g = pl.program_id(0)
    rows = xs_ref.shape[0]
    nseq = rows // SEQ4
    rid = lax.broadcasted_iota(jnp.int32, (rows, 1), 0)
    t = _imod(rid, SEQ4)
    seq = _idiv(rid, SEQ4)
    tok_lane = lax.broadcasted_iota(jnp.int32, (1, rows), 1)

    xs = _silu(_conv_rows(xs_ref[...], bufx_ref[...], cwx_ref[...], cbx_ref[...], t))
    bmb = _silu(_conv_rows(bm_ref[...], bufb_ref[...], cwb_ref[...], cbb_ref[...], t)).astype(BF16)
    cmb = _silu(_conv_rows(cm_ref[...], bufc_ref[...], cwc_ref[...], cbc_ref[...], t)).astype(BF16)

    lane0 = pl.multiple_of(GL_DT + g * S_HPG, S_HPG)
    dt8, la8 = _ssd_gate_rows(gt_ref, gbc_ref, avc_ref, lane0)
    src = lax.broadcasted_iota(jnp.int32, (rows, rows), 0)
    tgt = lax.broadcasted_iota(jnp.int32, (rows, rows), 1)
    same = _idiv(src, SEQ4) == _idiv(tgt, SEQ4)
    causal = same & (src <= tgt)
    cum8 = _dot_exact(la8, jnp.where(causal, 1.0, 0.0))
    a_end = _dot_exact(cum8, jnp.where(same & (_imod(src, SEQ4) == SEQ4 - 1), 1.0, 0.0))
    w8 = jnp.exp(a_end - cum8)
    ea8 = jnp.exp(cum8)
    ea_end = jnp.exp(a_end)

    cbt = jnp.where(causal, _dot_nt(bmb, cmb), 0.0)
    ys, xdts = _ssd_intra(xs.T, dt8, cum8, cbt)
    xwt_ref[...] = jnp.concatenate([(xdts[r] * w8[r:r + 1, :]).astype(BF16) for r in range(S_HPG)], axis=0)
    yit_ref[...] = jnp.zeros_like(yit_ref)

    def per_sequence(bi, carry):
        h_old = h0_ref[bi, 0]
        own = _idiv(tok_lane, SEQ4) == bi
        yit_ref[...] += _dot_nt(h_old.astype(BF16), jnp.where(seq == bi, cmb, jnp.zeros_like(cmb)))
        upd = _dot(jnp.where(own, xwt_ref[...], jnp.zeros_like(xwt_ref)), bmb)
        d8 = jnp.sum(jnp.where(tok_lane == bi * SEQ4 + SEQ4 - 1, ea_end, 0.0), axis=1, keepdims=True)
        for r in range(S_HPG):
            h_ref[bi, 0, HEAD_ROWS[r], :] = d8[r:r + 1, :] * h_old[HEAD_ROWS[r], :] + upd[HEAD_ROWS[r], :]
        return carry

    lax.fori_loop(0, nseq, per_sequence, 0, unroll=4 if nseq % 4 == 0 else 1)
    yit = yit_ref[...]
    yt = jnp.concatenate([ys[r] + ea8[r:r + 1, :] * yit[HEAD_ROWS[r], :] for r in range(S_HPG)], axis=0)
    ys_ref[...] = _group_out(yt.T, xs, dsk_ref[...], zs_ref[...], nw_ref[...]).astype(BF16)


SSD_SAMPLE_BB = 32


def _ssd_sample(proj, gates_t, nseq_total, gate_bias_col, a_col, dsk_row, buf_rows, h0, conv_w, conv_b, norm_w):
    bb = min(SSD_SAMPLE_BB, nseq_total)
    rows = bb * SEQ4
    nblk = nseq_total // bb
    tok = lambda width, col0: pl.BlockSpec((rows, width), lambda g, i: (i, col0 // width + g))
    vec = lambda r, width, col0: pl.BlockSpec((r, width), lambda g, i: (0, col0 // width + g))
    col = pl.BlockSpec((LANES, 1), lambda g, i: (0, 0))
    h_spec = pl.BlockSpec((bb, 1, S_GW, S_STATE), lambda g, i: (i, g, 0, 0))
    off_b = D_S
    off_c = D_S + S_GROUPS * S_STATE
    return pl.pallas_call(
        _ssd_sample_body,
        grid=(S_GROUPS, nblk),
        in_specs=[
            tok(S_GW, COL_XS), tok(S_STATE, COL_BM), tok(S_STATE, COL_CM), tok(S_GW, COL_ZS),
            pl.BlockSpec((LANES, rows), lambda g, i: (0, i)),
            col, col, vec(1, S_GW, 0),
            tok(S_GW, 0), tok(S_STATE, off_b), tok(S_STATE, off_c),
            h_spec,
            vec(CONV_K, S_GW, 0), vec(1, S_GW, 0),
            vec(CONV_K, S_STATE, off_b), vec(1, S_STATE, off_b),
            vec(CONV_K, S_STATE, off_c), vec(1, S_STATE, off_c),
            vec(1, S_GW, 0),
        ],
        out_specs=[pl.BlockSpec((rows, S_GW), lambda g, i: (i, g)), h_spec],
        out_shape=[
            jax.ShapeDtypeStruct((nseq_total * SEQ4, D_S), BF16),
            jax.ShapeDtypeStruct((nseq_total, S_GROUPS, S_GW, S_STATE), F32),
        ],
        scratch_shapes=[pltpu.VMEM((S_GW, rows), F32), pltpu.VMEM((S_GW, rows), BF16)],
        compiler_params=pltpu.CompilerParams(
            dimension_semantics=("parallel", "parallel"), vmem_limit_bytes=VMEM_LIMIT),
        name="ssd_sample",
    )(proj, proj, proj, proj, gates_t, gate_bias_col, a_col, dsk_row, buf_rows, buf_rows, buf_rows, h0,
      conv_w, conv_b, conv_w, conv_b, conv_w, conv_b, norm_w)


def _out_proj_body(x_ref, hm_ref, ys_ref, w_ref, out_ref):
    k = pl.program_id(2)

    @pl.when(k == 0)
    def _():
        out_ref[...] = x_ref[...] + _dot(hm_ref[...], w_ref[...])

    @pl.when(k == 1)
    def _():
        out_ref[...] += _dot(ys_ref[...], w_ref[...])


def _out_proj(x, hm, ys, w_out):
    m = x.shape[0]
    tm = min(512, m)
    tn = 1024
    return pl.pallas_call(
        _out_proj_body,
        grid=(m // tm, D_MODEL // tn, 2),
        in_specs=[
            pl.BlockSpec((tm, tn), lambda i, j, k: (i, j)),
            pl.BlockSpec((tm, D_M), lambda i, j, k: (i, 0)),
            pl.BlockSpec((tm, D_S), lambda i, j, k: (i, 0)),
            pl.BlockSpec((D_M, tn), lambda i, j, k: (k, j)),
        ],
        out_specs=pl.BlockSpec((tm, tn), lambda i, j, k: (i, j)),
        out_shape=jax.ShapeDtypeStruct((m, D_MODEL), F32),
        compiler_params=pltpu.CompilerParams(
            dimension_semantics=("parallel", "parallel", "arbitrary"), vmem_limit_bytes=VMEM_LIMIT),
        name="out_proj",
    )(x, hm, ys, w_out)


def _ple_body(x1_ref, x1t_ref, p_ref, gn_ref, wg_ref, wp_ref, fn_ref, out_ref, xn_ref, ss_ref):
    j = pl.program_id(1)
    nj = pl.num_programs(1)
    tn = wg_ref.shape[1]

    @pl.when(j == 0)
    def _():
        _rmsnorm_rows_to(x1_ref, gn_ref[...], xn_ref)
        ss_ref[...] = jnp.zeros_like(ss_ref)

    gate = _sigmoid(_dot(xn_ref[...], wg_ref[...]))
    x2 = x1t_ref[...] + gate * _dot(p_ref[...].astype(BF16), wp_ref[...])
    ss_ref[...] += jnp.sum(x2 * x2, axis=1, keepdims=True)
    for jj in range(out_ref.shape[1] // tn):
        @pl.when(j == jj)
        def _(jj=jj):
            out_ref[:, jj * tn:(jj + 1) * tn] = x2

    @pl.when(j == nj - 1)
    def _():
        fn = fn_ref[...]

        def body(r, carry):
            sl = pl.ds(pl.multiple_of(r * NORM_ROWS, NORM_ROWS), NORM_ROWS)
            scale = lax.rsqrt(ss_ref[sl, :] * (1.0 / D_MODEL) + EPS)
            out_ref[sl, :] = out_ref[sl, :] * scale * fn
            return carry
        lax.fori_loop(0, out_ref.shape[0] // NORM_ROWS, body, 0)


def _ple(x1, p, ple_norm, w_gate, w_proj, final_norm):
    m = x1.shape[0]
    tm = min(512, m)
    tn = 512
    return pl.pallas_call(
        _ple_body,
        grid=(m // tm, D_MODEL // tn),
        in_specs=[
            pl.BlockSpec((tm, D_MODEL), lambda i, j: (i, 0)),
            pl.BlockSpec((tm, tn), lambda i, j: (i, j)),
            pl.BlockSpec((tm, PLE_DIM), lambda i, j: (i, 0)),
            pl.BlockSpec((1, D_MODEL), lambda i, j: (0, 0)),
            pl.BlockSpec((D_MODEL, tn), lambda i, j: (0, j)),
            pl.BlockSpec((PLE_DIM, tn), lambda i, j: (0, j)),
            pl.BlockSpec((1, D_MODEL), lambda i, j: (0, 0)),
        ],
        out_specs=pl.BlockSpec((tm, D_MODEL), lambda i, j: (i, 0)),
        out_shape=jax.ShapeDtypeStruct((m, D_MODEL), F32),
        scratch_shapes=[pltpu.VMEM((tm, D_MODEL), BF16), pltpu.VMEM((tm, 1), F32)],
        compiler_params=pltpu.CompilerParams(
            dimension_semantics=("parallel", "arbitrary"), vmem_limit_bytes=VMEM_LIMIT),
        name="ple",
    )(x1, x1, p, ple_norm, w_gate, w_proj, final_norm)


def _pad_lanes(v, offset):
    out = jnp.zeros((1, LANES), F32)
    return lax.dynamic_update_slice(out, v.reshape(1, -1).astype(F32), (0, offset))


def kernel(x_prompt, x_sample, p_prompt, p_sample, state_m_C, state_m_n, state_m_m, state_m_conv, state_s_ssm, state_s_conv, norm_in, w_in, b_ig, b_fg, m_conv_w, m_conv_b, w_q, w_k, m_norm, s_conv_w, s_conv_b, dt_bias, a_log, d_skip, s_norm, w_out, ple_proj, ple_gate, ple_norm, final_norm):
    assert w_in.shape[0] == 1, "single layer"
    bsz, seqlen, _ = x_prompt.shape
    nseq, dec_seq, _ = x_sample.shape
    assert dec_seq == SEQ4

    assert w_in.shape[2] == N_MAIN + GATE_COLS + S_HEADS
    w_main, w_gate = _w_prep(jnp.swapaxes(w_in, 1, 2))
    gate_bias = jnp.concatenate(
        [b_ig[0], b_fg[0], dt_bias[0], jnp.zeros((LANES - 2 * M_HEADS - S_HEADS,), F32)]).reshape(1, LANES)
    gate_bias_col = gate_bias.reshape(LANES, 1)
    a_col = _pad_lanes(a_log[0], GL_DT).reshape(LANES, 1)
    dsk_row = jnp.repeat(d_skip[0], S_HEAD_DIM).reshape(1, D_S)
    wq = w_q[0].astype(BF16)
    wkt = (jnp.swapaxes(w_k[0], 1, 2) * (M_QK ** -0.5)).astype(BF16)
    w_out_b = w_out[0].astype(BF16)
    ple_gate_b = ple_gate[0].astype(BF16)
    ple_proj_b = ple_proj[0].astype(BF16)
    norm_in_r = norm_in[0].reshape(1, D_MODEL)
    ple_norm_r = ple_norm[0].reshape(1, D_MODEL)
    final_norm_r = final_norm.reshape(1, D_MODEL)
    m_conv_b_r = m_conv_b[0].reshape(1, D_M)
    s_conv_b_r = s_conv_b[0].reshape(1, CONV_DIM)
    m_norm_r = m_norm[0].reshape(1, D_M)
    s_norm_r = s_norm[0].reshape(1, D_S)

    xp = x_prompt.reshape(bsz * seqlen, D_MODEL)
    proj_p, _, gates_pt = _in_proj(xp, norm_in_r, w_main, w_gate)
    hm_p, pc, pn, pm = _mlstm_prompt(proj_p, gates_pt, bsz, seqlen, gate_bias_col, m_conv_w[0], m_conv_b_r, wq, wkt, m_norm_r)
    ys_p, ph = _ssd_prompt(proj_p, gates_pt, bsz, seqlen, gate_bias_col, a_col, dsk_row, s_conv_w[0], s_conv_b_r, s_norm_r)
    x1_p = _out_proj(xp, hm_p, ys_p, w_out_b)
    y_p = _ple(x1_p, p_prompt[0].reshape(bsz * seqlen, PLE_DIM), ple_norm_r, ple_gate_b, ple_proj_b, final_norm_r)

    proj_p3 = proj_p.reshape(bsz, seqlen, N_MAIN)
    tail = seqlen - (CONV_K - 1)
    p_mconv = proj_p3[:, tail:, COL_U:COL_U + D_M]
    p_sconv = proj_p3[:, tail:, COL_XS:COL_XS + CONV_DIM]

    xs_ = x_sample.reshape(nseq * SEQ4, D_MODEL)
    proj_s, gates_s, gates_st = _in_proj(xs_, norm_in_r, w_main, w_gate)
    pad_rows = lambda st: jnp.pad(st, ((0, 0), (0, SEQ4 - (CONV_K - 1)), (0, 0))).reshape(nseq * SEQ4, st.shape[-1])
    m_rows = jnp.pad(jnp.repeat(state_m_m[0], SEQ4, axis=0), ((0, 0), (0, LANES - M_HEADS)))
    hm_s, sc, sn, sm = _mlstm_sample(
        proj_s, gates_s, nseq, gate_bias, pad_rows(state_m_conv[0]), m_rows,
        state_m_C[0], state_m_n[0].reshape(nseq, M_HEADS, 1, M_QK), m_conv_w[0], m_conv_b_r, wq, wkt, m_norm_r)
    ys_s, sh = _ssd_sample(
        proj_s, gates_st, nseq, gate_bias_col, a_col, dsk_row, pad_rows(state_s_conv[0]),
        state_s_ssm[0].reshape(nseq, S_GROUPS, S_GW, S_STATE), s_conv_w[0], s_conv_b_r, s_norm_r)
    x1_s = _out_proj(xs_, hm_s, ys_s, w_out_b)
    y_s = _ple(x1_s, p_sample[0].reshape(nseq * SEQ4, PLE_DIM), ple_norm_r, ple_gate_b, ple_proj_b, final_norm_r)

    keep = SEQ4 - (CONV_K - 1)
    s_mconv = proj_s[:, COL_U:COL_U + D_M].reshape(nseq, SEQ4, D_M)[:, keep:]
    s_sconv = proj_s[:, COL_XS:COL_XS + CONV_DIM].reshape(nseq, SEQ4, CONV_DIM)[:, keep:]
    s_m = jnp.transpose(sm[:, SEQ4 - 1::SEQ4, 0])

    return (
        y_p.reshape(bsz, seqlen, D_MODEL),
        y_s.reshape(nseq, SEQ4, D_MODEL),
        pc[None], pn.reshape(1, bsz, M_HEADS, M_QK), pm[:, :, 0, 0][None], p_mconv[None],
        ph.reshape(1, bsz, S_HEADS, S_HEAD_DIM, S_STATE), p_sconv[None],
        sc[None], sn.reshape(1, nseq, M_HEADS, M_QK), s_m[None], s_mconv[None],
        sh.reshape(1, nseq, S_HEADS, S_HEAD_DIM, S_STATE), s_sconv[None],
    )
```

```python
import functools

import jax
import jax.numpy as jnp
from jax import lax
from jax.experimental import pallas as pl
from jax.experimental.pallas import tpu as pltpu

F32 = jnp.float32
BF16 = jnp.bfloat16
HIGHEST = lax.Precision.HIGHEST

D_MODEL = 4096
D_M = 4096
D_S = 4096
M_HEADS = 8
M_V = 512
M_QK = 256
S_HEADS = 64
S_HEAD_DIM = 64
S_STATE = 128
S_GROUPS = 8
S_HPG = 8
S_GW = S_HPG * S_HEAD_DIM
CONV_DIM = D_S + 2 * S_GROUPS * S_STATE
CONV_K = 4
PLE_DIM = 256
GATE_CAP = 15.0
EPS = 1e-6

COL_U, COL_V, COL_O, COL_ZM, COL_ZS, COL_XS = 0, 4096, 8192, 12288, 16384, 20480
COL_BM = COL_XS + D_S
COL_CM = COL_BM + S_GROUPS * S_STATE
N_MAIN = COL_XS + CONV_DIM
LANES = 128
GL_IG, GL_FG, GL_DT = 0, M_HEADS, 2 * M_HEADS

PROMPT_CHUNK = 256
SAMPLE_BB = 16
VMEM_LIMIT = 56 * 1024 * 1024

NT_DIMS = (((1,), (1,)), ((), ()))


def _dot(a, b):
    return jnp.dot(a, b, preferred_element_type=F32)


def _dot_nt(a, b):
    return lax.dot_general(a, b, NT_DIMS, preferred_element_type=F32)


def _dot_exact(a, b):
    return jnp.dot(a, b, preferred_element_type=F32, precision=HIGHEST)


def _sigmoid(x):
    return 0.5 * jnp.tanh(0.5 * x) + 0.5


def _silu(x):
    return x * _sigmoid(x)


def _softcap(x):
    return GATE_CAP * jnp.tanh(x / GATE_CAP)


def _softplus(x):
    return jnp.maximum(x, 0.0) + jnp.log1p(jnp.exp(-jnp.abs(x)))


def _log_sigmoid(x):
    return -_softplus(-x)


def _idiv(x, d):
    return lax.shift_right_logical(x, d.bit_length() - 1)


def _imod(x, d):
    return lax.bitwise_and(x, d - 1)


def _lane_pick(x, lane_idx):
    lane = lax.broadcasted_iota(jnp.int32, (1, x.shape[1]), 1)
    return jnp.sum(jnp.where(lane == lane_idx, x, 0.0), axis=1, keepdims=True)


def _col_to_row(col, eye):
    return jnp.sum(jnp.where(eye, col, 0.0), axis=0, keepdims=True)


def _row_to_col(row, eye):
    return jnp.sum(jnp.where(eye, row, 0.0), axis=1, keepdims=True)


NORM_ROWS = 32


def _rmsnorm_rows_to(x_ref, g, dst_ref):
    def body(r, carry):
        sl = pl.ds(pl.multiple_of(r * NORM_ROWS, NORM_ROWS), NORM_ROWS)
        x = x_ref[sl, :]
        ms = jnp.mean(x * x, axis=-1, keepdims=True)
        dst_ref[sl, :] = (x * lax.rsqrt(ms + EPS) * g).astype(BF16)
        return carry
    lax.fori_loop(0, x_ref.shape[0] // NORM_ROWS, body, 0)


def _in_proj_body(x_ref, g_ref, w_ref, wg_ref, out_ref, gates_ref, gatest_ref, xn_ref):
    @pl.when(pl.program_id(1) == 0)
    def _():
        _rmsnorm_rows_to(x_ref, g_ref[...], xn_ref)
        gates = _dot_nt(xn_ref[...], wg_ref[...])
        gates_ref[...] = gates
        gatest_ref[...] = gates.T
    out_ref[...] = _dot_nt(xn_ref[...], w_ref[...])


def _in_proj(x, norm_w, w_main, w_gate):
    m = x.shape[0]
    tm = min(512, m)
    tn = 1024
    grid = (m // tm, N_MAIN // tn)
    return pl.pallas_call(
        _in_proj_body,
        grid=grid,
        in_specs=[
            pl.BlockSpec((tm, D_MODEL), lambda i, j: (i, 0)),
            pl.BlockSpec((1, D_MODEL), lambda i, j: (0, 0)),
            pl.BlockSpec((tn, D_MODEL), lambda i, j: (j, 0)),
            pl.BlockSpec((LANES, D_MODEL), lambda i, j: (0, 0)),
        ],
        out_specs=[
            pl.BlockSpec((tm, tn), lambda i, j: (i, j)),
            pl.BlockSpec((tm, LANES), lambda i, j: (i, 0)),
            pl.BlockSpec((LANES, tm), lambda i, j: (0, i)),
        ],
        out_shape=[jax.ShapeDtypeStruct((m, N_MAIN), F32), jax.ShapeDtypeStruct((m, LANES), F32),
                   jax.ShapeDtypeStruct((LANES, m), F32)],
        scratch_shapes=[pltpu.VMEM((tm, D_MODEL), BF16)],
        compiler_params=pltpu.CompilerParams(
            dimension_semantics=("parallel", "arbitrary"), vmem_limit_bytes=VMEM_LIMIT),
        name="in_proj",
    )(x, norm_w, w_main, w_gate)


def _rmsnorm_cast_body(x_ref, g_ref, out_ref):
    _rmsnorm_rows_to(x_ref, g_ref[...], out_ref)


def _rmsnorm_cast(x, norm_w):
    m = x.shape[0]
    tm = min(256, m)
    return pl.pallas_call(
        _rmsnorm_cast_body,
        grid=(m // tm,),
        in_specs=[pl.BlockSpec((tm, D_MODEL), lambda i: (i, 0)), pl.BlockSpec((1, D_MODEL), lambda i: (0, 0))],
        out_specs=pl.BlockSpec((tm, D_MODEL), lambda i: (i, 0)),
        out_shape=jax.ShapeDtypeStruct((m, D_MODEL), BF16),
        compiler_params=pltpu.CompilerParams(dimension_semantics=("parallel",), vmem_limit_bytes=VMEM_LIMIT),
        name="rmsnorm_cast",
    )(x, norm_w)


def _in_proj_w32_body(xn_ref, w_ref, ga_ref, gb_ref, out_ref, gates_ref, gatest_ref, wg_ref):
    @pl.when(pl.program_id(1) == 0)
    def _():
        wg_ref[0:GL_DT, :] = ga_ref[0].astype(BF16)
        wg_ref[GL_DT:GL_DT + S_HEADS, :] = gb_ref[0].astype(BF16)
        wg_ref[GL_DT + S_HEADS:LANES, :] = jnp.zeros((LANES - GL_DT - S_HEADS, D_MODEL), BF16)
        gates = _dot_nt(xn_ref[...], wg_ref[...])
        gates_ref[...] = gates
        gatest_ref[...] = gates.T
    out_ref[...] = _dot_nt(xn_ref[...], w_ref[0].astype(BF16))


def _in_proj_w32(xn, w_in_t):
    m = xn.shape[0]
    tm = min(1024, m)
    tn = 512

    def src_row(j):
        return pl.multiple_of(j * tn + (j // (N_ALIGNED // tn)) * GATE_COLS, GATE_COLS)

    el = pl.Element
    return pl.pallas_call(
        _in_proj_w32_body,
        grid=(m // tm, N_MAIN // tn),
        in_specs=[
            pl.BlockSpec((tm, D_MODEL), lambda i, j: (i, 0)),
            pl.BlockSpec((el(1), el(tn), el(D_MODEL)), lambda i, j: (0, src_row(j), 0)),
            pl.BlockSpec((el(1), el(GATE_COLS), el(D_MODEL)), lambda i, j: (0, N_ALIGNED, 0)),
            pl.BlockSpec((el(1), el(S_HEADS), el(D_MODEL)), lambda i, j: (0, N_MAIN + GATE_COLS, 0)),
        ],
        out_specs=[
            pl.BlockSpec((tm, tn), lambda i, j: (i, j)),
            pl.BlockSpec((tm, LANES), lambda i, j: (i, 0)),
            pl.BlockSpec((LANES, tm), lambda i, j: (0, i)),
        ],
        out_shape=[jax.ShapeDtypeStruct((m, N_MAIN), F32), jax.ShapeDtypeStruct((m, LANES), F32),
                   jax.ShapeDtypeStruct((LANES, m), F32)],
        scratch_shapes=[pltpu.VMEM((LANES, D_MODEL), BF16)],
        compiler_params=pltpu.CompilerParams(
            dimension_semantics=("parallel", "arbitrary"), vmem_limit_bytes=VMEM_LIMIT),
        name="in_proj",
    )(xn, w_in_t, w_in_t, w_in_t)


GATE_COLS = 2 * M_HEADS
N_ALIGNED = 4 * D_M


def _w_prep_body(src_ref, ga_ref, gb_ref, out_ref, wg_ref):
    out_ref[...] = src_ref[0].astype(BF16)

    @pl.when(pl.program_id(0) == 0)
    def _():
        wg_ref[0:GL_DT, :] = ga_ref[0].astype(BF16)
        wg_ref[GL_DT:GL_DT + S_HEADS, :] = gb_ref[0].astype(BF16)
        wg_ref[GL_DT + S_HEADS:LANES, :] = jnp.zeros((LANES - GL_DT - S_HEADS, D_MODEL), BF16)


def _w_prep(w_in_t):
    rb = 512

    def src_row(i):
        return pl.multiple_of(i * rb + (i // (N_ALIGNED // rb)) * GATE_COLS, GATE_COLS)

    return pl.pallas_call(
        _w_prep_body,
        grid=(N_MAIN // rb,),
        in_specs=[
            pl.BlockSpec((pl.Element(1), pl.Element(rb), pl.Element(D_MODEL)), lambda i: (0, src_row(i), 0)),
            pl.BlockSpec((pl.Element(1), pl.Element(GATE_COLS), pl.Element(D_MODEL)), lambda i: (0, N_ALIGNED, 0)),
            pl.BlockSpec((pl.Element(1), pl.Element(S_HEADS), pl.Element(D_MODEL)),
                         lambda i: (0, N_MAIN + GATE_COLS, 0)),
        ],
        out_specs=[pl.BlockSpec((rb, D_MODEL), lambda i: (i, 0)), pl.BlockSpec((LANES, D_MODEL), lambda i: (0, 0))],
        out_shape=[jax.ShapeDtypeStruct((N_MAIN, D_MODEL), BF16), jax.ShapeDtypeStruct((LANES, D_MODEL), BF16)],
        compiler_params=pltpu.CompilerParams(dimension_semantics=("arbitrary",), vmem_limit_bytes=VMEM_LIMIT),
        name="w_prep",
    )(w_in_t, w_in_t, w_in_t)


TAIL = 8


def _conv_carry(u, ext_ref, cw, cb):
    c = u.shape[0]
    ext_ref[TAIL:TAIL + c, :] = u
    acc = cb + u * cw[CONV_K - 1:CONV_K, :]
    for j in range(1, CONV_K):
        acc = acc + ext_ref[TAIL - j:TAIL - j + c, :] * cw[CONV_K - 1 - j:CONV_K - j, :]
    ext_ref[0:TAIL, :] = u[c - TAIL:c, :]
    return acc


def _conv_rows(u, buf, cw, cb, t):
    rows = u.shape[0]
    acc = cb + u * cw[CONV_K - 1:CONV_K, :]
    for j in range(1, CONV_K):
        prev = buf if j == CONV_K - 1 else pltpu.roll(buf, rows + j - (CONV_K - 1), 0)
        acc = acc + jnp.where(t >= j, pltpu.roll(u, j, 0), prev) * cw[CONV_K - 1 - j:CONV_K - j, :]
    return acc


def _head_out(hv, o, z, nw):
    og = _sigmoid(o) * hv
    ms = jnp.mean(og * og, axis=-1, keepdims=True)
    return (og * lax.rsqrt(ms + EPS) * nw) * _silu(z)


def _group_out(y, xs, dsk, z, nw):
    gated = (y + dsk * xs) * _silu(z)
    ms = jnp.mean(gated * gated, axis=-1, keepdims=True)
    return gated * lax.rsqrt(ms + EPS) * nw


def _mlstm_prompt_body(u_ref, v_ref, o_ref, z_ref, gt_ref, gb_ref, cw_ref, cb_ref, wq_ref, wkt_ref, nw_ref,
                       hm_ref, c_ref, n_ref, m_ref, ext_ref, mst_ref):
    c = u_ref.shape[0]
    hps = wq_ref.shape[0]
    h0 = pl.program_id(1) * hps

    @pl.when(pl.program_id(2) == 0)
    def _():
        c_ref[...] = jnp.zeros_like(c_ref)
        n_ref[...] = jnp.zeros_like(n_ref)
        mst_ref[...] = jnp.full_like(mst_ref, -jnp.inf)
        ext_ref[0:TAIL, :] = jnp.zeros((TAIL, ext_ref.shape[1]), F32)

    ucb = _silu(_conv_carry(u_ref[...], ext_ref, cw_ref[...], cb_ref[...])).astype(BF16)
    vb = v_ref[...].astype(BF16)

    rowi = lax.broadcasted_iota(jnp.int32, (c, c), 0)
    coli = lax.broadcasted_iota(jnp.int32, (c, c), 1)
    tri = coli <= rowi
    eye = coli == rowi
    ig8 = _softcap(gt_ref[GL_IG:GL_IG + M_HEADS, :] + gb_ref[GL_IG:GL_IG + M_HEADS, :])
    lf8 = _log_sigmoid(_softcap(gt_ref[GL_FG:GL_FG + M_HEADS, :] + gb_ref[GL_FG:GL_FG + M_HEADS, :]))
    b8 = _dot_exact(lf8, jnp.where(rowi <= coli, 1.0, 0.0))
    head_row = lax.broadcasted_iota(jnp.int32, (M_HEADS, 1), 0)

    for hh in range(hps):
        cols = slice(hh * M_V, (hh + 1) * M_V)
        pick = head_row == h0 + hh
        ig_row = jnp.sum(jnp.where(pick, ig8, 0.0), axis=0, keepdims=True)
        b_row = jnp.sum(jnp.where(pick, b8, 0.0), axis=0, keepdims=True)
        b_col = _row_to_col(b_row, eye)
        m_prev = mst_ref[hh:hh + 1, :][:, 0:1]

        q = _dot(ucb[:, cols], wq_ref[hh])
        kt = _dot_nt(wkt_ref[hh], ucb[:, cols])
        qb = q.astype(BF16)
        ktb = kt.astype(BF16)

        r_row = ig_row - b_row
        dmat = jnp.where(tri, b_col + r_row, -jnp.inf)
        inter = b_col + m_prev
        m_t = jnp.maximum(inter, jnp.max(dmat, axis=1, keepdims=True))
        a_inter = jnp.exp(inter - m_t)
        s = _dot(qb, ktb) * jnp.exp(dmat - m_t)
        c_old = c_ref[0, hh]
        n_old = n_ref[0, hh]
        num = _dot(s.astype(BF16), vb[:, cols]) + a_inter * _dot(qb, c_old.astype(BF16))
        qn = _dot_nt(qb, jnp.broadcast_to(n_old, (8, M_QK)).astype(BF16))[:, 0:1]
        den = jnp.sum(s, axis=1, keepdims=True) + a_inter * qn
        hv = num / jnp.maximum(jnp.abs(den), jnp.exp(-m_t))
        hm_ref[:, cols] = _head_out(hv, o_ref[:, cols], z_ref[:, cols], nw_ref[:, cols]).astype(BF16)

        b_end = b_row[:, c - 1:c]
        lw_row = b_end + r_row
        m_new = jnp.maximum(b_end + m_prev, jnp.max(lw_row, axis=1, keepdims=True))
        decay = jnp.exp(b_end + m_prev - m_new)
        wk_row = jnp.exp(lw_row - m_new)
        c_ref[0, hh] = decay * c_old + _dot((kt * wk_row).astype(BF16), vb[:, cols])
        n_ref[0, hh] = decay * n_old + _dot_nt(jnp.broadcast_to(wk_row, (8, c)).astype(BF16), ktb)[0:1, :]
        mst_ref[hh:hh + 1, :] = jnp.broadcast_to(m_new, (1, LANES))
        m_ref[0, hh] = jnp.broadcast_to(m_new, (1, LANES))


MLSTM_HPS = 2


def _mlstm_prompt(proj, gates_t, bsz, seqlen, gate_bias_col, conv_w, conv_b, wq, wkt, norm_w):
    c = min(PROMPT_CHUNK, seqlen)
    nch = seqlen // c
    hps = MLSTM_HPS
    width = hps * M_V

    def tok(col0):
        return pl.BlockSpec((c, width), lambda b, h, k: (b * nch + k, col0 // width + h))

    head_vec = lambda rows: pl.BlockSpec((rows, width), lambda b, h, k: (0, h))
    return pl.pallas_call(
        _mlstm_prompt_body,
        grid=(bsz, M_HEADS // hps, nch),
        in_specs=[
            tok(COL_U), tok(COL_V), tok(COL_O), tok(COL_ZM),
            pl.BlockSpec((LANES, c), lambda b, h, k: (0, b * nch + k)),
            pl.BlockSpec((LANES, 1), lambda b, h, k: (0, 0)),
            head_vec(CONV_K), head_vec(1),
            pl.BlockSpec((hps, M_V, M_QK), lambda b, h, k: (h, 0, 0)),
            pl.BlockSpec((hps, M_QK, M_V), lambda b, h, k: (h, 0, 0)),
            head_vec(1),
        ],
        out_specs=[
            pl.BlockSpec((c, width), lambda b, h, k: (b * nch + k, h)),
            pl.BlockSpec((1, hps, M_QK, M_V), lambda b, h, k: (b, h, 0, 0)),
            pl.BlockSpec((1, hps, 1, M_QK), lambda b, h, k: (b, h, 0, 0)),
            pl.BlockSpec((1, hps, 1, LANES), lambda b, h, k: (b, h, 0, 0)),
        ],
        out_shape=[
            jax.ShapeDtypeStruct((bsz * seqlen, D_M), BF16),
            jax.ShapeDtypeStruct((bsz, M_HEADS, M_QK, M_V), F32),
            jax.ShapeDtypeStruct((bsz, M_HEADS, 1, M_QK), F32),
            jax.ShapeDtypeStruct((bsz, M_HEADS, 1, LANES), F32),
        ],
        scratch_shapes=[pltpu.VMEM((TAIL + c, width), F32), pltpu.VMEM((8, LANES), F32)],
        compiler_params=pltpu.CompilerParams(
            dimension_semantics=("parallel", "parallel", "arbitrary"), vmem_limit_bytes=VMEM_LIMIT),
        name="mlstm_prompt",
    )(proj, proj, proj, proj, gates_t, gate_bias_col, conv_w, conv_b, wq, wkt, norm_w)


LOG2E = 1.4426950408889634
HEAD_ROWS = [slice(r * S_HEAD_DIM, (r + 1) * S_HEAD_DIM) for r in range(S_HPG)]


def _ssd_gate_rows(gt_ref, gbc_ref, avc_ref, lane0):
    rows = pl.ds(lane0, S_HPG)
    dt8 = _softplus(gt_ref[rows, :] + gbc_ref[rows, :])
    return dt8, dt8 * (-jnp.exp(avc_ref[rows, :]))


def _ssd_intra(xst, dt8, cum8, cbt):
    c = xst.shape[1]
    a2r = cum8 * LOG2E
    a2c = jnp.concatenate([a2r, jnp.zeros((LANES - S_HPG, c), F32)], axis=0).T
    ys, xdts = [], []
    for r in range(S_HPG):
        xdt = xst[HEAD_ROWS[r], :] * dt8[r:r + 1, :]
        decay = jnp.exp2(jnp.minimum(a2r[r:r + 1, :] - a2c[:, r:r + 1], 0.0))
        ys.append(_dot(xdt.astype(BF16), (cbt * decay).astype(BF16)))
        xdts.append(xdt)
    return ys, xdts


def _ssd_prompt_body(xs_ref, bm_ref, cm_ref, zs_ref, gt_ref, gbc_ref, avc_ref, dsk_ref,
                     cwx_ref, cbx_ref, cwb_ref, cbb_ref, cwc_ref, cbc_ref, nw_ref,
                     ys_ref, h_ref, extx_ref, extb_ref, extc_ref):
    c = xs_ref.shape[0]
    gps = h_ref.shape[1]
    g0 = pl.program_id(1) * gps

    @pl.when(pl.program_id(2) == 0)
    def _():
        h_ref[...] = jnp.zeros_like(h_ref)
        extx_ref[0:TAIL, :] = jnp.zeros((TAIL, extx_ref.shape[1]), F32)
        extb_ref[0:TAIL, :] = jnp.zeros((TAIL, extb_ref.shape[1]), F32)
        extc_ref[0:TAIL, :] = jnp.zeros((TAIL, extc_ref.shape[1]), F32)

    xs_all = _silu(_conv_carry(xs_ref[...], extx_ref, cwx_ref[...], cbx_ref[...]))
    bmb_all = _silu(_conv_carry(bm_ref[...], extb_ref, cwb_ref[...], cbb_ref[...])).astype(BF16)
    cmb_all = _silu(_conv_carry(cm_ref[...], extc_ref, cwc_ref[...], cbc_ref[...])).astype(BF16)

    src = lax.broadcasted_iota(jnp.int32, (c, c), 0)
    tgt = lax.broadcasted_iota(jnp.int32, (c, c), 1)
    causal = src <= tgt
    upper = jnp.where(causal, 1.0, 0.0)

    for gg in range(gps):
        cols = slice(gg * S_GW, (gg + 1) * S_GW)
        xs = xs_all[:, cols]
        bmb = bmb_all[:, gg * S_STATE:(gg + 1) * S_STATE]
        cmb = cmb_all[:, gg * S_STATE:(gg + 1) * S_STATE]
        lane0 = pl.multiple_of(GL_DT + (g0 + gg) * S_HPG, S_HPG)
        dt8, la8 = _ssd_gate_rows(gt_ref, gbc_ref, avc_ref, lane0)
        cum8 = _dot_exact(la8, upper)
        a_end = cum8[:, c - 1:c]
        w8 = jnp.exp(a_end - cum8)
        ea8 = jnp.exp(cum8)

        cbt = jnp.where(causal, _dot_nt(bmb, cmb), 0.0)
        ys, xdts = _ssd_intra(xs.T, dt8, cum8, cbt)
        h_old = h_ref[0, gg]
        yit = _dot_nt(h_old.astype(BF16), cmb)
        yt = jnp.concatenate([ys[r] + ea8[r:r + 1, :] * yit[HEAD_ROWS[r], :] for r in range(S_HPG)], axis=0)
        ys_ref[:, cols] = _group_out(yt.T, xs, dsk_ref[:, cols], zs_ref[:, cols], nw_ref[:, cols]).astype(BF16)

        xwt = jnp.concatenate([(xdts[r] * w8[r:r + 1, :]).astype(BF16) for r in range(S_HPG)], axis=0)
        upd = _dot(xwt, bmb)
        ea_end = jnp.exp(a_end)
        for r in range(S_HPG):
            h_ref[0, gg, HEAD_ROWS[r], :] = ea_end[r:r + 1, :] * h_old[HEAD_ROWS[r], :] + upd[HEAD_ROWS[r], :]


SSD_GPS = 2


def _ssd_prompt(proj, gates_t, bsz, seqlen, gate_bias_col, a_col, dsk_row, conv_w, conv_b, norm_w):
    c = min(PROMPT_CHUNK, seqlen)
    nch = seqlen // c
    gps = SSD_GPS
    tok = lambda width, col0: pl.BlockSpec((c, gps * width), lambda b, g, k: (b * nch + k, col0 // (gps * width) + g))
    vec = lambda rows, width, col0: pl.BlockSpec((rows, gps * width), lambda b, g, k: (0, col0 // (gps * width) + g))
    col = pl.BlockSpec((LANES, 1), lambda b, g, k: (0, 0))
    return pl.pallas_call(
        _ssd_prompt_body,
        grid=(bsz, S_GROUPS // gps, nch),
        in_specs=[
            tok(S_GW, COL_XS), tok(S_STATE, COL_BM), tok(S_STATE, COL_CM), tok(S_GW, COL_ZS),
            pl.BlockSpec((LANES, c), lambda b, g, k: (0, b * nch + k)),
            col, col, vec(1, S_GW, 0),
            vec(CONV_K, S_GW, 0), vec(1, S_GW, 0),
            vec(CONV_K, S_STATE, D_S), vec(1, S_STATE, D_S),
            vec(CONV_K, S_STATE, D_S + S_GROUPS * S_STATE), vec(1, S_STATE, D_S + S_GROUPS * S_STATE),
            vec(1, S_GW, 0),
        ],
        out_specs=[
            pl.BlockSpec((c, gps * S_GW), lambda b, g, k: (b * nch + k, g)),
            pl.BlockSpec((1, gps, S_GW, S_STATE), lambda b, g, k: (b, g, 0, 0)),
        ],
        out_shape=[
            jax.ShapeDtypeStruct((bsz * seqlen, D_S), BF16),
            jax.ShapeDtypeStruct((bsz, S_GROUPS, S_GW, S_STATE), F32),
        ],
        scratch_shapes=[pltpu.VMEM((TAIL + c, gps * S_GW), F32), pltpu.VMEM((TAIL + c, gps * S_STATE), F32),
                        pltpu.VMEM((TAIL + c, gps * S_STATE), F32)],
        compiler_params=pltpu.CompilerParams(
            dimension_semantics=("parallel", "parallel", "arbitrary"), vmem_limit_bytes=VMEM_LIMIT),
        name="ssd_prompt",
    )(proj, proj, proj, proj, gates_t, gate_bias_col, a_col, dsk_row,
      conv_w, conv_b, conv_w, conv_b, conv_w, conv_b, norm_w)


SEQ4 = 4


def _row_masks(rows):
    rowi = lax.broadcasted_iota(jnp.int32, (rows, rows), 0)
    coli = lax.broadcasted_iota(jnp.int32, (rows, rows), 1)
    same = _idiv(rowi, SEQ4) == _idiv(coli, SEQ4)
    return same, same & (coli <= rowi), coli == rowi, same & (_imod(coli, SEQ4) == SEQ4 - 1)


def _seg_cumsum(x, t):
    out = x
    for j in range(1, SEQ4):
        out = out + jnp.where(t >= j, pltpu.roll(x, j, 0), 0.0)
    return out


def _mlstm_sample_body(u_ref, v_ref, o_ref, z_ref, gt_ref, gb_ref, buf_ref, mrow_ref, c0_ref, n0_ref,
                       cw_ref, cb_ref, wq_ref, wkt_ref, nw_ref,
                       hm_ref, c_ref, n_ref, m_ref):
    h = pl.program_id(0)
    rows = u_ref.shape[0]
    nseq = rows // SEQ4
    rid = lax.broadcasted_iota(jnp.int32, (rows, 1), 0)
    t = _imod(rid, SEQ4)
    seq = _idiv(rid, SEQ4)
    seq_lane = _idiv(lax.broadcasted_iota(jnp.int32, (1, rows), 1), SEQ4)

    ucb = _silu(_conv_rows(u_ref[...], buf_ref[...], cw_ref[...], cb_ref[...], t)).astype(BF16)
    q = _dot(ucb, wq_ref[0])
    kt = _dot_nt(wkt_ref[0], ucb)
    qb = q.astype(BF16)
    vb = v_ref[...].astype(BF16)

    capped = _softcap(gt_ref[...] + gb_ref[...])
    ig_col = _lane_pick(capped, GL_IG + h)
    b_col = _lane_pick(_seg_cumsum(_log_sigmoid(capped), t), GL_FG + h)
    m_prev = _lane_pick(mrow_ref[...], h)

    same, tri, eye, last = _row_masks(rows)
    b_row = _col_to_row(b_col, eye)
    ig_row = _col_to_row(ig_col, eye)
    dmat = jnp.where(tri, b_col - b_row + ig_row, -jnp.inf)
    inter = b_col + m_prev
    m_t = jnp.maximum(inter, jnp.max(dmat, axis=1, keepdims=True))
    a_inter = jnp.exp(inter - m_t)
    s = _dot(qb, kt.astype(BF16)) * jnp.exp(dmat - m_t)

    b_end = jnp.sum(jnp.where(last, b_row, 0.0), axis=1, keepdims=True)
    lw = b_end - b_col + ig_col
    m_new = jnp.maximum(b_end + m_prev, jnp.max(jnp.where(same, _col_to_row(lw, eye), -jnp.inf), axis=1, keepdims=True))
    decay = jnp.exp(b_end + m_prev - m_new)
    kwt = kt * _col_to_row(jnp.exp(lw - m_new), eye)
    eye_k = (lax.broadcasted_iota(jnp.int32, (M_QK, M_QK), 0) == lax.broadcasted_iota(jnp.int32, (M_QK, M_QK), 1))

    num_inter = jnp.zeros((rows, M_V), F32)
    qn = jnp.zeros((rows, 1), F32)
    for bi in range(nseq):
        own = seq == bi
        c_old = c0_ref[bi, 0]
        n_old = n0_ref[bi, 0]
        num_inter = jnp.where(own, _dot(qb, c_old.astype(BF16)), num_inter)
        qn = jnp.where(own, jnp.sum(q * n_old, axis=1, keepdims=True), qn)
        kw_b = jnp.where(seq_lane == bi, kwt, 0.0)
        d_b = jnp.sum(jnp.where(own & (t == SEQ4 - 1), decay, 0.0), axis=0, keepdims=True)
        c_ref[bi, 0] = d_b * c_old + _dot(kw_b.astype(BF16), vb)
        n_ref[bi, 0] = d_b * n_old + _col_to_row(jnp.sum(kw_b, axis=1, keepdims=True), eye_k)

    num = _dot(s.astype(BF16), vb) + a_inter * num_inter
    den = jnp.sum(s, axis=1, keepdims=True) + a_inter * qn
    hv = num / jnp.maximum(jnp.abs(den), jnp.exp(-m_t))
    hm_ref[...] = _head_out(hv, o_ref[...], z_ref[...], nw_ref[...]).astype(BF16)
    m_ref[0] = jnp.broadcast_to(m_new, (rows, LANES))


def _mlstm_sample(proj, gates, nseq_total, gate_bias, buf_rows, m_rows, c0, n0, conv_w, conv_b, wq, wkt, norm_w):
    bb = min(SAMPLE_BB, nseq_total)
    rows = bb * SEQ4
    nblk = nseq_total // bb
    tok = lambda col0: pl.BlockSpec((rows, M_V), lambda h, i: (i, col0 // M_V + h))
    head_vec = lambda r: pl.BlockSpec((r, M_V), lambda h, i: (0, h))
    c_spec = pl.BlockSpec((bb, 1, M_QK, M_V), lambda h, i: (i, h, 0, 0))
    n_spec = pl.BlockSpec((bb, 1, 1, M_QK), lambda h, i: (i, h, 0, 0))
    return pl.pallas_call(
        _mlstm_sample_body,
        grid=(M_HEADS, nblk),
        in_specs=[
            tok(COL_U), tok(COL_V), tok(COL_O), tok(COL_ZM),
            pl.BlockSpec((rows, LANES), lambda h, i: (i, 0)),
            pl.BlockSpec((1, LANES), lambda h, i: (0, 0)),
            pl.BlockSpec((rows, M_V), lambda h, i: (i, h)),
            pl.BlockSpec((rows, LANES), lambda h, i: (i, 0)),
            c_spec, n_spec,
            head_vec(CONV_K), head_vec(1),
            pl.BlockSpec((1, M_V, M_QK), lambda h, i: (h, 0, 0)),
            pl.BlockSpec((1, M_QK, M_V), lambda h, i: (h, 0, 0)),
            head_vec(1),
        ],
        out_specs=[
            pl.BlockSpec((rows, M_V), lambda h, i: (i, h)),
            c_spec, n_spec,
            pl.BlockSpec((1, rows, LANES), lambda h, i: (h, i, 0)),
        ],
        out_shape=[
            jax.ShapeDtypeStruct((nseq_total * SEQ4, D_M), BF16),
            jax.ShapeDtypeStruct((nseq_total, M_HEADS, M_QK, M_V), F32),
            jax.ShapeDtypeStruct((nseq_total, M_HEADS, 1, M_QK), F32),
            jax.ShapeDtypeStruct((M_HEADS, nseq_total * SEQ4, LANES), F32),
        ],
        compiler_params=pltpu.CompilerParams(
            dimension_semantics=("parallel", "parallel"), vmem_limit_bytes=VMEM_LIMIT),
        name="mlstm_sample",
    )(proj, proj, proj, proj, gates, gate_bias, buf_rows, m_rows, c0, n0, conv_w, conv_b, wq, wkt, norm_w)


def _ssd_sample_body(xs_ref, bm_ref, cm_ref, zs_ref, gt_ref, gbc_ref, avc_ref, dsk_ref,
                     bufx_ref, bufb_ref, bufc_ref, h0_ref,
                     cwx_ref, cbx_ref, cwb_ref, cbb_ref, cwc_ref, cbc_ref, nw_ref,
                     ys_ref, h_ref, yit_ref, xwt_ref):
    g = pl.program_id(0)
    rows = xs_ref.shape[0]
    nseq = rows // SEQ4
    rid = lax.broadcasted_iota(jnp.int32, (rows, 1), 0)
    t = _imod(rid, SEQ4)
    seq = _idiv(rid, SEQ4)
    tok_lane = lax.broadcasted_iota(jnp.int32, (1, rows), 1)

    xs = _silu(_conv_rows(xs_ref[...], bufx_ref[...], cwx_ref[...], cbx_ref[...], t))
    bmb = _silu(_conv_rows(bm_ref[...], bufb_ref[...], cwb_ref[...], cbb_ref[...], t)).astype(BF16)
    cmb = _silu(_conv_rows(cm_ref[...], bufc_ref[...], cwc_ref[...], cbc_ref[...], t)).astype(BF16)

    lane0 = pl.multiple_of(GL_DT + g * S_HPG, S_HPG)
    dt8, la8 = _ssd_gate_rows(gt_ref, gbc_ref, avc_ref, lane0)
    src = lax.broadcasted_iota(jnp.int32, (rows, rows), 0)
    tgt = lax.broadcasted_iota(jnp.int32, (rows, rows), 1)
    same = _idiv(src, SEQ4) == _idiv(tgt, SEQ4)
    causal = same & (src <= tgt)
    cum8 = _dot_exact(la8, jnp.where(causal, 1.0, 0.0))
    a_end = _dot_exact(cum8, jnp.where(same & (_imod(src, SEQ4) == SEQ4 - 1), 1.0, 0.0))
    w8 = jnp.exp(a_end - cum8)
    ea8 = jnp.exp(cum8)
    ea_end = jnp.exp(a_end)

    cbt = jnp.where(causal, _dot_nt(bmb, cmb), 0.0)
    ys, xdts = _ssd_intra(xs.T, dt8, cum8, cbt)
    xwt_ref[...] = jnp.concatenate([(xdts[r] * w8[r:r + 1, :]).astype(BF16) for r in range(S_HPG)], axis=0)
    yit_ref[...] = jnp.zeros_like(yit_ref)

    def per_sequence(bi, carry):
        h_old = h0_ref[bi, 0]
        own = _idiv(tok_lane, SEQ4) == bi
        yit_ref[...] += _dot_nt(h_old.astype(BF16), jnp.where(seq == bi, cmb, jnp.zeros_like(cmb)))
        upd = _dot(jnp.where(own, xwt_ref[...], jnp.zeros_like(xwt_ref)), bmb)
        d8 = jnp.sum(jnp.where(tok_lane == bi * SEQ4 + SEQ4 - 1, ea_end, 0.0), axis=1, keepdims=True)
        for r in range(S_HPG):
            h_ref[bi, 0, HEAD_ROWS[r], :] = d8[r:r + 1, :] * h_old[HEAD_ROWS[r], :] + upd[HEAD_ROWS[r], :]
        return carry

    lax.fori_loop(0, nseq, per_sequence, 0, unroll=4 if nseq % 4 == 0 else 1)
    yit = yit_ref[...]
    yt = jnp.concatenate([ys[r] + ea8[r:r + 1, :] * yit[HEAD_ROWS[r], :] for r in range(S_HPG)], axis=0)
    ys_ref[...] = _group_out(yt.T, xs, dsk_ref[...], zs_ref[...], nw_ref[...]).astype(BF16)


SSD_SAMPLE_BB = 32


def _ssd_sample(proj, gates_t, nseq_total, gate_bias_col, a_col, dsk_row, buf_rows, h0, conv_w, conv_b, norm_w):
    bb = min(SSD_SAMPLE_BB, nseq_total)
    rows = bb * SEQ4
    nblk = nseq_total // bb
    tok = lambda width, col0: pl.BlockSpec((rows, width), lambda g, i: (i, col0 // width + g))
    vec = lambda r, width, col0: pl.BlockSpec((r, width), lambda g, i: (0, col0 // width + g))
    col = pl.BlockSpec((LANES, 1), lambda g, i: (0, 0))
    h_spec = pl.BlockSpec((bb, 1, S_GW, S_STATE), lambda g, i: (i, g, 0, 0))
    off_b = D_S
    off_c = D_S + S_GROUPS * S_STATE
    return pl.pallas_call(
        _ssd_sample_body,
        grid=(S_GROUPS, nblk),
        in_specs=[
            tok(S_GW, COL_XS), tok(S_STATE, COL_BM), tok(S_STATE, COL_CM), tok(S_GW, COL_ZS),
            pl.BlockSpec((LANES, rows), lambda g, i: (0, i)),
            col, col, vec(1, S_GW, 0),
            tok(S_GW, 0), tok(S_STATE, off_b), tok(S_STATE, off_c),
            h_spec,
            vec(CONV_K, S_GW, 0), vec(1, S_GW, 0),
            vec(CONV_K, S_STATE, off_b), vec(1, S_STATE, off_b),
            vec(CONV_K, S_STATE, off_c), vec(1, S_STATE, off_c),
            vec(1, S_GW, 0),
        ],
        out_specs=[pl.BlockSpec((rows, S_GW), lambda g, i: (i, g)), h_spec],
        out_shape=[
            jax.ShapeDtypeStruct((nseq_total * SEQ4, D_S), BF16),
            jax.ShapeDtypeStruct((nseq_total, S_GROUPS, S_GW, S_STATE), F32),
        ],
        scratch_shapes=[pltpu.VMEM((S_GW, rows), F32), pltpu.VMEM((S_GW, rows), BF16)],
        compiler_params=pltpu.CompilerParams(
            dimension_semantics=("parallel", "parallel"), vmem_limit_bytes=VMEM_LIMIT),
        name="ssd_sample",
    )(proj, proj, proj, proj, gates_t, gate_bias_col, a_col, dsk_row, buf_rows, buf_rows, buf_rows, h0,
      conv_w, conv_b, conv_w, conv_b, conv_w, conv_b, norm_w)


def _out_proj_body(x_ref, hm_ref, ys_ref, w_ref, gn_ref, out_ref, x1g_ref, ss_ref):
    j = pl.program_id(1)
    k = pl.program_id(2)

    @pl.when(k == 0)
    def _():
        out_ref[...] = x_ref[...] + _dot(hm_ref[...], w_ref[...])

    @pl.when(k == 1)
    def _():
        x1 = out_ref[...] + _dot(ys_ref[...], w_ref[...])
        out_ref[...] = x1
        x1g_ref[...] = (x1 * gn_ref[...]).astype(BF16)
        part = jnp.sum(x1 * x1, axis=1, keepdims=True)

        @pl.when(j == 0)
        def _():
            ss_ref[...] = part

        @pl.when(j != 0)
        def _():
            ss_ref[...] += part


def _out_proj(x, hm, ys, w_out, ple_norm):
    m = x.shape[0]
    tm = min(512, m)
    tn = 1024
    return pl.pallas_call(
        _out_proj_body,
        grid=(m // tm, D_MODEL // tn, 2),
        in_specs=[
            pl.BlockSpec((tm, tn), lambda i, j, k: (i, j)),
            pl.BlockSpec((tm, D_M), lambda i, j, k: (i, 0)),
            pl.BlockSpec((tm, D_S), lambda i, j, k: (i, 0)),
            pl.BlockSpec((D_M, tn), lambda i, j, k: (k, j)),
            pl.BlockSpec((1, tn), lambda i, j, k: (0, j)),
        ],
        out_specs=[
            pl.BlockSpec((tm, tn), lambda i, j, k: (i, j)),
            pl.BlockSpec((tm, tn), lambda i, j, k: (i, j)),
            pl.BlockSpec((tm, 1), lambda i, j, k: (i, 0)),
        ],
        out_shape=[jax.ShapeDtypeStruct((m, D_MODEL), F32), jax.ShapeDtypeStruct((m, D_MODEL), BF16),
                   jax.ShapeDtypeStruct((m, 1), F32)],
        compiler_params=pltpu.CompilerParams(
            dimension_semantics=("parallel", "arbitrary", "arbitrary"), vmem_limit_bytes=VMEM_LIMIT),
        name="out_proj",
    )(x, hm, ys, w_out, ple_norm)


def _ple_body(x1g_ref, ss1_ref, x1t_ref, p_ref, wg_ref, wp_ref, fn_ref, out_ref, ss_ref):
    j = pl.program_id(1)
    nj = pl.num_programs(1)
    tn = wg_ref.shape[1]

    @pl.when(j == 0)
    def _():
        ss_ref[...] = jnp.zeros_like(ss_ref)

    rs = lax.rsqrt(ss1_ref[...] * (1.0 / D_MODEL) + EPS)
    gate = _sigmoid(rs * _dot(x1g_ref[...], wg_ref[...]))
    x2 = x1t_ref[...] + gate * _dot(p_ref[...].astype(BF16), wp_ref[...])
    ss_ref[...] += jnp.sum(x2 * x2, axis=1, keepdims=True)
    for jj in range(out_ref.shape[1] // tn):
        @pl.when(j == jj)
        def _(jj=jj):
            out_ref[:, jj * tn:(jj + 1) * tn] = x2

    @pl.when(j == nj - 1)
    def _():
        fn = fn_ref[...]

        def body(r, carry):
            sl = pl.ds(pl.multiple_of(r * NORM_ROWS, NORM_ROWS), NORM_ROWS)
            scale = lax.rsqrt(ss_ref[sl, :] * (1.0 / D_MODEL) + EPS)
            out_ref[sl, :] = out_ref[sl, :] * scale * fn
            return carry
        lax.fori_loop(0, out_ref.shape[0] // NORM_ROWS, body, 0)


def _ple(x1, x1g, ss1, p, w_gate, w_proj, final_norm):
    m = x1.shape[0]
    tm = min(512, m)
    tn = 512
    return pl.pallas_call(
        _ple_body,
        grid=(m // tm, D_MODEL // tn),
        in_specs=[
            pl.BlockSpec((tm, D_MODEL), lambda i, j: (i, 0)),
            pl.BlockSpec((tm, 1), lambda i, j: (i, 0)),
            pl.BlockSpec((tm, tn), lambda i, j: (i, j)),
            pl.BlockSpec((tm, PLE_DIM), lambda i, j: (i, 0)),
            pl.BlockSpec((D_MODEL, tn), lambda i, j: (0, j)),
            pl.BlockSpec((PLE_DIM, tn), lambda i, j: (0, j)),
            pl.BlockSpec((1, D_MODEL), lambda i, j: (0, 0)),
        ],
        out_specs=pl.BlockSpec((tm, D_MODEL), lambda i, j: (i, 0)),
        out_shape=jax.ShapeDtypeStruct((m, D_MODEL), F32),
        scratch_shapes=[pltpu.VMEM((tm, 1), F32)],
        compiler_params=pltpu.CompilerParams(
            dimension_semantics=("parallel", "arbitrary"), vmem_limit_bytes=VMEM_LIMIT),
        name="ple",
    )(x1g, ss1, x1, p, w_gate, w_proj, final_norm)


def _pad_lanes(v, offset):
    out = jnp.zeros((1, LANES), F32)
    return lax.dynamic_update_slice(out, v.reshape(1, -1).astype(F32), (0, offset))


def kernel(x_prompt, x_sample, p_prompt, p_sample, state_m_C, state_m_n, state_m_m, state_m_conv, state_s_ssm, state_s_conv, norm_in, w_in, b_ig, b_fg, m_conv_w, m_conv_b, w_q, w_k, m_norm, s_conv_w, s_conv_b, dt_bias, a_log, d_skip, s_norm, w_out, ple_proj, ple_gate, ple_norm, final_norm):
    assert w_in.shape[0] == 1, "single layer"
    bsz, seqlen, _ = x_prompt.shape
    nseq, dec_seq, _ = x_sample.shape
    assert dec_seq == SEQ4

    assert w_in.shape[2] == N_MAIN + GATE_COLS + S_HEADS
    w_in_t = jnp.swapaxes(w_in, 1, 2)
    gate_bias = jnp.concatenate(
        [b_ig[0], b_fg[0], dt_bias[0], jnp.zeros((LANES - 2 * M_HEADS - S_HEADS,), F32)]).reshape(1, LANES)
    gate_bias_col = gate_bias.reshape(LANES, 1)
    a_col = _pad_lanes(a_log[0], GL_DT).reshape(LANES, 1)
    dsk_row = jnp.repeat(d_skip[0], S_HEAD_DIM).reshape(1, D_S)
    wq = w_q[0].astype(BF16)
    wkt = (jnp.swapaxes(w_k[0], 1, 2) * (M_QK ** -0.5)).astype(BF16)
    w_out_b = w_out[0].astype(BF16)
    ple_gate_b = ple_gate[0].astype(BF16)
    ple_proj_b = ple_proj[0].astype(BF16)
    norm_in_r = norm_in[0].reshape(1, D_MODEL)
    ple_norm_r = ple_norm[0].reshape(1, D_MODEL)
    final_norm_r = final_norm.reshape(1, D_MODEL)
    m_conv_b_r = m_conv_b[0].reshape(1, D_M)
    s_conv_b_r = s_conv_b[0].reshape(1, CONV_DIM)
    m_norm_r = m_norm[0].reshape(1, D_M)
    s_norm_r = s_norm[0].reshape(1, D_S)

    xp = x_prompt.reshape(bsz * seqlen, D_MODEL)
    proj_p, _, gates_pt = _in_proj_w32(_rmsnorm_cast(xp, norm_in_r), w_in_t)
    hm_p, pc, pn, pm = _mlstm_prompt(proj_p, gates_pt, bsz, seqlen, gate_bias_col, m_conv_w[0], m_conv_b_r, wq, wkt, m_norm_r)
    ys_p, ph = _ssd_prompt(proj_p, gates_pt, bsz, seqlen, gate_bias_col, a_col, dsk_row, s_conv_w[0], s_conv_b_r, s_norm_r)
    x1_p, x1g_p, ss1_p = _out_proj(xp, hm_p, ys_p, w_out_b, ple_norm_r)
    y_p = _ple(x1_p, x1g_p, ss1_p, p_prompt[0].reshape(bsz * seqlen, PLE_DIM), ple_gate_b, ple_proj_b, final_norm_r)

    proj_p3 = proj_p.reshape(bsz, seqlen, N_MAIN)
    tail = seqlen - (CONV_K - 1)
    p_mconv = proj_p3[:, tail:, COL_U:COL_U + D_M]
    p_sconv = proj_p3[:, tail:, COL_XS:COL_XS + CONV_DIM]

    xs_ = x_sample.reshape(nseq * SEQ4, D_MODEL)
    proj_s, gates_s, gates_st = _in_proj_w32(_rmsnorm_cast(xs_, norm_in_r), w_in_t)
    pad_rows = lambda st: jnp.pad(st, ((0, 0), (0, SEQ4 - (CONV_K - 1)), (0, 0))).reshape(nseq * SEQ4, st.shape[-1])
    m_rows = jnp.pad(jnp.repeat(state_m_m[0], SEQ4, axis=0), ((0, 0), (0, LANES - M_HEADS)))
    hm_s, sc, sn, sm = _mlstm_sample(
        proj_s, gates_s, nseq, gate_bias, pad_rows(state_m_conv[0]), m_rows,
        state_m_C[0], state_m_n[0].reshape(nseq, M_HEADS, 1, M_QK), m_conv_w[0], m_conv_b_r, wq, wkt, m_norm_r)
    ys_s, sh = _ssd_sample(
        proj_s, gates_st, nseq, gate_bias_col, a_col, dsk_row, pad_rows(state_s_conv[0]),
        state_s_ssm[0].reshape(nseq, S_GROUPS, S_GW, S_STATE), s_conv_w[0], s_conv_b_r, s_norm_r)
    x1_s, x1g_s, ss1_s = _out_proj(xs_, hm_s, ys_s, w_out_b, ple_norm_r)
    y_s = _ple(x1_s, x1g_s, ss1_s, p_sample[0].reshape(nseq * SEQ4, PLE_DIM), ple_gate_b, ple_proj_b, final_norm_r)

    keep = SEQ4 - (CONV_K - 1)
    s_mconv = proj_s[:, COL_U:COL_U + D_M].reshape(nseq, SEQ4, D_M)[:, keep:]
    s_sconv = proj_s[:, COL_XS:COL_XS + CONV_DIM].reshape(nseq, SEQ4, CONV_DIM)[:, keep:]
    s_m = jnp.transpose(sm[:, SEQ4 - 1::SEQ4, 0])

    return (
        y_p.reshape(bsz, seqlen, D_MODEL),
        y_s.reshape(nseq, SEQ4, D_MODEL),
        pc[None], pn.reshape(1, bsz, M_HEADS, M_QK), pm[:, :, 0, 0][None], p_mconv[None],
        ph.reshape(1, bsz, S_HEADS, S_HEAD_DIM, S_STATE), p_sconv[None],
        sc[None], sn.reshape(1, nseq, M_HEADS, M_QK), s_m[None], s_mconv[None],
        sh.reshape(1, nseq, S_HEADS, S_HEAD_DIM, S_STATE), s_sconv[None],
    )
```

```python
import functools
from typing import Callable, NamedTuple

import jax
import jax.numpy as jnp
from jax import lax
from jax.experimental import pallas as pl
from jax.experimental.pallas import tpu as pltpu

F32 = jnp.float32
BF16 = jnp.bfloat16
HIGHEST = lax.Precision.HIGHEST

D_MODEL = 4096
D_M = 4096
D_S = 4096
M_HEADS = 8
M_V = 512
M_QK = 256
S_HEADS = 64
S_HEAD_DIM = 64
S_STATE = 128
S_GROUPS = 8
S_HPG = 8
S_GW = S_HPG * S_HEAD_DIM
CONV_DIM = D_S + 2 * S_GROUPS * S_STATE
CONV_K = 4
PLE_DIM = 256
GATE_CAP = 15.0
EPS = 1e-6

COL_U, COL_V, COL_O, COL_ZM, COL_ZS, COL_XS = 0, 4096, 8192, 12288, 16384, 20480
COL_BM = COL_XS + D_S
COL_CM = COL_BM + S_GROUPS * S_STATE
N_MAIN = COL_XS + CONV_DIM
LANES = 128
GL_IG, GL_FG, GL_DT = 0, M_HEADS, 2 * M_HEADS

PROMPT_CHUNK = 256
SAMPLE_BB = 16
VMEM_LIMIT = 62 * 1024 * 1024

NT_DIMS = (((1,), (1,)), ((), ()))


def _dot(a, b):
    return jnp.dot(a, b, preferred_element_type=F32)


def _dot_nt(a, b):
    return lax.dot_general(a, b, NT_DIMS, preferred_element_type=F32)


def _dot_exact(a, b):
    return jnp.dot(a, b, preferred_element_type=F32, precision=HIGHEST)


def _sigmoid(x):
    return 0.5 * jnp.tanh(0.5 * x) + 0.5


def _silu(x):
    return x * _sigmoid(x)


def _softcap(x):
    return GATE_CAP * jnp.tanh(x / GATE_CAP)


def _softplus(x):
    return jnp.maximum(x, 0.0) + jnp.log1p(jnp.exp(-jnp.abs(x)))


def _log_sigmoid(x):
    return -_softplus(-x)


def _idiv(x, d):
    return lax.shift_right_logical(x, d.bit_length() - 1)


def _imod(x, d):
    return lax.bitwise_and(x, d - 1)


def _lane_pick(x, lane_idx):
    lane = lax.broadcasted_iota(jnp.int32, (1, x.shape[1]), 1)
    return jnp.sum(jnp.where(lane == lane_idx, x, 0.0), axis=1, keepdims=True)


def _col_to_row(col, eye):
    return jnp.sum(jnp.where(eye, col, 0.0), axis=0, keepdims=True)


def _row_to_col(row, eye):
    return jnp.sum(jnp.where(eye, row, 0.0), axis=1, keepdims=True)


class _Part(NamedTuple):
    body: Callable
    ids: Callable
    in_specs: list
    operands: list
    out_specs: list
    out_shape: list
    scratch: list


def _same_ids(*g):
    return g


def _on_grid(ids, block_shape, index_map):
    return pl.BlockSpec(block_shape, lambda *g: index_map(*ids(*g)))


def _call_parts(name, grid, parts):
    n_in = [len(p.in_specs) for p in parts]
    n_out = [len(p.out_specs) for p in parts]
    n_scr = [len(p.scratch) for p in parts]

    def body(*refs):
        gids = tuple(pl.program_id(a) for a in range(len(grid)))
        ins, outs, scr = refs[:sum(n_in)], refs[sum(n_in):sum(n_in) + sum(n_out)], refs[sum(n_in) + sum(n_out):]
        for p, part in enumerate(parts):
            a, b, c = sum(n_in[:p]), sum(n_out[:p]), sum(n_scr[:p])
            part.body(part.ids(*gids), *ins[a:a + n_in[p]], *outs[b:b + n_out[p]], *scr[c:c + n_scr[p]])

    results = pl.pallas_call(
        body,
        grid=grid,
        in_specs=[s for p in parts for s in p.in_specs],
        out_specs=[s for p in parts for s in p.out_specs],
        out_shape=[s for p in parts for s in p.out_shape],
        scratch_shapes=[s for p in parts for s in p.scratch],
        compiler_params=pltpu.CompilerParams(
            dimension_semantics=("arbitrary",) * len(grid), vmem_limit_bytes=VMEM_LIMIT),
        name=name,
    )(*[o for p in parts for o in p.operands])
    return [results[sum(n_out[:p]):sum(n_out[:p]) + n_out[p]] for p in range(len(parts))]


NORM_ROWS = 32


def _rmsnorm_rows_to(x_ref, g, dst_ref):
    def body(r, carry):
        sl = pl.ds(pl.multiple_of(r * NORM_ROWS, NORM_ROWS), NORM_ROWS)
        x = x_ref[sl, :]
        ms = jnp.mean(x * x, axis=-1, keepdims=True)
        dst_ref[sl, :] = (x * lax.rsqrt(ms + EPS) * g).astype(BF16)
        return carry
    lax.fori_loop(0, x_ref.shape[0] // NORM_ROWS, body, 0)


def _in_proj_body(x_ref, g_ref, w_ref, wg_ref, out_ref, gates_ref, gatest_ref, xn_ref):
    @pl.when(pl.program_id(1) == 0)
    def _():
        _rmsnorm_rows_to(x_ref, g_ref[...], xn_ref)
        gates = _dot_nt(xn_ref[...], wg_ref[...])
        gates_ref[...] = gates
        gatest_ref[...] = gates.T
    out_ref[...] = _dot_nt(xn_ref[...], w_ref[...])


def _in_proj(x, norm_w, w_main, w_gate):
    m = x.shape[0]
    tm = min(512, m)
    tn = 1024
    grid = (m // tm, N_MAIN // tn)
    return pl.pallas_call(
        _in_proj_body,
        grid=grid,
        in_specs=[
            pl.BlockSpec((tm, D_MODEL), lambda i, j: (i, 0)),
            pl.BlockSpec((1, D_MODEL), lambda i, j: (0, 0)),
            pl.BlockSpec((tn, D_MODEL), lambda i, j: (j, 0)),
            pl.BlockSpec((LANES, D_MODEL), lambda i, j: (0, 0)),
        ],
        out_specs=[
            pl.BlockSpec((tm, tn), lambda i, j: (i, j)),
            pl.BlockSpec((tm, LANES), lambda i, j: (i, 0)),
            pl.BlockSpec((LANES, tm), lambda i, j: (0, i)),
        ],
        out_shape=[jax.ShapeDtypeStruct((m, N_MAIN), F32), jax.ShapeDtypeStruct((m, LANES), F32),
                   jax.ShapeDtypeStruct((LANES, m), F32)],
        scratch_shapes=[pltpu.VMEM((tm, D_MODEL), BF16)],
        compiler_params=pltpu.CompilerParams(
            dimension_semantics=("parallel", "arbitrary"), vmem_limit_bytes=VMEM_LIMIT),
        name="in_proj",
    )(x, norm_w, w_main, w_gate)


def _rmsnorm_cast_body(x_ref, g_ref, out_ref):
    _rmsnorm_rows_to(x_ref, g_ref[...], out_ref)


def _rmsnorm_cast(x, norm_w):
    m = x.shape[0]
    tm = min(256, m)
    return pl.pallas_call(
        _rmsnorm_cast_body,
        grid=(m // tm,),
        in_specs=[pl.BlockSpec((tm, D_MODEL), lambda i: (i, 0)), pl.BlockSpec((1, D_MODEL), lambda i: (0, 0))],
        out_specs=pl.BlockSpec((tm, D_MODEL), lambda i: (i, 0)),
        out_shape=jax.ShapeDtypeStruct((m, D_MODEL), BF16),
        compiler_params=pltpu.CompilerParams(dimension_semantics=("parallel",), vmem_limit_bytes=VMEM_LIMIT),
        name="rmsnorm_cast",
    )(x, norm_w)


def _in_proj_w32_body(xn_ref, w_ref, ga_ref, gb_ref, out_ref, gates_ref, gatest_ref, wg_ref):
    @pl.when(pl.program_id(1) == 0)
    def _():
        wg_ref[0:GL_DT, :] = ga_ref[0].astype(BF16)
        wg_ref[GL_DT:GL_DT + S_HEADS, :] = gb_ref[0].astype(BF16)
        wg_ref[GL_DT + S_HEADS:LANES, :] = jnp.zeros((LANES - GL_DT - S_HEADS, D_MODEL), BF16)
        gates = _dot_nt(xn_ref[...], wg_ref[...])
        gates_ref[...] = gates
        gatest_ref[...] = gates.T
    out_ref[...] = _dot_nt(xn_ref[...], w_ref[0].astype(BF16))


def _in_proj_w32(xn, w_in_t):
    m = xn.shape[0]
    tm = min(1024, m)
    tn = 512

    def src_row(j):
        return pl.multiple_of(j * tn + (j // (N_ALIGNED // tn)) * GATE_COLS, GATE_COLS)

    el = pl.Element
    return pl.pallas_call(
        _in_proj_w32_body,
        grid=(m // tm, N_MAIN // tn),
        in_specs=[
            pl.BlockSpec((tm, D_MODEL), lambda i, j: (i, 0)),
            pl.BlockSpec((el(1), el(tn), el(D_MODEL)), lambda i, j: (0, src_row(j), 0)),
            pl.BlockSpec((el(1), el(GATE_COLS), el(D_MODEL)), lambda i, j: (0, N_ALIGNED, 0)),
            pl.BlockSpec((el(1), el(S_HEADS), el(D_MODEL)), lambda i, j: (0, N_MAIN + GATE_COLS, 0)),
        ],
        out_specs=[
            pl.BlockSpec((tm, tn), lambda i, j: (i, j)),
            pl.BlockSpec((tm, LANES), lambda i, j: (i, 0)),
            pl.BlockSpec((LANES, tm), lambda i, j: (0, i)),
        ],
        out_shape=[jax.ShapeDtypeStruct((m, N_MAIN), F32), jax.ShapeDtypeStruct((m, LANES), F32),
                   jax.ShapeDtypeStruct((LANES, m), F32)],
        scratch_shapes=[pltpu.VMEM((LANES, D_MODEL), BF16)],
        compiler_params=pltpu.CompilerParams(
            dimension_semantics=("parallel", "arbitrary"), vmem_limit_bytes=VMEM_LIMIT),
        name="in_proj",
    )(xn, w_in_t, w_in_t, w_in_t)


GATE_COLS = 2 * M_HEADS
N_ALIGNED = 4 * D_M


def _w_prep_body(src_ref, ga_ref, gb_ref, out_ref, wg_ref):
    out_ref[...] = src_ref[0].astype(BF16)

    @pl.when(pl.program_id(0) == 0)
    def _():
        wg_ref[0:GL_DT, :] = ga_ref[0].astype(BF16)
        wg_ref[GL_DT:GL_DT + S_HEADS, :] = gb_ref[0].astype(BF16)
        wg_ref[GL_DT + S_HEADS:LANES, :] = jnp.zeros((LANES - GL_DT - S_HEADS, D_MODEL), BF16)


def _w_prep(w_in_t):
    rb = 512

    def src_row(i):
        return pl.multiple_of(i * rb + (i // (N_ALIGNED // rb)) * GATE_COLS, GATE_COLS)

    return pl.pallas_call(
        _w_prep_body,
        grid=(N_MAIN // rb,),
        in_specs=[
            pl.BlockSpec((pl.Element(1), pl.Element(rb), pl.Element(D_MODEL)), lambda i: (0, src_row(i), 0)),
            pl.BlockSpec((pl.Element(1), pl.Element(GATE_COLS), pl.Element(D_MODEL)), lambda i: (0, N_ALIGNED, 0)),
            pl.BlockSpec((pl.Element(1), pl.Element(S_HEADS), pl.Element(D_MODEL)),
                         lambda i: (0, N_MAIN + GATE_COLS, 0)),
        ],
        out_specs=[pl.BlockSpec((rb, D_MODEL), lambda i: (i, 0)), pl.BlockSpec((LANES, D_MODEL), lambda i: (0, 0))],
        out_shape=[jax.ShapeDtypeStruct((N_MAIN, D_MODEL), BF16), jax.ShapeDtypeStruct((LANES, D_MODEL), BF16)],
        compiler_params=pltpu.CompilerParams(dimension_semantics=("arbitrary",), vmem_limit_bytes=VMEM_LIMIT),
        name="w_prep",
    )(w_in_t, w_in_t, w_in_t)


TAIL = 8


def _conv_carry(u, ext_ref, cw, cb):
    c = u.shape[0]
    ext_ref[TAIL:TAIL + c, :] = u
    acc = cb + u * cw[CONV_K - 1:CONV_K, :]
    for j in range(1, CONV_K):
        acc = acc + ext_ref[TAIL - j:TAIL - j + c, :] * cw[CONV_K - 1 - j:CONV_K - j, :]
    ext_ref[0:TAIL, :] = u[c - TAIL:c, :]
    return acc


def _conv_rows(u, buf, cw, cb, t):
    rows = u.shape[0]
    acc = cb + u * cw[CONV_K - 1:CONV_K, :]
    for j in range(1, CONV_K):
        prev = buf if j == CONV_K - 1 else pltpu.roll(buf, rows + j - (CONV_K - 1), 0)
        acc = acc + jnp.where(t >= j, pltpu.roll(u, j, 0), prev) * cw[CONV_K - 1 - j:CONV_K - j, :]
    return acc


def _head_out(hv, o, z, nw):
    og = _sigmoid(o) * hv
    ms = jnp.mean(og * og, axis=-1, keepdims=True)
    return (og * lax.rsqrt(ms + EPS) * nw) * _silu(z)


def _group_out(y, xs, dsk, z, nw):
    gated = (y + dsk * xs) * _silu(z)
    ms = jnp.mean(gated * gated, axis=-1, keepdims=True)
    return gated * lax.rsqrt(ms + EPS) * nw


def _mlstm_prompt_body(u_ref, v_ref, o_ref, z_ref, gt_ref, gb_ref, cw_ref, cb_ref, wq_ref, wkt_ref, nw_ref,
                       hm_ref, c_ref, n_ref, m_ref, ext_ref, mst_ref):
    c = u_ref.shape[0]
    hps = wq_ref.shape[0]
    h0 = pl.program_id(1) * hps

    @pl.when(pl.program_id(2) == 0)
    def _():
        c_ref[...] = jnp.zeros_like(c_ref)
        n_ref[...] = jnp.zeros_like(n_ref)
        mst_ref[...] = jnp.full_like(mst_ref, -jnp.inf)
        ext_ref[0:TAIL, :] = jnp.zeros((TAIL, ext_ref.shape[1]), F32)

    ucb = _silu(_conv_carry(u_ref[...], ext_ref, cw_ref[...], cb_ref[...])).astype(BF16)
    vb = v_ref[...].astype(BF16)

    rowi = lax.broadcasted_iota(jnp.int32, (c, c), 0)
    coli = lax.broadcasted_iota(jnp.int32, (c, c), 1)
    tri = coli <= rowi
    eye = coli == rowi
    ig8 = _softcap(gt_ref[GL_IG:GL_IG + M_HEADS, :] + gb_ref[GL_IG:GL_IG + M_HEADS, :])
    lf8 = _log_sigmoid(_softcap(gt_ref[GL_FG:GL_FG + M_HEADS, :] + gb_ref[GL_FG:GL_FG + M_HEADS, :]))
    b8 = _dot_exact(lf8, jnp.where(rowi <= coli, 1.0, 0.0))
    head_row = lax.broadcasted_iota(jnp.int32, (M_HEADS, 1), 0)

    for hh in range(hps):
        cols = slice(hh * M_V, (hh + 1) * M_V)
        pick = head_row == h0 + hh
        ig_row = jnp.sum(jnp.where(pick, ig8, 0.0), axis=0, keepdims=True)
        b_row = jnp.sum(jnp.where(pick, b8, 0.0), axis=0, keepdims=True)
        b_col = _row_to_col(b_row, eye)
        m_prev = mst_ref[hh:hh + 1, :][:, 0:1]

        q = _dot(ucb[:, cols], wq_ref[hh])
        kt = _dot_nt(wkt_ref[hh], ucb[:, cols])
        qb = q.astype(BF16)
        ktb = kt.astype(BF16)

        r_row = ig_row - b_row
        dmat = jnp.where(tri, b_col + r_row, -jnp.inf)
        inter = b_col + m_prev
        m_t = jnp.maximum(inter, jnp.max(dmat, axis=1, keepdims=True))
        a_inter = jnp.exp(inter - m_t)
        s = _dot(qb, ktb) * jnp.exp(dmat - m_t)
        c_old = c_ref[0, hh]
        n_old = n_ref[0, hh]
        num = _dot(s.astype(BF16), vb[:, cols]) + a_inter * _dot(qb, c_old.astype(BF16))
        qn = _dot_nt(qb, jnp.broadcast_to(n_old, (8, M_QK)).astype(BF16))[:, 0:1]
        den = jnp.sum(s, axis=1, keepdims=True) + a_inter * qn
        hv = num / jnp.maximum(jnp.abs(den), jnp.exp(-m_t))
        hm_ref[:, cols] = _head_out(hv, o_ref[:, cols], z_ref[:, cols], nw_ref[:, cols]).astype(BF16)

        b_end = b_row[:, c - 1:c]
        lw_row = b_end + r_row
        m_new = jnp.maximum(b_end + m_prev, jnp.max(lw_row, axis=1, keepdims=True))
        decay = jnp.exp(b_end + m_prev - m_new)
        wk_row = jnp.exp(lw_row - m_new)
        c_ref[0, hh] = decay * c_old + _dot((kt * wk_row).astype(BF16), vb[:, cols])
        n_ref[0, hh] = decay * n_old + _dot_nt(jnp.broadcast_to(wk_row, (8, c)).astype(BF16), ktb)[0:1, :]
        mst_ref[hh:hh + 1, :] = jnp.broadcast_to(m_new, (1, LANES))
        m_ref[0, hh] = jnp.broadcast_to(m_new, (1, LANES))


MLSTM_HPS = 2


def _mlstm_prompt(proj, gates_t, bsz, seqlen, gate_bias_col, conv_w, conv_b, wq, wkt, norm_w):
    c = min(PROMPT_CHUNK, seqlen)
    nch = seqlen // c
    hps = MLSTM_HPS
    width = hps * M_V

    def tok(col0):
        return pl.BlockSpec((c, width), lambda b, h, k: (b * nch + k, col0 // width + h))

    head_vec = lambda rows: pl.BlockSpec((rows, width), lambda b, h, k: (0, h))
    return pl.pallas_call(
        _mlstm_prompt_body,
        grid=(bsz, M_HEADS // hps, nch),
        in_specs=[
            tok(COL_U), tok(COL_V), tok(COL_O), tok(COL_ZM),
            pl.BlockSpec((LANES, c), lambda b, h, k: (0, b * nch + k)),
            pl.BlockSpec((LANES, 1), lambda b, h, k: (0, 0)),
            head_vec(CONV_K), head_vec(1),
            pl.BlockSpec((hps, M_V, M_QK), lambda b, h, k: (h, 0, 0)),
            pl.BlockSpec((hps, M_QK, M_V), lambda b, h, k: (h, 0, 0)),
            head_vec(1),
        ],
        out_specs=[
            pl.BlockSpec((c, width), lambda b, h, k: (b * nch + k, h)),
            pl.BlockSpec((1, hps, M_QK, M_V), lambda b, h, k: (b, h, 0, 0)),
            pl.BlockSpec((1, hps, 1, M_QK), lambda b, h, k: (b, h, 0, 0)),
            pl.BlockSpec((1, hps, 1, LANES), lambda b, h, k: (b, h, 0, 0)),
        ],
        out_shape=[
            jax.ShapeDtypeStruct((bsz * seqlen, D_M), BF16),
            jax.ShapeDtypeStruct((bsz, M_HEADS, M_QK, M_V), F32),
            jax.ShapeDtypeStruct((bsz, M_HEADS, 1, M_QK), F32),
            jax.ShapeDtypeStruct((bsz, M_HEADS, 1, LANES), F32),
        ],
        scratch_shapes=[pltpu.VMEM((TAIL + c, width), F32), pltpu.VMEM((8, LANES), F32)],
        compiler_params=pltpu.CompilerParams(
            dimension_semantics=("parallel", "parallel", "arbitrary"), vmem_limit_bytes=VMEM_LIMIT),
        name="mlstm_prompt",
    )(proj, proj, proj, proj, gates_t, gate_bias_col, conv_w, conv_b, wq, wkt, norm_w)


LOG2E = 1.4426950408889634
HEAD_ROWS = [slice(r * S_HEAD_DIM, (r + 1) * S_HEAD_DIM) for r in range(S_HPG)]


def _ssd_gate_rows(gt_ref, gbc_ref, avc_ref, lane0):
    rows = pl.ds(lane0, S_HPG)
    dt8 = _softplus(gt_ref[rows, :] + gbc_ref[rows, :])
    return dt8, dt8 * (-jnp.exp(avc_ref[rows, :]))


def _ssd_intra(xst, dt8, cum8, cbt):
    c = xst.shape[1]
    a2r = cum8 * LOG2E
    a2c = jnp.concatenate([a2r, jnp.zeros((LANES - S_HPG, c), F32)], axis=0).T
    ys, xdts = [], []
    for r in range(S_HPG):
        xdt = xst[HEAD_ROWS[r], :] * dt8[r:r + 1, :]
        decay = jnp.exp2(jnp.minimum(a2r[r:r + 1, :] - a2c[:, r:r + 1], 0.0))
        ys.append(_dot(xdt.astype(BF16), (cbt * decay).astype(BF16)))
        xdts.append(xdt)
    return ys, xdts


def _ssd_prompt_body(xs_ref, bm_ref, cm_ref, zs_ref, gt_ref, gbc_ref, avc_ref, dsk_ref,
                     cwx_ref, cbx_ref, cwb_ref, cbb_ref, cwc_ref, cbc_ref, nw_ref,
                     ys_ref, h_ref, extx_ref, extb_ref, extc_ref):
    c = xs_ref.shape[0]
    gps = h_ref.shape[1]
    g0 = pl.program_id(1) * gps

    @pl.when(pl.program_id(2) == 0)
    def _():
        h_ref[...] = jnp.zeros_like(h_ref)
        extx_ref[0:TAIL, :] = jnp.zeros((TAIL, extx_ref.shape[1]), F32)
        extb_ref[0:TAIL, :] = jnp.zeros((TAIL, extb_ref.shape[1]), F32)
        extc_ref[0:TAIL, :] = jnp.zeros((TAIL, extc_ref.shape[1]), F32)

    xs_all = _silu(_conv_carry(xs_ref[...], extx_ref, cwx_ref[...], cbx_ref[...]))
    bmb_all = _silu(_conv_carry(bm_ref[...], extb_ref, cwb_ref[...], cbb_ref[...])).astype(BF16)
    cmb_all = _silu(_conv_carry(cm_ref[...], extc_ref, cwc_ref[...], cbc_ref[...])).astype(BF16)

    src = lax.broadcasted_iota(jnp.int32, (c, c), 0)
    tgt = lax.broadcasted_iota(jnp.int32, (c, c), 1)
    causal = src <= tgt
    upper = jnp.where(causal, 1.0, 0.0)

    for gg in range(gps):
        cols = slice(gg * S_GW, (gg + 1) * S_GW)
        xs = xs_all[:, cols]
        bmb = bmb_all[:, gg * S_STATE:(gg + 1) * S_STATE]
        cmb = cmb_all[:, gg * S_STATE:(gg + 1) * S_STATE]
        lane0 = pl.multiple_of(GL_DT + (g0 + gg) * S_HPG, S_HPG)
        dt8, la8 = _ssd_gate_rows(gt_ref, gbc_ref, avc_ref, lane0)
        cum8 = _dot_exact(la8, upper)
        a_end = cum8[:, c - 1:c]
        w8 = jnp.exp(a_end - cum8)
        ea8 = jnp.exp(cum8)

        cbt = jnp.where(causal, _dot_nt(bmb, cmb), 0.0)
        ys, xdts = _ssd_intra(xs.T, dt8, cum8, cbt)
        h_old = h_ref[0, gg]
        yit = _dot_nt(h_old.astype(BF16), cmb)
        yt = jnp.concatenate([ys[r] + ea8[r:r + 1, :] * yit[HEAD_ROWS[r], :] for r in range(S_HPG)], axis=0)
        ys_ref[:, cols] = _group_out(yt.T, xs, dsk_ref[:, cols], zs_ref[:, cols], nw_ref[:, cols]).astype(BF16)

        xwt = jnp.concatenate([(xdts[r] * w8[r:r + 1, :]).astype(BF16) for r in range(S_HPG)], axis=0)
        upd = _dot(xwt, bmb)
        ea_end = jnp.exp(a_end)
        for r in range(S_HPG):
            h_ref[0, gg, HEAD_ROWS[r], :] = ea_end[r:r + 1, :] * h_old[HEAD_ROWS[r], :] + upd[HEAD_ROWS[r], :]


SSD_GPS = 2


def _ssd_prompt(proj, gates_t, bsz, seqlen, gate_bias_col, a_col, dsk_row, conv_w, conv_b, norm_w):
    c = min(PROMPT_CHUNK, seqlen)
    nch = seqlen // c
    gps = SSD_GPS
    tok = lambda width, col0: pl.BlockSpec((c, gps * width), lambda b, g, k: (b * nch + k, col0 // (gps * width) + g))
    vec = lambda rows, width, col0: pl.BlockSpec((rows, gps * width), lambda b, g, k: (0, col0 // (gps * width) + g))
    col = pl.BlockSpec((LANES, 1), lambda b, g, k: (0, 0))
    return pl.pallas_call(
        _ssd_prompt_body,
        grid=(bsz, S_GROUPS // gps, nch),
        in_specs=[
            tok(S_GW, COL_XS), tok(S_STATE, COL_BM), tok(S_STATE, COL_CM), tok(S_GW, COL_ZS),
            pl.BlockSpec((LANES, c), lambda b, g, k: (0, b * nch + k)),
            col, col, vec(1, S_GW, 0),
            vec(CONV_K, S_GW, 0), vec(1, S_GW, 0),
            vec(CONV_K, S_STATE, D_S), vec(1, S_STATE, D_S),
            vec(CONV_K, S_STATE, D_S + S_GROUPS * S_STATE), vec(1, S_STATE, D_S + S_GROUPS * S_STATE),
            vec(1, S_GW, 0),
        ],
        out_specs=[
            pl.BlockSpec((c, gps * S_GW), lambda b, g, k: (b * nch + k, g)),
            pl.BlockSpec((1, gps, S_GW, S_STATE), lambda b, g, k: (b, g, 0, 0)),
        ],
        out_shape=[
            jax.ShapeDtypeStruct((bsz * seqlen, D_S), BF16),
            jax.ShapeDtypeStruct((bsz, S_GROUPS, S_GW, S_STATE), F32),
        ],
        scratch_shapes=[pltpu.VMEM((TAIL + c, gps * S_GW), F32), pltpu.VMEM((TAIL + c, gps * S_STATE), F32),
                        pltpu.VMEM((TAIL + c, gps * S_STATE), F32)],
        compiler_params=pltpu.CompilerParams(
            dimension_semantics=("parallel", "parallel", "arbitrary"), vmem_limit_bytes=VMEM_LIMIT),
        name="ssd_prompt",
    )(proj, proj, proj, proj, gates_t, gate_bias_col, a_col, dsk_row,
      conv_w, conv_b, conv_w, conv_b, conv_w, conv_b, norm_w)


SEQ4 = 4


def _row_masks(rows):
    rowi = lax.broadcasted_iota(jnp.int32, (rows, rows), 0)
    coli = lax.broadcasted_iota(jnp.int32, (rows, rows), 1)
    same = _idiv(rowi, SEQ4) == _idiv(coli, SEQ4)
    return same, same & (coli <= rowi), coli == rowi, same & (_imod(coli, SEQ4) == SEQ4 - 1)


def _seg_cumsum(x, t):
    out = x
    for j in range(1, SEQ4):
        out = out + jnp.where(t >= j, pltpu.roll(x, j, 0), 0.0)
    return out


def _mlstm_sample_body(ids, u_ref, v_ref, o_ref, z_ref, gt_ref, gb_ref, buf_ref, mrow_ref, c0_ref, n0_ref,
                       cw_ref, cb_ref, wq_ref, wkt_ref, nw_ref,
                       hm_ref, c_ref, n_ref, m_ref):
    h = ids[0]
    rows = u_ref.shape[0]
    nseq = rows // SEQ4
    rid = lax.broadcasted_iota(jnp.int32, (rows, 1), 0)
    t = _imod(rid, SEQ4)
    seq = _idiv(rid, SEQ4)
    seq_lane = _idiv(lax.broadcasted_iota(jnp.int32, (1, rows), 1), SEQ4)

    ucb = _silu(_conv_rows(u_ref[...], buf_ref[...], cw_ref[...], cb_ref[...], t)).astype(BF16)
    q = _dot(ucb, wq_ref[0])
    kt = _dot_nt(wkt_ref[0], ucb)
    qb = q.astype(BF16)
    vb = v_ref[...].astype(BF16)

    capped = _softcap(gt_ref[...] + gb_ref[...])
    ig_col = _lane_pick(capped, GL_IG + h)
    b_col = _lane_pick(_seg_cumsum(_log_sigmoid(capped), t), GL_FG + h)
    m_prev = _lane_pick(mrow_ref[...], h)

    same, tri, eye, last = _row_masks(rows)
    b_row = _col_to_row(b_col, eye)
    ig_row = _col_to_row(ig_col, eye)
    dmat = jnp.where(tri, b_col - b_row + ig_row, -jnp.inf)
    inter = b_col + m_prev
    m_t = jnp.maximum(inter, jnp.max(dmat, axis=1, keepdims=True))
    a_inter = jnp.exp(inter - m_t)
    s = _dot(qb, kt.astype(BF16)) * jnp.exp(dmat - m_t)

    b_end = jnp.sum(jnp.where(last, b_row, 0.0), axis=1, keepdims=True)
    lw = b_end - b_col + ig_col
    m_new = jnp.maximum(b_end + m_prev, jnp.max(jnp.where(same, _col_to_row(lw, eye), -jnp.inf), axis=1, keepdims=True))
    decay = jnp.exp(b_end + m_prev - m_new)
    kwt = kt * _col_to_row(jnp.exp(lw - m_new), eye)
    eye_k = (lax.broadcasted_iota(jnp.int32, (M_QK, M_QK), 0) == lax.broadcasted_iota(jnp.int32, (M_QK, M_QK), 1))

    num_inter = jnp.zeros((rows, M_V), F32)
    qn = jnp.zeros((rows, 1), F32)
    for bi in range(nseq):
        own = seq == bi
        c_old = c0_ref[bi, 0]
        n_old = n0_ref[bi, 0]
        num_inter = jnp.where(own, _dot(qb, c_old.astype(BF16)), num_inter)
        qn = jnp.where(own, jnp.sum(q * n_old, axis=1, keepdims=True), qn)
        kw_b = jnp.where(seq_lane == bi, kwt, 0.0)
        d_b = jnp.sum(jnp.where(own & (t == SEQ4 - 1), decay, 0.0), axis=0, keepdims=True)
        c_ref[bi, 0] = d_b * c_old + _dot(kw_b.astype(BF16), vb)
        n_ref[bi, 0] = d_b * n_old + _col_to_row(jnp.sum(kw_b, axis=1, keepdims=True), eye_k)

    num = _dot(s.astype(BF16), vb) + a_inter * num_inter
    den = jnp.sum(s, axis=1, keepdims=True) + a_inter * qn
    hv = num / jnp.maximum(jnp.abs(den), jnp.exp(-m_t))
    hm_ref[...] = _head_out(hv, o_ref[...], z_ref[...], nw_ref[...]).astype(BF16)
    m_ref[0] = jnp.broadcast_to(m_new, (rows, LANES))


def _mlstm_sample_part(proj, gates, nseq_total, gate_bias, buf_rows, m_rows, c0, n0, conv_w, conv_b, wq, wkt, norm_w,
                       bb, ids=_same_ids):
    rows = bb * SEQ4
    nblk = nseq_total // bb
    spec = functools.partial(_on_grid, ids)
    tok = lambda col0: spec((rows, M_V), lambda h, i: (i, col0 // M_V + h))
    head_vec = lambda r: spec((r, M_V), lambda h, i: (0, h))
    c_spec = spec((bb, 1, M_QK, M_V), lambda h, i: (i, h, 0, 0))
    n_spec = spec((bb, 1, 1, M_QK), lambda h, i: (i, h, 0, 0))
    return _Part(
        body=_mlstm_sample_body, ids=ids,
        in_specs=[
            tok(COL_U), tok(COL_V), tok(COL_O), tok(COL_ZM),
            spec((rows, LANES), lambda h, i: (i, 0)),
            spec((1, LANES), lambda h, i: (0, 0)),
            spec((rows, M_V), lambda h, i: (i, h)),
            spec((rows, LANES), lambda h, i: (i, 0)),
            c_spec, n_spec,
            head_vec(CONV_K), head_vec(1),
            spec((1, M_V, M_QK), lambda h, i: (h, 0, 0)),
            spec((1, M_QK, M_V), lambda h, i: (h, 0, 0)),
            head_vec(1),
        ],
        operands=[proj, proj, proj, proj, gates, gate_bias, buf_rows, m_rows, c0, n0, conv_w, conv_b, wq, wkt, norm_w],
        out_specs=[
            spec((rows, M_V), lambda h, i: (i, h)),
            c_spec, n_spec,
            spec((1, rows, LANES), lambda h, i: (h, i, 0)),
        ],
        out_shape=[
            jax.ShapeDtypeStruct((nseq_total * SEQ4, D_M), BF16),
            jax.ShapeDtypeStruct((nseq_total, M_HEADS, M_QK, M_V), F32),
            jax.ShapeDtypeStruct((nseq_total, M_HEADS, 1, M_QK), F32),
            jax.ShapeDtypeStruct((M_HEADS, nseq_total * SEQ4, LANES), F32),
        ],
        scratch=[],
    ), (M_HEADS, nblk)


def _ssd_sample_body(xs_ref, bm_ref, cm_ref, zs_ref, gt_ref, gbc_ref, avc_ref, dsk_ref,
                     bufx_ref, bufb_ref, bufc_ref, h0_ref,
                     cwx_ref, cbx_ref, cwb_ref, cbb_ref, cwc_ref, cbc_ref, nw_ref,
                     ys_ref, h_ref, yit_ref, xwt_ref):
    g = pl.program_id(0)
    rows = xs_ref.shape[0]
    nseq = rows // SEQ4
    rid = lax.broadcasted_iota(jnp.int32, (rows, 1), 0)
    t = _imod(rid, SEQ4)
    seq = _idiv(rid, SEQ4)
    tok_lane = lax.broadcasted_iota(jnp.int32, (1, rows), 1)

    xs = _silu(_conv_rows(xs_ref[...], bufx_ref[...], cwx_ref[...], cbx_ref[...], t))
    bmb = _silu(_conv_rows(bm_ref[...], bufb_ref[...], cwb_ref[...], cbb_ref[...], t)).astype(BF16)
    cmb = _silu(_conv_rows(cm_ref[...], bufc_ref[...], cwc_ref[...], cbc_ref[...], t)).astype(BF16)

    lane0 = pl.multiple_of(GL_DT + g * S_HPG, S_HPG)
    dt8, la8 = _ssd_gate_rows(gt_ref, gbc_ref, avc_ref, lane0)
    src = lax.broadcasted_iota(jnp.int32, (rows, rows), 0)
    tgt = lax.broadcasted_iota(jnp.int32, (rows, rows), 1)
    same = _idiv(src, SEQ4) == _idiv(tgt, SEQ4)
    causal = same & (src <= tgt)
    cum8 = _dot_exact(la8, jnp.where(causal, 1.0, 0.0))
    a_end = _dot_exact(cum8, jnp.where(same & (_imod(src, SEQ4) == SEQ4 - 1), 1.0, 0.0))
    w8 = jnp.exp(a_end - cum8)
    ea8 = jnp.exp(cum8)
    ea_end = jnp.exp(a_end)

    cbt = jnp.where(causal, _dot_nt(bmb, cmb), 0.0)
    ys, xdts = _ssd_intra(xs.T, dt8, cum8, cbt)
    xwt_ref[...] = jnp.concatenate([(xdts[r] * w8[r:r + 1, :]).astype(BF16) for r in range(S_HPG)], axis=0)
    yit_ref[...] = jnp.zeros_like(yit_ref)

    def per_sequence(bi, carry):
        h_old = h0_ref[bi, 0]
        own = _idiv(tok_lane, SEQ4) == bi
        yit_ref[...] += _dot_nt(h_old.astype(BF16), jnp.where(seq == bi, cmb, jnp.zeros_like(cmb)))
        upd = _dot(jnp.where(own, xwt_ref[...], jnp.zeros_like(xwt_ref)), bmb)
        d8 = jnp.sum(jnp.where(tok_lane == bi * SEQ4 + SEQ4 - 1, ea_end, 0.0), axis=1, keepdims=True)
        for r in range(S_HPG):
            h_ref[bi, 0, HEAD_ROWS[r], :] = d8[r:r + 1, :] * h_old[HEAD_ROWS[r], :] + upd[HEAD_ROWS[r], :]
        return carry

    lax.fori_loop(0, nseq, per_sequence, 0, unroll=4 if nseq % 4 == 0 else 1)
    yit = yit_ref[...]
    yt = jnp.concatenate([ys[r] + ea8[r:r + 1, :] * yit[HEAD_ROWS[r], :] for r in range(S_HPG)], axis=0)
    ys_ref[...] = _group_out(yt.T, xs, dsk_ref[...], zs_ref[...], nw_ref[...]).astype(BF16)


SSD_SAMPLE_BB = 32


def _ssd_sample(proj, gates_t, nseq_total, gate_bias_col, a_col, dsk_row, buf_rows, h0, conv_w, conv_b, norm_w):
    bb = min(SSD_SAMPLE_BB, nseq_total)
    rows = bb * SEQ4
    nblk = nseq_total // bb
    tok = lambda width, col0: pl.BlockSpec((rows, width), lambda g, i: (i, col0 // width + g))
    vec = lambda r, width, col0: pl.BlockSpec((r, width), lambda g, i: (0, col0 // width + g))
    col = pl.BlockSpec((LANES, 1), lambda g, i: (0, 0))
    h_spec = pl.BlockSpec((bb, 1, S_GW, S_STATE), lambda g, i: (i, g, 0, 0))
    off_b = D_S
    off_c = D_S + S_GROUPS * S_STATE
    return pl.pallas_call(
        _ssd_sample_body,
        grid=(S_GROUPS, nblk),
        in_specs=[
            tok(S_GW, COL_XS), tok(S_STATE, COL_BM), tok(S_STATE, COL_CM), tok(S_GW, COL_ZS),
            pl.BlockSpec((LANES, rows), lambda g, i: (0, i)),
            col, col, vec(1, S_GW, 0),
            tok(S_GW, 0), tok(S_STATE, off_b), tok(S_STATE, off_c),
            h_spec,
            vec(CONV_K, S_GW, 0), vec(1, S_GW, 0),
            vec(CONV_K, S_STATE, off_b), vec(1, S_STATE, off_b),
            vec(CONV_K, S_STATE, off_c), vec(1, S_STATE, off_c),
            vec(1, S_GW, 0),
        ],
        out_specs=[pl.BlockSpec((rows, S_GW), lambda g, i: (i, g)), h_spec],
        out_shape=[
            jax.ShapeDtypeStruct((nseq_total * SEQ4, D_S), BF16),
            jax.ShapeDtypeStruct((nseq_total, S_GROUPS, S_GW, S_STATE), F32),
        ],
        scratch_shapes=[pltpu.VMEM((S_GW, rows), F32), pltpu.VMEM((S_GW, rows), BF16)],
        compiler_params=pltpu.CompilerParams(
            dimension_semantics=("parallel", "parallel"), vmem_limit_bytes=VMEM_LIMIT),
        name="ssd_sample",
    )(proj, proj, proj, proj, gates_t, gate_bias_col, a_col, dsk_row, buf_rows, buf_rows, buf_rows, h0,
      conv_w, conv_b, conv_w, conv_b, conv_w, conv_b, norm_w)


OUT_TM, OUT_TN = 512, 512


def _out_proj_body(ids, x_ref, hm_ref, ys_ref, wt_ref, wb_ref, gn_ref, out_ref, x1g_ref, ssp_ref):
    x1 = x_ref[...] + _dot(hm_ref[...], wt_ref[...]) + _dot(ys_ref[...], wb_ref[...])
    out_ref[...] = x1
    x1g_ref[...] = (x1 * gn_ref[...]).astype(BF16)
    ssp_ref[0] = jnp.sum(x1 * x1, axis=1, keepdims=True)


def _out_proj_part(x, hm, ys, w_out, ple_norm, ids=_same_ids):
    m = x.shape[0]
    tm, tn = min(OUT_TM, m), OUT_TN
    nj = D_MODEL // tn
    spec = functools.partial(_on_grid, ids)
    return _Part(
        body=_out_proj_body, ids=ids,
        in_specs=[
            spec((tm, tn), lambda i, j: (i, j)),
            spec((tm, D_M), lambda i, j: (i, 0)),
            spec((tm, D_S), lambda i, j: (i, 0)),
            spec((D_M, tn), lambda i, j: (0, j)),
            spec((D_S, tn), lambda i, j: (1, j)),
            spec((1, tn), lambda i, j: (0, j)),
        ],
        operands=[x, hm, ys, w_out, w_out, ple_norm],
        out_specs=[
            spec((tm, tn), lambda i, j: (i, j)),
            spec((tm, tn), lambda i, j: (i, j)),
            spec((1, tm, 1), lambda i, j: (j, i, 0)),
        ],
        out_shape=[jax.ShapeDtypeStruct((m, D_MODEL), F32), jax.ShapeDtypeStruct((m, D_MODEL), BF16),
                   jax.ShapeDtypeStruct((nj, m, 1), F32)],
        scratch=[],
    ), (m // tm, nj)


def _ple_body(x1g_ref, ss1_ref, x1t_ref, p_ref, wg_ref, wp_ref, fn_ref, out_ref, ss_ref):
    j = pl.program_id(1)
    nj = pl.num_programs(1)
    tn = wg_ref.shape[1]

    @pl.when(j == 0)
    def _():
        ss_ref[...] = jnp.zeros_like(ss_ref)

    rs = lax.rsqrt(jnp.sum(ss1_ref[...], axis=0) * (1.0 / D_MODEL) + EPS)
    gate = _sigmoid(rs * _dot(x1g_ref[...], wg_ref[...]))
    x2 = x1t_ref[...] + gate * _dot(p_ref[...].astype(BF16), wp_ref[...])
    ss_ref[...] += jnp.sum(x2 * x2, axis=1, keepdims=True)
    for jj in range(out_ref.shape[1] // tn):
        @pl.when(j == jj)
        def _(jj=jj):
            out_ref[:, jj * tn:(jj + 1) * tn] = x2

    @pl.when(j == nj - 1)
    def _():
        fn = fn_ref[...]

        def body(r, carry):
            sl = pl.ds(pl.multiple_of(r * NORM_ROWS, NORM_ROWS), NORM_ROWS)
            scale = lax.rsqrt(ss_ref[sl, :] * (1.0 / D_MODEL) + EPS)
            out_ref[sl, :] = out_ref[sl, :] * scale * fn
            return carry
        lax.fori_loop(0, out_ref.shape[0] // NORM_ROWS, body, 0)


def _ple(x1, x1g, ss1, p, w_gate, w_proj, final_norm):
    m = x1.shape[0]
    tm = min(512, m)
    tn = 512
    return pl.pallas_call(
        _ple_body,
        grid=(m // tm, D_MODEL // tn),
        in_specs=[
            pl.BlockSpec((tm, D_MODEL), lambda i, j: (i, 0)),
            pl.BlockSpec((ss1.shape[0], tm, 1), lambda i, j: (0, i, 0)),
            pl.BlockSpec((tm, tn), lambda i, j: (i, j)),
            pl.BlockSpec((tm, PLE_DIM), lambda i, j: (i, 0)),
            pl.BlockSpec((D_MODEL, tn), lambda i, j: (0, j)),
            pl.BlockSpec((PLE_DIM, tn), lambda i, j: (0, j)),
            pl.BlockSpec((1, D_MODEL), lambda i, j: (0, 0)),
        ],
        out_specs=pl.BlockSpec((tm, D_MODEL), lambda i, j: (i, 0)),
        out_shape=jax.ShapeDtypeStruct((m, D_MODEL), F32),
        scratch_shapes=[pltpu.VMEM((tm, 1), F32)],
        compiler_params=pltpu.CompilerParams(
            dimension_semantics=("parallel", "arbitrary"), vmem_limit_bytes=VMEM_LIMIT),
        name="ple",
    )(x1g, ss1, x1, p, w_gate, w_proj, final_norm)


def _pad_lanes(v, offset):
    out = jnp.zeros((1, LANES), F32)
    return lax.dynamic_update_slice(out, v.reshape(1, -1).astype(F32), (0, offset))


def kernel(x_prompt, x_sample, p_prompt, p_sample, state_m_C, state_m_n, state_m_m, state_m_conv, state_s_ssm, state_s_conv, norm_in, w_in, b_ig, b_fg, m_conv_w, m_conv_b, w_q, w_k, m_norm, s_conv_w, s_conv_b, dt_bias, a_log, d_skip, s_norm, w_out, ple_proj, ple_gate, ple_norm, final_norm):
    assert w_in.shape[0] == 1, "single layer"
    bsz, seqlen, _ = x_prompt.shape
    nseq, dec_seq, _ = x_sample.shape
    assert dec_seq == SEQ4

    assert w_in.shape[2] == N_MAIN + GATE_COLS + S_HEADS
    w_in_t = jnp.swapaxes(w_in, 1, 2)
    gate_bias = jnp.concatenate(
        [b_ig[0], b_fg[0], dt_bias[0], jnp.zeros((LANES - 2 * M_HEADS - S_HEADS,), F32)]).reshape(1, LANES)
    gate_bias_col = gate_bias.reshape(LANES, 1)
    a_col = _pad_lanes(a_log[0], GL_DT).reshape(LANES, 1)
    dsk_row = jnp.repeat(d_skip[0], S_HEAD_DIM).reshape(1, D_S)
    wq = w_q[0].astype(BF16)
    wkt = (jnp.swapaxes(w_k[0], 1, 2) * (M_QK ** -0.5)).astype(BF16)
    w_out_b = w_out[0].astype(BF16)
    ple_gate_b = ple_gate[0].astype(BF16)
    ple_proj_b = ple_proj[0].astype(BF16)
    norm_in_r = norm_in[0].reshape(1, D_MODEL)
    ple_norm_r = ple_norm[0].reshape(1, D_MODEL)
    final_norm_r = final_norm.reshape(1, D_MODEL)
    m_conv_b_r = m_conv_b[0].reshape(1, D_M)
    s_conv_b_r = s_conv_b[0].reshape(1, CONV_DIM)
    m_norm_r = m_norm[0].reshape(1, D_M)
    s_norm_r = s_norm[0].reshape(1, D_S)

    xp = x_prompt.reshape(bsz * seqlen, D_MODEL)
    proj_p, _, gates_pt = _in_proj_w32(_rmsnorm_cast(xp, norm_in_r), w_in_t)
    hm_p, pc, pn, pm = _mlstm_prompt(proj_p, gates_pt, bsz, seqlen, gate_bias_col, m_conv_w[0], m_conv_b_r, wq, wkt, m_norm_r)
    ys_p, ph = _ssd_prompt(proj_p, gates_pt, bsz, seqlen, gate_bias_col, a_col, dsk_row, s_conv_w[0], s_conv_b_r, s_norm_r)

    proj_p3 = proj_p.reshape(bsz, seqlen, N_MAIN)
    tail = seqlen - (CONV_K - 1)
    p_mconv = proj_p3[:, tail:, COL_U:COL_U + D_M]
    p_sconv = proj_p3[:, tail:, COL_XS:COL_XS + CONV_DIM]

    xs_ = x_sample.reshape(nseq * SEQ4, D_MODEL)
    proj_s, gates_s, gates_st = _in_proj_w32(_rmsnorm_cast(xs_, norm_in_r), w_in_t)
    pad_rows = lambda st: jnp.pad(st, ((0, 0), (0, SEQ4 - (CONV_K - 1)), (0, 0))).reshape(nseq * SEQ4, st.shape[-1])
    m_rows = jnp.pad(jnp.repeat(state_m_m[0], SEQ4, axis=0), ((0, 0), (0, LANES - M_HEADS)))

    sample_args = (proj_s, gates_s, nseq, gate_bias, pad_rows(state_m_conv[0]), m_rows, state_m_C[0],
                   state_m_n[0].reshape(nseq, M_HEADS, 1, M_QK), m_conv_w[0], m_conv_b_r, wq, wkt, m_norm_r)
    out_part, out_grid = _out_proj_part(xp, hm_p, ys_p, w_out_b, ple_norm_r)
    bb = SAMPLE_BB
    while bb > 1 and (nseq % bb or M_HEADS * (nseq // bb) < out_grid[0] * out_grid[1]):
        bb //= 2
    if M_HEADS * (nseq // bb) == out_grid[0] * out_grid[1]:
        nblk = nseq // bb
        as_head_block = lambda i, j: ((i * out_grid[1] + j) // nblk, (i * out_grid[1] + j) % nblk)
        scan_part, _ = _mlstm_sample_part(*sample_args, bb, ids=as_head_block)
        (x1_p, x1g_p, ss1_p), (hm_s, sc, sn, sm) = _call_parts("out_proj_mlstm_sample", out_grid, [out_part, scan_part])
    else:
        (x1_p, x1g_p, ss1_p), = _call_parts("out_proj", out_grid, [out_part])
        scan_part, scan_grid = _mlstm_sample_part(*sample_args, bb)
        (hm_s, sc, sn, sm), = _call_parts("mlstm_sample", scan_grid, [scan_part])
    y_p = _ple(x1_p, x1g_p, ss1_p, p_prompt[0].reshape(bsz * seqlen, PLE_DIM), ple_gate_b, ple_proj_b, final_norm_r)

    ys_s, sh = _ssd_sample(
        proj_s, gates_st, nseq, gate_bias_col, a_col, dsk_row, pad_rows(state_s_conv[0]),
        state_s_ssm[0].reshape(nseq, S_GROUPS, S_GW, S_STATE), s_conv_w[0], s_conv_b_r, s_norm_r)
    out_part_s, out_grid_s = _out_proj_part(xs_, hm_s, ys_s, w_out_b, ple_norm_r)
    (x1_s, x1g_s, ss1_s), = _call_parts("out_proj", out_grid_s, [out_part_s])
    y_s = _ple(x1_s, x1g_s, ss1_s, p_sample[0].reshape(nseq * SEQ4, PLE_DIM), ple_gate_b, ple_proj_b, final_norm_r)

    keep = SEQ4 - (CONV_K - 1)
    s_mconv = proj_s[:, COL_U:COL_U + D_M].reshape(nseq, SEQ4, D_M)[:, keep:]
    s_sconv = proj_s[:, COL_XS:COL_XS + CONV_DIM].reshape(nseq, SEQ4, CONV_DIM)[:, keep:]
    s_m = jnp.transpose(sm[:, SEQ4 - 1::SEQ4, 0])

    return (
        y_p.reshape(bsz, seqlen, D_MODEL),
        y_s.reshape(nseq, SEQ4, D_MODEL),
        pc[None], pn.reshape(1, bsz, M_HEADS, M_QK), pm[:, :, 0, 0][None], p_mconv[None],
        ph.reshape(1, bsz, S_HEADS, S_HEAD_DIM, S_STATE), p_sconv[None],
        sc[None], sn.reshape(1, nseq, M_HEADS, M_QK), s_m[None], s_mconv[None],
        sh.reshape(1, nseq, S_HEADS, S_HEAD_DIM, S_STATE), s_sconv[None],
    )
```

```python
import functools
from typing import Callable, NamedTuple

import jax
import jax.numpy as jnp
from jax import lax
from jax.experimental import pallas as pl
from jax.experimental.pallas import tpu as pltpu

F32 = jnp.float32
BF16 = jnp.bfloat16
HIGHEST = lax.Precision.HIGHEST

D_MODEL = 4096
D_M = 4096
D_S = 4096
M_HEADS = 8
M_V = 512
M_QK = 256
S_HEADS = 64
S_HEAD_DIM = 64
S_STATE = 128
S_GROUPS = 8
S_HPG = 8
S_GW = S_HPG * S_HEAD_DIM
CONV_DIM = D_S + 2 * S_GROUPS * S_STATE
CONV_K = 4
PLE_DIM = 256
GATE_CAP = 15.0
EPS = 1e-6

COL_U, COL_V, COL_O, COL_ZM, COL_ZS, COL_XS = 0, 4096, 8192, 12288, 16384, 20480
COL_BM = COL_XS + D_S
COL_CM = COL_BM + S_GROUPS * S_STATE
N_MAIN = COL_XS + CONV_DIM
LANES = 128
GL_IG, GL_FG, GL_DT = 0, M_HEADS, 2 * M_HEADS

PROMPT_CHUNK = 256
SAMPLE_BB = 16
VMEM_LIMIT = 62 * 1024 * 1024

NT_DIMS = (((1,), (1,)), ((), ()))


def _dot(a, b):
    return jnp.dot(a, b, preferred_element_type=F32)


def _dot_nt(a, b):
    return lax.dot_general(a, b, NT_DIMS, preferred_element_type=F32)


def _dot_exact(a, b):
    return jnp.dot(a, b, preferred_element_type=F32, precision=HIGHEST)


def _sigmoid(x):
    return 0.5 * jnp.tanh(0.5 * x) + 0.5


def _silu(x):
    return x * _sigmoid(x)


def _softcap(x):
    return GATE_CAP * jnp.tanh(x / GATE_CAP)


def _softplus(x):
    return jnp.maximum(x, 0.0) + jnp.log1p(jnp.exp(-jnp.abs(x)))


def _log_sigmoid(x):
    return -_softplus(-x)


def _idiv(x, d):
    return lax.shift_right_logical(x, d.bit_length() - 1)


def _imod(x, d):
    return lax.bitwise_and(x, d - 1)


def _lane_pick(x, lane_idx):
    lane = lax.broadcasted_iota(jnp.int32, (1, x.shape[1]), 1)
    return jnp.sum(jnp.where(lane == lane_idx, x, 0.0), axis=1, keepdims=True)


def _col_to_row(col, eye):
    return jnp.sum(jnp.where(eye, col, 0.0), axis=0, keepdims=True)


def _row_to_col(row, eye):
    return jnp.sum(jnp.where(eye, row, 0.0), axis=1, keepdims=True)


class _Part(NamedTuple):
    body: Callable
    ids: Callable
    in_specs: list
    operands: list
    out_specs: list
    out_shape: list
    scratch: list


def _same_ids(*g):
    return g


def _on_grid(ids, block_shape, index_map):
    return pl.BlockSpec(block_shape, lambda *g: index_map(*ids(*g)))


def _call_parts(name, grid, parts):
    n_in = [len(p.in_specs) for p in parts]
    n_out = [len(p.out_specs) for p in parts]
    n_scr = [len(p.scratch) for p in parts]

    def body(*refs):
        gids = tuple(pl.program_id(a) for a in range(len(grid)))
        ins, outs, scr = refs[:sum(n_in)], refs[sum(n_in):sum(n_in) + sum(n_out)], refs[sum(n_in) + sum(n_out):]
        for p, part in enumerate(parts):
            a, b, c = sum(n_in[:p]), sum(n_out[:p]), sum(n_scr[:p])
            part.body(part.ids(*gids), *ins[a:a + n_in[p]], *outs[b:b + n_out[p]], *scr[c:c + n_scr[p]])

    results = pl.pallas_call(
        body,
        grid=grid,
        in_specs=[s for p in parts for s in p.in_specs],
        out_specs=[s for p in parts for s in p.out_specs],
        out_shape=[s for p in parts for s in p.out_shape],
        scratch_shapes=[s for p in parts for s in p.scratch],
        compiler_params=pltpu.CompilerParams(
            dimension_semantics=("arbitrary",) * len(grid), vmem_limit_bytes=VMEM_LIMIT),
        name=name,
    )(*[o for p in parts for o in p.operands])
    return [results[sum(n_out[:p]):sum(n_out[:p]) + n_out[p]] for p in range(len(parts))]


NORM_ROWS = 32


def _rmsnorm_rows_to(x_ref, g, dst_ref):
    def body(r, carry):
        sl = pl.ds(pl.multiple_of(r * NORM_ROWS, NORM_ROWS), NORM_ROWS)
        x = x_ref[sl, :]
        ms = jnp.mean(x * x, axis=-1, keepdims=True)
        dst_ref[sl, :] = (x * lax.rsqrt(ms + EPS) * g).astype(BF16)
        return carry
    lax.fori_loop(0, x_ref.shape[0] // NORM_ROWS, body, 0)


def _in_proj_body(x_ref, g_ref, w_ref, wg_ref, out_ref, gates_ref, gatest_ref, xn_ref):
    @pl.when(pl.program_id(1) == 0)
    def _():
        _rmsnorm_rows_to(x_ref, g_ref[...], xn_ref)
        gates = _dot_nt(xn_ref[...], wg_ref[...])
        gates_ref[...] = gates
        gatest_ref[...] = gates.T
    out_ref[...] = _dot_nt(xn_ref[...], w_ref[...])


def _in_proj(x, norm_w, w_main, w_gate):
    m = x.shape[0]
    tm = min(512, m)
    tn = 1024
    grid = (m // tm, N_MAIN // tn)
    return pl.pallas_call(
        _in_proj_body,
        grid=grid,
        in_specs=[
            pl.BlockSpec((tm, D_MODEL), lambda i, j: (i, 0)),
            pl.BlockSpec((1, D_MODEL), lambda i, j: (0, 0)),
            pl.BlockSpec((tn, D_MODEL), lambda i, j: (j, 0)),
            pl.BlockSpec((LANES, D_MODEL), lambda i, j: (0, 0)),
        ],
        out_specs=[
            pl.BlockSpec((tm, tn), lambda i, j: (i, j)),
            pl.BlockSpec((tm, LANES), lambda i, j: (i, 0)),
            pl.BlockSpec((LANES, tm), lambda i, j: (0, i)),
        ],
        out_shape=[jax.ShapeDtypeStruct((m, N_MAIN), F32), jax.ShapeDtypeStruct((m, LANES), F32),
                   jax.ShapeDtypeStruct((LANES, m), F32)],
        scratch_shapes=[pltpu.VMEM((tm, D_MODEL), BF16)],
        compiler_params=pltpu.CompilerParams(
            dimension_semantics=("parallel", "arbitrary"), vmem_limit_bytes=VMEM_LIMIT),
        name="in_proj",
    )(x, norm_w, w_main, w_gate)


def _rmsnorm_cast_body(x_ref, g_ref, out_ref):
    _rmsnorm_rows_to(x_ref, g_ref[...], out_ref)


def _rmsnorm_cast(x, norm_w):
    m = x.shape[0]
    tm = min(256, m)
    return pl.pallas_call(
        _rmsnorm_cast_body,
        grid=(m // tm,),
        in_specs=[pl.BlockSpec((tm, D_MODEL), lambda i: (i, 0)), pl.BlockSpec((1, D_MODEL), lambda i: (0, 0))],
        out_specs=pl.BlockSpec((tm, D_MODEL), lambda i: (i, 0)),
        out_shape=jax.ShapeDtypeStruct((m, D_MODEL), BF16),
        compiler_params=pltpu.CompilerParams(dimension_semantics=("parallel",), vmem_limit_bytes=VMEM_LIMIT),
        name="rmsnorm_cast",
    )(x, norm_w)


def _in_proj_w32_body(xn_ref, w_ref, ga_ref, gb_ref, out_ref, gates_ref, gatest_ref, wg_ref):
    @pl.when(pl.program_id(1) == 0)
    def _():
        wg_ref[0:GL_DT, :] = ga_ref[0].astype(BF16)
        wg_ref[GL_DT:GL_DT + S_HEADS, :] = gb_ref[0].astype(BF16)
        wg_ref[GL_DT + S_HEADS:LANES, :] = jnp.zeros((LANES - GL_DT - S_HEADS, D_MODEL), BF16)
        gates = _dot_nt(xn_ref[...], wg_ref[...])
        gates_ref[...] = gates
        gatest_ref[...] = gates.T
    out_ref[...] = _dot_nt(xn_ref[...], w_ref[0].astype(BF16))


def _in_proj_w32(xn, w_in_t):
    m = xn.shape[0]
    tm = min(2048, m)
    tn = 512

    def src_row(j):
        return pl.multiple_of(j * tn + (j // (N_ALIGNED // tn)) * GATE_COLS, GATE_COLS)

    el = pl.Element
    return pl.pallas_call(
        _in_proj_w32_body,
        grid=(m // tm, N_MAIN // tn),
        in_specs=[
            pl.BlockSpec((tm, D_MODEL), lambda i, j: (i, 0), pipeline_mode=pl.Buffered(1)),
            pl.BlockSpec((el(1), el(tn), el(D_MODEL)), lambda i, j: (0, src_row(j), 0)),
            pl.BlockSpec((el(1), el(GATE_COLS), el(D_MODEL)), lambda i, j: (0, N_ALIGNED, 0)),
            pl.BlockSpec((el(1), el(S_HEADS), el(D_MODEL)), lambda i, j: (0, N_MAIN + GATE_COLS, 0)),
        ],
        out_specs=[
            pl.BlockSpec((tm, tn), lambda i, j: (i, j)),
            pl.BlockSpec((tm, LANES), lambda i, j: (i, 0)),
            pl.BlockSpec((LANES, tm), lambda i, j: (0, i)),
        ],
        out_shape=[jax.ShapeDtypeStruct((m, N_MAIN), F32), jax.ShapeDtypeStruct((m, LANES), F32),
                   jax.ShapeDtypeStruct((LANES, m), F32)],
        scratch_shapes=[pltpu.VMEM((LANES, D_MODEL), BF16)],
        compiler_params=pltpu.CompilerParams(
            dimension_semantics=("parallel", "arbitrary"), vmem_limit_bytes=VMEM_LIMIT),
        name="in_proj",
    )(xn, w_in_t, w_in_t, w_in_t)


GATE_COLS = 2 * M_HEADS
N_ALIGNED = 4 * D_M


def _w_prep_body(src_ref, ga_ref, gb_ref, out_ref, wg_ref):
    out_ref[...] = src_ref[0].astype(BF16)

    @pl.when(pl.program_id(0) == 0)
    def _():
        wg_ref[0:GL_DT, :] = ga_ref[0].astype(BF16)
        wg_ref[GL_DT:GL_DT + S_HEADS, :] = gb_ref[0].astype(BF16)
        wg_ref[GL_DT + S_HEADS:LANES, :] = jnp.zeros((LANES - GL_DT - S_HEADS, D_MODEL), BF16)


def _w_prep(w_in_t):
    rb = 512

    def src_row(i):
        return pl.multiple_of(i * rb + (i // (N_ALIGNED // rb)) * GATE_COLS, GATE_COLS)

    return pl.pallas_call(
        _w_prep_body,
        grid=(N_MAIN // rb,),
        in_specs=[
            pl.BlockSpec((pl.Element(1), pl.Element(rb), pl.Element(D_MODEL)), lambda i: (0, src_row(i), 0)),
            pl.BlockSpec((pl.Element(1), pl.Element(GATE_COLS), pl.Element(D_MODEL)), lambda i: (0, N_ALIGNED, 0)),
            pl.BlockSpec((pl.Element(1), pl.Element(S_HEADS), pl.Element(D_MODEL)),
                         lambda i: (0, N_MAIN + GATE_COLS, 0)),
        ],
        out_specs=[pl.BlockSpec((rb, D_MODEL), lambda i: (i, 0)), pl.BlockSpec((LANES, D_MODEL), lambda i: (0, 0))],
        out_shape=[jax.ShapeDtypeStruct((N_MAIN, D_MODEL), BF16), jax.ShapeDtypeStruct((LANES, D_MODEL), BF16)],
        compiler_params=pltpu.CompilerParams(dimension_semantics=("arbitrary",), vmem_limit_bytes=VMEM_LIMIT),
        name="w_prep",
    )(w_in_t, w_in_t, w_in_t)


TAIL = 8


def _conv_carry(u, ext_ref, cw, cb):
    c = u.shape[0]
    ext_ref[TAIL:TAIL + c, :] = u
    acc = cb + u * cw[CONV_K - 1:CONV_K, :]
    for j in range(1, CONV_K):
        acc = acc + ext_ref[TAIL - j:TAIL - j + c, :] * cw[CONV_K - 1 - j:CONV_K - j, :]
    ext_ref[0:TAIL, :] = u[c - TAIL:c, :]
    return acc


def _conv_rows(u, buf, cw, cb, t):
    rows = u.shape[0]
    acc = cb + u * cw[CONV_K - 1:CONV_K, :]
    for j in range(1, CONV_K):
        prev = buf if j == CONV_K - 1 else pltpu.roll(buf, rows + j - (CONV_K - 1), 0)
        acc = acc + jnp.where(t >= j, pltpu.roll(u, j, 0), prev) * cw[CONV_K - 1 - j:CONV_K - j, :]
    return acc


def _head_out(hv, o, z, nw):
    og = _sigmoid(o) * hv
    ms = jnp.mean(og * og, axis=-1, keepdims=True)
    return (og * lax.rsqrt(ms + EPS) * nw) * _silu(z)


def _group_out(y, xs, dsk, z, nw):
    gated = (y + dsk * xs) * _silu(z)
    ms = jnp.mean(gated * gated, axis=-1, keepdims=True)
    return gated * lax.rsqrt(ms + EPS) * nw


def _mlstm_prompt_body(u_ref, v_ref, o_ref, z_ref, gt_ref, gb_ref, cw_ref, cb_ref, wq_ref, wkt_ref, nw_ref,
                       hm_ref, c_ref, n_ref, m_ref, ext_ref, mst_ref):
    c = u_ref.shape[0]
    hps = wq_ref.shape[0]
    h0 = pl.program_id(1) * hps

    @pl.when(pl.program_id(2) == 0)
    def _():
        c_ref[...] = jnp.zeros_like(c_ref)
        n_ref[...] = jnp.zeros_like(n_ref)
        mst_ref[...] = jnp.full_like(mst_ref, -jnp.inf)
        ext_ref[0:TAIL, :] = jnp.zeros((TAIL, ext_ref.shape[1]), F32)

    ucb = _silu(_conv_carry(u_ref[...], ext_ref, cw_ref[...], cb_ref[...])).astype(BF16)
    vb = v_ref[...].astype(BF16)

    rowi = lax.broadcasted_iota(jnp.int32, (c, c), 0)
    coli = lax.broadcasted_iota(jnp.int32, (c, c), 1)
    tri = coli <= rowi
    eye = coli == rowi
    ig8 = _softcap(gt_ref[GL_IG:GL_IG + M_HEADS, :] + gb_ref[GL_IG:GL_IG + M_HEADS, :])
    lf8 = _log_sigmoid(_softcap(gt_ref[GL_FG:GL_FG + M_HEADS, :] + gb_ref[GL_FG:GL_FG + M_HEADS, :]))
    b8 = _dot_exact(lf8, jnp.where(rowi <= coli, 1.0, 0.0))
    head_row = lax.broadcasted_iota(jnp.int32, (M_HEADS, 1), 0)

    for hh in range(hps):
        cols = slice(hh * M_V, (hh + 1) * M_V)
        pick = head_row == h0 + hh
        ig_row = jnp.sum(jnp.where(pick, ig8, 0.0), axis=0, keepdims=True)
        b_row = jnp.sum(jnp.where(pick, b8, 0.0), axis=0, keepdims=True)
        b_col = _row_to_col(b_row, eye)
        m_prev = mst_ref[hh:hh + 1, :][:, 0:1]

        q = _dot(ucb[:, cols], wq_ref[hh])
        kt = _dot_nt(wkt_ref[hh], ucb[:, cols])
        qb = q.astype(BF16)
        ktb = kt.astype(BF16)

        r_row = ig_row - b_row
        dmat = jnp.where(tri, b_col + r_row, -jnp.inf)
        inter = b_col + m_prev
        m_t = jnp.maximum(inter, jnp.max(dmat, axis=1, keepdims=True))
        a_inter = jnp.exp(inter - m_t)
        s = _dot(qb, ktb) * jnp.exp(dmat - m_t)
        c_old = c_ref[0, hh]
        n_old = n_ref[0, hh]
        num = _dot(s.astype(BF16), vb[:, cols]) + a_inter * _dot(qb, c_old.astype(BF16))
        qn = _dot_nt(qb, jnp.broadcast_to(n_old, (8, M_QK)).astype(BF16))[:, 0:1]
        den = jnp.sum(s, axis=1, keepdims=True) + a_inter * qn
        hv = num / jnp.maximum(jnp.abs(den), jnp.exp(-m_t))
        hm_ref[:, cols] = _head_out(hv, o_ref[:, cols], z_ref[:, cols], nw_ref[:, cols]).astype(BF16)

        b_end = b_row[:, c - 1:c]
        lw_row = b_end + r_row
        m_new = jnp.maximum(b_end + m_prev, jnp.max(lw_row, axis=1, keepdims=True))
        decay = jnp.exp(b_end + m_prev - m_new)
        wk_row = jnp.exp(lw_row - m_new)
        c_ref[0, hh] = decay * c_old + _dot((kt * wk_row).astype(BF16), vb[:, cols])
        n_ref[0, hh] = decay * n_old + _dot_nt(jnp.broadcast_to(wk_row, (8, c)).astype(BF16), ktb)[0:1, :]
        mst_ref[hh:hh + 1, :] = jnp.broadcast_to(m_new, (1, LANES))
        m_ref[0, hh] = jnp.broadcast_to(m_new, (1, LANES))


MLSTM_HPS = 4


def _mlstm_prompt(proj, gates_t, bsz, seqlen, gate_bias_col, conv_w, conv_b, wq, wkt, norm_w):
    c = min(PROMPT_CHUNK, seqlen)
    nch = seqlen // c
    hps = MLSTM_HPS
    width = hps * M_V

    def tok(col0):
        return pl.BlockSpec((c, width), lambda b, h, k: (b * nch + k, col0 // width + h))

    head_vec = lambda rows: pl.BlockSpec((rows, width), lambda b, h, k: (0, h))
    return pl.pallas_call(
        _mlstm_prompt_body,
        grid=(bsz, M_HEADS // hps, nch),
        in_specs=[
            tok(COL_U), tok(COL_V), tok(COL_O), tok(COL_ZM),
            pl.BlockSpec((LANES, c), lambda b, h, k: (0, b * nch + k)),
            pl.BlockSpec((LANES, 1), lambda b, h, k: (0, 0)),
            head_vec(CONV_K), head_vec(1),
            pl.BlockSpec((hps, M_V, M_QK), lambda b, h, k: (h, 0, 0)),
            pl.BlockSpec((hps, M_QK, M_V), lambda b, h, k: (h, 0, 0)),
            head_vec(1),
        ],
        out_specs=[
            pl.BlockSpec((c, width), lambda b, h, k: (b * nch + k, h)),
            pl.BlockSpec((1, hps, M_QK, M_V), lambda b, h, k: (b, h, 0, 0)),
            pl.BlockSpec((1, hps, 1, M_QK), lambda b, h, k: (b, h, 0, 0)),
            pl.BlockSpec((1, hps, 1, LANES), lambda b, h, k: (b, h, 0, 0)),
        ],
        out_shape=[
            jax.ShapeDtypeStruct((bsz * seqlen, D_M), BF16),
            jax.ShapeDtypeStruct((bsz, M_HEADS, M_QK, M_V), F32),
            jax.ShapeDtypeStruct((bsz, M_HEADS, 1, M_QK), F32),
            jax.ShapeDtypeStruct((bsz, M_HEADS, 1, LANES), F32),
        ],
        scratch_shapes=[pltpu.VMEM((TAIL + c, width), F32), pltpu.VMEM((8, LANES), F32)],
        compiler_params=pltpu.CompilerParams(
            dimension_semantics=("parallel", "parallel", "arbitrary"), vmem_limit_bytes=VMEM_LIMIT),
        name="mlstm_prompt",
    )(proj, proj, proj, proj, gates_t, gate_bias_col, conv_w, conv_b, wq, wkt, norm_w)


LOG2E = 1.4426950408889634
HEAD_ROWS = [slice(r * S_HEAD_DIM, (r + 1) * S_HEAD_DIM) for r in range(S_HPG)]


def _ssd_gate_rows(gt_ref, gbc_ref, avc_ref, lane0):
    rows = pl.ds(lane0, S_HPG)
    dt8 = _softplus(gt_ref[rows, :] + gbc_ref[rows, :])
    return dt8, dt8 * (-jnp.exp(avc_ref[rows, :]))


def _ssd_intra(xst, dt8, cum8, cbt):
    c = xst.shape[1]
    a2r = cum8 * LOG2E
    a2c = jnp.concatenate([a2r, jnp.zeros((LANES - S_HPG, c), F32)], axis=0).T
    ys, xdts = [], []
    for r in range(S_HPG):
        xdt = xst[HEAD_ROWS[r], :] * dt8[r:r + 1, :]
        decay = jnp.exp2(jnp.minimum(a2r[r:r + 1, :] - a2c[:, r:r + 1], 0.0))
        ys.append(_dot(xdt.astype(BF16), (cbt * decay).astype(BF16)))
        xdts.append(xdt)
    return ys, xdts


def _ssd_prompt_body(xs_ref, bm_ref, cm_ref, zs_ref, gt_ref, gbc_ref, avc_ref, dsk_ref,
                     cwx_ref, cbx_ref, cwb_ref, cbb_ref, cwc_ref, cbc_ref, nw_ref,
                     ys_ref, h_ref, extx_ref, extb_ref, extc_ref):
    c = xs_ref.shape[0]
    gps = h_ref.shape[1]
    g0 = pl.program_id(1) * gps

    @pl.when(pl.program_id(2) == 0)
    def _():
        h_ref[...] = jnp.zeros_like(h_ref)
        extx_ref[0:TAIL, :] = jnp.zeros((TAIL, extx_ref.shape[1]), F32)
        extb_ref[0:TAIL, :] = jnp.zeros((TAIL, extb_ref.shape[1]), F32)
        extc_ref[0:TAIL, :] = jnp.zeros((TAIL, extc_ref.shape[1]), F32)

    xs_all = _silu(_conv_carry(xs_ref[...], extx_ref, cwx_ref[...], cbx_ref[...]))
    bmb_all = _silu(_conv_carry(bm_ref[...], extb_ref, cwb_ref[...], cbb_ref[...])).astype(BF16)
    cmb_all = _silu(_conv_carry(cm_ref[...], extc_ref, cwc_ref[...], cbc_ref[...])).astype(BF16)

    src = lax.broadcasted_iota(jnp.int32, (c, c), 0)
    tgt = lax.broadcasted_iota(jnp.int32, (c, c), 1)
    causal = src <= tgt
    upper = jnp.where(causal, 1.0, 0.0)

    for gg in range(gps):
        cols = slice(gg * S_GW, (gg + 1) * S_GW)
        xs = xs_all[:, cols]
        bmb = bmb_all[:, gg * S_STATE:(gg + 1) * S_STATE]
        cmb = cmb_all[:, gg * S_STATE:(gg + 1) * S_STATE]
        lane0 = pl.multiple_of(GL_DT + (g0 + gg) * S_HPG, S_HPG)
        dt8, la8 = _ssd_gate_rows(gt_ref, gbc_ref, avc_ref, lane0)
        cum8 = _dot_exact(la8, upper)
        a_end = cum8[:, c - 1:c]
        w8 = jnp.exp(a_end - cum8)
        ea8 = jnp.exp(cum8)

        cbt = jnp.where(causal, _dot_nt(bmb, cmb), 0.0)
        ys, xdts = _ssd_intra(xs.T, dt8, cum8, cbt)
        h_old = h_ref[0, gg]
        yit = _dot_nt(h_old.astype(BF16), cmb)
        yt = jnp.concatenate([ys[r] + ea8[r:r + 1, :] * yit[HEAD_ROWS[r], :] for r in range(S_HPG)], axis=0)
        ys_ref[:, cols] = _group_out(yt.T, xs, dsk_ref[:, cols], zs_ref[:, cols], nw_ref[:, cols]).astype(BF16)

        xwt = jnp.concatenate([(xdts[r] * w8[r:r + 1, :]).astype(BF16) for r in range(S_HPG)], axis=0)
        upd = _dot(xwt, bmb)
        ea_end = jnp.exp(a_end)
        for r in range(S_HPG):
            h_ref[0, gg, HEAD_ROWS[r], :] = ea_end[r:r + 1, :] * h_old[HEAD_ROWS[r], :] + upd[HEAD_ROWS[r], :]


SSD_GPS = 4


def _ssd_prompt(proj, gates_t, bsz, seqlen, gate_bias_col, a_col, dsk_row, conv_w, conv_b, norm_w):
    c = min(PROMPT_CHUNK, seqlen)
    nch = seqlen // c
    gps = SSD_GPS
    tok = lambda width, col0: pl.BlockSpec((c, gps * width), lambda b, g, k: (b * nch + k, col0 // (gps * width) + g))
    vec = lambda rows, width, col0: pl.BlockSpec((rows, gps * width), lambda b, g, k: (0, col0 // (gps * width) + g))
    col = pl.BlockSpec((LANES, 1), lambda b, g, k: (0, 0))
    return pl.pallas_call(
        _ssd_prompt_body,
        grid=(bsz, S_GROUPS // gps, nch),
        in_specs=[
            tok(S_GW, COL_XS), tok(S_STATE, COL_BM), tok(S_STATE, COL_CM), tok(S_GW, COL_ZS),
            pl.BlockSpec((LANES, c), lambda b, g, k: (0, b * nch + k)),
            col, col, vec(1, S_GW, 0),
            vec(CONV_K, S_GW, 0), vec(1, S_GW, 0),
            vec(CONV_K, S_STATE, D_S), vec(1, S_STATE, D_S),
            vec(CONV_K, S_STATE, D_S + S_GROUPS * S_STATE), vec(1, S_STATE, D_S + S_GROUPS * S_STATE),
            vec(1, S_GW, 0),
        ],
        out_specs=[
            pl.BlockSpec((c, gps * S_GW), lambda b, g, k: (b * nch + k, g)),
            pl.BlockSpec((1, gps, S_GW, S_STATE), lambda b, g, k: (b, g, 0, 0)),
        ],
        out_shape=[
            jax.ShapeDtypeStruct((bsz * seqlen, D_S), BF16),
            jax.ShapeDtypeStruct((bsz, S_GROUPS, S_GW, S_STATE), F32),
        ],
        scratch_shapes=[pltpu.VMEM((TAIL + c, gps * S_GW), F32), pltpu.VMEM((TAIL + c, gps * S_STATE), F32),
                        pltpu.VMEM((TAIL + c, gps * S_STATE), F32)],
        compiler_params=pltpu.CompilerParams(
            dimension_semantics=("parallel", "parallel", "arbitrary"), vmem_limit_bytes=VMEM_LIMIT),
        name="ssd_prompt",
    )(proj, proj, proj, proj, gates_t, gate_bias_col, a_col, dsk_row,
      conv_w, conv_b, conv_w, conv_b, conv_w, conv_b, norm_w)


SEQ4 = 4


def _row_masks(rows):
    rowi = lax.broadcasted_iota(jnp.int32, (rows, rows), 0)
    coli = lax.broadcasted_iota(jnp.int32, (rows, rows), 1)
    same = _idiv(rowi, SEQ4) == _idiv(coli, SEQ4)
    return same, same & (coli <= rowi), coli == rowi, same & (_imod(coli, SEQ4) == SEQ4 - 1)


def _seg_cumsum(x, t):
    out = x
    for j in range(1, SEQ4):
        out = out + jnp.where(t >= j, pltpu.roll(x, j, 0), 0.0)
    return out


def _mlstm_sample_body(ids, u_ref, v_ref, o_ref, z_ref, gt_ref, gb_ref, buf_ref, mrow_ref, c0_ref, n0_ref,
                       cw_ref, cb_ref, wq_ref, wkt_ref, nw_ref,
                       hm_ref, c_ref, n_ref, m_ref):
    h = ids[0]
    rows = u_ref.shape[0]
    nseq = rows // SEQ4
    rid = lax.broadcasted_iota(jnp.int32, (rows, 1), 0)
    t = _imod(rid, SEQ4)
    seq = _idiv(rid, SEQ4)
    seq_lane = _idiv(lax.broadcasted_iota(jnp.int32, (1, rows), 1), SEQ4)

    ucb = _silu(_conv_rows(u_ref[...], buf_ref[...], cw_ref[...], cb_ref[...], t)).astype(BF16)
    q = _dot(ucb, wq_ref[0])
    kt = _dot_nt(wkt_ref[0], ucb)
    qb = q.astype(BF16)
    vb = v_ref[...].astype(BF16)

    capped = _softcap(gt_ref[...] + gb_ref[...])
    ig_col = _lane_pick(capped, GL_IG + h)
    b_col = _lane_pick(_seg_cumsum(_log_sigmoid(capped), t), GL_FG + h)
    m_prev = _lane_pick(mrow_ref[...], h)

    same, tri, eye, last = _row_masks(rows)
    b_row = _col_to_row(b_col, eye)
    ig_row = _col_to_row(ig_col, eye)
    dmat = jnp.where(tri, b_col - b_row + ig_row, -jnp.inf)
    inter = b_col + m_prev
    m_t = jnp.maximum(inter, jnp.max(dmat, axis=1, keepdims=True))
    a_inter = jnp.exp(inter - m_t)
    s = _dot(qb, kt.astype(BF16)) * jnp.exp(dmat - m_t)

    b_end = jnp.sum(jnp.where(last, b_row, 0.0), axis=1, keepdims=True)
    lw = b_end - b_col + ig_col
    m_new = jnp.maximum(b_end + m_prev, jnp.max(jnp.where(same, _col_to_row(lw, eye), -jnp.inf), axis=1, keepdims=True))
    decay = jnp.exp(b_end + m_prev - m_new)
    kwt = kt * _col_to_row(jnp.exp(lw - m_new), eye)
    eye_k = (lax.broadcasted_iota(jnp.int32, (M_QK, M_QK), 0) == lax.broadcasted_iota(jnp.int32, (M_QK, M_QK), 1))

    num_inter = jnp.zeros((rows, M_V), F32)
    qn = jnp.zeros((rows, 1), F32)
    for bi in range(nseq):
        own = seq == bi
        c_old = c0_ref[bi, 0]
        n_old = n0_ref[bi, 0]
        num_inter = jnp.where(own, _dot(qb, c_old.astype(BF16)), num_inter)
        qn = jnp.where(own, jnp.sum(q * n_old, axis=1, keepdims=True), qn)
        kw_b = jnp.where(seq_lane == bi, kwt, 0.0)
        d_b = jnp.sum(jnp.where(own & (t == SEQ4 - 1), decay, 0.0), axis=0, keepdims=True)
        c_ref[bi, 0] = d_b * c_old + _dot(kw_b.astype(BF16), vb)
        n_ref[bi, 0] = d_b * n_old + _col_to_row(jnp.sum(kw_b, axis=1, keepdims=True), eye_k)

    num = _dot(s.astype(BF16), vb) + a_inter * num_inter
    den = jnp.sum(s, axis=1, keepdims=True) + a_inter * qn
    hv = num / jnp.maximum(jnp.abs(den), jnp.exp(-m_t))
    hm_ref[...] = _head_out(hv, o_ref[...], z_ref[...], nw_ref[...]).astype(BF16)
    m_ref[0] = jnp.broadcast_to(m_new, (rows, LANES))


def _mlstm_sample_part(proj, gates, nseq_total, gate_bias, buf_rows, m_rows, c0, n0, conv_w, conv_b, wq, wkt, norm_w,
                       bb, ids=_same_ids):
    rows = bb * SEQ4
    nblk = nseq_total // bb
    spec = functools.partial(_on_grid, ids)
    tok = lambda col0: spec((rows, M_V), lambda h, i: (i, col0 // M_V + h))
    head_vec = lambda r: spec((r, M_V), lambda h, i: (0, h))
    c_spec = spec((bb, 1, M_QK, M_V), lambda h, i: (i, h, 0, 0))
    n_spec = spec((bb, 1, 1, M_QK), lambda h, i: (i, h, 0, 0))
    return _Part(
        body=_mlstm_sample_body, ids=ids,
        in_specs=[
            tok(COL_U), tok(COL_V), tok(COL_O), tok(COL_ZM),
            spec((rows, LANES), lambda h, i: (i, 0)),
            spec((1, LANES), lambda h, i: (0, 0)),
            spec((rows, M_V), lambda h, i: (i, h)),
            spec((rows, LANES), lambda h, i: (i, 0)),
            c_spec, n_spec,
            head_vec(CONV_K), head_vec(1),
            spec((1, M_V, M_QK), lambda h, i: (h, 0, 0)),
            spec((1, M_QK, M_V), lambda h, i: (h, 0, 0)),
            head_vec(1),
        ],
        operands=[proj, proj, proj, proj, gates, gate_bias, buf_rows, m_rows, c0, n0, conv_w, conv_b, wq, wkt, norm_w],
        out_specs=[
            spec((rows, M_V), lambda h, i: (i, h)),
            c_spec, n_spec,
            spec((1, rows, LANES), lambda h, i: (h, i, 0)),
        ],
        out_shape=[
            jax.ShapeDtypeStruct((nseq_total * SEQ4, D_M), BF16),
            jax.ShapeDtypeStruct((nseq_total, M_HEADS, M_QK, M_V), F32),
            jax.ShapeDtypeStruct((nseq_total, M_HEADS, 1, M_QK), F32),
            jax.ShapeDtypeStruct((M_HEADS, nseq_total * SEQ4, LANES), F32),
        ],
        scratch=[],
    ), (M_HEADS, nblk)


def _ssd_sample_body(xs_ref, bm_ref, cm_ref, zs_ref, gt_ref, gbc_ref, avc_ref, dsk_ref,
                     bufx_ref, bufb_ref, bufc_ref, h0_ref,
                     cwx_ref, cbx_ref, cwb_ref, cbb_ref, cwc_ref, cbc_ref, nw_ref,
                     ys_ref, h_ref, yit_ref, xwt_ref):
    g = pl.program_id(0)
    rows = xs_ref.shape[0]
    nseq = rows // SEQ4
    rid = lax.broadcasted_iota(jnp.int32, (rows, 1), 0)
    t = _imod(rid, SEQ4)
    seq = _idiv(rid, SEQ4)
    tok_lane = lax.broadcasted_iota(jnp.int32, (1, rows), 1)

    xs = _silu(_conv_rows(xs_ref[...], bufx_ref[...], cwx_ref[...], cbx_ref[...], t))
    bmb = _silu(_conv_rows(bm_ref[...], bufb_ref[...], cwb_ref[...], cbb_ref[...], t)).astype(BF16)
    cmb = _silu(_conv_rows(cm_ref[...], bufc_ref[...], cwc_ref[...], cbc_ref[...], t)).astype(BF16)

    lane0 = pl.multiple_of(GL_DT + g * S_HPG, S_HPG)
    dt8, la8 = _ssd_gate_rows(gt_ref, gbc_ref, avc_ref, lane0)
    src = lax.broadcasted_iota(jnp.int32, (rows, rows), 0)
    tgt = lax.broadcasted_iota(jnp.int32, (rows, rows), 1)
    same = _idiv(src, SEQ4) == _idiv(tgt, SEQ4)
    causal = same & (src <= tgt)
    cum8 = _dot_exact(la8, jnp.where(causal, 1.0, 0.0))
    a_end = _dot_exact(cum8, jnp.where(same & (_imod(src, SEQ4) == SEQ4 - 1), 1.0, 0.0))
    w8 = jnp.exp(a_end - cum8)
    ea8 = jnp.exp(cum8)
    ea_end = jnp.exp(a_end)

    cbt = jnp.where(causal, _dot_nt(bmb, cmb), 0.0)
    ys, xdts = _ssd_intra(xs.T, dt8, cum8, cbt)
    xwt_ref[...] = jnp.concatenate([(xdts[r] * w8[r:r + 1, :]).astype(BF16) for r in range(S_HPG)], axis=0)
    yit_ref[...] = jnp.zeros_like(yit_ref)

    def per_sequence(bi, carry):
        h_old = h0_ref[bi, 0]
        own = _idiv(tok_lane, SEQ4) == bi
        yit_ref[...] += _dot_nt(h_old.astype(BF16), jnp.where(seq == bi, cmb, jnp.zeros_like(cmb)))
        upd = _dot(jnp.where(own, xwt_ref[...], jnp.zeros_like(xwt_ref)), bmb)
        d8 = jnp.sum(jnp.where(tok_lane == bi * SEQ4 + SEQ4 - 1, ea_end, 0.0), axis=1, keepdims=True)
        for r in range(S_HPG):
            h_ref[bi, 0, HEAD_ROWS[r], :] = d8[r:r + 1, :] * h_old[HEAD_ROWS[r], :] + upd[HEAD_ROWS[r], :]
        return carry

    lax.fori_loop(0, nseq, per_sequence, 0, unroll=4 if nseq % 4 == 0 else 1)
    yit = yit_ref[...]
    yt = jnp.concatenate([ys[r] + ea8[r:r + 1, :] * yit[HEAD_ROWS[r], :] for r in range(S_HPG)], axis=0)
    ys_ref[...] = _group_out(yt.T, xs, dsk_ref[...], zs_ref[...], nw_ref[...]).astype(BF16)


SSD_SAMPLE_BB = 32


def _ssd_sample(proj, gates_t, nseq_total, gate_bias_col, a_col, dsk_row, buf_rows, h0, conv_w, conv_b, norm_w):
    bb = min(SSD_SAMPLE_BB, nseq_total)
    rows = bb * SEQ4
    nblk = nseq_total // bb
    tok = lambda width, col0: pl.BlockSpec((rows, width), lambda g, i: (i, col0 // width + g))
    vec = lambda r, width, col0: pl.BlockSpec((r, width), lambda g, i: (0, col0 // width + g))
    col = pl.BlockSpec((LANES, 1), lambda g, i: (0, 0))
    h_spec = pl.BlockSpec((bb, 1, S_GW, S_STATE), lambda g, i: (i, g, 0, 0))
    off_b = D_S
    off_c = D_S + S_GROUPS * S_STATE
    return pl.pallas_call(
        _ssd_sample_body,
        grid=(S_GROUPS, nblk),
        in_specs=[
            tok(S_GW, COL_XS), tok(S_STATE, COL_BM), tok(S_STATE, COL_CM), tok(S_GW, COL_ZS),
            pl.BlockSpec((LANES, rows), lambda g, i: (0, i)),
            col, col, vec(1, S_GW, 0),
            tok(S_GW, 0), tok(S_STATE, off_b), tok(S_STATE, off_c),
            h_spec,
            vec(CONV_K, S_GW, 0), vec(1, S_GW, 0),
            vec(CONV_K, S_STATE, off_b), vec(1, S_STATE, off_b),
            vec(CONV_K, S_STATE, off_c), vec(1, S_STATE, off_c),
            vec(1, S_GW, 0),
        ],
        out_specs=[pl.BlockSpec((rows, S_GW), lambda g, i: (i, g)), h_spec],
        out_shape=[
            jax.ShapeDtypeStruct((nseq_total * SEQ4, D_S), BF16),
            jax.ShapeDtypeStruct((nseq_total, S_GROUPS, S_GW, S_STATE), F32),
        ],
        scratch_shapes=[pltpu.VMEM((S_GW, rows), F32), pltpu.VMEM((S_GW, rows), BF16)],
        compiler_params=pltpu.CompilerParams(
            dimension_semantics=("parallel", "parallel"), vmem_limit_bytes=VMEM_LIMIT),
        name="ssd_sample",
    )(proj, proj, proj, proj, gates_t, gate_bias_col, a_col, dsk_row, buf_rows, buf_rows, buf_rows, h0,
      conv_w, conv_b, conv_w, conv_b, conv_w, conv_b, norm_w)


OUT_TM, OUT_TN = 512, 512


def _out_proj_body(ids, x_ref, hm_ref, ys_ref, wt_ref, wb_ref, gn_ref, out_ref, x1g_ref, ssp_ref):
    x1 = x_ref[...] + _dot(hm_ref[...], wt_ref[...]) + _dot(ys_ref[...], wb_ref[...])
    out_ref[...] = x1
    x1g_ref[...] = (x1 * gn_ref[...]).astype(BF16)
    ssp_ref[0] = jnp.sum(x1 * x1, axis=1, keepdims=True)


def _out_proj_part(x, hm, ys, w_out, ple_norm, ids=_same_ids):
    m = x.shape[0]
    tm, tn = min(OUT_TM, m), OUT_TN
    nj = D_MODEL // tn
    spec = functools.partial(_on_grid, ids)
    return _Part(
        body=_out_proj_body, ids=ids,
        in_specs=[
            spec((tm, tn), lambda i, j: (i, j)),
            spec((tm, D_M), lambda i, j: (i, 0)),
            spec((tm, D_S), lambda i, j: (i, 0)),
            spec((D_M, tn), lambda i, j: (0, j)),
            spec((D_S, tn), lambda i, j: (1, j)),
            spec((1, tn), lambda i, j: (0, j)),
        ],
        operands=[x, hm, ys, w_out, w_out, ple_norm],
        out_specs=[
            spec((tm, tn), lambda i, j: (i, j)),
            spec((tm, tn), lambda i, j: (i, j)),
            spec((1, tm, 1), lambda i, j: (j, i, 0)),
        ],
        out_shape=[jax.ShapeDtypeStruct((m, D_MODEL), F32), jax.ShapeDtypeStruct((m, D_MODEL), BF16),
                   jax.ShapeDtypeStruct((nj, m, 1), F32)],
        scratch=[],
    ), (m // tm, nj)


def _ple_body(x1g_ref, ss1_ref, x1t_ref, p_ref, wg_ref, wp_ref, fn_ref, out_ref, ss_ref):
    j = pl.program_id(1)
    nj = pl.num_programs(1)
    tn = wg_ref.shape[1]

    @pl.when(j == 0)
    def _():
        ss_ref[...] = jnp.zeros_like(ss_ref)

    rs = lax.rsqrt(jnp.sum(ss1_ref[...], axis=0) * (1.0 / D_MODEL) + EPS)
    gate = _sigmoid(rs * _dot(x1g_ref[...], wg_ref[...]))
    x2 = x1t_ref[...] + gate * _dot(p_ref[...].astype(BF16), wp_ref[...])
    ss_ref[...] += jnp.sum(x2 * x2, axis=1, keepdims=True)
    for jj in range(out_ref.shape[1] // tn):
        @pl.when(j == jj)
        def _(jj=jj):
            out_ref[:, jj * tn:(jj + 1) * tn] = x2

    @pl.when(j == nj - 1)
    def _():
        fn = fn_ref[...]

        def body(r, carry):
            sl = pl.ds(pl.multiple_of(r * NORM_ROWS, NORM_ROWS), NORM_ROWS)
            scale = lax.rsqrt(ss_ref[sl, :] * (1.0 / D_MODEL) + EPS)
            out_ref[sl, :] = out_ref[sl, :] * scale * fn
            return carry
        lax.fori_loop(0, out_ref.shape[0] // NORM_ROWS, body, 0)


def _ple(x1, x1g, ss1, p, w_gate, w_proj, final_norm):
    m = x1.shape[0]
    tm = min(512, m)
    tn = 1024
    return pl.pallas_call(
        _ple_body,
        grid=(m // tm, D_MODEL // tn),
        in_specs=[
            pl.BlockSpec((tm, D_MODEL), lambda i, j: (i, 0)),
            pl.BlockSpec((ss1.shape[0], tm, 1), lambda i, j: (0, i, 0)),
            pl.BlockSpec((tm, tn), lambda i, j: (i, j)),
            pl.BlockSpec((tm, PLE_DIM), lambda i, j: (i, 0)),
            pl.BlockSpec((D_MODEL, tn), lambda i, j: (0, j)),
            pl.BlockSpec((PLE_DIM, tn), lambda i, j: (0, j)),
            pl.BlockSpec((1, D_MODEL), lambda i, j: (0, 0)),
        ],
        out_specs=pl.BlockSpec((tm, D_MODEL), lambda i, j: (i, 0)),
        out_shape=jax.ShapeDtypeStruct((m, D_MODEL), F32),
        scratch_shapes=[pltpu.VMEM((tm, 1), F32)],
        compiler_params=pltpu.CompilerParams(
            dimension_semantics=("parallel", "arbitrary"), vmem_limit_bytes=VMEM_LIMIT),
        name="ple",
    )(x1g, ss1, x1, p, w_gate, w_proj, final_norm)


def _pad_lanes(v, offset):
    out = jnp.zeros((1, LANES), F32)
    return lax.dynamic_update_slice(out, v.reshape(1, -1).astype(F32), (0, offset))


def kernel(x_prompt, x_sample, p_prompt, p_sample, state_m_C, state_m_n, state_m_m, state_m_conv, state_s_ssm, state_s_conv, norm_in, w_in, b_ig, b_fg, m_conv_w, m_conv_b, w_q, w_k, m_norm, s_conv_w, s_conv_b, dt_bias, a_log, d_skip, s_norm, w_out, ple_proj, ple_gate, ple_norm, final_norm):
    assert w_in.shape[0] == 1, "single layer"
    bsz, seqlen, _ = x_prompt.shape
    nseq, dec_seq, _ = x_sample.shape
    assert dec_seq == SEQ4

    assert w_in.shape[2] == N_MAIN + GATE_COLS + S_HEADS
    w_in_t = jnp.swapaxes(w_in, 1, 2)
    gate_bias = jnp.concatenate(
        [b_ig[0], b_fg[0], dt_bias[0], jnp.zeros((LANES - 2 * M_HEADS - S_HEADS,), F32)]).reshape(1, LANES)
    gate_bias_col = gate_bias.reshape(LANES, 1)
    a_col = _pad_lanes(a_log[0], GL_DT).reshape(LANES, 1)
    dsk_row = jnp.repeat(d_skip[0], S_HEAD_DIM).reshape(1, D_S)
    wq = w_q[0].astype(BF16)
    wkt = (jnp.swapaxes(w_k[0], 1, 2) * (M_QK ** -0.5)).astype(BF16)
    w_out_b = w_out[0].astype(BF16)
    ple_gate_b = ple_gate[0].astype(BF16)
    ple_proj_b = ple_proj[0].astype(BF16)
    norm_in_r = norm_in[0].reshape(1, D_MODEL)
    ple_norm_r = ple_norm[0].reshape(1, D_MODEL)
    final_norm_r = final_norm.reshape(1, D_MODEL)
    m_conv_b_r = m_conv_b[0].reshape(1, D_M)
    s_conv_b_r = s_conv_b[0].reshape(1, CONV_DIM)
    m_norm_r = m_norm[0].reshape(1, D_M)
    s_norm_r = s_norm[0].reshape(1, D_S)

    xp = x_prompt.reshape(bsz * seqlen, D_MODEL)
    xs_ = x_sample.reshape(nseq * SEQ4, D_MODEL)
    proj_p, _, gates_pt = _in_proj_w32(_rmsnorm_cast(xp, norm_in_r), w_in_t)
    proj_s, gates_s, gates_st = _in_proj_w32(_rmsnorm_cast(xs_, norm_in_r), w_in_t)
    hm_p, pc, pn, pm = _mlstm_prompt(proj_p, gates_pt, bsz, seqlen, gate_bias_col, m_conv_w[0], m_conv_b_r, wq, wkt, m_norm_r)
    ys_p, ph = _ssd_prompt(proj_p, gates_pt, bsz, seqlen, gate_bias_col, a_col, dsk_row, s_conv_w[0], s_conv_b_r, s_norm_r)

    proj_p3 = proj_p.reshape(bsz, seqlen, N_MAIN)
    tail = seqlen - (CONV_K - 1)
    p_mconv = proj_p3[:, tail:, COL_U:COL_U + D_M]
    p_sconv = proj_p3[:, tail:, COL_XS:COL_XS + CONV_DIM]

    pad_rows = lambda st: jnp.pad(st, ((0, 0), (0, SEQ4 - (CONV_K - 1)), (0, 0))).reshape(nseq * SEQ4, st.shape[-1])
    m_rows = jnp.pad(jnp.repeat(state_m_m[0], SEQ4, axis=0), ((0, 0), (0, LANES - M_HEADS)))

    sample_args = (proj_s, gates_s, nseq, gate_bias, pad_rows(state_m_conv[0]), m_rows, state_m_C[0],
                   state_m_n[0].reshape(nseq, M_HEADS, 1, M_QK), m_conv_w[0], m_conv_b_r, wq, wkt, m_norm_r)
    out_part, out_grid = _out_proj_part(xp, hm_p, ys_p, w_out_b, ple_norm_r)
    bb = SAMPLE_BB
    while bb > 1 and (nseq % bb or M_HEADS * (nseq // bb) < out_grid[0] * out_grid[1]):
        bb //= 2
    if M_HEADS * (nseq // bb) == out_grid[0] * out_grid[1]:
        nblk = nseq // bb
        as_head_block = lambda i, j: ((i * out_grid[1] + j) // nblk, (i * out_grid[1] + j) % nblk)
        scan_part, _ = _mlstm_sample_part(*sample_args, bb, ids=as_head_block)
        (x1_p, x1g_p, ss1_p), (hm_s, sc, sn, sm) = _call_parts("out_proj_mlstm_sample", out_grid, [out_part, scan_part])
    else:
        (x1_p, x1g_p, ss1_p), = _call_parts("out_proj", out_grid, [out_part])
        scan_part, scan_grid = _mlstm_sample_part(*sample_args, bb)
        (hm_s, sc, sn, sm), = _call_parts("mlstm_sample", scan_grid, [scan_part])
    y_p = _ple(x1_p, x1g_p, ss1_p, p_prompt[0].reshape(bsz * seqlen, PLE_DIM), ple_gate_b, ple_proj_b, final_norm_r)

    ys_s, sh = _ssd_sample(
        proj_s, gates_st, nseq, gate_bias_col, a_col, dsk_row, pad_rows(state_s_conv[0]),
        state_s_ssm[0].reshape(nseq, S_GROUPS, S_GW, S_STATE), s_conv_w[0], s_conv_b_r, s_norm_r)
    out_part_s, out_grid_s = _out_proj_part(xs_, hm_s, ys_s, w_out_b, ple_norm_r)
    (x1_s, x1g_s, ss1_s), = _call_parts("out_proj", out_grid_s, [out_part_s])
    y_s = _ple(x1_s, x1g_s, ss1_s, p_sample[0].reshape(nseq * SEQ4, PLE_DIM), ple_gate_b, ple_proj_b, final_norm_r)

    keep = SEQ4 - (CONV_K - 1)
    s_mconv = proj_s[:, COL_U:COL_U + D_M].reshape(nseq, SEQ4, D_M)[:, keep:]
    s_sconv = proj_s[:, COL_XS:COL_XS + CONV_DIM].reshape(nseq, SEQ4, CONV_DIM)[:, keep:]
    s_m = jnp.transpose(sm[:, SEQ4 - 1::SEQ4, 0])

    return (
        y_p.reshape(bsz, seqlen, D_MODEL),
        y_s.reshape(nseq, SEQ4, D_MODEL),
        pc[None], pn.reshape(1, bsz, M_HEADS, M_QK), pm[:, :, 0, 0][None], p_mconv[None],
        ph.reshape(1, bsz, S_HEADS, S_HEAD_DIM, S_STATE), p_sconv[None],
        sc[None], sn.reshape(1, nseq, M_HEADS, M_QK), s_m[None], s_mconv[None],
        sh.reshape(1, nseq, S_HEADS, S_HEAD_DIM, S_STATE), s_sconv[None],
    )
```

```python
import functools
from typing import Callable, NamedTuple

import jax
import jax.numpy as jnp
from jax import lax
from jax.experimental import pallas as pl
from jax.experimental.pallas import tpu as pltpu

F32 = jnp.float32
BF16 = jnp.bfloat16
HIGHEST = lax.Precision.HIGHEST

D_MODEL = 4096
D_M = 4096
D_S = 4096
M_HEADS = 8
M_V = 512
M_QK = 256
S_HEADS = 64
S_HEAD_DIM = 64
S_STATE = 128
S_GROUPS = 8
S_HPG = 8
S_GW = S_HPG * S_HEAD_DIM
CONV_DIM = D_S + 2 * S_GROUPS * S_STATE
CONV_K = 4
PLE_DIM = 256
GATE_CAP = 15.0
EPS = 1e-6

COL_U, COL_V, COL_O, COL_ZM, COL_ZS, COL_XS = 0, 4096, 8192, 12288, 16384, 20480
COL_BM = COL_XS + D_S
COL_CM = COL_BM + S_GROUPS * S_STATE
N_MAIN = COL_XS + CONV_DIM
LANES = 128
GL_IG, GL_FG, GL_DT = 0, M_HEADS, 2 * M_HEADS

PROMPT_CHUNK = 256
SAMPLE_BB = 16
VMEM_LIMIT = 62 * 1024 * 1024

NT_DIMS = (((1,), (1,)), ((), ()))


def _dot(a, b):
    return jnp.dot(a, b, preferred_element_type=F32)


def _dot_nt(a, b):
    return lax.dot_general(a, b, NT_DIMS, preferred_element_type=F32)


def _dot_exact(a, b):
    return jnp.dot(a, b, preferred_element_type=F32, precision=HIGHEST)


def _sigmoid(x):
    return 0.5 * jnp.tanh(0.5 * x) + 0.5


def _silu(x):
    return x * _sigmoid(x)


def _softcap(x):
    return GATE_CAP * jnp.tanh(x / GATE_CAP)


def _softplus(x):
    return jnp.maximum(x, 0.0) + jnp.log1p(jnp.exp(-jnp.abs(x)))


def _log_sigmoid(x):
    return -_softplus(-x)


def _idiv(x, d):
    return lax.shift_right_logical(x, d.bit_length() - 1)


def _imod(x, d):
    return lax.bitwise_and(x, d - 1)


def _lane_pick(x, lane_idx):
    lane = lax.broadcasted_iota(jnp.int32, (1, x.shape[1]), 1)
    return jnp.sum(jnp.where(lane == lane_idx, x, 0.0), axis=1, keepdims=True)


def _col_to_row(col, eye):
    return jnp.sum(jnp.where(eye, col, 0.0), axis=0, keepdims=True)


def _row_to_col(row, eye):
    return jnp.sum(jnp.where(eye, row, 0.0), axis=1, keepdims=True)


class _Part(NamedTuple):
    body: Callable
    ids: Callable
    in_specs: list
    operands: list
    out_specs: list
    out_shape: list
    scratch: list


def _same_ids(*g):
    return g


def _on_grid(ids, block_shape, index_map, **kwargs):
    return pl.BlockSpec(block_shape, lambda *g: index_map(*ids(*g)), **kwargs)


def _call_parts(name, grid, parts):
    n_in = [len(p.in_specs) for p in parts]
    n_out = [len(p.out_specs) for p in parts]
    n_scr = [len(p.scratch) for p in parts]

    def body(*refs):
        gids = tuple(pl.program_id(a) for a in range(len(grid)))
        ins, outs, scr = refs[:sum(n_in)], refs[sum(n_in):sum(n_in) + sum(n_out)], refs[sum(n_in) + sum(n_out):]
        for p, part in enumerate(parts):
            a, b, c = sum(n_in[:p]), sum(n_out[:p]), sum(n_scr[:p])
            part.body(part.ids(*gids), *ins[a:a + n_in[p]], *outs[b:b + n_out[p]], *scr[c:c + n_scr[p]])

    results = pl.pallas_call(
        body,
        grid=grid,
        in_specs=[s for p in parts for s in p.in_specs],
        out_specs=[s for p in parts for s in p.out_specs],
        out_shape=[s for p in parts for s in p.out_shape],
        scratch_shapes=[s for p in parts for s in p.scratch],
        compiler_params=pltpu.CompilerParams(
            dimension_semantics=("arbitrary",) * len(grid), vmem_limit_bytes=VMEM_LIMIT),
        name=name,
    )(*[o for p in parts for o in p.operands])
    return [results[sum(n_out[:p]):sum(n_out[:p]) + n_out[p]] for p in range(len(parts))]


def _linear_step(grid):
    def lin(*g):
        s = g[0]
        for size, idx in zip(grid[1:], g[1:]):
            s = s * size + idx
        return s
    return lin


def _cast_body(ids, src_ref, dst_ref):
    dst_ref[...] = src_ref[...].astype(dst_ref.dtype)


def _cast_part(w, nsteps, step):
    rows = w.shape[0] // nsteps
    assert rows * nsteps == w.shape[0] and rows % 16 == 0
    spec = pl.BlockSpec((rows, w.shape[1]), lambda *g: (step(*g), 0))
    return _Part(body=_cast_body, ids=_same_ids, in_specs=[spec], operands=[w], out_specs=[spec],
                 out_shape=[jax.ShapeDtypeStruct(w.shape, BF16)], scratch=[])


NORM_ROWS = 32


def _rmsnorm_rows_to(x_ref, g, dst_ref):
    def body(r, carry):
        sl = pl.ds(pl.multiple_of(r * NORM_ROWS, NORM_ROWS), NORM_ROWS)
        x = x_ref[sl, :]
        ms = jnp.mean(x * x, axis=-1, keepdims=True)
        dst_ref[sl, :] = (x * lax.rsqrt(ms + EPS) * g).astype(BF16)
        return carry
    lax.fori_loop(0, x_ref.shape[0] // NORM_ROWS, body, 0)


def _in_proj_body(x_ref, g_ref, w_ref, wg_ref, out_ref, gates_ref, gatest_ref, xn_ref):
    @pl.when(pl.program_id(1) == 0)
    def _():
        _rmsnorm_rows_to(x_ref, g_ref[...], xn_ref)
        gates = _dot_nt(xn_ref[...], wg_ref[...])
        gates_ref[...] = gates
        gatest_ref[...] = gates.T
    out_ref[...] = _dot_nt(xn_ref[...], w_ref[...])


def _in_proj(x, norm_w, w_main, w_gate):
    m = x.shape[0]
    tm = min(512, m)
    tn = 1024
    grid = (m // tm, N_MAIN // tn)
    return pl.pallas_call(
        _in_proj_body,
        grid=grid,
        in_specs=[
            pl.BlockSpec((tm, D_MODEL), lambda i, j: (i, 0)),
            pl.BlockSpec((1, D_MODEL), lambda i, j: (0, 0)),
            pl.BlockSpec((tn, D_MODEL), lambda i, j: (j, 0)),
            pl.BlockSpec((LANES, D_MODEL), lambda i, j: (0, 0)),
        ],
        out_specs=[
            pl.BlockSpec((tm, tn), lambda i, j: (i, j)),
            pl.BlockSpec((tm, LANES), lambda i, j: (i, 0)),
            pl.BlockSpec((LANES, tm), lambda i, j: (0, i)),
        ],
        out_shape=[jax.ShapeDtypeStruct((m, N_MAIN), F32), jax.ShapeDtypeStruct((m, LANES), F32),
                   jax.ShapeDtypeStruct((LANES, m), F32)],
        scratch_shapes=[pltpu.VMEM((tm, D_MODEL), BF16)],
        compiler_params=pltpu.CompilerParams(
            dimension_semantics=("parallel", "arbitrary"), vmem_limit_bytes=VMEM_LIMIT),
        name="in_proj",
    )(x, norm_w, w_main, w_gate)


def _rmsnorm_cast_body(x_ref, g_ref, out_ref):
    _rmsnorm_rows_to(x_ref, g_ref[...], out_ref)


def _rmsnorm_cast(x, norm_w):
    m = x.shape[0]
    tm = min(256, m)
    return pl.pallas_call(
        _rmsnorm_cast_body,
        grid=(m // tm,),
        in_specs=[pl.BlockSpec((tm, D_MODEL), lambda i: (i, 0)), pl.BlockSpec((1, D_MODEL), lambda i: (0, 0))],
        out_specs=pl.BlockSpec((tm, D_MODEL), lambda i: (i, 0)),
        out_shape=jax.ShapeDtypeStruct((m, D_MODEL), BF16),
        compiler_params=pltpu.CompilerParams(dimension_semantics=("parallel",), vmem_limit_bytes=VMEM_LIMIT),
        name="rmsnorm_cast",
    )(x, norm_w)


def _in_proj_w32_body(ids, xn_ref, w_ref, ga_ref, gb_ref, out_ref, gates_ref, gatest_ref, wg_ref):
    @pl.when(ids[1] == 0)
    def _():
        wg_ref[0:GL_DT, :] = ga_ref[0].astype(BF16)
        wg_ref[GL_DT:GL_DT + S_HEADS, :] = gb_ref[0].astype(BF16)
        wg_ref[GL_DT + S_HEADS:LANES, :] = jnp.zeros((LANES - GL_DT - S_HEADS, D_MODEL), BF16)
        gates = _dot_nt(xn_ref[...], wg_ref[...])
        gates_ref[...] = gates
        gatest_ref[...] = gates.T
    out_ref[...] = _dot_nt(xn_ref[...], w_ref[0].astype(BF16))


IN_TN = 512


def _in_proj_part(xn, w_in_t, ids=_same_ids):
    m = xn.shape[0]
    tm = min(2048, m)
    tn = IN_TN

    def src_row(j):
        return pl.multiple_of(j * tn + (j // (N_ALIGNED // tn)) * GATE_COLS, GATE_COLS)

    el = pl.Element
    spec = functools.partial(_on_grid, ids)
    return _Part(
        body=_in_proj_w32_body, ids=ids,
        in_specs=[
            spec((tm, D_MODEL), lambda i, j: (i, 0), pipeline_mode=pl.Buffered(1)),
            spec((el(1), el(tn), el(D_MODEL)), lambda i, j: (0, src_row(j), 0)),
            spec((el(1), el(GATE_COLS), el(D_MODEL)), lambda i, j: (0, N_ALIGNED, 0)),
            spec((el(1), el(S_HEADS), el(D_MODEL)), lambda i, j: (0, N_MAIN + GATE_COLS, 0)),
        ],
        operands=[xn, w_in_t, w_in_t, w_in_t],
        out_specs=[
            spec((tm, tn), lambda i, j: (i, j)),
            spec((tm, LANES), lambda i, j: (i, 0)),
            spec((LANES, tm), lambda i, j: (0, i)),
        ],
        out_shape=[jax.ShapeDtypeStruct((m, N_MAIN), F32), jax.ShapeDtypeStruct((m, LANES), F32),
                   jax.ShapeDtypeStruct((LANES, m), F32)],
        scratch=[pltpu.VMEM((LANES, D_MODEL), BF16)],
    ), (m // tm, N_MAIN // tn)


GATE_COLS = 2 * M_HEADS
N_ALIGNED = 4 * D_M


def _w_prep_body(src_ref, ga_ref, gb_ref, out_ref, wg_ref):
    out_ref[...] = src_ref[0].astype(BF16)

    @pl.when(pl.program_id(0) == 0)
    def _():
        wg_ref[0:GL_DT, :] = ga_ref[0].astype(BF16)
        wg_ref[GL_DT:GL_DT + S_HEADS, :] = gb_ref[0].astype(BF16)
        wg_ref[GL_DT + S_HEADS:LANES, :] = jnp.zeros((LANES - GL_DT - S_HEADS, D_MODEL), BF16)


def _w_prep(w_in_t):
    rb = 512

    def src_row(i):
        return pl.multiple_of(i * rb + (i // (N_ALIGNED // rb)) * GATE_COLS, GATE_COLS)

    return pl.pallas_call(
        _w_prep_body,
        grid=(N_MAIN // rb,),
        in_specs=[
            pl.BlockSpec((pl.Element(1), pl.Element(rb), pl.Element(D_MODEL)), lambda i: (0, src_row(i), 0)),
            pl.BlockSpec((pl.Element(1), pl.Element(GATE_COLS), pl.Element(D_MODEL)), lambda i: (0, N_ALIGNED, 0)),
            pl.BlockSpec((pl.Element(1), pl.Element(S_HEADS), pl.Element(D_MODEL)),
                         lambda i: (0, N_MAIN + GATE_COLS, 0)),
        ],
        out_specs=[pl.BlockSpec((rb, D_MODEL), lambda i: (i, 0)), pl.BlockSpec((LANES, D_MODEL), lambda i: (0, 0))],
        out_shape=[jax.ShapeDtypeStruct((N_MAIN, D_MODEL), BF16), jax.ShapeDtypeStruct((LANES, D_MODEL), BF16)],
        compiler_params=pltpu.CompilerParams(dimension_semantics=("arbitrary",), vmem_limit_bytes=VMEM_LIMIT),
        name="w_prep",
    )(w_in_t, w_in_t, w_in_t)


TAIL = 8


def _conv_carry(u, ext_ref, cw, cb):
    c = u.shape[0]
    ext_ref[TAIL:TAIL + c, :] = u
    acc = cb + u * cw[CONV_K - 1:CONV_K, :]
    for j in range(1, CONV_K):
        acc = acc + ext_ref[TAIL - j:TAIL - j + c, :] * cw[CONV_K - 1 - j:CONV_K - j, :]
    ext_ref[0:TAIL, :] = u[c - TAIL:c, :]
    return acc


def _conv_rows(u, buf, cw, cb, t):
    rows = u.shape[0]
    acc = cb + u * cw[CONV_K - 1:CONV_K, :]
    for j in range(1, CONV_K):
        prev = buf if j == CONV_K - 1 else pltpu.roll(buf, rows + j - (CONV_K - 1), 0)
        acc = acc + jnp.where(t >= j, pltpu.roll(u, j, 0), prev) * cw[CONV_K - 1 - j:CONV_K - j, :]
    return acc


def _head_out(hv, o, z, nw):
    og = _sigmoid(o) * hv
    ms = jnp.mean(og * og, axis=-1, keepdims=True)
    return (og * lax.rsqrt(ms + EPS) * nw) * _silu(z)


def _group_out(y, xs, dsk, z, nw):
    gated = (y + dsk * xs) * _silu(z)
    ms = jnp.mean(gated * gated, axis=-1, keepdims=True)
    return gated * lax.rsqrt(ms + EPS) * nw


def _mlstm_prompt_body(ids, u_ref, v_ref, o_ref, z_ref, gt_ref, gb_ref, cw_ref, cb_ref, wq_ref, wkt_ref, nw_ref,
                       hm_ref, c_ref, n_ref, m_ref, ext_ref, mst_ref):
    c = u_ref.shape[0]
    hps = wq_ref.shape[0]
    h0 = ids[1] * hps

    @pl.when(ids[2] == 0)
    def _():
        c_ref[...] = jnp.zeros_like(c_ref)
        n_ref[...] = jnp.zeros_like(n_ref)
        mst_ref[...] = jnp.full_like(mst_ref, -jnp.inf)
        ext_ref[0:TAIL, :] = jnp.zeros((TAIL, ext_ref.shape[1]), F32)

    ucb = _silu(_conv_carry(u_ref[...], ext_ref, cw_ref[...], cb_ref[...])).astype(BF16)
    vb = v_ref[...].astype(BF16)

    rowi = lax.broadcasted_iota(jnp.int32, (c, c), 0)
    coli = lax.broadcasted_iota(jnp.int32, (c, c), 1)
    tri = coli <= rowi
    eye = coli == rowi
    ig8 = _softcap(gt_ref[GL_IG:GL_IG + M_HEADS, :] + gb_ref[GL_IG:GL_IG + M_HEADS, :])
    lf8 = _log_sigmoid(_softcap(gt_ref[GL_FG:GL_FG + M_HEADS, :] + gb_ref[GL_FG:GL_FG + M_HEADS, :]))
    b8 = _dot_exact(lf8, jnp.where(rowi <= coli, 1.0, 0.0))
    head_row = lax.broadcasted_iota(jnp.int32, (M_HEADS, 1), 0)

    for hh in range(hps):
        cols = slice(hh * M_V, (hh + 1) * M_V)
        pick = head_row == h0 + hh
        ig_row = jnp.sum(jnp.where(pick, ig8, 0.0), axis=0, keepdims=True)
        b_row = jnp.sum(jnp.where(pick, b8, 0.0), axis=0, keepdims=True)
        b_col = _row_to_col(b_row, eye)
        m_prev = mst_ref[hh:hh + 1, :][:, 0:1]

        q = _dot(ucb[:, cols], wq_ref[hh])
        kt = _dot_nt(wkt_ref[hh], ucb[:, cols])
        qb = q.astype(BF16)
        ktb = kt.astype(BF16)

        r_row = ig_row - b_row
        dmat = jnp.where(tri, b_col + r_row, -jnp.inf)
        inter = b_col + m_prev
        m_t = jnp.maximum(inter, jnp.max(dmat, axis=1, keepdims=True))
        a_inter = jnp.exp(inter - m_t)
        s = _dot(qb, ktb) * jnp.exp(dmat - m_t)
        c_old = c_ref[0, hh]
        n_old = n_ref[0, hh]
        num = _dot(s.astype(BF16), vb[:, cols]) + a_inter * _dot(qb, c_old.astype(BF16))
        qn = _dot_nt(qb, jnp.broadcast_to(n_old, (8, M_QK)).astype(BF16))[:, 0:1]
        den = jnp.sum(s, axis=1, keepdims=True) + a_inter * qn
        hv = num / jnp.maximum(jnp.abs(den), jnp.exp(-m_t))
        hm_ref[:, cols] = _head_out(hv, o_ref[:, cols], z_ref[:, cols], nw_ref[:, cols]).astype(BF16)

        b_end = b_row[:, c - 1:c]
        lw_row = b_end + r_row
        m_new = jnp.maximum(b_end + m_prev, jnp.max(lw_row, axis=1, keepdims=True))
        decay = jnp.exp(b_end + m_prev - m_new)
        wk_row = jnp.exp(lw_row - m_new)
        c_ref[0, hh] = decay * c_old + _dot((kt * wk_row).astype(BF16), vb[:, cols])
        n_ref[0, hh] = decay * n_old + _dot_nt(jnp.broadcast_to(wk_row, (8, c)).astype(BF16), ktb)[0:1, :]
        mst_ref[hh:hh + 1, :] = jnp.broadcast_to(m_new, (1, LANES))
        m_ref[0, hh] = jnp.broadcast_to(m_new, (1, LANES))


MLSTM_HPS = 4


def _mlstm_prompt_part(proj, gates_t, bsz, seqlen, gate_bias_col, conv_w, conv_b, wq, wkt, norm_w):
    c = min(PROMPT_CHUNK, seqlen)
    nch = seqlen // c
    hps = MLSTM_HPS
    width = hps * M_V

    def tok(col0):
        return pl.BlockSpec((c, width), lambda b, h, k: (b * nch + k, col0 // width + h))

    head_vec = lambda rows: pl.BlockSpec((rows, width), lambda b, h, k: (0, h))
    return _Part(
        body=_mlstm_prompt_body, ids=_same_ids,
        in_specs=[
            tok(COL_U), tok(COL_V), tok(COL_O), tok(COL_ZM),
            pl.BlockSpec((LANES, c), lambda b, h, k: (0, b * nch + k)),
            pl.BlockSpec((LANES, 1), lambda b, h, k: (0, 0)),
            head_vec(CONV_K), head_vec(1),
            pl.BlockSpec((hps, M_V, M_QK), lambda b, h, k: (h, 0, 0)),
            pl.BlockSpec((hps, M_QK, M_V), lambda b, h, k: (h, 0, 0)),
            head_vec(1),
        ],
        operands=[proj, proj, proj, proj, gates_t, gate_bias_col, conv_w, conv_b, wq, wkt, norm_w],
        out_specs=[
            pl.BlockSpec((c, width), lambda b, h, k: (b * nch + k, h)),
            pl.BlockSpec((1, hps, M_QK, M_V), lambda b, h, k: (b, h, 0, 0)),
            pl.BlockSpec((1, hps, 1, M_QK), lambda b, h, k: (b, h, 0, 0)),
            pl.BlockSpec((1, hps, 1, LANES), lambda b, h, k: (b, h, 0, 0)),
        ],
        out_shape=[
            jax.ShapeDtypeStruct((bsz * seqlen, D_M), BF16),
            jax.ShapeDtypeStruct((bsz, M_HEADS, M_QK, M_V), F32),
            jax.ShapeDtypeStruct((bsz, M_HEADS, 1, M_QK), F32),
            jax.ShapeDtypeStruct((bsz, M_HEADS, 1, LANES), F32),
        ],
        scratch=[pltpu.VMEM((TAIL + c, width), F32), pltpu.VMEM((8, LANES), F32)],
    ), (bsz, M_HEADS // hps, nch)


LOG2E = 1.4426950408889634
HEAD_ROWS = [slice(r * S_HEAD_DIM, (r + 1) * S_HEAD_DIM) for r in range(S_HPG)]


def _ssd_gate_rows(gt_ref, gbc_ref, avc_ref, lane0):
    rows = pl.ds(lane0, S_HPG)
    dt8 = _softplus(gt_ref[rows, :] + gbc_ref[rows, :])
    return dt8, dt8 * (-jnp.exp(avc_ref[rows, :]))


def _ssd_intra(xst, dt8, cum8, cbt):
    c = xst.shape[1]
    a2r = cum8 * LOG2E
    a2c = jnp.concatenate([a2r, jnp.zeros((LANES - S_HPG, c), F32)], axis=0).T
    ys, xdts = [], []
    for r in range(S_HPG):
        xdt = xst[HEAD_ROWS[r], :] * dt8[r:r + 1, :]
        decay = jnp.exp2(jnp.minimum(a2r[r:r + 1, :] - a2c[:, r:r + 1], 0.0))
        ys.append(_dot(xdt.astype(BF16), (cbt * decay).astype(BF16)))
        xdts.append(xdt)
    return ys, xdts


def _ssd_prompt_body(ids, xs_ref, bm_ref, cm_ref, zs_ref, gt_ref, gbc_ref, avc_ref, dsk_ref,
                     cwx_ref, cbx_ref, cwb_ref, cbb_ref, cwc_ref, cbc_ref, nw_ref,
                     ys_ref, h_ref, extx_ref, extb_ref, extc_ref):
    c = xs_ref.shape[0]
    gps = h_ref.shape[1]
    g0 = ids[1] * gps

    @pl.when(ids[2] == 0)
    def _():
        h_ref[...] = jnp.zeros_like(h_ref)
        extx_ref[0:TAIL, :] = jnp.zeros((TAIL, extx_ref.shape[1]), F32)
        extb_ref[0:TAIL, :] = jnp.zeros((TAIL, extb_ref.shape[1]), F32)
        extc_ref[0:TAIL, :] = jnp.zeros((TAIL, extc_ref.shape[1]), F32)

    xs_all = _silu(_conv_carry(xs_ref[...], extx_ref, cwx_ref[...], cbx_ref[...]))
    bmb_all = _silu(_conv_carry(bm_ref[...], extb_ref, cwb_ref[...], cbb_ref[...])).astype(BF16)
    cmb_all = _silu(_conv_carry(cm_ref[...], extc_ref, cwc_ref[...], cbc_ref[...])).astype(BF16)

    src = lax.broadcasted_iota(jnp.int32, (c, c), 0)
    tgt = lax.broadcasted_iota(jnp.int32, (c, c), 1)
    causal = src <= tgt
    upper = jnp.where(causal, 1.0, 0.0)

    for gg in range(gps):
        cols = slice(gg * S_GW, (gg + 1) * S_GW)
        xs = xs_all[:, cols]
        bmb = bmb_all[:, gg * S_STATE:(gg + 1) * S_STATE]
        cmb = cmb_all[:, gg * S_STATE:(gg + 1) * S_STATE]
        lane0 = pl.multiple_of(GL_DT + (g0 + gg) * S_HPG, S_HPG)
        dt8, la8 = _ssd_gate_rows(gt_ref, gbc_ref, avc_ref, lane0)
        cum8 = _dot_exact(la8, upper)
        a_end = cum8[:, c - 1:c]
        w8 = jnp.exp(a_end - cum8)
        ea8 = jnp.exp(cum8)

        cbt = jnp.where(causal, _dot_nt(bmb, cmb), 0.0)
        ys, xdts = _ssd_intra(xs.T, dt8, cum8, cbt)
        h_old = h_ref[0, gg]
        yit = _dot_nt(h_old.astype(BF16), cmb)
        yt = jnp.concatenate([ys[r] + ea8[r:r + 1, :] * yit[HEAD_ROWS[r], :] for r in range(S_HPG)], axis=0)
        ys_ref[:, cols] = _group_out(yt.T, xs, dsk_ref[:, cols], zs_ref[:, cols], nw_ref[:, cols]).astype(BF16)

        xwt = jnp.concatenate([(xdts[r] * w8[r:r + 1, :]).astype(BF16) for r in range(S_HPG)], axis=0)
        upd = _dot(xwt, bmb)
        ea_end = jnp.exp(a_end)
        for r in range(S_HPG):
            h_ref[0, gg, HEAD_ROWS[r], :] = ea_end[r:r + 1, :] * h_old[HEAD_ROWS[r], :] + upd[HEAD_ROWS[r], :]


SSD_GPS = 4


def _ssd_prompt_part(proj, gates_t, bsz, seqlen, gate_bias_col, a_col, dsk_row, conv_w, conv_b, norm_w):
    c = min(PROMPT_CHUNK, seqlen)
    nch = seqlen // c
    gps = SSD_GPS
    tok = lambda width, col0: pl.BlockSpec((c, gps * width), lambda b, g, k: (b * nch + k, col0 // (gps * width) + g))
    vec = lambda rows, width, col0: pl.BlockSpec((rows, gps * width), lambda b, g, k: (0, col0 // (gps * width) + g))
    col = pl.BlockSpec((LANES, 1), lambda b, g, k: (0, 0))
    return _Part(
        body=_ssd_prompt_body, ids=_same_ids,
        in_specs=[
            tok(S_GW, COL_XS), tok(S_STATE, COL_BM), tok(S_STATE, COL_CM), tok(S_GW, COL_ZS),
            pl.BlockSpec((LANES, c), lambda b, g, k: (0, b * nch + k)),
            col, col, vec(1, S_GW, 0),
            vec(CONV_K, S_GW, 0), vec(1, S_GW, 0),
            vec(CONV_K, S_STATE, D_S), vec(1, S_STATE, D_S),
            vec(CONV_K, S_STATE, D_S + S_GROUPS * S_STATE), vec(1, S_STATE, D_S + S_GROUPS * S_STATE),
            vec(1, S_GW, 0),
        ],
        operands=[proj, proj, proj, proj, gates_t, gate_bias_col, a_col, dsk_row,
                  conv_w, conv_b, conv_w, conv_b, conv_w, conv_b, norm_w],
        out_specs=[
            pl.BlockSpec((c, gps * S_GW), lambda b, g, k: (b * nch + k, g)),
            pl.BlockSpec((1, gps, S_GW, S_STATE), lambda b, g, k: (b, g, 0, 0)),
        ],
        out_shape=[
            jax.ShapeDtypeStruct((bsz * seqlen, D_S), BF16),
            jax.ShapeDtypeStruct((bsz, S_GROUPS, S_GW, S_STATE), F32),
        ],
        scratch=[pltpu.VMEM((TAIL + c, gps * S_GW), F32), pltpu.VMEM((TAIL + c, gps * S_STATE), F32),
                 pltpu.VMEM((TAIL + c, gps * S_STATE), F32)],
    ), (bsz, S_GROUPS // gps, nch)


SEQ4 = 4


def _row_masks(rows):
    rowi = lax.broadcasted_iota(jnp.int32, (rows, rows), 0)
    coli = lax.broadcasted_iota(jnp.int32, (rows, rows), 1)
    same = _idiv(rowi, SEQ4) == _idiv(coli, SEQ4)
    return same, same & (coli <= rowi), coli == rowi, same & (_imod(coli, SEQ4) == SEQ4 - 1)


def _seg_cumsum(x, t):
    out = x
    for j in range(1, SEQ4):
        out = out + jnp.where(t >= j, pltpu.roll(x, j, 0), 0.0)
    return out


def _mlstm_sample_body(ids, u_ref, v_ref, o_ref, z_ref, gt_ref, gb_ref, buf_ref, mrow_ref, c0_ref, n0_ref,
                       cw_ref, cb_ref, wq_ref, wkt_ref, nw_ref,
                       hm_ref, c_ref, n_ref, m_ref):
    h = ids[0]
    rows = u_ref.shape[0]
    nseq = rows // SEQ4
    rid = lax.broadcasted_iota(jnp.int32, (rows, 1), 0)
    t = _imod(rid, SEQ4)
    seq = _idiv(rid, SEQ4)
    seq_lane = _idiv(lax.broadcasted_iota(jnp.int32, (1, rows), 1), SEQ4)

    ucb = _silu(_conv_rows(u_ref[...], buf_ref[...], cw_ref[...], cb_ref[...], t)).astype(BF16)
    q = _dot(ucb, wq_ref[0])
    kt = _dot_nt(wkt_ref[0], ucb)
    qb = q.astype(BF16)
    vb = v_ref[...].astype(BF16)

    capped = _softcap(gt_ref[...] + gb_ref[...])
    ig_col = _lane_pick(capped, GL_IG + h)
    b_col = _lane_pick(_seg_cumsum(_log_sigmoid(capped), t), GL_FG + h)
    m_prev = _lane_pick(mrow_ref[...], h)

    same, tri, eye, last = _row_masks(rows)
    b_row = _col_to_row(b_col, eye)
    ig_row = _col_to_row(ig_col, eye)
    dmat = jnp.where(tri, b_col - b_row + ig_row, -jnp.inf)
    inter = b_col + m_prev
    m_t = jnp.maximum(inter, jnp.max(dmat, axis=1, keepdims=True))
    a_inter = jnp.exp(inter - m_t)
    s = _dot(qb, kt.astype(BF16)) * jnp.exp(dmat - m_t)

    b_end = jnp.sum(jnp.where(last, b_row, 0.0), axis=1, keepdims=True)
    lw = b_end - b_col + ig_col
    m_new = jnp.maximum(b_end + m_prev, jnp.max(jnp.where(same, _col_to_row(lw, eye), -jnp.inf), axis=1, keepdims=True))
    decay = jnp.exp(b_end + m_prev - m_new)
    kwt = kt * _col_to_row(jnp.exp(lw - m_new), eye)
    eye_k = (lax.broadcasted_iota(jnp.int32, (M_QK, M_QK), 0) == lax.broadcasted_iota(jnp.int32, (M_QK, M_QK), 1))

    num_inter = jnp.zeros((rows, M_V), F32)
    qn = jnp.zeros((rows, 1), F32)
    for bi in range(nseq):
        own = seq == bi
        c_old = c0_ref[bi, 0]
        n_old = n0_ref[bi, 0]
        num_inter = jnp.where(own, _dot(qb, c_old.astype(BF16)), num_inter)
        qn = jnp.where(own, jnp.sum(q * n_old, axis=1, keepdims=True), qn)
        kw_b = jnp.where(seq_lane == bi, kwt, 0.0)
        d_b = jnp.sum(jnp.where(own & (t == SEQ4 - 1), decay, 0.0), axis=0, keepdims=True)
        c_ref[bi, 0] = d_b * c_old + _dot(kw_b.astype(BF16), vb)
        n_ref[bi, 0] = d_b * n_old + _col_to_row(jnp.sum(kw_b, axis=1, keepdims=True), eye_k)

    num = _dot(s.astype(BF16), vb) + a_inter * num_inter
    den = jnp.sum(s, axis=1, keepdims=True) + a_inter * qn
    hv = num / jnp.maximum(jnp.abs(den), jnp.exp(-m_t))
    hm_ref[...] = _head_out(hv, o_ref[...], z_ref[...], nw_ref[...]).astype(BF16)
    m_ref[0] = jnp.broadcast_to(m_new, (rows, LANES))


def _mlstm_sample_part(proj, gates, nseq_total, gate_bias, buf_rows, m_rows, c0, n0, conv_w, conv_b, wq, wkt, norm_w,
                       bb, ids=_same_ids):
    rows = bb * SEQ4
    nblk = nseq_total // bb
    spec = functools.partial(_on_grid, ids)
    tok = lambda col0: spec((rows, M_V), lambda h, i: (i, col0 // M_V + h))
    head_vec = lambda r: spec((r, M_V), lambda h, i: (0, h))
    c_spec = spec((bb, 1, M_QK, M_V), lambda h, i: (i, h, 0, 0))
    n_spec = spec((bb, 1, 1, M_QK), lambda h, i: (i, h, 0, 0))
    return _Part(
        body=_mlstm_sample_body, ids=ids,
        in_specs=[
            tok(COL_U), tok(COL_V), tok(COL_O), tok(COL_ZM),
            spec((rows, LANES), lambda h, i: (i, 0)),
            spec((1, LANES), lambda h, i: (0, 0)),
            spec((rows, M_V), lambda h, i: (i, h)),
            spec((rows, LANES), lambda h, i: (i, 0)),
            c_spec, n_spec,
            head_vec(CONV_K), head_vec(1),
            spec((1, M_V, M_QK), lambda h, i: (h, 0, 0)),
            spec((1, M_QK, M_V), lambda h, i: (h, 0, 0)),
            head_vec(1),
        ],
        operands=[proj, proj, proj, proj, gates, gate_bias, buf_rows, m_rows, c0, n0, conv_w, conv_b, wq, wkt, norm_w],
        out_specs=[
            spec((rows, M_V), lambda h, i: (i, h)),
            c_spec, n_spec,
            spec((1, rows, LANES), lambda h, i: (h, i, 0)),
        ],
        out_shape=[
            jax.ShapeDtypeStruct((nseq_total * SEQ4, D_M), BF16),
            jax.ShapeDtypeStruct((nseq_total, M_HEADS, M_QK, M_V), F32),
            jax.ShapeDtypeStruct((nseq_total, M_HEADS, 1, M_QK), F32),
            jax.ShapeDtypeStruct((M_HEADS, nseq_total * SEQ4, LANES), F32),
        ],
        scratch=[],
    ), (M_HEADS, nblk)


def _ssd_sample_body(xs_ref, bm_ref, cm_ref, zs_ref, gt_ref, gbc_ref, avc_ref, dsk_ref,
                     bufx_ref, bufb_ref, bufc_ref, h0_ref,
                     cwx_ref, cbx_ref, cwb_ref, cbb_ref, cwc_ref, cbc_ref, nw_ref,
                     ys_ref, h_ref, yit_ref, xwt_ref):
    g = pl.program_id(0)
    rows = xs_ref.shape[0]
    nseq = rows // SEQ4
    rid = lax.broadcasted_iota(jnp.int32, (rows, 1), 0)
    t = _imod(rid, SEQ4)
    seq = _idiv(rid, SEQ4)
    tok_lane = lax.broadcasted_iota(jnp.int32, (1, rows), 1)

    xs = _silu(_conv_rows(xs_ref[...], bufx_ref[...], cwx_ref[...], cbx_ref[...], t))
    bmb = _silu(_conv_rows(bm_ref[...], bufb_ref[...], cwb_ref[...], cbb_ref[...], t)).astype(BF16)
    cmb = _silu(_conv_rows(cm_ref[...], bufc_ref[...], cwc_ref[...], cbc_ref[...], t)).astype(BF16)

    lane0 = pl.multiple_of(GL_DT + g * S_HPG, S_HPG)
    dt8, la8 = _ssd_gate_rows(gt_ref, gbc_ref, avc_ref, lane0)
    src = lax.broadcasted_iota(jnp.int32, (rows, rows), 0)
    tgt = lax.broadcasted_iota(jnp.int32, (rows, rows), 1)
    same = _idiv(src, SEQ4) == _idiv(tgt, SEQ4)
    causal = same & (src <= tgt)
    cum8 = _dot_exact(la8, jnp.where(causal, 1.0, 0.0))
    a_end = _dot_exact(cum8, jnp.where(same & (_imod(src, SEQ4) == SEQ4 - 1), 1.0, 0.0))
    w8 = jnp.exp(a_end - cum8)
    ea8 = jnp.exp(cum8)
    ea_end = jnp.exp(a_end)

    cbt = jnp.where(causal, _dot_nt(bmb, cmb), 0.0)
    ys, xdts = _ssd_intra(xs.T, dt8, cum8, cbt)
    xwt_ref[...] = jnp.concatenate([(xdts[r] * w8[r:r + 1, :]).astype(BF16) for r in range(S_HPG)], axis=0)
    yit_ref[...] = jnp.zeros_like(yit_ref)

    def per_sequence(bi, carry):
        h_old = h0_ref[bi, 0]
        own = _idiv(tok_lane, SEQ4) == bi
        yit_ref[...] += _dot_nt(h_old.astype(BF16), jnp.where(seq == bi, cmb, jnp.zeros_like(cmb)))
        upd = _dot(jnp.where(own, xwt_ref[...], jnp.zeros_like(xwt_ref)), bmb)
        d8 = jnp.sum(jnp.where(tok_lane == bi * SEQ4 + SEQ4 - 1, ea_end, 0.0), axis=1, keepdims=True)
        for r in range(S_HPG):
            h_ref[bi, 0, HEAD_ROWS[r], :] = d8[r:r + 1, :] * h_old[HEAD_ROWS[r], :] + upd[HEAD_ROWS[r], :]
        return carry

    lax.fori_loop(0, nseq, per_sequence, 0, unroll=8 if nseq % 8 == 0 else 1)
    yit = yit_ref[...]
    yt = jnp.concatenate([ys[r] + ea8[r:r + 1, :] * yit[HEAD_ROWS[r], :] for r in range(S_HPG)], axis=0)
    ys_ref[...] = _group_out(yt.T, xs, dsk_ref[...], zs_ref[...], nw_ref[...]).astype(BF16)


SSD_SAMPLE_BB = 32


def _ssd_sample(proj, gates_t, nseq_total, gate_bias_col, a_col, dsk_row, buf_rows, h0, conv_w, conv_b, norm_w):
    bb = min(SSD_SAMPLE_BB, nseq_total)
    rows = bb * SEQ4
    nblk = nseq_total // bb
    tok = lambda width, col0: pl.BlockSpec((rows, width), lambda g, i: (i, col0 // width + g))
    vec = lambda r, width, col0: pl.BlockSpec((r, width), lambda g, i: (0, col0 // width + g))
    col = pl.BlockSpec((LANES, 1), lambda g, i: (0, 0))
    h_spec = pl.BlockSpec((bb, 1, S_GW, S_STATE), lambda g, i: (i, g, 0, 0))
    off_b = D_S
    off_c = D_S + S_GROUPS * S_STATE
    return pl.pallas_call(
        _ssd_sample_body,
        grid=(S_GROUPS, nblk),
        in_specs=[
            tok(S_GW, COL_XS), tok(S_STATE, COL_BM), tok(S_STATE, COL_CM), tok(S_GW, COL_ZS),
            pl.BlockSpec((LANES, rows), lambda g, i: (0, i)),
            col, col, vec(1, S_GW, 0),
            tok(S_GW, 0), tok(S_STATE, off_b), tok(S_STATE, off_c),
            h_spec,
            vec(CONV_K, S_GW, 0), vec(1, S_GW, 0),
            vec(CONV_K, S_STATE, off_b), vec(1, S_STATE, off_b),
            vec(CONV_K, S_STATE, off_c), vec(1, S_STATE, off_c),
            vec(1, S_GW, 0),
        ],
        out_specs=[pl.BlockSpec((rows, S_GW), lambda g, i: (i, g)), h_spec],
        out_shape=[
            jax.ShapeDtypeStruct((nseq_total * SEQ4, D_S), BF16),
            jax.ShapeDtypeStruct((nseq_total, S_GROUPS, S_GW, S_STATE), F32),
        ],
        scratch_shapes=[pltpu.VMEM((S_GW, rows), F32), pltpu.VMEM((S_GW, rows), BF16)],
        compiler_params=pltpu.CompilerParams(
            dimension_semantics=("parallel", "parallel"), vmem_limit_bytes=VMEM_LIMIT),
        name="ssd_sample",
    )(proj, proj, proj, proj, gates_t, gate_bias_col, a_col, dsk_row, buf_rows, buf_rows, buf_rows, h0,
      conv_w, conv_b, conv_w, conv_b, conv_w, conv_b, norm_w)


OUT_TM, OUT_TN = 512, 512


def _out_proj_body(ids, x_ref, hm_ref, ys_ref, wt_ref, wb_ref, gn_ref, out_ref, x1g_ref, ssp_ref):
    x1 = x_ref[...] + _dot(hm_ref[...], wt_ref[...]) + _dot(ys_ref[...], wb_ref[...])
    out_ref[...] = x1
    x1g_ref[...] = (x1 * gn_ref[...]).astype(BF16)
    ssp_ref[0] = jnp.sum(x1 * x1, axis=1, keepdims=True)


def _out_proj_part(x, hm, ys, w_out, ple_norm, ids=_same_ids):
    m = x.shape[0]
    tm, tn = min(OUT_TM, m), OUT_TN
    nj = D_MODEL // tn
    spec = functools.partial(_on_grid, ids)
    return _Part(
        body=_out_proj_body, ids=ids,
        in_specs=[
            spec((tm, tn), lambda i, j: (i, j)),
            spec((tm, D_M), lambda i, j: (i, 0)),
            spec((tm, D_S), lambda i, j: (i, 0)),
            spec((D_M, tn), lambda i, j: (0, j)),
            spec((D_S, tn), lambda i, j: (1, j)),
            spec((1, tn), lambda i, j: (0, j)),
        ],
        operands=[x, hm, ys, w_out, w_out, ple_norm],
        out_specs=[
            spec((tm, tn), lambda i, j: (i, j)),
            spec((tm, tn), lambda i, j: (i, j)),
            spec((1, tm, 1), lambda i, j: (j, i, 0)),
        ],
        out_shape=[jax.ShapeDtypeStruct((m, D_MODEL), F32), jax.ShapeDtypeStruct((m, D_MODEL), BF16),
                   jax.ShapeDtypeStruct((nj, m, 1), F32)],
        scratch=[],
    ), (m // tm, nj)


def _ple_body(x1g_ref, ss1_ref, x1t_ref, p_ref, wg_ref, wp_ref, fn_ref, out_ref, ss_ref):
    j = pl.program_id(1)
    nj = pl.num_programs(1)
    tn = wg_ref.shape[1]

    @pl.when(j == 0)
    def _():
        ss_ref[...] = jnp.zeros_like(ss_ref)

    rs = lax.rsqrt(jnp.sum(ss1_ref[...], axis=0) * (1.0 / D_MODEL) + EPS)
    gate = _sigmoid(rs * _dot(x1g_ref[...], wg_ref[...]))
    x2 = x1t_ref[...] + gate * _dot(p_ref[...].astype(BF16), wp_ref[...])
    ss_ref[...] += jnp.sum(x2 * x2, axis=1, keepdims=True)
    for jj in range(out_ref.shape[1] // tn):
        @pl.when(j == jj)
        def _(jj=jj):
            out_ref[:, jj * tn:(jj + 1) * tn] = x2

    @pl.when(j == nj - 1)
    def _():
        fn = fn_ref[...]

        def body(r, carry):
            sl = pl.ds(pl.multiple_of(r * NORM_ROWS, NORM_ROWS), NORM_ROWS)
            scale = lax.rsqrt(ss_ref[sl, :] * (1.0 / D_MODEL) + EPS)
            out_ref[sl, :] = out_ref[sl, :] * scale * fn
            return carry
        lax.fori_loop(0, out_ref.shape[0] // NORM_ROWS, body, 0)


def _ple(x1, x1g, ss1, p, w_gate, w_proj, final_norm):
    m = x1.shape[0]
    tm = min(512, m)
    tn = 1024
    return pl.pallas_call(
        _ple_body,
        grid=(m // tm, D_MODEL // tn),
        in_specs=[
            pl.BlockSpec((tm, D_MODEL), lambda i, j: (i, 0)),
            pl.BlockSpec((ss1.shape[0], tm, 1), lambda i, j: (0, i, 0)),
            pl.BlockSpec((tm, tn), lambda i, j: (i, j)),
            pl.BlockSpec((tm, PLE_DIM), lambda i, j: (i, 0)),
            pl.BlockSpec((D_MODEL, tn), lambda i, j: (0, j)),
            pl.BlockSpec((PLE_DIM, tn), lambda i, j: (0, j)),
            pl.BlockSpec((1, D_MODEL), lambda i, j: (0, 0)),
        ],
        out_specs=pl.BlockSpec((tm, D_MODEL), lambda i, j: (i, 0)),
        out_shape=jax.ShapeDtypeStruct((m, D_MODEL), F32),
        scratch_shapes=[pltpu.VMEM((tm, 1), F32)],
        compiler_params=pltpu.CompilerParams(
            dimension_semantics=("parallel", "arbitrary"), vmem_limit_bytes=VMEM_LIMIT),
        name="ple",
    )(x1g, ss1, x1, p, w_gate, w_proj, final_norm)


def _pad_lanes(v, offset):
    out = jnp.zeros((1, LANES), F32)
    return lax.dynamic_update_slice(out, v.reshape(1, -1).astype(F32), (0, offset))


def kernel(x_prompt, x_sample, p_prompt, p_sample, state_m_C, state_m_n, state_m_m, state_m_conv, state_s_ssm, state_s_conv, norm_in, w_in, b_ig, b_fg, m_conv_w, m_conv_b, w_q, w_k, m_norm, s_conv_w, s_conv_b, dt_bias, a_log, d_skip, s_norm, w_out, ple_proj, ple_gate, ple_norm, final_norm):
    assert w_in.shape[0] == 1, "single layer"
    bsz, seqlen, _ = x_prompt.shape
    nseq, dec_seq, _ = x_sample.shape
    assert dec_seq == SEQ4

    assert w_in.shape[2] == N_MAIN + GATE_COLS + S_HEADS
    w_in_t = jnp.swapaxes(w_in, 1, 2)
    gate_bias = jnp.concatenate(
        [b_ig[0], b_fg[0], dt_bias[0], jnp.zeros((LANES - 2 * M_HEADS - S_HEADS,), F32)]).reshape(1, LANES)
    gate_bias_col = gate_bias.reshape(LANES, 1)
    a_col = _pad_lanes(a_log[0], GL_DT).reshape(LANES, 1)
    dsk_row = jnp.repeat(d_skip[0], S_HEAD_DIM).reshape(1, D_S)
    wq = w_q[0].astype(BF16)
    wkt = (jnp.swapaxes(w_k[0], 1, 2) * (M_QK ** -0.5)).astype(BF16)
    ple_proj_b = ple_proj[0].astype(BF16)
    norm_in_r = norm_in[0].reshape(1, D_MODEL)
    ple_norm_r = ple_norm[0].reshape(1, D_MODEL)
    final_norm_r = final_norm.reshape(1, D_MODEL)
    m_conv_b_r = m_conv_b[0].reshape(1, D_M)
    s_conv_b_r = s_conv_b[0].reshape(1, CONV_DIM)
    m_norm_r = m_norm[0].reshape(1, D_M)
    s_norm_r = s_norm[0].reshape(1, D_S)

    xp = x_prompt.reshape(bsz * seqlen, D_MODEL)
    xs_ = x_sample.reshape(nseq * SEQ4, D_MODEL)
    in_part, in_grid = _in_proj_part(_rmsnorm_cast(xp, norm_in_r), w_in_t)
    (proj_p, _, gates_pt), = _call_parts("in_proj", in_grid, [in_part])

    mlstm_part, mlstm_grid = _mlstm_prompt_part(
        proj_p, gates_pt, bsz, seqlen, gate_bias_col, m_conv_w[0], m_conv_b_r, wq, wkt, m_norm_r)
    nsteps = mlstm_grid[0] * mlstm_grid[1] * mlstm_grid[2]
    step = _linear_step(mlstm_grid)
    (hm_p, pc, pn, pm), (w_out_b,), (ple_gate_b,) = _call_parts(
        "mlstm_prompt_casts", mlstm_grid,
        [mlstm_part, _cast_part(w_out[0], nsteps, step), _cast_part(ple_gate[0], nsteps, step)])

    ssd_part, ssd_grid = _ssd_prompt_part(
        proj_p, gates_pt, bsz, seqlen, gate_bias_col, a_col, dsk_row, s_conv_w[0], s_conv_b_r, s_norm_r)
    (ys_p, ph), = _call_parts("ssd_prompt", ssd_grid, [ssd_part])
    in_part_s, in_grid_s = _in_proj_part(_rmsnorm_cast(xs_, norm_in_r), w_in_t)
    (proj_s, gates_s, gates_st), = _call_parts("in_proj", in_grid_s, [in_part_s])

    proj_p3 = proj_p.reshape(bsz, seqlen, N_MAIN)
    tail = seqlen - (CONV_K - 1)
    p_mconv = proj_p3[:, tail:, COL_U:COL_U + D_M]
    p_sconv = proj_p3[:, tail:, COL_XS:COL_XS + CONV_DIM]

    pad_rows = lambda st: jnp.pad(st, ((0, 0), (0, SEQ4 - (CONV_K - 1)), (0, 0))).reshape(nseq * SEQ4, st.shape[-1])
    m_rows = jnp.pad(jnp.repeat(state_m_m[0], SEQ4, axis=0), ((0, 0), (0, LANES - M_HEADS)))

    sample_args = (proj_s, gates_s, nseq, gate_bias, pad_rows(state_m_conv[0]), m_rows, state_m_C[0],
                   state_m_n[0].reshape(nseq, M_HEADS, 1, M_QK), m_conv_w[0], m_conv_b_r, wq, wkt, m_norm_r)
    out_part, out_grid = _out_proj_part(xp, hm_p, ys_p, w_out_b, ple_norm_r)
    bb = SAMPLE_BB
    while bb > 1 and (nseq % bb or M_HEADS * (nseq // bb) < out_grid[0] * out_grid[1]):
        bb //= 2
    if M_HEADS * (nseq // bb) == out_grid[0] * out_grid[1]:
        nblk = nseq // bb
        as_head_block = lambda i, j: ((i * out_grid[1] + j) // nblk, (i * out_grid[1] + j) % nblk)
        scan_part, _ = _mlstm_sample_part(*sample_args, bb, ids=as_head_block)
        (x1_p, x1g_p, ss1_p), (hm_s, sc, sn, sm) = _call_parts("out_proj_mlstm_sample", out_grid, [out_part, scan_part])
    else:
        (x1_p, x1g_p, ss1_p), = _call_parts("out_proj", out_grid, [out_part])
        scan_part, scan_grid = _mlstm_sample_part(*sample_args, bb)
        (hm_s, sc, sn, sm), = _call_parts("mlstm_sample", scan_grid, [scan_part])
    y_p = _ple(x1_p, x1g_p, ss1_p, p_prompt[0].reshape(bsz * seqlen, PLE_DIM), ple_gate_b, ple_proj_b, final_norm_r)

    ys_s, sh = _ssd_sample(
        proj_s, gates_st, nseq, gate_bias_col, a_col, dsk_row, pad_rows(state_s_conv[0]),
        state_s_ssm[0].reshape(nseq, S_GROUPS, S_GW, S_STATE), s_conv_w[0], s_conv_b_r, s_norm_r)
    out_part_s, out_grid_s = _out_proj_part(xs_, hm_s, ys_s, w_out_b, ple_norm_r)
    (x1_s, x1g_s, ss1_s), = _call_parts("out_proj", out_grid_s, [out_part_s])
    y_s = _ple(x1_s, x1g_s, ss1_s, p_sample[0].reshape(nseq * SEQ4, PLE_DIM), ple_gate_b, ple_proj_b, final_norm_r)

    keep = SEQ4 - (CONV_K - 1)
    s_mconv = proj_s[:, COL_U:COL_U + D_M].reshape(nseq, SEQ4, D_M)[:, keep:]
    s_sconv = proj_s[:, COL_XS:COL_XS + CONV_DIM].reshape(nseq, SEQ4, CONV_DIM)[:, keep:]
    s_m = jnp.transpose(sm[:, SEQ4 - 1::SEQ4, 0])

    return (
        y_p.reshape(bsz, seqlen, D_MODEL),
        y_s.reshape(nseq, SEQ4, D_MODEL),
        pc[None], pn.reshape(1, bsz, M_HEADS, M_QK), pm[:, :, 0, 0][None], p_mconv[None],
        ph.reshape(1, bsz, S_HEADS, S_HEAD_DIM, S_STATE), p_sconv[None],
        sc[None], sn.reshape(1, nseq, M_HEADS, M_QK), s_m[None], s_mconv[None],
        sh.reshape(1, nseq, S_HEADS, S_HEAD_DIM, S_STATE), s_sconv[None],
    )
```

```python
import functools
from typing import Callable, NamedTuple

import jax
import jax.numpy as jnp
from jax import lax
from jax.experimental import pallas as pl
from jax.experimental.pallas import tpu as pltpu

F32 = jnp.float32
BF16 = jnp.bfloat16
HIGHEST = lax.Precision.HIGHEST

D_MODEL = 4096
D_M = 4096
D_S = 4096
M_HEADS = 8
M_V = 512
M_QK = 256
S_HEADS = 64
S_HEAD_DIM = 64
S_STATE = 128
S_GROUPS = 8
S_HPG = 8
S_GW = S_HPG * S_HEAD_DIM
CONV_DIM = D_S + 2 * S_GROUPS * S_STATE
CONV_K = 4
PLE_DIM = 256
GATE_CAP = 15.0
EPS = 1e-6

COL_U, COL_V, COL_O, COL_ZM, COL_ZS, COL_XS = 0, 4096, 8192, 12288, 16384, 20480
COL_BM = COL_XS + D_S
COL_CM = COL_BM + S_GROUPS * S_STATE
N_MAIN = COL_XS + CONV_DIM
LANES = 128
GL_IG, GL_FG, GL_DT = 0, M_HEADS, 2 * M_HEADS

PROMPT_CHUNK = 256
SAMPLE_BB = 16
VMEM_LIMIT = 62 * 1024 * 1024

NT_DIMS = (((1,), (1,)), ((), ()))


def _dot(a, b):
    return jnp.dot(a, b, preferred_element_type=F32)


def _dot_nt(a, b):
    return lax.dot_general(a, b, NT_DIMS, preferred_element_type=F32)


def _dot_exact(a, b):
    return jnp.dot(a, b, preferred_element_type=F32, precision=HIGHEST)


def _sigmoid(x):
    return 0.5 * jnp.tanh(0.5 * x) + 0.5


def _silu(x):
    return x * _sigmoid(x)


def _softcap(x):
    return GATE_CAP * jnp.tanh(x / GATE_CAP)


def _softplus(x):
    return jnp.maximum(x, 0.0) + jnp.log1p(jnp.exp(-jnp.abs(x)))


def _log_sigmoid(x):
    return -_softplus(-x)


def _idiv(x, d):
    return lax.shift_right_logical(x, d.bit_length() - 1)


def _imod(x, d):
    return lax.bitwise_and(x, d - 1)


def _lane_pick(x, lane_idx):
    lane = lax.broadcasted_iota(jnp.int32, (1, x.shape[1]), 1)
    return jnp.sum(jnp.where(lane == lane_idx, x, 0.0), axis=1, keepdims=True)


def _col_to_row(col, eye):
    return jnp.sum(jnp.where(eye, col, 0.0), axis=0, keepdims=True)


def _row_to_col(row, eye):
    return jnp.sum(jnp.where(eye, row, 0.0), axis=1, keepdims=True)


class _Part(NamedTuple):
    body: Callable
    ids: Callable
    in_specs: list
    operands: list
    out_specs: list
    out_shape: list
    scratch: list


def _same_ids(*g):
    return g


def _on_grid(ids, block_shape, index_map, **kwargs):
    return pl.BlockSpec(block_shape, lambda *g: index_map(*ids(*g)), **kwargs)


def _call_parts(name, grid, parts):
    n_in = [len(p.in_specs) for p in parts]
    n_out = [len(p.out_specs) for p in parts]
    n_scr = [len(p.scratch) for p in parts]

    def body(*refs):
        gids = tuple(pl.program_id(a) for a in range(len(grid)))
        ins, outs, scr = refs[:sum(n_in)], refs[sum(n_in):sum(n_in) + sum(n_out)], refs[sum(n_in) + sum(n_out):]
        for p, part in enumerate(parts):
            a, b, c = sum(n_in[:p]), sum(n_out[:p]), sum(n_scr[:p])
            part.body(part.ids(*gids), *ins[a:a + n_in[p]], *outs[b:b + n_out[p]], *scr[c:c + n_scr[p]])

    results = pl.pallas_call(
        body,
        grid=grid,
        in_specs=[s for p in parts for s in p.in_specs],
        out_specs=[s for p in parts for s in p.out_specs],
        out_shape=[s for p in parts for s in p.out_shape],
        scratch_shapes=[s for p in parts for s in p.scratch],
        compiler_params=pltpu.CompilerParams(
            dimension_semantics=("arbitrary",) * len(grid), vmem_limit_bytes=VMEM_LIMIT),
        name=name,
    )(*[o for p in parts for o in p.operands])
    return [results[sum(n_out[:p]):sum(n_out[:p]) + n_out[p]] for p in range(len(parts))]


def _linear_step(grid):
    def lin(*g):
        s = g[0]
        for size, idx in zip(grid[1:], g[1:]):
            s = s * size + idx
        return s
    return lin


def _cast_body(ids, src_ref, dst_ref):
    dst_ref[...] = src_ref[...].astype(dst_ref.dtype)


def _cast_part(w, nsteps, step):
    rows = w.shape[0] // nsteps
    assert rows * nsteps == w.shape[0] and rows % 16 == 0
    spec = pl.BlockSpec((rows, w.shape[1]), lambda *g: (step(*g), 0))
    return _Part(body=_cast_body, ids=_same_ids, in_specs=[spec], operands=[w], out_specs=[spec],
                 out_shape=[jax.ShapeDtypeStruct(w.shape, BF16)], scratch=[])


NORM_ROWS = 32


def _rmsnorm_rows_to(x_ref, g, dst_ref):
    def body(r, carry):
        sl = pl.ds(pl.multiple_of(r * NORM_ROWS, NORM_ROWS), NORM_ROWS)
        x = x_ref[sl, :]
        ms = jnp.mean(x * x, axis=-1, keepdims=True)
        dst_ref[sl, :] = (x * lax.rsqrt(ms + EPS) * g).astype(BF16)
        return carry
    lax.fori_loop(0, x_ref.shape[0] // NORM_ROWS, body, 0)


def _in_proj_body(x_ref, g_ref, w_ref, wg_ref, out_ref, gates_ref, gatest_ref, xn_ref):
    @pl.when(pl.program_id(1) == 0)
    def _():
        _rmsnorm_rows_to(x_ref, g_ref[...], xn_ref)
        gates = _dot_nt(xn_ref[...], wg_ref[...])
        gates_ref[...] = gates
        gatest_ref[...] = gates.T
    out_ref[...] = _dot_nt(xn_ref[...], w_ref[...])


def _in_proj(x, norm_w, w_main, w_gate):
    m = x.shape[0]
    tm = min(512, m)
    tn = 1024
    grid = (m // tm, N_MAIN // tn)
    return pl.pallas_call(
        _in_proj_body,
        grid=grid,
        in_specs=[
            pl.BlockSpec((tm, D_MODEL), lambda i, j: (i, 0)),
            pl.BlockSpec((1, D_MODEL), lambda i, j: (0, 0)),
            pl.BlockSpec((tn, D_MODEL), lambda i, j: (j, 0)),
            pl.BlockSpec((LANES, D_MODEL), lambda i, j: (0, 0)),
        ],
        out_specs=[
            pl.BlockSpec((tm, tn), lambda i, j: (i, j)),
            pl.BlockSpec((tm, LANES), lambda i, j: (i, 0)),
            pl.BlockSpec((LANES, tm), lambda i, j: (0, i)),
        ],
        out_shape=[jax.ShapeDtypeStruct((m, N_MAIN), F32), jax.ShapeDtypeStruct((m, LANES), F32),
                   jax.ShapeDtypeStruct((LANES, m), F32)],
        scratch_shapes=[pltpu.VMEM((tm, D_MODEL), BF16)],
        compiler_params=pltpu.CompilerParams(
            dimension_semantics=("parallel", "arbitrary"), vmem_limit_bytes=VMEM_LIMIT),
        name="in_proj",
    )(x, norm_w, w_main, w_gate)


def _rmsnorm_cast_body(x_ref, g_ref, out_ref):
    _rmsnorm_rows_to(x_ref, g_ref[...], out_ref)


def _rmsnorm_cast(x, norm_w):
    m = x.shape[0]
    tm = min(256, m)
    return pl.pallas_call(
        _rmsnorm_cast_body,
        grid=(m // tm,),
        in_specs=[pl.BlockSpec((tm, D_MODEL), lambda i: (i, 0)), pl.BlockSpec((1, D_MODEL), lambda i: (0, 0))],
        out_specs=pl.BlockSpec((tm, D_MODEL), lambda i: (i, 0)),
        out_shape=jax.ShapeDtypeStruct((m, D_MODEL), BF16),
        compiler_params=pltpu.CompilerParams(dimension_semantics=("parallel",), vmem_limit_bytes=VMEM_LIMIT),
        name="rmsnorm_cast",
    )(x, norm_w)


def _in_proj_w32_body(ids, xn_ref, w_ref, ga_ref, gb_ref, out_ref, gates_ref, gatest_ref, wg_ref):
    @pl.when(ids[1] == 0)
    def _():
        wg_ref[0:GL_DT, :] = ga_ref[0].astype(BF16)
        wg_ref[GL_DT:GL_DT + S_HEADS, :] = gb_ref[0].astype(BF16)
        wg_ref[GL_DT + S_HEADS:LANES, :] = jnp.zeros((LANES - GL_DT - S_HEADS, D_MODEL), BF16)
        gates = _dot_nt(xn_ref[...], wg_ref[...])
        gates_ref[...] = gates
        gatest_ref[...] = gates.T
    out_ref[...] = _dot_nt(xn_ref[...], w_ref[0].astype(BF16))


IN_TN = 512


def _in_proj_part(xn, w_in_t, ids=_same_ids):
    m = xn.shape[0]
    tm = min(2048, m)
    tn = IN_TN

    def src_row(j):
        return pl.multiple_of(j * tn + (j // (N_ALIGNED // tn)) * GATE_COLS, GATE_COLS)

    el = pl.Element
    spec = functools.partial(_on_grid, ids)
    return _Part(
        body=_in_proj_w32_body, ids=ids,
        in_specs=[
            spec((tm, D_MODEL), lambda i, j: (i, 0), pipeline_mode=pl.Buffered(1)),
            spec((el(1), el(tn), el(D_MODEL)), lambda i, j: (0, src_row(j), 0)),
            spec((el(1), el(GATE_COLS), el(D_MODEL)), lambda i, j: (0, N_ALIGNED, 0)),
            spec((el(1), el(S_HEADS), el(D_MODEL)), lambda i, j: (0, N_MAIN + GATE_COLS, 0)),
        ],
        operands=[xn, w_in_t, w_in_t, w_in_t],
        out_specs=[
            spec((tm, tn), lambda i, j: (i, j)),
            spec((tm, LANES), lambda i, j: (i, 0)),
            spec((LANES, tm), lambda i, j: (0, i)),
        ],
        out_shape=[jax.ShapeDtypeStruct((m, N_MAIN), F32), jax.ShapeDtypeStruct((m, LANES), F32),
                   jax.ShapeDtypeStruct((LANES, m), F32)],
        scratch=[pltpu.VMEM((LANES, D_MODEL), BF16)],
    ), (m // tm, N_MAIN // tn)


GATE_COLS = 2 * M_HEADS
N_ALIGNED = 4 * D_M


def _w_prep_body(src_ref, ga_ref, gb_ref, out_ref, wg_ref):
    out_ref[...] = src_ref[0].astype(BF16)

    @pl.when(pl.program_id(0) == 0)
    def _():
        wg_ref[0:GL_DT, :] = ga_ref[0].astype(BF16)
        wg_ref[GL_DT:GL_DT + S_HEADS, :] = gb_ref[0].astype(BF16)
        wg_ref[GL_DT + S_HEADS:LANES, :] = jnp.zeros((LANES - GL_DT - S_HEADS, D_MODEL), BF16)


def _w_prep(w_in_t):
    rb = 512

    def src_row(i):
        return pl.multiple_of(i * rb + (i // (N_ALIGNED // rb)) * GATE_COLS, GATE_COLS)

    return pl.pallas_call(
        _w_prep_body,
        grid=(N_MAIN // rb,),
        in_specs=[
            pl.BlockSpec((pl.Element(1), pl.Element(rb), pl.Element(D_MODEL)), lambda i: (0, src_row(i), 0)),
            pl.BlockSpec((pl.Element(1), pl.Element(GATE_COLS), pl.Element(D_MODEL)), lambda i: (0, N_ALIGNED, 0)),
            pl.BlockSpec((pl.Element(1), pl.Element(S_HEADS), pl.Element(D_MODEL)),
                         lambda i: (0, N_MAIN + GATE_COLS, 0)),
        ],
        out_specs=[pl.BlockSpec((rb, D_MODEL), lambda i: (i, 0)), pl.BlockSpec((LANES, D_MODEL), lambda i: (0, 0))],
        out_shape=[jax.ShapeDtypeStruct((N_MAIN, D_MODEL), BF16), jax.ShapeDtypeStruct((LANES, D_MODEL), BF16)],
        compiler_params=pltpu.CompilerParams(dimension_semantics=("arbitrary",), vmem_limit_bytes=VMEM_LIMIT),
        name="w_prep",
    )(w_in_t, w_in_t, w_in_t)


TAIL = 8


def _conv_carry(u, ext_ref, cw, cb):
    c = u.shape[0]
    ext_ref[TAIL:TAIL + c, :] = u
    acc = cb + u * cw[CONV_K - 1:CONV_K, :]
    for j in range(1, CONV_K):
        acc = acc + ext_ref[TAIL - j:TAIL - j + c, :] * cw[CONV_K - 1 - j:CONV_K - j, :]
    ext_ref[0:TAIL, :] = u[c - TAIL:c, :]
    return acc


def _conv_rows(u, buf, cw, cb, t):
    rows = u.shape[0]
    acc = cb + u * cw[CONV_K - 1:CONV_K, :]
    for j in range(1, CONV_K):
        prev = buf if j == CONV_K - 1 else pltpu.roll(buf, rows + j - (CONV_K - 1), 0)
        acc = acc + jnp.where(t >= j, pltpu.roll(u, j, 0), prev) * cw[CONV_K - 1 - j:CONV_K - j, :]
    return acc


def _head_out(hv, o, z, nw):
    og = _sigmoid(o) * hv
    ms = jnp.mean(og * og, axis=-1, keepdims=True)
    return (og * lax.rsqrt(ms + EPS) * nw) * _silu(z)


def _group_out(y, xs, dsk, z, nw):
    gated = (y + dsk * xs) * _silu(z)
    ms = jnp.mean(gated * gated, axis=-1, keepdims=True)
    return gated * lax.rsqrt(ms + EPS) * nw


def _mlstm_prompt_body(ids, u_ref, v_ref, o_ref, z_ref, gt_ref, gb_ref, cw_ref, cb_ref, wq_ref, wkt_ref, nw_ref,
                       hm_ref, c_ref, n_ref, m_ref, ext_ref, mst_ref):
    c = u_ref.shape[0]
    hps = wq_ref.shape[0]
    h0 = ids[1] * hps

    @pl.when(ids[2] == 0)
    def _():
        c_ref[...] = jnp.zeros_like(c_ref)
        n_ref[...] = jnp.zeros_like(n_ref)
        mst_ref[...] = jnp.full_like(mst_ref, -jnp.inf)
        ext_ref[0:TAIL, :] = jnp.zeros((TAIL, ext_ref.shape[1]), F32)

    ucb = _silu(_conv_carry(u_ref[...], ext_ref, cw_ref[...], cb_ref[...])).astype(BF16)
    vb = v_ref[...].astype(BF16)

    rowi = lax.broadcasted_iota(jnp.int32, (c, c), 0)
    coli = lax.broadcasted_iota(jnp.int32, (c, c), 1)
    tri = coli <= rowi
    eye = coli == rowi
    ig8 = _softcap(gt_ref[GL_IG:GL_IG + M_HEADS, :] + gb_ref[GL_IG:GL_IG + M_HEADS, :])
    lf8 = _log_sigmoid(_softcap(gt_ref[GL_FG:GL_FG + M_HEADS, :] + gb_ref[GL_FG:GL_FG + M_HEADS, :]))
    b8 = _dot_exact(lf8, jnp.where(rowi <= coli, 1.0, 0.0))
    head_row = lax.broadcasted_iota(jnp.int32, (M_HEADS, 1), 0)

    for hh in range(hps):
        cols = slice(hh * M_V, (hh + 1) * M_V)
        pick = head_row == h0 + hh
        ig_row = jnp.sum(jnp.where(pick, ig8, 0.0), axis=0, keepdims=True)
        b_row = jnp.sum(jnp.where(pick, b8, 0.0), axis=0, keepdims=True)
        b_col = _row_to_col(b_row, eye)
        m_prev = mst_ref[hh:hh + 1, :][:, 0:1]

        q = _dot(ucb[:, cols], wq_ref[hh])
        kt = _dot_nt(wkt_ref[hh], ucb[:, cols])
        qb = q.astype(BF16)
        ktb = kt.astype(BF16)

        r_row = ig_row - b_row
        dmat = jnp.where(tri, b_col + r_row, -jnp.inf)
        inter = b_col + m_prev
        m_t = jnp.maximum(inter, jnp.max(dmat, axis=1, keepdims=True))
        a_inter = jnp.exp(inter - m_t)
        s = _dot(qb, ktb) * jnp.exp(dmat - m_t)
        c_old = c_ref[0, hh]
        n_old = n_ref[0, hh]
        num = _dot(s.astype(BF16), vb[:, cols]) + a_inter * _dot(qb, c_old.astype(BF16))
        qn = _dot_nt(qb, jnp.broadcast_to(n_old, (8, M_QK)).astype(BF16))[:, 0:1]
        den = jnp.sum(s, axis=1, keepdims=True) + a_inter * qn
        hv = num / jnp.maximum(jnp.abs(den), jnp.exp(-m_t))
        hm_ref[:, cols] = _head_out(hv, o_ref[:, cols], z_ref[:, cols], nw_ref[:, cols]).astype(BF16)

        b_end = b_row[:, c - 1:c]
        lw_row = b_end + r_row
        m_new = jnp.maximum(b_end + m_prev, jnp.max(lw_row, axis=1, keepdims=True))
        decay = jnp.exp(b_end + m_prev - m_new)
        wk_row = jnp.exp(lw_row - m_new)
        c_ref[0, hh] = decay * c_old + _dot((kt * wk_row).astype(BF16), vb[:, cols])
        n_ref[0, hh] = decay * n_old + _dot_nt(jnp.broadcast_to(wk_row, (8, c)).astype(BF16), ktb)[0:1, :]
        mst_ref[hh:hh + 1, :] = jnp.broadcast_to(m_new, (1, LANES))
        m_ref[0, hh] = jnp.broadcast_to(m_new, (1, LANES))


MLSTM_HPS = 4


def _mlstm_prompt_part(proj, gates_t, bsz, seqlen, gate_bias_col, conv_w, conv_b, wq, wkt, norm_w):
    c = min(PROMPT_CHUNK, seqlen)
    nch = seqlen // c
    hps = MLSTM_HPS
    width = hps * M_V

    def tok(col0):
        return pl.BlockSpec((c, width), lambda b, h, k: (b * nch + k, col0 // width + h))

    head_vec = lambda rows: pl.BlockSpec((rows, width), lambda b, h, k: (0, h))
    return _Part(
        body=_mlstm_prompt_body, ids=_same_ids,
        in_specs=[
            tok(COL_U), tok(COL_V), tok(COL_O), tok(COL_ZM),
            pl.BlockSpec((LANES, c), lambda b, h, k: (0, b * nch + k)),
            pl.BlockSpec((LANES, 1), lambda b, h, k: (0, 0)),
            head_vec(CONV_K), head_vec(1),
            pl.BlockSpec((hps, M_V, M_QK), lambda b, h, k: (h, 0, 0)),
            pl.BlockSpec((hps, M_QK, M_V), lambda b, h, k: (h, 0, 0)),
            head_vec(1),
        ],
        operands=[proj, proj, proj, proj, gates_t, gate_bias_col, conv_w, conv_b, wq, wkt, norm_w],
        out_specs=[
            pl.BlockSpec((c, width), lambda b, h, k: (b * nch + k, h)),
            pl.BlockSpec((1, hps, M_QK, M_V), lambda b, h, k: (b, h, 0, 0)),
            pl.BlockSpec((1, hps, 1, M_QK), lambda b, h, k: (b, h, 0, 0)),
            pl.BlockSpec((1, hps, 1, LANES), lambda b, h, k: (b, h, 0, 0)),
        ],
        out_shape=[
            jax.ShapeDtypeStruct((bsz * seqlen, D_M), BF16),
            jax.ShapeDtypeStruct((bsz, M_HEADS, M_QK, M_V), F32),
            jax.ShapeDtypeStruct((bsz, M_HEADS, 1, M_QK), F32),
            jax.ShapeDtypeStruct((bsz, M_HEADS, 1, LANES), F32),
        ],
        scratch=[pltpu.VMEM((TAIL + c, width), F32), pltpu.VMEM((8, LANES), F32)],
    ), (bsz, M_HEADS // hps, nch)


LOG2E = 1.4426950408889634
HEAD_ROWS = [slice(r * S_HEAD_DIM, (r + 1) * S_HEAD_DIM) for r in range(S_HPG)]


def _ssd_gate_rows(gt_ref, gbc_ref, avc_ref, lane0):
    rows = pl.ds(lane0, S_HPG)
    dt8 = _softplus(gt_ref[rows, :] + gbc_ref[rows, :])
    return dt8, dt8 * (-jnp.exp(avc_ref[rows, :]))


def _ssd_intra(xst, dt8, cum8, cbt):
    c = xst.shape[1]
    a2r = cum8 * LOG2E
    a2c = jnp.concatenate([a2r, jnp.zeros((LANES - S_HPG, c), F32)], axis=0).T
    ys, xdts = [], []
    for r in range(S_HPG):
        xdt = xst[HEAD_ROWS[r], :] * dt8[r:r + 1, :]
        decay = jnp.exp2(jnp.minimum(a2r[r:r + 1, :] - a2c[:, r:r + 1], 0.0))
        ys.append(_dot(xdt.astype(BF16), (cbt * decay).astype(BF16)))
        xdts.append(xdt)
    return ys, xdts


def _ssd_prompt_body(ids, xs_ref, bm_ref, cm_ref, zs_ref, gt_ref, gbc_ref, avc_ref, dsk_ref,
                     cwx_ref, cbx_ref, cwb_ref, cbb_ref, cwc_ref, cbc_ref, nw_ref,
                     ys_ref, h_ref, extx_ref, extb_ref, extc_ref):
    c = xs_ref.shape[0]
    gps = h_ref.shape[1]
    g0 = ids[1] * gps

    @pl.when(ids[2] == 0)
    def _():
        h_ref[...] = jnp.zeros_like(h_ref)
        extx_ref[0:TAIL, :] = jnp.zeros((TAIL, extx_ref.shape[1]), F32)
        extb_ref[0:TAIL, :] = jnp.zeros((TAIL, extb_ref.shape[1]), F32)
        extc_ref[0:TAIL, :] = jnp.zeros((TAIL, extc_ref.shape[1]), F32)

    xs_all = _silu(_conv_carry(xs_ref[...], extx_ref, cwx_ref[...], cbx_ref[...]))
    bmb_all = _silu(_conv_carry(bm_ref[...], extb_ref, cwb_ref[...], cbb_ref[...])).astype(BF16)
    cmb_all = _silu(_conv_carry(cm_ref[...], extc_ref, cwc_ref[...], cbc_ref[...])).astype(BF16)

    src = lax.broadcasted_iota(jnp.int32, (c, c), 0)
    tgt = lax.broadcasted_iota(jnp.int32, (c, c), 1)
    causal = src <= tgt
    upper = jnp.where(causal, 1.0, 0.0)

    for gg in range(gps):
        cols = slice(gg * S_GW, (gg + 1) * S_GW)
        xs = xs_all[:, cols]
        bmb = bmb_all[:, gg * S_STATE:(gg + 1) * S_STATE]
        cmb = cmb_all[:, gg * S_STATE:(gg + 1) * S_STATE]
        lane0 = pl.multiple_of(GL_DT + (g0 + gg) * S_HPG, S_HPG)
        dt8, la8 = _ssd_gate_rows(gt_ref, gbc_ref, avc_ref, lane0)
        cum8 = _dot_exact(la8, upper)
        a_end = cum8[:, c - 1:c]
        w8 = jnp.exp(a_end - cum8)
        ea8 = jnp.exp(cum8)

        cbt = jnp.where(causal, _dot_nt(bmb, cmb), 0.0)
        ys, xdts = _ssd_intra(xs.T, dt8, cum8, cbt)
        h_old = h_ref[0, gg]
        yit = _dot_nt(h_old.astype(BF16), cmb)
        yt = jnp.concatenate([ys[r] + ea8[r:r + 1, :] * yit[HEAD_ROWS[r], :] for r in range(S_HPG)], axis=0)
        ys_ref[:, cols] = _group_out(yt.T, xs, dsk_ref[:, cols], zs_ref[:, cols], nw_ref[:, cols]).astype(BF16)

        xwt = jnp.concatenate([(xdts[r] * w8[r:r + 1, :]).astype(BF16) for r in range(S_HPG)], axis=0)
        upd = _dot(xwt, bmb)
        ea_end = jnp.exp(a_end)
        for r in range(S_HPG):
            h_ref[0, gg, HEAD_ROWS[r], :] = ea_end[r:r + 1, :] * h_old[HEAD_ROWS[r], :] + upd[HEAD_ROWS[r], :]


SSD_GPS = 4


def _ssd_prompt_part(proj, gates_t, bsz, seqlen, gate_bias_col, a_col, dsk_row, conv_w, conv_b, norm_w):
    c = min(PROMPT_CHUNK, seqlen)
    nch = seqlen // c
    gps = SSD_GPS
    tok = lambda width, col0: pl.BlockSpec((c, gps * width), lambda b, g, k: (b * nch + k, col0 // (gps * width) + g))
    vec = lambda rows, width, col0: pl.BlockSpec((rows, gps * width), lambda b, g, k: (0, col0 // (gps * width) + g))
    col = pl.BlockSpec((LANES, 1), lambda b, g, k: (0, 0))
    return _Part(
        body=_ssd_prompt_body, ids=_same_ids,
        in_specs=[
            tok(S_GW, COL_XS), tok(S_STATE, COL_BM), tok(S_STATE, COL_CM), tok(S_GW, COL_ZS),
            pl.BlockSpec((LANES, c), lambda b, g, k: (0, b * nch + k)),
            col, col, vec(1, S_GW, 0),
            vec(CONV_K, S_GW, 0), vec(1, S_GW, 0),
            vec(CONV_K, S_STATE, D_S), vec(1, S_STATE, D_S),
            vec(CONV_K, S_STATE, D_S + S_GROUPS * S_STATE), vec(1, S_STATE, D_S + S_GROUPS * S_STATE),
            vec(1, S_GW, 0),
        ],
        operands=[proj, proj, proj, proj, gates_t, gate_bias_col, a_col, dsk_row,
                  conv_w, conv_b, conv_w, conv_b, conv_w, conv_b, norm_w],
        out_specs=[
            pl.BlockSpec((c, gps * S_GW), lambda b, g, k: (b * nch + k, g)),
            pl.BlockSpec((1, gps, S_GW, S_STATE), lambda b, g, k: (b, g, 0, 0)),
        ],
        out_shape=[
            jax.ShapeDtypeStruct((bsz * seqlen, D_S), BF16),
            jax.ShapeDtypeStruct((bsz, S_GROUPS, S_GW, S_STATE), F32),
        ],
        scratch=[pltpu.VMEM((TAIL + c, gps * S_GW), F32), pltpu.VMEM((TAIL + c, gps * S_STATE), F32),
                 pltpu.VMEM((TAIL + c, gps * S_STATE), F32)],
    ), (bsz, S_GROUPS // gps, nch)


SEQ4 = 4


def _row_masks(rows):
    rowi = lax.broadcasted_iota(jnp.int32, (rows, rows), 0)
    coli = lax.broadcasted_iota(jnp.int32, (rows, rows), 1)
    same = _idiv(rowi, SEQ4) == _idiv(coli, SEQ4)
    return same, same & (coli <= rowi), coli == rowi, same & (_imod(coli, SEQ4) == SEQ4 - 1)


def _seg_cumsum(x, t):
    out = x
    for j in range(1, SEQ4):
        out = out + jnp.where(t >= j, pltpu.roll(x, j, 0), 0.0)
    return out


def _mlstm_sample_body(ids, u_ref, v_ref, o_ref, z_ref, gt_ref, gb_ref, buf_ref, mrow_ref, c0_ref, n0_ref,
                       cw_ref, cb_ref, wq_ref, wkt_ref, nw_ref,
                       hm_ref, c_ref, n_ref, m_ref):
    h = ids[0]
    rows = u_ref.shape[0]
    nseq = rows // SEQ4
    rid = lax.broadcasted_iota(jnp.int32, (rows, 1), 0)
    t = _imod(rid, SEQ4)
    seq = _idiv(rid, SEQ4)
    seq_lane = _idiv(lax.broadcasted_iota(jnp.int32, (1, rows), 1), SEQ4)

    ucb = _silu(_conv_rows(u_ref[...], buf_ref[...], cw_ref[...], cb_ref[...], t)).astype(BF16)
    q = _dot(ucb, wq_ref[0])
    kt = _dot_nt(wkt_ref[0], ucb)
    qb = q.astype(BF16)
    vb = v_ref[...].astype(BF16)

    capped = _softcap(gt_ref[...] + gb_ref[...])
    ig_col = _lane_pick(capped, GL_IG + h)
    b_col = _lane_pick(_seg_cumsum(_log_sigmoid(capped), t), GL_FG + h)
    m_prev = _lane_pick(mrow_ref[...], h)

    same, tri, eye, last = _row_masks(rows)
    b_row = _col_to_row(b_col, eye)
    ig_row = _col_to_row(ig_col, eye)
    dmat = jnp.where(tri, b_col - b_row + ig_row, -jnp.inf)
    inter = b_col + m_prev
    m_t = jnp.maximum(inter, jnp.max(dmat, axis=1, keepdims=True))
    a_inter = jnp.exp(inter - m_t)
    s = _dot(qb, kt.astype(BF16)) * jnp.exp(dmat - m_t)

    b_end = jnp.sum(jnp.where(last, b_row, 0.0), axis=1, keepdims=True)
    lw = b_end - b_col + ig_col
    m_new = jnp.maximum(b_end + m_prev, jnp.max(jnp.where(same, _col_to_row(lw, eye), -jnp.inf), axis=1, keepdims=True))
    decay = jnp.exp(b_end + m_prev - m_new)
    kwt = kt * _col_to_row(jnp.exp(lw - m_new), eye)
    eye_k = (lax.broadcasted_iota(jnp.int32, (M_QK, M_QK), 0) == lax.broadcasted_iota(jnp.int32, (M_QK, M_QK), 1))

    num_inter = jnp.zeros((rows, M_V), F32)
    qn = jnp.zeros((rows, 1), F32)
    for bi in range(nseq):
        own = seq == bi
        c_old = c0_ref[bi, 0]
        n_old = n0_ref[bi, 0]
        num_inter = jnp.where(own, _dot(qb, c_old.astype(BF16)), num_inter)
        qn = jnp.where(own, jnp.sum(q * n_old, axis=1, keepdims=True), qn)
        kw_b = jnp.where(seq_lane == bi, kwt, 0.0)
        d_b = jnp.sum(jnp.where(own & (t == SEQ4 - 1), decay, 0.0), axis=0, keepdims=True)
        c_ref[bi, 0] = d_b * c_old + _dot(kw_b.astype(BF16), vb)
        n_ref[bi, 0] = d_b * n_old + _col_to_row(jnp.sum(kw_b, axis=1, keepdims=True), eye_k)

    num = _dot(s.astype(BF16), vb) + a_inter * num_inter
    den = jnp.sum(s, axis=1, keepdims=True) + a_inter * qn
    hv = num / jnp.maximum(jnp.abs(den), jnp.exp(-m_t))
    hm_ref[...] = _head_out(hv, o_ref[...], z_ref[...], nw_ref[...]).astype(BF16)
    m_ref[0] = jnp.broadcast_to(m_new, (rows, LANES))


def _mlstm_sample_part(proj, gates, nseq_total, gate_bias, buf_rows, m_rows, c0, n0, conv_w, conv_b, wq, wkt, norm_w,
                       bb, ids=_same_ids):
    rows = bb * SEQ4
    nblk = nseq_total // bb
    spec = functools.partial(_on_grid, ids)
    tok = lambda col0: spec((rows, M_V), lambda h, i: (i, col0 // M_V + h))
    head_vec = lambda r: spec((r, M_V), lambda h, i: (0, h))
    c_spec = spec((bb, 1, M_QK, M_V), lambda h, i: (i, h, 0, 0))
    n_spec = spec((bb, 1, 1, M_QK), lambda h, i: (i, h, 0, 0))
    return _Part(
        body=_mlstm_sample_body, ids=ids,
        in_specs=[
            tok(COL_U), tok(COL_V), tok(COL_O), tok(COL_ZM),
            spec((rows, LANES), lambda h, i: (i, 0)),
            spec((1, LANES), lambda h, i: (0, 0)),
            spec((rows, M_V), lambda h, i: (i, h)),
            spec((rows, LANES), lambda h, i: (i, 0)),
            c_spec, n_spec,
            head_vec(CONV_K), head_vec(1),
            spec((1, M_V, M_QK), lambda h, i: (h, 0, 0)),
            spec((1, M_QK, M_V), lambda h, i: (h, 0, 0)),
            head_vec(1),
        ],
        operands=[proj, proj, proj, proj, gates, gate_bias, buf_rows, m_rows, c0, n0, conv_w, conv_b, wq, wkt, norm_w],
        out_specs=[
            spec((rows, M_V), lambda h, i: (i, h)),
            c_spec, n_spec,
            spec((1, rows, LANES), lambda h, i: (h, i, 0)),
        ],
        out_shape=[
            jax.ShapeDtypeStruct((nseq_total * SEQ4, D_M), BF16),
            jax.ShapeDtypeStruct((nseq_total, M_HEADS, M_QK, M_V), F32),
            jax.ShapeDtypeStruct((nseq_total, M_HEADS, 1, M_QK), F32),
            jax.ShapeDtypeStruct((M_HEADS, nseq_total * SEQ4, LANES), F32),
        ],
        scratch=[],
    ), (M_HEADS, nblk)


def _ssd_sample_body(xs_ref, bm_ref, cm_ref, zs_ref, gt_ref, gbc_ref, avc_ref, dsk_ref,
                     bufx_ref, bufb_ref, bufc_ref, h0_ref,
                     cwx_ref, cbx_ref, cwb_ref, cbb_ref, cwc_ref, cbc_ref, nw_ref,
                     ys_ref, h_ref, yit_ref, xwt_ref):
    g = pl.program_id(0)
    rows = xs_ref.shape[0]
    nseq = rows // SEQ4
    rid = lax.broadcasted_iota(jnp.int32, (rows, 1), 0)
    t = _imod(rid, SEQ4)
    seq = _idiv(rid, SEQ4)
    tok_lane = lax.broadcasted_iota(jnp.int32, (1, rows), 1)

    xs = _silu(_conv_rows(xs_ref[...], bufx_ref[...], cwx_ref[...], cbx_ref[...], t))
    bmb = _silu(_conv_rows(bm_ref[...], bufb_ref[...], cwb_ref[...], cbb_ref[...], t)).astype(BF16)
    cmb = _silu(_conv_rows(cm_ref[...], bufc_ref[...], cwc_ref[...], cbc_ref[...], t)).astype(BF16)

    lane0 = pl.multiple_of(GL_DT + g * S_HPG, S_HPG)
    dt8, la8 = _ssd_gate_rows(gt_ref, gbc_ref, avc_ref, lane0)
    src = lax.broadcasted_iota(jnp.int32, (rows, rows), 0)
    tgt = lax.broadcasted_iota(jnp.int32, (rows, rows), 1)
    same = _idiv(src, SEQ4) == _idiv(tgt, SEQ4)
    causal = same & (src <= tgt)
    cum8 = _dot_exact(la8, jnp.where(causal, 1.0, 0.0))
    a_end = _dot_exact(cum8, jnp.where(same & (_imod(src, SEQ4) == SEQ4 - 1), 1.0, 0.0))
    w8 = jnp.exp(a_end - cum8)
    ea8 = jnp.exp(cum8)
    ea_end = jnp.exp(a_end)

    cbt = jnp.where(causal, _dot_nt(bmb, cmb), 0.0)
    ys, xdts = _ssd_intra(xs.T, dt8, cum8, cbt)
    xwt_ref[...] = jnp.concatenate([(xdts[r] * w8[r:r + 1, :]).astype(BF16) for r in range(S_HPG)], axis=0)
    yit_ref[...] = jnp.zeros_like(yit_ref)

    def per_sequence(bi, carry):
        h_old = h0_ref[bi, 0]
        own = _idiv(tok_lane, SEQ4) == bi
        yit_ref[...] += _dot_nt(h_old.astype(BF16), jnp.where(seq == bi, cmb, jnp.zeros_like(cmb)))
        upd = _dot(jnp.where(own, xwt_ref[...], jnp.zeros_like(xwt_ref)), bmb)
        d8 = jnp.sum(jnp.where(tok_lane == bi * SEQ4 + SEQ4 - 1, ea_end, 0.0), axis=1, keepdims=True)
        for r in range(S_HPG):
            h_ref[bi, 0, HEAD_ROWS[r], :] = d8[r:r + 1, :] * h_old[HEAD_ROWS[r], :] + upd[HEAD_ROWS[r], :]
        return carry

    lax.fori_loop(0, nseq, per_sequence, 0, unroll=8 if nseq % 8 == 0 else 1)
    yit = yit_ref[...]
    yt = jnp.concatenate([ys[r] + ea8[r:r + 1, :] * yit[HEAD_ROWS[r], :] for r in range(S_HPG)], axis=0)
    ys_ref[...] = _group_out(yt.T, xs, dsk_ref[...], zs_ref[...], nw_ref[...]).astype(BF16)


SSD_SAMPLE_BB = 32


def _ssd_sample(proj, gates_t, nseq_total, gate_bias_col, a_col, dsk_row, buf_rows, h0, conv_w, conv_b, norm_w):
    bb = min(SSD_SAMPLE_BB, nseq_total)
    rows = bb * SEQ4
    nblk = nseq_total // bb
    tok = lambda width, col0: pl.BlockSpec((rows, width), lambda g, i: (i, col0 // width + g))
    vec = lambda r, width, col0: pl.BlockSpec((r, width), lambda g, i: (0, col0 // width + g))
    col = pl.BlockSpec((LANES, 1), lambda g, i: (0, 0))
    h_spec = pl.BlockSpec((bb, 1, S_GW, S_STATE), lambda g, i: (i, g, 0, 0))
    off_b = D_S
    off_c = D_S + S_GROUPS * S_STATE
    return pl.pallas_call(
        _ssd_sample_body,
        grid=(S_GROUPS, nblk),
        in_specs=[
            tok(S_GW, COL_XS), tok(S_STATE, COL_BM), tok(S_STATE, COL_CM), tok(S_GW, COL_ZS),
            pl.BlockSpec((LANES, rows), lambda g, i: (0, i)),
            col, col, vec(1, S_GW, 0),
            tok(S_GW, 0), tok(S_STATE, off_b), tok(S_STATE, off_c),
            h_spec,
            vec(CONV_K, S_GW, 0), vec(1, S_GW, 0),
            vec(CONV_K, S_STATE, off_b), vec(1, S_STATE, off_b),
            vec(CONV_K, S_STATE, off_c), vec(1, S_STATE, off_c),
            vec(1, S_GW, 0),
        ],
        out_specs=[pl.BlockSpec((rows, S_GW), lambda g, i: (i, g)), h_spec],
        out_shape=[
            jax.ShapeDtypeStruct((nseq_total * SEQ4, D_S), BF16),
            jax.ShapeDtypeStruct((nseq_total, S_GROUPS, S_GW, S_STATE), F32),
        ],
        scratch_shapes=[pltpu.VMEM((S_GW, rows), F32), pltpu.VMEM((S_GW, rows), BF16)],
        compiler_params=pltpu.CompilerParams(
            dimension_semantics=("parallel", "parallel"), vmem_limit_bytes=VMEM_LIMIT),
        name="ssd_sample",
    )(proj, proj, proj, proj, gates_t, gate_bias_col, a_col, dsk_row, buf_rows, buf_rows, buf_rows, h0,
      conv_w, conv_b, conv_w, conv_b, conv_w, conv_b, norm_w)


OUT_TM, OUT_TN = 512, 512


def _out_proj_body(ids, x_ref, hm_ref, ys_ref, wt_ref, wb_ref, gn_ref, out_ref, x1g_ref, ssp_ref):
    x1 = x_ref[...] + _dot(hm_ref[...], wt_ref[...]) + _dot(ys_ref[...], wb_ref[...])
    out_ref[...] = x1
    x1g_ref[...] = (x1 * gn_ref[...]).astype(BF16)
    ssp_ref[0] = jnp.sum(x1 * x1, axis=1, keepdims=True)


def _out_proj_part(x, hm, ys, w_out, ple_norm, ids=_same_ids):
    m = x.shape[0]
    tm, tn = min(OUT_TM, m), OUT_TN
    nj = D_MODEL // tn
    spec = functools.partial(_on_grid, ids)
    return _Part(
        body=_out_proj_body, ids=ids,
        in_specs=[
            spec((tm, tn), lambda i, j: (i, j)),
            spec((tm, D_M), lambda i, j: (i, 0)),
            spec((tm, D_S), lambda i, j: (i, 0)),
            spec((D_M, tn), lambda i, j: (0, j)),
            spec((D_S, tn), lambda i, j: (1, j)),
            spec((1, tn), lambda i, j: (0, j)),
        ],
        operands=[x, hm, ys, w_out, w_out, ple_norm],
        out_specs=[
            spec((tm, tn), lambda i, j: (i, j)),
            spec((tm, tn), lambda i, j: (i, j)),
            spec((1, tm, 1), lambda i, j: (j, i, 0)),
        ],
        out_shape=[jax.ShapeDtypeStruct((m, D_MODEL), F32), jax.ShapeDtypeStruct((m, D_MODEL), BF16),
                   jax.ShapeDtypeStruct((nj, m, 1), F32)],
        scratch=[],
    ), (m // tm, nj)


def _ple_body(x1g_ref, ss1_ref, x1t_ref, p_ref, wg_ref, wp_ref, fn_ref, out_ref, ss_ref):
    j = pl.program_id(1)
    nj = pl.num_programs(1)
    tn = wg_ref.shape[1]

    @pl.when(j == 0)
    def _():
        ss_ref[...] = jnp.zeros_like(ss_ref)

    rs = lax.rsqrt(jnp.sum(ss1_ref[...], axis=0) * (1.0 / D_MODEL) + EPS)
    gate = _sigmoid(rs * _dot(x1g_ref[...], wg_ref[...]))
    x2 = x1t_ref[...] + gate * _dot(p_ref[...].astype(BF16), wp_ref[...])
    ss_ref[...] += jnp.sum(x2 * x2, axis=1, keepdims=True)
    for jj in range(out_ref.shape[1] // tn):
        @pl.when(j == jj)
        def _(jj=jj):
            out_ref[:, jj * tn:(jj + 1) * tn] = x2

    @pl.when(j == nj - 1)
    def _():
        fn = fn_ref[...]

        def body(r, carry):
            sl = pl.ds(pl.multiple_of(r * NORM_ROWS, NORM_ROWS), NORM_ROWS)
            scale = lax.rsqrt(ss_ref[sl, :] * (1.0 / D_MODEL) + EPS)
            out_ref[sl, :] = out_ref[sl, :] * scale * fn
            return carry
        lax.fori_loop(0, out_ref.shape[0] // NORM_ROWS, body, 0)


def _ple(x1, x1g, ss1, p, w_gate, w_proj, final_norm):
    m = x1.shape[0]
    tm = min(512, m)
    tn = 1024
    return pl.pallas_call(
        _ple_body,
        grid=(m // tm, D_MODEL // tn),
        in_specs=[
            pl.BlockSpec((tm, D_MODEL), lambda i, j: (i, 0)),
            pl.BlockSpec((ss1.shape[0], tm, 1), lambda i, j: (0, i, 0)),
            pl.BlockSpec((tm, tn), lambda i, j: (i, j)),
            pl.BlockSpec((tm, PLE_DIM), lambda i, j: (i, 0)),
            pl.BlockSpec((D_MODEL, tn), lambda i, j: (0, j)),
            pl.BlockSpec((PLE_DIM, tn), lambda i, j: (0, j)),
            pl.BlockSpec((1, D_MODEL), lambda i, j: (0, 0)),
        ],
        out_specs=pl.BlockSpec((tm, D_MODEL), lambda i, j: (i, 0)),
        out_shape=jax.ShapeDtypeStruct((m, D_MODEL), F32),
        scratch_shapes=[pltpu.VMEM((tm, 1), F32)],
        compiler_params=pltpu.CompilerParams(
            dimension_semantics=("parallel", "arbitrary"), vmem_limit_bytes=VMEM_LIMIT),
        name="ple",
    )(x1g, ss1, x1, p, w_gate, w_proj, final_norm)


CONV_TC = 2048


def _slot_row_match(nseq, slot_major_axis, first_token):
    shape = (nseq * SEQ4, (CONV_K - 1) * nseq) if slot_major_axis == 1 else ((CONV_K - 1) * nseq, nseq * SEQ4)
    sm = lax.broadcasted_iota(jnp.int32, shape, slot_major_axis)
    row = lax.broadcasted_iota(jnp.int32, shape, 1 - slot_major_axis)
    hit = None
    for slot in range(CONV_K - 1):
        b = sm - slot * nseq
        this = (b >= 0) & (b < nseq) & (row == b * SEQ4 + first_token + slot)
        hit = this if hit is None else hit | this
    return jnp.where(hit, 1.0, 0.0)


def _conv_rows_in_body(buf_ref, rows_ref):
    nseq = buf_ref.shape[1]
    pick = _slot_row_match(nseq, 1, 0)
    rows_ref[...] = _dot_exact(pick, buf_ref[...].reshape((CONV_K - 1) * nseq, buf_ref.shape[2]))


def _conv_rows_in(buf):
    _, nseq, ch = buf.shape
    tc = min(CONV_TC, ch)
    return pl.pallas_call(
        _conv_rows_in_body,
        grid=(ch // tc,),
        in_specs=[pl.BlockSpec((CONV_K - 1, nseq, tc), lambda j: (0, 0, j))],
        out_specs=pl.BlockSpec((nseq * SEQ4, tc), lambda j: (0, j)),
        out_shape=jax.ShapeDtypeStruct((nseq * SEQ4, ch), F32),
        compiler_params=pltpu.CompilerParams(dimension_semantics=("parallel",), vmem_limit_bytes=VMEM_LIMIT),
        name="conv_rows_in",
    )(buf)


def _conv_rows_out_body(rows_ref, buf_ref):
    nseq = buf_ref.shape[1]
    pick = _slot_row_match(nseq, 0, SEQ4 - (CONV_K - 1))
    buf_ref[...] = _dot_exact(pick, rows_ref[...]).reshape(buf_ref.shape)


def _conv_rows_out(proj, col0, ch):
    nseq = proj.shape[0] // SEQ4
    tc = min(CONV_TC, ch)
    assert col0 % tc == 0 and ch % tc == 0
    return pl.pallas_call(
        _conv_rows_out_body,
        grid=(ch // tc,),
        in_specs=[pl.BlockSpec((nseq * SEQ4, tc), lambda j: (0, col0 // tc + j))],
        out_specs=pl.BlockSpec((CONV_K - 1, nseq, tc), lambda j: (0, 0, j)),
        out_shape=jax.ShapeDtypeStruct((CONV_K - 1, nseq, ch), F32),
        compiler_params=pltpu.CompilerParams(dimension_semantics=("parallel",), vmem_limit_bytes=VMEM_LIMIT),
        name="conv_rows_out",
    )(proj)


def _pad_lanes(v, offset):
    out = jnp.zeros((1, LANES), F32)
    return lax.dynamic_update_slice(out, v.reshape(1, -1).astype(F32), (0, offset))


def kernel(x_prompt, x_sample, p_prompt, p_sample, state_m_C, state_m_n, state_m_m, state_m_conv, state_s_ssm, state_s_conv, norm_in, w_in, b_ig, b_fg, m_conv_w, m_conv_b, w_q, w_k, m_norm, s_conv_w, s_conv_b, dt_bias, a_log, d_skip, s_norm, w_out, ple_proj, ple_gate, ple_norm, final_norm):
    assert w_in.shape[0] == 1, "single layer"
    bsz, seqlen, _ = x_prompt.shape
    nseq, dec_seq, _ = x_sample.shape
    assert dec_seq == SEQ4

    assert w_in.shape[2] == N_MAIN + GATE_COLS + S_HEADS
    w_in_t = jnp.swapaxes(w_in, 1, 2)
    gate_bias = jnp.concatenate(
        [b_ig[0], b_fg[0], dt_bias[0], jnp.zeros((LANES - 2 * M_HEADS - S_HEADS,), F32)]).reshape(1, LANES)
    gate_bias_col = gate_bias.reshape(LANES, 1)
    a_col = _pad_lanes(a_log[0], GL_DT).reshape(LANES, 1)
    dsk_row = jnp.repeat(d_skip[0], S_HEAD_DIM).reshape(1, D_S)
    wq = w_q[0].astype(BF16)
    wkt = (jnp.swapaxes(w_k[0], 1, 2) * (M_QK ** -0.5)).astype(BF16)
    ple_proj_b = ple_proj[0].astype(BF16)
    norm_in_r = norm_in[0].reshape(1, D_MODEL)
    ple_norm_r = ple_norm[0].reshape(1, D_MODEL)
    final_norm_r = final_norm.reshape(1, D_MODEL)
    m_conv_b_r = m_conv_b[0].reshape(1, D_M)
    s_conv_b_r = s_conv_b[0].reshape(1, CONV_DIM)
    m_norm_r = m_norm[0].reshape(1, D_M)
    s_norm_r = s_norm[0].reshape(1, D_S)

    xp = x_prompt.reshape(bsz * seqlen, D_MODEL)
    xs_ = x_sample.reshape(nseq * SEQ4, D_MODEL)
    in_part, in_grid = _in_proj_part(_rmsnorm_cast(xp, norm_in_r), w_in_t)
    (proj_p, _, gates_pt), = _call_parts("in_proj", in_grid, [in_part])

    mlstm_part, mlstm_grid = _mlstm_prompt_part(
        proj_p, gates_pt, bsz, seqlen, gate_bias_col, m_conv_w[0], m_conv_b_r, wq, wkt, m_norm_r)
    nsteps = mlstm_grid[0] * mlstm_grid[1] * mlstm_grid[2]
    step = _linear_step(mlstm_grid)
    (hm_p, pc, pn, pm), (w_out_b,), (ple_gate_b,) = _call_parts(
        "mlstm_prompt_casts", mlstm_grid,
        [mlstm_part, _cast_part(w_out[0], nsteps, step), _cast_part(ple_gate[0], nsteps, step)])

    ssd_part, ssd_grid = _ssd_prompt_part(
        proj_p, gates_pt, bsz, seqlen, gate_bias_col, a_col, dsk_row, s_conv_w[0], s_conv_b_r, s_norm_r)
    (ys_p, ph), = _call_parts("ssd_prompt", ssd_grid, [ssd_part])
    in_part_s, in_grid_s = _in_proj_part(_rmsnorm_cast(xs_, norm_in_r), w_in_t)
    (proj_s, gates_s, gates_st), = _call_parts("in_proj", in_grid_s, [in_part_s])

    proj_p3 = proj_p.reshape(bsz, seqlen, N_MAIN)
    tail = seqlen - (CONV_K - 1)
    p_mconv = proj_p3[:, tail:, COL_U:COL_U + D_M]
    p_sconv = proj_p3[:, tail:, COL_XS:COL_XS + CONV_DIM]

    pad_rows = lambda st: _conv_rows_in(jnp.swapaxes(st, 0, 1))
    m_rows = jnp.pad(jnp.repeat(state_m_m[0], SEQ4, axis=0), ((0, 0), (0, LANES - M_HEADS)))

    sample_args = (proj_s, gates_s, nseq, gate_bias, pad_rows(state_m_conv[0]), m_rows, state_m_C[0],
                   state_m_n[0].reshape(nseq, M_HEADS, 1, M_QK), m_conv_w[0], m_conv_b_r, wq, wkt, m_norm_r)
    out_part, out_grid = _out_proj_part(xp, hm_p, ys_p, w_out_b, ple_norm_r)
    bb = SAMPLE_BB
    while bb > 1 and (nseq % bb or M_HEADS * (nseq // bb) < out_grid[0] * out_grid[1]):
        bb //= 2
    if M_HEADS * (nseq // bb) == out_grid[0] * out_grid[1]:
        nblk = nseq // bb
        as_head_block = lambda i, j: ((i * out_grid[1] + j) // nblk, (i * out_grid[1] + j) % nblk)
        scan_part, _ = _mlstm_sample_part(*sample_args, bb, ids=as_head_block)
        (x1_p, x1g_p, ss1_p), (hm_s, sc, sn, sm) = _call_parts("out_proj_mlstm_sample", out_grid, [out_part, scan_part])
    else:
        (x1_p, x1g_p, ss1_p), = _call_parts("out_proj", out_grid, [out_part])
        scan_part, scan_grid = _mlstm_sample_part(*sample_args, bb)
        (hm_s, sc, sn, sm), = _call_parts("mlstm_sample", scan_grid, [scan_part])
    y_p = _ple(x1_p, x1g_p, ss1_p, p_prompt[0].reshape(bsz * seqlen, PLE_DIM), ple_gate_b, ple_proj_b, final_norm_r)

    ys_s, sh = _ssd_sample(
        proj_s, gates_st, nseq, gate_bias_col, a_col, dsk_row, pad_rows(state_s_conv[0]),
        state_s_ssm[0].reshape(nseq, S_GROUPS, S_GW, S_STATE), s_conv_w[0], s_conv_b_r, s_norm_r)
    out_part_s, out_grid_s = _out_proj_part(xs_, hm_s, ys_s, w_out_b, ple_norm_r)
    (x1_s, x1g_s, ss1_s), = _call_parts("out_proj", out_grid_s, [out_part_s])
    y_s = _ple(x1_s, x1g_s, ss1_s, p_sample[0].reshape(nseq * SEQ4, PLE_DIM), ple_gate_b, ple_proj_b, final_norm_r)

    s_mconv = jnp.swapaxes(_conv_rows_out(proj_s, COL_U, D_M), 0, 1)
    s_sconv = jnp.swapaxes(_conv_rows_out(proj_s, COL_XS, CONV_DIM), 0, 1)
    s_m = jnp.transpose(sm[:, SEQ4 - 1::SEQ4, 0])

    return (
        y_p.reshape(bsz, seqlen, D_MODEL),
        y_s.reshape(nseq, SEQ4, D_MODEL),
        pc[None], pn.reshape(1, bsz, M_HEADS, M_QK), pm[:, :, 0, 0][None], p_mconv[None],
        ph.reshape(1, bsz, S_HEADS, S_HEAD_DIM, S_STATE), p_sconv[None],
        sc[None], sn.reshape(1, nseq, M_HEADS, M_QK), s_m[None], s_mconv[None],
        sh.reshape(1, nseq, S_HEADS, S_HEAD_DIM, S_STATE), s_sconv[None],
    )
```

```python
import functools
from typing import Callable, NamedTuple

import jax
import jax.numpy as jnp
from jax import lax
from jax.experimental import pallas as pl
from jax.experimental.pallas import tpu as pltpu

F32 = jnp.float32
BF16 = jnp.bfloat16

D_MODEL = 4096
D_M = 4096
D_S = 4096
M_HEADS = 8
M_V = 512
M_QK = 256
S_HEADS = 64
S_HEAD_DIM = 64
S_STATE = 128
S_GROUPS = 8
S_HPG = 8
S_GW = S_HPG * S_HEAD_DIM
CONV_DIM = D_S + 2 * S_GROUPS * S_STATE
CONV_K = 4
PLE_DIM = 256
GATE_CAP = 15.0
EPS = 1e-6

COL_U, COL_V, COL_O, COL_ZM, COL_ZS, COL_XS = 0, 4096, 8192, 12288, 16384, 20480
COL_BM = COL_XS + D_S
COL_CM = COL_BM + S_GROUPS * S_STATE
N_MAIN = COL_XS + CONV_DIM
LANES = 128
GL_IG, GL_FG, GL_DT = 0, M_HEADS, 2 * M_HEADS

PROMPT_CHUNK = 256
SAMPLE_BB = 16
VMEM_LIMIT = 62 * 1024 * 1024

NT_DIMS = (((1,), (1,)), ((), ()))


def _dot(a, b):
    return jnp.dot(a, b, preferred_element_type=F32)


def _dot_nt(a, b):
    return lax.dot_general(a, b, NT_DIMS, preferred_element_type=F32)


def _rows_dot_01(a, b01):
    n = a.shape[0]
    hi = a.astype(BF16).astype(F32)
    rest = a - hi
    mid = rest.astype(BF16).astype(F32)
    pieces = jnp.concatenate([hi, mid, rest - mid], axis=0).astype(BF16)
    r = _dot(pieces, b01.astype(BF16))
    return (r[0:n] + r[n:2 * n]) + r[2 * n:3 * n]


def _dot_select(pick, x):
    hi = x.astype(BF16)
    rest = x - hi.astype(F32)
    mid = rest.astype(BF16)
    lo = (rest - mid.astype(F32)).astype(BF16)
    p = pick.astype(BF16)
    return (_dot(p, hi) + _dot(p, mid)) + _dot(p, lo)


def _sigmoid(x):
    return 0.5 * jnp.tanh(0.5 * x) + 0.5


def _silu(x):
    return x * _sigmoid(x)


def _softcap(x):
    return GATE_CAP * jnp.tanh(x / GATE_CAP)


def _softplus(x):
    return jnp.maximum(x, 0.0) + jnp.log1p(jnp.exp(-jnp.abs(x)))


def _log_sigmoid(x):
    return -_softplus(-x)


def _idiv(x, d):
    return lax.shift_right_logical(x, d.bit_length() - 1)


def _imod(x, d):
    return lax.bitwise_and(x, d - 1)


def _lane_pick(x, lane_idx):
    lane = lax.broadcasted_iota(jnp.int32, (1, x.shape[1]), 1)
    return jnp.sum(jnp.where(lane == lane_idx, x, 0.0), axis=1, keepdims=True)


def _col_to_row(col, eye):
    return jnp.sum(jnp.where(eye, col, 0.0), axis=0, keepdims=True)


def _row_to_col(row, eye):
    return jnp.sum(jnp.where(eye, row, 0.0), axis=1, keepdims=True)


class _Part(NamedTuple):
    body: Callable
    ids: Callable
    in_specs: list
    operands: list
    out_specs: list
    out_shape: list
    scratch: list


def _same_ids(*g):
    return g


def _on_grid(ids, block_shape, index_map, **kwargs):
    return pl.BlockSpec(block_shape, lambda *g: index_map(*ids(*g)), **kwargs)


def _call_parts(name, grid, parts):
    n_in = [len(p.in_specs) for p in parts]
    n_out = [len(p.out_specs) for p in parts]
    n_scr = [len(p.scratch) for p in parts]

    def body(*refs):
        gids = tuple(pl.program_id(a) for a in range(len(grid)))
        ins, outs, scr = refs[:sum(n_in)], refs[sum(n_in):sum(n_in) + sum(n_out)], refs[sum(n_in) + sum(n_out):]
        for p, part in enumerate(parts):
            a, b, c = sum(n_in[:p]), sum(n_out[:p]), sum(n_scr[:p])
            part.body(part.ids(*gids), *ins[a:a + n_in[p]], *outs[b:b + n_out[p]], *scr[c:c + n_scr[p]])

    results = pl.pallas_call(
        body,
        grid=grid,
        in_specs=[s for p in parts for s in p.in_specs],
        out_specs=[s for p in parts for s in p.out_specs],
        out_shape=[s for p in parts for s in p.out_shape],
        scratch_shapes=[s for p in parts for s in p.scratch],
        compiler_params=pltpu.CompilerParams(
            dimension_semantics=("arbitrary",) * len(grid), vmem_limit_bytes=VMEM_LIMIT),
        name=name,
    )(*[o for p in parts for o in p.operands])
    return [results[sum(n_out[:p]):sum(n_out[:p]) + n_out[p]] for p in range(len(parts))]


def _linear_step(grid):
    def lin(*g):
        s = g[0]
        for size, idx in zip(grid[1:], g[1:]):
            s = s * size + idx
        return s
    return lin


def _cast_body(ids, src_ref, dst_ref):
    dst_ref[...] = src_ref[...].astype(dst_ref.dtype)


def _cast_part(w, nsteps, step):
    rows = w.shape[0] // nsteps
    assert rows * nsteps == w.shape[0] and rows % 16 == 0
    spec = pl.BlockSpec((rows, w.shape[1]), lambda *g: (step(*g), 0))
    return _Part(body=_cast_body, ids=_same_ids, in_specs=[spec], operands=[w], out_specs=[spec],
                 out_shape=[jax.ShapeDtypeStruct(w.shape, BF16)], scratch=[])


NORM_ROWS = 32


def _rmsnorm_rows_to(x_ref, g, dst_ref):
    def body(r, carry):
        sl = pl.ds(pl.multiple_of(r * NORM_ROWS, NORM_ROWS), NORM_ROWS)
        x = x_ref[sl, :]
        ms = jnp.mean(x * x, axis=-1, keepdims=True)
        dst_ref[sl, :] = (x * lax.rsqrt(ms + EPS) * g).astype(BF16)
        return carry
    lax.fori_loop(0, x_ref.shape[0] // NORM_ROWS, body, 0)


def _in_proj_body(x_ref, g_ref, w_ref, wg_ref, out_ref, gates_ref, gatest_ref, xn_ref):
    @pl.when(pl.program_id(1) == 0)
    def _():
        _rmsnorm_rows_to(x_ref, g_ref[...], xn_ref)
        gates = _dot_nt(xn_ref[...], wg_ref[...])
        gates_ref[...] = gates
        gatest_ref[...] = gates.T
    out_ref[...] = _dot_nt(xn_ref[...], w_ref[...])


def _in_proj(x, norm_w, w_main, w_gate):
    m = x.shape[0]
    tm = min(512, m)
    tn = 1024
    grid = (m // tm, N_MAIN // tn)
    return pl.pallas_call(
        _in_proj_body,
        grid=grid,
        in_specs=[
            pl.BlockSpec((tm, D_MODEL), lambda i, j: (i, 0)),
            pl.BlockSpec((1, D_MODEL), lambda i, j: (0, 0)),
            pl.BlockSpec((tn, D_MODEL), lambda i, j: (j, 0)),
            pl.BlockSpec((LANES, D_MODEL), lambda i, j: (0, 0)),
        ],
        out_specs=[
            pl.BlockSpec((tm, tn), lambda i, j: (i, j)),
            pl.BlockSpec((tm, LANES), lambda i, j: (i, 0)),
            pl.BlockSpec((LANES, tm), lambda i, j: (0, i)),
        ],
        out_shape=[jax.ShapeDtypeStruct((m, N_MAIN), F32), jax.ShapeDtypeStruct((m, LANES), F32),
                   jax.ShapeDtypeStruct((LANES, m), F32)],
        scratch_shapes=[pltpu.VMEM((tm, D_MODEL), BF16)],
        compiler_params=pltpu.CompilerParams(
            dimension_semantics=("parallel", "arbitrary"), vmem_limit_bytes=VMEM_LIMIT),
        name="in_proj",
    )(x, norm_w, w_main, w_gate)


def _rmsnorm_cast_body(x_ref, g_ref, out_ref):
    _rmsnorm_rows_to(x_ref, g_ref[...], out_ref)


def _rmsnorm_cast(x, norm_w):
    m = x.shape[0]
    tm = min(512, m)
    return pl.pallas_call(
        _rmsnorm_cast_body,
        grid=(m // tm,),
        in_specs=[pl.BlockSpec((tm, D_MODEL), lambda i: (i, 0)), pl.BlockSpec((1, D_MODEL), lambda i: (0, 0))],
        out_specs=pl.BlockSpec((tm, D_MODEL), lambda i: (i, 0)),
        out_shape=jax.ShapeDtypeStruct((m, D_MODEL), BF16),
        compiler_params=pltpu.CompilerParams(dimension_semantics=("parallel",), vmem_limit_bytes=VMEM_LIMIT),
        name="rmsnorm_cast",
    )(x, norm_w)


def _in_proj_w32_body(ids, xn_ref, w_ref, ga_ref, gb_ref, out_ref, gates_ref, gatest_ref, wg_ref):
    @pl.when(ids[1] == 0)
    def _():
        wg_ref[0:GL_DT, :] = ga_ref[0].astype(BF16)
        wg_ref[GL_DT:GL_DT + S_HEADS, :] = gb_ref[0].astype(BF16)
        wg_ref[GL_DT + S_HEADS:LANES, :] = jnp.zeros((LANES - GL_DT - S_HEADS, D_MODEL), BF16)
        gates = _dot_nt(xn_ref[...], wg_ref[...])
        gates_ref[...] = gates
        gatest_ref[...] = gates.T
    out_ref[...] = _dot_nt(xn_ref[...], w_ref[0].astype(BF16))


IN_TN = 512


def _in_proj_part(xn, w_in_t, ids=_same_ids):
    m = xn.shape[0]
    tm = min(2048, m)
    tn = IN_TN

    def src_row(j):
        return pl.multiple_of(j * tn + (j // (N_ALIGNED // tn)) * GATE_COLS, GATE_COLS)

    el = pl.Element
    spec = functools.partial(_on_grid, ids)
    return _Part(
        body=_in_proj_w32_body, ids=ids,
        in_specs=[
            spec((tm, D_MODEL), lambda i, j: (i, 0), pipeline_mode=pl.Buffered(1)),
            spec((el(1), el(tn), el(D_MODEL)), lambda i, j: (0, src_row(j), 0)),
            spec((el(1), el(GATE_COLS), el(D_MODEL)), lambda i, j: (0, N_ALIGNED, 0)),
            spec((el(1), el(S_HEADS), el(D_MODEL)), lambda i, j: (0, N_MAIN + GATE_COLS, 0)),
        ],
        operands=[xn, w_in_t, w_in_t, w_in_t],
        out_specs=[
            spec((tm, tn), lambda i, j: (i, j)),
            spec((tm, LANES), lambda i, j: (i, 0)),
            spec((LANES, tm), lambda i, j: (0, i)),
        ],
        out_shape=[jax.ShapeDtypeStruct((m, N_MAIN), F32), jax.ShapeDtypeStruct((m, LANES), F32),
                   jax.ShapeDtypeStruct((LANES, m), F32)],
        scratch=[pltpu.VMEM((LANES, D_MODEL), BF16)],
    ), (m // tm, N_MAIN // tn)


GATE_COLS = 2 * M_HEADS
N_ALIGNED = 4 * D_M


def _w_prep_body(src_ref, ga_ref, gb_ref, out_ref, wg_ref):
    out_ref[...] = src_ref[0].astype(BF16)

    @pl.when(pl.program_id(0) == 0)
    def _():
        wg_ref[0:GL_DT, :] = ga_ref[0].astype(BF16)
        wg_ref[GL_DT:GL_DT + S_HEADS, :] = gb_ref[0].astype(BF16)
        wg_ref[GL_DT + S_HEADS:LANES, :] = jnp.zeros((LANES - GL_DT - S_HEADS, D_MODEL), BF16)


def _w_prep(w_in_t):
    rb = 512

    def src_row(i):
        return pl.multiple_of(i * rb + (i // (N_ALIGNED // rb)) * GATE_COLS, GATE_COLS)

    return pl.pallas_call(
        _w_prep_body,
        grid=(N_MAIN // rb,),
        in_specs=[
            pl.BlockSpec((pl.Element(1), pl.Element(rb), pl.Element(D_MODEL)), lambda i: (0, src_row(i), 0)),
            pl.BlockSpec((pl.Element(1), pl.Element(GATE_COLS), pl.Element(D_MODEL)), lambda i: (0, N_ALIGNED, 0)),
            pl.BlockSpec((pl.Element(1), pl.Element(S_HEADS), pl.Element(D_MODEL)),
                         lambda i: (0, N_MAIN + GATE_COLS, 0)),
        ],
        out_specs=[pl.BlockSpec((rb, D_MODEL), lambda i: (i, 0)), pl.BlockSpec((LANES, D_MODEL), lambda i: (0, 0))],
        out_shape=[jax.ShapeDtypeStruct((N_MAIN, D_MODEL), BF16), jax.ShapeDtypeStruct((LANES, D_MODEL), BF16)],
        compiler_params=pltpu.CompilerParams(dimension_semantics=("arbitrary",), vmem_limit_bytes=VMEM_LIMIT),
        name="w_prep",
    )(w_in_t, w_in_t, w_in_t)


TAIL = 8


def _conv_carry(u, ext_ref, cw, cb):
    c = u.shape[0]
    ext_ref[TAIL:TAIL + c, :] = u
    acc = cb + u * cw[CONV_K - 1:CONV_K, :]
    for j in range(1, CONV_K):
        acc = acc + ext_ref[TAIL - j:TAIL - j + c, :] * cw[CONV_K - 1 - j:CONV_K - j, :]
    ext_ref[0:TAIL, :] = u[c - TAIL:c, :]
    return acc


def _conv_rows(u, buf, cw, cb, t):
    rows = u.shape[0]
    acc = cb + u * cw[CONV_K - 1:CONV_K, :]
    for j in range(1, CONV_K):
        prev = buf if j == CONV_K - 1 else pltpu.roll(buf, rows + j - (CONV_K - 1), 0)
        acc = acc + jnp.where(t >= j, pltpu.roll(u, j, 0), prev) * cw[CONV_K - 1 - j:CONV_K - j, :]
    return acc


def _head_out(hv, o, z, nw):
    og = _sigmoid(o) * hv
    ms = jnp.mean(og * og, axis=-1, keepdims=True)
    return (og * lax.rsqrt(ms + EPS) * nw) * _silu(z)


def _group_out(y, xs, dsk, z, nw):
    gated = (y + dsk * xs) * _silu(z)
    ms = jnp.mean(gated * gated, axis=-1, keepdims=True)
    return gated * lax.rsqrt(ms + EPS) * nw


def _mlstm_prompt_body(ids, u_ref, v_ref, o_ref, z_ref, gt_ref, gb_ref, cw_ref, cb_ref, wq_ref, wkt_ref, nw_ref,
                       hm_ref, c_ref, n_ref, m_ref, ext_ref, mst_ref):
    c = u_ref.shape[0]
    hps = wq_ref.shape[0]
    h0 = ids[1] * hps

    @pl.when(ids[2] == 0)
    def _():
        c_ref[...] = jnp.zeros_like(c_ref)
        n_ref[...] = jnp.zeros_like(n_ref)
        mst_ref[...] = jnp.full_like(mst_ref, -jnp.inf)
        ext_ref[0:TAIL, :] = jnp.zeros((TAIL, ext_ref.shape[1]), F32)

    ucb = _silu(_conv_carry(u_ref[...], ext_ref, cw_ref[...], cb_ref[...])).astype(BF16)
    vb = v_ref[...].astype(BF16)

    rowi = lax.broadcasted_iota(jnp.int32, (c, c), 0)
    coli = lax.broadcasted_iota(jnp.int32, (c, c), 1)
    tri = coli <= rowi
    eye = coli == rowi
    ig8 = _softcap(gt_ref[GL_IG:GL_IG + M_HEADS, :] + gb_ref[GL_IG:GL_IG + M_HEADS, :])
    lf8 = _log_sigmoid(_softcap(gt_ref[GL_FG:GL_FG + M_HEADS, :] + gb_ref[GL_FG:GL_FG + M_HEADS, :]))
    b8 = _rows_dot_01(lf8, jnp.where(rowi <= coli, 1.0, 0.0))
    head_row = lax.broadcasted_iota(jnp.int32, (M_HEADS, 1), 0)

    for hh in range(hps):
        cols = slice(hh * M_V, (hh + 1) * M_V)
        pick = head_row == h0 + hh
        ig_row = jnp.sum(jnp.where(pick, ig8, 0.0), axis=0, keepdims=True)
        b_row = jnp.sum(jnp.where(pick, b8, 0.0), axis=0, keepdims=True)
        b_col = _row_to_col(b_row, eye)
        m_prev = mst_ref[hh:hh + 1, :][:, 0:1]

        q = _dot(ucb[:, cols], wq_ref[hh])
        kt = _dot_nt(wkt_ref[hh], ucb[:, cols])
        qb = q.astype(BF16)
        ktb = kt.astype(BF16)

        r_row = ig_row - b_row
        dmat = jnp.where(tri, b_col + r_row, -jnp.inf)
        inter = b_col + m_prev
        m_t = jnp.maximum(inter, jnp.max(dmat, axis=1, keepdims=True))
        a_inter = jnp.exp(inter - m_t)
        s = _dot(qb, ktb) * jnp.exp(dmat - m_t)
        c_old = c_ref[0, hh]
        n_old = n_ref[0, hh]
        num = _dot(s.astype(BF16), vb[:, cols]) + a_inter * _dot(qb, c_old.astype(BF16))
        qn = _dot_nt(qb, jnp.broadcast_to(n_old, (8, M_QK)).astype(BF16))[:, 0:1]
        den = jnp.sum(s, axis=1, keepdims=True) + a_inter * qn
        hv = num / jnp.maximum(jnp.abs(den), jnp.exp(-m_t))
        hm_ref[:, cols] = _head_out(hv, o_ref[:, cols], z_ref[:, cols], nw_ref[:, cols]).astype(BF16)

        b_end = b_row[:, c - 1:c]
        lw_row = b_end + r_row
        m_new = jnp.maximum(b_end + m_prev, jnp.max(lw_row, axis=1, keepdims=True))
        decay = jnp.exp(b_end + m_prev - m_new)
        wk_row = jnp.exp(lw_row - m_new)
        c_ref[0, hh] = decay * c_old + _dot((kt * wk_row).astype(BF16), vb[:, cols])
        n_ref[0, hh] = decay * n_old + _dot_nt(jnp.broadcast_to(wk_row, (8, c)).astype(BF16), ktb)[0:1, :]
        mst_ref[hh:hh + 1, :] = jnp.broadcast_to(m_new, (1, LANES))
        m_ref[0, hh] = jnp.broadcast_to(m_new, (1, LANES))


MLSTM_HPS = 4


def _mlstm_prompt_part(proj, gates_t, bsz, seqlen, gate_bias_col, conv_w, conv_b, wq, wkt, norm_w):
    c = min(PROMPT_CHUNK, seqlen)
    nch = seqlen // c
    hps = MLSTM_HPS
    width = hps * M_V

    def tok(col0):
        return pl.BlockSpec((c, width), lambda b, h, k: (b * nch + k, col0 // width + h))

    head_vec = lambda rows: pl.BlockSpec((rows, width), lambda b, h, k: (0, h))
    return _Part(
        body=_mlstm_prompt_body, ids=_same_ids,
        in_specs=[
            tok(COL_U), tok(COL_V), tok(COL_O), tok(COL_ZM),
            pl.BlockSpec((LANES, c), lambda b, h, k: (0, b * nch + k)),
            pl.BlockSpec((LANES, 1), lambda b, h, k: (0, 0)),
            head_vec(CONV_K), head_vec(1),
            pl.BlockSpec((hps, M_V, M_QK), lambda b, h, k: (h, 0, 0)),
            pl.BlockSpec((hps, M_QK, M_V), lambda b, h, k: (h, 0, 0)),
            head_vec(1),
        ],
        operands=[proj, proj, proj, proj, gates_t, gate_bias_col, conv_w, conv_b, wq, wkt, norm_w],
        out_specs=[
            pl.BlockSpec((c, width), lambda b, h, k: (b * nch + k, h)),
            pl.BlockSpec((1, hps, M_QK, M_V), lambda b, h, k: (b, h, 0, 0)),
            pl.BlockSpec((1, hps, 1, M_QK), lambda b, h, k: (b, h, 0, 0)),
            pl.BlockSpec((1, hps, 1, LANES), lambda b, h, k: (b, h, 0, 0)),
        ],
        out_shape=[
            jax.ShapeDtypeStruct((bsz * seqlen, D_M), BF16),
            jax.ShapeDtypeStruct((bsz, M_HEADS, M_QK, M_V), F32),
            jax.ShapeDtypeStruct((bsz, M_HEADS, 1, M_QK), F32),
            jax.ShapeDtypeStruct((bsz, M_HEADS, 1, LANES), F32),
        ],
        scratch=[pltpu.VMEM((TAIL + c, width), F32), pltpu.VMEM((8, LANES), F32)],
    ), (bsz, M_HEADS // hps, nch)


LOG2E = 1.4426950408889634
HEAD_ROWS = [slice(r * S_HEAD_DIM, (r + 1) * S_HEAD_DIM) for r in range(S_HPG)]


def _ssd_gate_rows(gt_ref, gbc_ref, avc_ref, lane0):
    rows = pl.ds(lane0, S_HPG)
    dt8 = _softplus(gt_ref[rows, :] + gbc_ref[rows, :])
    return dt8, dt8 * (-jnp.exp(avc_ref[rows, :]))


def _ssd_intra(xst, dt8, cum8, cbt):
    c = xst.shape[1]
    a2r = cum8 * LOG2E
    a2c = jnp.concatenate([a2r, jnp.zeros((LANES - S_HPG, c), F32)], axis=0).T
    ys, xdts = [], []
    for r in range(S_HPG):
        xdt = xst[HEAD_ROWS[r], :] * dt8[r:r + 1, :]
        decay = jnp.exp2(jnp.minimum(a2r[r:r + 1, :] - a2c[:, r:r + 1], 0.0))
        ys.append(_dot(xdt.astype(BF16), (cbt * decay).astype(BF16)))
        xdts.append(xdt)
    return ys, xdts


def _ssd_prompt_body(ids, xs_ref, bm_ref, cm_ref, zs_ref, gt_ref, gbc_ref, avc_ref, dsk_ref,
                     cwx_ref, cbx_ref, cwb_ref, cbb_ref, cwc_ref, cbc_ref, nw_ref,
                     ys_ref, h_ref, extx_ref, extb_ref, extc_ref):
    c = xs_ref.shape[0]
    gps = h_ref.shape[1]
    g0 = ids[1] * gps

    @pl.when(ids[2] == 0)
    def _():
        h_ref[...] = jnp.zeros_like(h_ref)
        extx_ref[0:TAIL, :] = jnp.zeros((TAIL, extx_ref.shape[1]), F32)
        extb_ref[0:TAIL, :] = jnp.zeros((TAIL, extb_ref.shape[1]), F32)
        extc_ref[0:TAIL, :] = jnp.zeros((TAIL, extc_ref.shape[1]), F32)

    xs_all = _silu(_conv_carry(xs_ref[...], extx_ref, cwx_ref[...], cbx_ref[...]))
    bmb_all = _silu(_conv_carry(bm_ref[...], extb_ref, cwb_ref[...], cbb_ref[...])).astype(BF16)
    cmb_all = _silu(_conv_carry(cm_ref[...], extc_ref, cwc_ref[...], cbc_ref[...])).astype(BF16)

    src = lax.broadcasted_iota(jnp.int32, (c, c), 0)
    tgt = lax.broadcasted_iota(jnp.int32, (c, c), 1)
    causal = src <= tgt
    upper = jnp.where(causal, 1.0, 0.0)

    for gg in range(gps):
        cols = slice(gg * S_GW, (gg + 1) * S_GW)
        xs = xs_all[:, cols]
        bmb = bmb_all[:, gg * S_STATE:(gg + 1) * S_STATE]
        cmb = cmb_all[:, gg * S_STATE:(gg + 1) * S_STATE]
        lane0 = pl.multiple_of(GL_DT + (g0 + gg) * S_HPG, S_HPG)
        dt8, la8 = _ssd_gate_rows(gt_ref, gbc_ref, avc_ref, lane0)
        cum8 = _rows_dot_01(la8, upper)
        a_end = cum8[:, c - 1:c]
        w8 = jnp.exp(a_end - cum8)
        ea8 = jnp.exp(cum8)

        cbt = jnp.where(causal, _dot_nt(bmb, cmb), 0.0)
        ys, xdts = _ssd_intra(xs.T, dt8, cum8, cbt)
        h_old = h_ref[0, gg]
        yit = _dot_nt(h_old.astype(BF16), cmb)
        yt = jnp.concatenate([ys[r] + ea8[r:r + 1, :] * yit[HEAD_ROWS[r], :] for r in range(S_HPG)], axis=0)
        ys_ref[:, cols] = _group_out(yt.T, xs, dsk_ref[:, cols], zs_ref[:, cols], nw_ref[:, cols]).astype(BF16)

        xwt = jnp.concatenate([(xdts[r] * w8[r:r + 1, :]).astype(BF16) for r in range(S_HPG)], axis=0)
        upd = _dot(xwt, bmb)
        ea_end = jnp.exp(a_end)
        for r in range(S_HPG):
            h_ref[0, gg, HEAD_ROWS[r], :] = ea_end[r:r + 1, :] * h_old[HEAD_ROWS[r], :] + upd[HEAD_ROWS[r], :]


SSD_GPS = 4


def _ssd_prompt_part(proj, gates_t, bsz, seqlen, gate_bias_col, a_col, dsk_row, conv_w, conv_b, norm_w):
    c = min(PROMPT_CHUNK, seqlen)
    nch = seqlen // c
    gps = SSD_GPS
    tok = lambda width, col0: pl.BlockSpec((c, gps * width), lambda b, g, k: (b * nch + k, col0 // (gps * width) + g))
    vec = lambda rows, width, col0: pl.BlockSpec((rows, gps * width), lambda b, g, k: (0, col0 // (gps * width) + g))
    col = pl.BlockSpec((LANES, 1), lambda b, g, k: (0, 0))
    return _Part(
        body=_ssd_prompt_body, ids=_same_ids,
        in_specs=[
            tok(S_GW, COL_XS), tok(S_STATE, COL_BM), tok(S_STATE, COL_CM), tok(S_GW, COL_ZS),
            pl.BlockSpec((LANES, c), lambda b, g, k: (0, b * nch + k)),
            col, col, vec(1, S_GW, 0),
            vec(CONV_K, S_GW, 0), vec(1, S_GW, 0),
            vec(CONV_K, S_STATE, D_S), vec(1, S_STATE, D_S),
            vec(CONV_K, S_STATE, D_S + S_GROUPS * S_STATE), vec(1, S_STATE, D_S + S_GROUPS * S_STATE),
            vec(1, S_GW, 0),
        ],
        operands=[proj, proj, proj, proj, gates_t, gate_bias_col, a_col, dsk_row,
                  conv_w, conv_b, conv_w, conv_b, conv_w, conv_b, norm_w],
        out_specs=[
            pl.BlockSpec((c, gps * S_GW), lambda b, g, k: (b * nch + k, g)),
            pl.BlockSpec((1, gps, S_GW, S_STATE), lambda b, g, k: (b, g, 0, 0)),
        ],
        out_shape=[
            jax.ShapeDtypeStruct((bsz * seqlen, D_S), BF16),
            jax.ShapeDtypeStruct((bsz, S_GROUPS, S_GW, S_STATE), F32),
        ],
        scratch=[pltpu.VMEM((TAIL + c, gps * S_GW), F32), pltpu.VMEM((TAIL + c, gps * S_STATE), F32),
                 pltpu.VMEM((TAIL + c, gps * S_STATE), F32)],
    ), (bsz, S_GROUPS // gps, nch)


SEQ4 = 4


def _row_masks(rows):
    rowi = lax.broadcasted_iota(jnp.int32, (rows, rows), 0)
    coli = lax.broadcasted_iota(jnp.int32, (rows, rows), 1)
    same = _idiv(rowi, SEQ4) == _idiv(coli, SEQ4)
    return same, same & (coli <= rowi), coli == rowi, same & (_imod(coli, SEQ4) == SEQ4 - 1)


def _seg_cumsum(x, t):
    out = x
    for j in range(1, SEQ4):
        out = out + jnp.where(t >= j, pltpu.roll(x, j, 0), 0.0)
    return out


def _mlstm_sample_body(ids, u_ref, v_ref, o_ref, z_ref, gt_ref, gb_ref, buf_ref, mrow_ref, c0_ref, n0_ref,
                       cw_ref, cb_ref, wq_ref, wkt_ref, nw_ref,
                       hm_ref, c_ref, n_ref, m_ref):
    h = ids[0]
    rows = u_ref.shape[0]
    nseq = rows // SEQ4
    rid = lax.broadcasted_iota(jnp.int32, (rows, 1), 0)
    t = _imod(rid, SEQ4)
    seq = _idiv(rid, SEQ4)
    seq_lane = _idiv(lax.broadcasted_iota(jnp.int32, (1, rows), 1), SEQ4)

    ucb = _silu(_conv_rows(u_ref[...], buf_ref[...], cw_ref[...], cb_ref[...], t)).astype(BF16)
    q = _dot(ucb, wq_ref[0])
    kt = _dot_nt(wkt_ref[0], ucb)
    qb = q.astype(BF16)
    vb = v_ref[...].astype(BF16)

    capped = _softcap(gt_ref[...] + gb_ref[...])
    ig_col = _lane_pick(capped, GL_IG + h)
    b_col = _lane_pick(_seg_cumsum(_log_sigmoid(capped), t), GL_FG + h)
    m_prev = _lane_pick(mrow_ref[...], h)

    same, tri, eye, last = _row_masks(rows)
    b_row = _col_to_row(b_col, eye)
    ig_row = _col_to_row(ig_col, eye)
    dmat = jnp.where(tri, b_col - b_row + ig_row, -jnp.inf)
    inter = b_col + m_prev
    m_t = jnp.maximum(inter, jnp.max(dmat, axis=1, keepdims=True))
    a_inter = jnp.exp(inter - m_t)
    s = _dot(qb, kt.astype(BF16)) * jnp.exp(dmat - m_t)

    b_end = jnp.sum(jnp.where(last, b_row, 0.0), axis=1, keepdims=True)
    lw = b_end - b_col + ig_col
    m_new = jnp.maximum(b_end + m_prev, jnp.max(jnp.where(same, _col_to_row(lw, eye), -jnp.inf), axis=1, keepdims=True))
    decay = jnp.exp(b_end + m_prev - m_new)
    kwt = kt * _col_to_row(jnp.exp(lw - m_new), eye)
    eye_k = (lax.broadcasted_iota(jnp.int32, (M_QK, M_QK), 0) == lax.broadcasted_iota(jnp.int32, (M_QK, M_QK), 1))

    num_inter = jnp.zeros((rows, M_V), F32)
    qn = jnp.zeros((rows, 1), F32)
    for bi in range(nseq):
        own = seq == bi
        c_old = c0_ref[bi, 0]
        n_old = n0_ref[bi, 0]
        num_inter = jnp.where(own, _dot(qb, c_old.astype(BF16)), num_inter)
        qn = jnp.where(own, jnp.sum(q * n_old, axis=1, keepdims=True), qn)
        kw_b = jnp.where(seq_lane == bi, kwt, 0.0)
        d_b = jnp.sum(jnp.where(own & (t == SEQ4 - 1), decay, 0.0), axis=0, keepdims=True)
        c_ref[bi, 0] = d_b * c_old + _dot(kw_b.astype(BF16), vb)
        n_ref[bi, 0] = d_b * n_old + _col_to_row(jnp.sum(kw_b, axis=1, keepdims=True), eye_k)

    num = _dot(s.astype(BF16), vb) + a_inter * num_inter
    den = jnp.sum(s, axis=1, keepdims=True) + a_inter * qn
    hv = num / jnp.maximum(jnp.abs(den), jnp.exp(-m_t))
    hm_ref[...] = _head_out(hv, o_ref[...], z_ref[...], nw_ref[...]).astype(BF16)
    m_ref[0] = jnp.broadcast_to(m_new, (rows, LANES))


def _mlstm_sample_part(proj, gates, nseq_total, gate_bias, buf_rows, m_rows, c0, n0, conv_w, conv_b, wq, wkt, norm_w,
                       bb, ids=_same_ids):
    rows = bb * SEQ4
    nblk = nseq_total // bb
    spec = functools.partial(_on_grid, ids)
    tok = lambda col0: spec((rows, M_V), lambda h, i: (i, col0 // M_V + h))
    head_vec = lambda r: spec((r, M_V), lambda h, i: (0, h))
    c_spec = spec((bb, 1, M_QK, M_V), lambda h, i: (i, h, 0, 0))
    n_spec = spec((bb, 1, 1, M_QK), lambda h, i: (i, h, 0, 0))
    return _Part(
        body=_mlstm_sample_body, ids=ids,
        in_specs=[
            tok(COL_U), tok(COL_V), tok(COL_O), tok(COL_ZM),
            spec((rows, LANES), lambda h, i: (i, 0)),
            spec((1, LANES), lambda h, i: (0, 0)),
            spec((rows, M_V), lambda h, i: (i, h)),
            spec((rows, LANES), lambda h, i: (i, 0)),
            c_spec, n_spec,
            head_vec(CONV_K), head_vec(1),
            spec((1, M_V, M_QK), lambda h, i: (h, 0, 0)),
            spec((1, M_QK, M_V), lambda h, i: (h, 0, 0)),
            head_vec(1),
        ],
        operands=[proj, proj, proj, proj, gates, gate_bias, buf_rows, m_rows, c0, n0, conv_w, conv_b, wq, wkt, norm_w],
        out_specs=[
            spec((rows, M_V), lambda h, i: (i, h)),
            c_spec, n_spec,
            spec((1, rows, LANES), lambda h, i: (h, i, 0)),
        ],
        out_shape=[
            jax.ShapeDtypeStruct((nseq_total * SEQ4, D_M), BF16),
            jax.ShapeDtypeStruct((nseq_total, M_HEADS, M_QK, M_V), F32),
            jax.ShapeDtypeStruct((nseq_total, M_HEADS, 1, M_QK), F32),
            jax.ShapeDtypeStruct((M_HEADS, nseq_total * SEQ4, LANES), F32),
        ],
        scratch=[],
    ), (M_HEADS, nblk)


def _ssd_sample_body(xs_ref, bm_ref, cm_ref, zs_ref, gt_ref, gbc_ref, avc_ref, dsk_ref,
                     bufx_ref, bufb_ref, bufc_ref, h0_ref,
                     cwx_ref, cbx_ref, cwb_ref, cbb_ref, cwc_ref, cbc_ref, nw_ref,
                     ys_ref, h_ref, yit_ref, xwt_ref):
    g = pl.program_id(0)
    rows = xs_ref.shape[0]
    nseq = rows // SEQ4
    rid = lax.broadcasted_iota(jnp.int32, (rows, 1), 0)
    t = _imod(rid, SEQ4)
    seq = _idiv(rid, SEQ4)
    tok_lane = lax.broadcasted_iota(jnp.int32, (1, rows), 1)

    xs = _silu(_conv_rows(xs_ref[...], bufx_ref[...], cwx_ref[...], cbx_ref[...], t))
    bmb = _silu(_conv_rows(bm_ref[...], bufb_ref[...], cwb_ref[...], cbb_ref[...], t)).astype(BF16)
    cmb = _silu(_conv_rows(cm_ref[...], bufc_ref[...], cwc_ref[...], cbc_ref[...], t)).astype(BF16)

    lane0 = pl.multiple_of(GL_DT + g * S_HPG, S_HPG)
    dt8, la8 = _ssd_gate_rows(gt_ref, gbc_ref, avc_ref, lane0)
    src = lax.broadcasted_iota(jnp.int32, (rows, rows), 0)
    tgt = lax.broadcasted_iota(jnp.int32, (rows, rows), 1)
    same = _idiv(src, SEQ4) == _idiv(tgt, SEQ4)
    causal = same & (src <= tgt)
    cum8 = _rows_dot_01(la8, jnp.where(causal, 1.0, 0.0))
    a_end = _rows_dot_01(cum8, jnp.where(same & (_imod(src, SEQ4) == SEQ4 - 1), 1.0, 0.0))
    w8 = jnp.exp(a_end - cum8)
    ea8 = jnp.exp(cum8)
    ea_end = jnp.exp(a_end)

    cbt = jnp.where(causal, _dot_nt(bmb, cmb), 0.0)
    ys, xdts = _ssd_intra(xs.T, dt8, cum8, cbt)
    xwt_ref[...] = jnp.concatenate([(xdts[r] * w8[r:r + 1, :]).astype(BF16) for r in range(S_HPG)], axis=0)
    yit_ref[...] = jnp.zeros_like(yit_ref)

    def per_sequence(bi, carry):
        h_old = h0_ref[bi, 0]
        own = _idiv(tok_lane, SEQ4) == bi
        yit_ref[...] += _dot_nt(h_old.astype(BF16), jnp.where(seq == bi, cmb, jnp.zeros_like(cmb)))
        upd = _dot(jnp.where(own, xwt_ref[...], jnp.zeros_like(xwt_ref)), bmb)
        d8 = jnp.sum(jnp.where(tok_lane == bi * SEQ4 + SEQ4 - 1, ea_end, 0.0), axis=1, keepdims=True)
        for r in range(S_HPG):
            h_ref[bi, 0, HEAD_ROWS[r], :] = d8[r:r + 1, :] * h_old[HEAD_ROWS[r], :] + upd[HEAD_ROWS[r], :]
        return carry

    lax.fori_loop(0, nseq, per_sequence, 0, unroll=8 if nseq % 8 == 0 else 1)
    yit = yit_ref[...]
    yt = jnp.concatenate([ys[r] + ea8[r:r + 1, :] * yit[HEAD_ROWS[r], :] for r in range(S_HPG)], axis=0)
    ys_ref[...] = _group_out(yt.T, xs, dsk_ref[...], zs_ref[...], nw_ref[...]).astype(BF16)


SSD_SAMPLE_BB = 32


def _ssd_sample(proj, gates_t, nseq_total, gate_bias_col, a_col, dsk_row, buf_rows, h0, conv_w, conv_b, norm_w):
    bb = min(SSD_SAMPLE_BB, nseq_total)
    rows = bb * SEQ4
    nblk = nseq_total // bb
    tok = lambda width, col0: pl.BlockSpec((rows, width), lambda g, i: (i, col0 // width + g))
    vec = lambda r, width, col0: pl.BlockSpec((r, width), lambda g, i: (0, col0 // width + g))
    col = pl.BlockSpec((LANES, 1), lambda g, i: (0, 0))
    h_spec = pl.BlockSpec((bb, 1, S_GW, S_STATE), lambda g, i: (i, g, 0, 0))
    off_b = D_S
    off_c = D_S + S_GROUPS * S_STATE
    return pl.pallas_call(
        _ssd_sample_body,
        grid=(S_GROUPS, nblk),
        in_specs=[
            tok(S_GW, COL_XS), tok(S_STATE, COL_BM), tok(S_STATE, COL_CM), tok(S_GW, COL_ZS),
            pl.BlockSpec((LANES, rows), lambda g, i: (0, i)),
            col, col, vec(1, S_GW, 0),
            tok(S_GW, 0), tok(S_STATE, off_b), tok(S_STATE, off_c),
            h_spec,
            vec(CONV_K, S_GW, 0), vec(1, S_GW, 0),
            vec(CONV_K, S_STATE, off_b), vec(1, S_STATE, off_b),
            vec(CONV_K, S_STATE, off_c), vec(1, S_STATE, off_c),
            vec(1, S_GW, 0),
        ],
        out_specs=[pl.BlockSpec((rows, S_GW), lambda g, i: (i, g)), h_spec],
        out_shape=[
            jax.ShapeDtypeStruct((nseq_total * SEQ4, D_S), BF16),
            jax.ShapeDtypeStruct((nseq_total, S_GROUPS, S_GW, S_STATE), F32),
        ],
        scratch_shapes=[pltpu.VMEM((S_GW, rows), F32), pltpu.VMEM((S_GW, rows), BF16)],
        compiler_params=pltpu.CompilerParams(
            dimension_semantics=("parallel", "parallel"), vmem_limit_bytes=VMEM_LIMIT),
        name="ssd_sample",
    )(proj, proj, proj, proj, gates_t, gate_bias_col, a_col, dsk_row, buf_rows, buf_rows, buf_rows, h0,
      conv_w, conv_b, conv_w, conv_b, conv_w, conv_b, norm_w)


OUT_TM, OUT_TN = 512, 512


def _out_proj_body(ids, x_ref, hm_ref, ys_ref, wt_ref, wb_ref, gn_ref, out_ref, x1g_ref, ssp_ref):
    x1 = x_ref[...] + _dot(hm_ref[...], wt_ref[...]) + _dot(ys_ref[...], wb_ref[...])
    out_ref[...] = x1
    x1g_ref[...] = (x1 * gn_ref[...]).astype(BF16)
    ssp_ref[0] = jnp.sum(x1 * x1, axis=1, keepdims=True)


def _out_proj_part(x, hm, ys, w_out, ple_norm, ids=_same_ids):
    m = x.shape[0]
    tm, tn = min(OUT_TM, m), OUT_TN
    nj = D_MODEL // tn
    spec = functools.partial(_on_grid, ids)
    return _Part(
        body=_out_proj_body, ids=ids,
        in_specs=[
            spec((tm, tn), lambda i, j: (i, j)),
            spec((tm, D_M), lambda i, j: (i, 0)),
            spec((tm, D_S), lambda i, j: (i, 0)),
            spec((D_M, tn), lambda i, j: (0, j)),
            spec((D_S, tn), lambda i, j: (1, j)),
            spec((1, tn), lambda i, j: (0, j)),
        ],
        operands=[x, hm, ys, w_out, w_out, ple_norm],
        out_specs=[
            spec((tm, tn), lambda i, j: (i, j)),
            spec((tm, tn), lambda i, j: (i, j)),
            spec((1, tm, 1), lambda i, j: (j, i, 0)),
        ],
        out_shape=[jax.ShapeDtypeStruct((m, D_MODEL), F32), jax.ShapeDtypeStruct((m, D_MODEL), BF16),
                   jax.ShapeDtypeStruct((nj, m, 1), F32)],
        scratch=[],
    ), (m // tm, nj)


def _ple_body(x1g_ref, ss1_ref, x1t_ref, p_ref, wg_ref, wp_ref, fn_ref, out_ref, ss_ref):
    j = pl.program_id(1)
    nj = pl.num_programs(1)
    tn = wg_ref.shape[1]

    @pl.when(j == 0)
    def _():
        ss_ref[...] = jnp.zeros_like(ss_ref)

    rs = lax.rsqrt(jnp.sum(ss1_ref[...], axis=0) * (1.0 / D_MODEL) + EPS)
    gate = _sigmoid(rs * _dot(x1g_ref[...], wg_ref[...]))
    x2 = x1t_ref[...] + gate * _dot(p_ref[...].astype(BF16), wp_ref[...])
    ss_ref[...] += jnp.sum(x2 * x2, axis=1, keepdims=True)
    for jj in range(out_ref.shape[1] // tn):
        @pl.when(j == jj)
        def _(jj=jj):
            out_ref[:, jj * tn:(jj + 1) * tn] = x2

    @pl.when(j == nj - 1)
    def _():
        fn = fn_ref[...]

        def body(r, carry):
            sl = pl.ds(pl.multiple_of(r * NORM_ROWS, NORM_ROWS), NORM_ROWS)
            scale = lax.rsqrt(ss_ref[sl, :] * (1.0 / D_MODEL) + EPS)
            out_ref[sl, :] = out_ref[sl, :] * scale * fn
            return carry
        lax.fori_loop(0, out_ref.shape[0] // NORM_ROWS, body, 0)


def _ple(x1, x1g, ss1, p, w_gate, w_proj, final_norm):
    m = x1.shape[0]
    tm = min(512, m)
    tn = 1024
    return pl.pallas_call(
        _ple_body,
        grid=(m // tm, D_MODEL // tn),
        in_specs=[
            pl.BlockSpec((tm, D_MODEL), lambda i, j: (i, 0)),
            pl.BlockSpec((ss1.shape[0], tm, 1), lambda i, j: (0, i, 0)),
            pl.BlockSpec((tm, tn), lambda i, j: (i, j)),
            pl.BlockSpec((tm, PLE_DIM), lambda i, j: (i, 0)),
            pl.BlockSpec((D_MODEL, tn), lambda i, j: (0, j)),
            pl.BlockSpec((PLE_DIM, tn), lambda i, j: (0, j)),
            pl.BlockSpec((1, D_MODEL), lambda i, j: (0, 0)),
        ],
        out_specs=pl.BlockSpec((tm, D_MODEL), lambda i, j: (i, 0)),
        out_shape=jax.ShapeDtypeStruct((m, D_MODEL), F32),
        scratch_shapes=[pltpu.VMEM((tm, 1), F32)],
        compiler_params=pltpu.CompilerParams(
            dimension_semantics=("parallel", "arbitrary"), vmem_limit_bytes=VMEM_LIMIT),
        name="ple",
    )(x1g, ss1, x1, p, w_gate, w_proj, final_norm)


CONV_TC = 2048


def _slot_row_match(nseq, slot_major_axis, first_token):
    shape = (nseq * SEQ4, (CONV_K - 1) * nseq) if slot_major_axis == 1 else ((CONV_K - 1) * nseq, nseq * SEQ4)
    sm = lax.broadcasted_iota(jnp.int32, shape, slot_major_axis)
    row = lax.broadcasted_iota(jnp.int32, shape, 1 - slot_major_axis)
    hit = None
    for slot in range(CONV_K - 1):
        b = sm - slot * nseq
        this = (b >= 0) & (b < nseq) & (row == b * SEQ4 + first_token + slot)
        hit = this if hit is None else hit | this
    return jnp.where(hit, 1.0, 0.0)


def _conv_rows_in_body(buf_ref, rows_ref):
    nseq = buf_ref.shape[1]
    pick = _slot_row_match(nseq, 1, 0)
    rows_ref[...] = _dot_select(pick, buf_ref[...].reshape((CONV_K - 1) * nseq, buf_ref.shape[2]))


def _conv_rows_in(buf):
    _, nseq, ch = buf.shape
    tc = min(CONV_TC, ch)
    return pl.pallas_call(
        _conv_rows_in_body,
        grid=(ch // tc,),
        in_specs=[pl.BlockSpec((CONV_K - 1, nseq, tc), lambda j: (0, 0, j))],
        out_specs=pl.BlockSpec((nseq * SEQ4, tc), lambda j: (0, j)),
        out_shape=jax.ShapeDtypeStruct((nseq * SEQ4, ch), F32),
        compiler_params=pltpu.CompilerParams(dimension_semantics=("parallel",), vmem_limit_bytes=VMEM_LIMIT),
        name="conv_rows_in",
    )(buf)


def _conv_rows_out_body(rows_ref, buf_ref):
    nseq = buf_ref.shape[1]
    pick = _slot_row_match(nseq, 0, SEQ4 - (CONV_K - 1))
    buf_ref[...] = _dot_select(pick, rows_ref[...]).reshape(buf_ref.shape)


def _conv_rows_out(proj, col0, ch):
    nseq = proj.shape[0] // SEQ4
    tc = min(CONV_TC, ch)
    assert col0 % tc == 0 and ch % tc == 0
    return pl.pallas_call(
        _conv_rows_out_body,
        grid=(ch // tc,),
        in_specs=[pl.BlockSpec((nseq * SEQ4, tc), lambda j: (0, col0 // tc + j))],
        out_specs=pl.BlockSpec((CONV_K - 1, nseq, tc), lambda j: (0, 0, j)),
        out_shape=jax.ShapeDtypeStruct((CONV_K - 1, nseq, ch), F32),
        compiler_params=pltpu.CompilerParams(dimension_semantics=("parallel",), vmem_limit_bytes=VMEM_LIMIT),
        name="conv_rows_out",
    )(proj)


def _pad_lanes(v, offset):
    out = jnp.zeros((1, LANES), F32)
    return lax.dynamic_update_slice(out, v.reshape(1, -1).astype(F32), (0, offset))


def kernel(x_prompt, x_sample, p_prompt, p_sample, state_m_C, state_m_n, state_m_m, state_m_conv, state_s_ssm, state_s_conv, norm_in, w_in, b_ig, b_fg, m_conv_w, m_conv_b, w_q, w_k, m_norm, s_conv_w, s_conv_b, dt_bias, a_log, d_skip, s_norm, w_out, ple_proj, ple_gate, ple_norm, final_norm):
    assert w_in.shape[0] == 1, "single layer"
    bsz, seqlen, _ = x_prompt.shape
    nseq, dec_seq, _ = x_sample.shape
    assert dec_seq == SEQ4

    assert w_in.shape[2] == N_MAIN + GATE_COLS + S_HEADS
    w_in_t = jnp.swapaxes(w_in, 1, 2)
    gate_bias = jnp.concatenate(
        [b_ig[0], b_fg[0], dt_bias[0], jnp.zeros((LANES - 2 * M_HEADS - S_HEADS,), F32)]).reshape(1, LANES)
    gate_bias_col = gate_bias.reshape(LANES, 1)
    a_col = _pad_lanes(a_log[0], GL_DT).reshape(LANES, 1)
    dsk_row = jnp.repeat(d_skip[0], S_HEAD_DIM).reshape(1, D_S)
    wq = w_q[0].astype(BF16)
    wkt = (jnp.swapaxes(w_k[0], 1, 2) * (M_QK ** -0.5)).astype(BF16)
    ple_proj_b = ple_proj[0].astype(BF16)
    norm_in_r = norm_in[0].reshape(1, D_MODEL)
    ple_norm_r = ple_norm[0].reshape(1, D_MODEL)
    final_norm_r = final_norm.reshape(1, D_MODEL)
    m_conv_b_r = m_conv_b[0].reshape(1, D_M)
    s_conv_b_r = s_conv_b[0].reshape(1, CONV_DIM)
    m_norm_r = m_norm[0].reshape(1, D_M)
    s_norm_r = s_norm[0].reshape(1, D_S)

    xp = x_prompt.reshape(bsz * seqlen, D_MODEL)
    xs_ = x_sample.reshape(nseq * SEQ4, D_MODEL)
    in_part, in_grid = _in_proj_part(_rmsnorm_cast(xp, norm_in_r), w_in_t)
    (proj_p, _, gates_pt), = _call_parts("in_proj", in_grid, [in_part])

    mlstm_part, mlstm_grid = _mlstm_prompt_part(
        proj_p, gates_pt, bsz, seqlen, gate_bias_col, m_conv_w[0], m_conv_b_r, wq, wkt, m_norm_r)
    nsteps = mlstm_grid[0] * mlstm_grid[1] * mlstm_grid[2]
    step = _linear_step(mlstm_grid)
    (hm_p, pc, pn, pm), (w_out_b,), (ple_gate_b,) = _call_parts(
        "mlstm_prompt_casts", mlstm_grid,
        [mlstm_part, _cast_part(w_out[0], nsteps, step), _cast_part(ple_gate[0], nsteps, step)])

    ssd_part, ssd_grid = _ssd_prompt_part(
        proj_p, gates_pt, bsz, seqlen, gate_bias_col, a_col, dsk_row, s_conv_w[0], s_conv_b_r, s_norm_r)
    (ys_p, ph), = _call_parts("ssd_prompt", ssd_grid, [ssd_part])
    in_part_s, in_grid_s = _in_proj_part(_rmsnorm_cast(xs_, norm_in_r), w_in_t)
    (proj_s, gates_s, gates_st), = _call_parts("in_proj", in_grid_s, [in_part_s])

    proj_p3 = proj_p.reshape(bsz, seqlen, N_MAIN)
    tail = seqlen - (CONV_K - 1)
    p_mconv = proj_p3[:, tail:, COL_U:COL_U + D_M]
    p_sconv = proj_p3[:, tail:, COL_XS:COL_XS + CONV_DIM]

    pad_rows = lambda st: _conv_rows_in(jnp.swapaxes(st, 0, 1))
    m_rows = jnp.pad(jnp.repeat(state_m_m[0], SEQ4, axis=0), ((0, 0), (0, LANES - M_HEADS)))

    sample_args = (proj_s, gates_s, nseq, gate_bias, pad_rows(state_m_conv[0]), m_rows, state_m_C[0],
                   state_m_n[0].reshape(nseq, M_HEADS, 1, M_QK), m_conv_w[0], m_conv_b_r, wq, wkt, m_norm_r)
    out_part, out_grid = _out_proj_part(xp, hm_p, ys_p, w_out_b, ple_norm_r)
    bb = SAMPLE_BB
    while bb > 1 and (nseq % bb or M_HEADS * (nseq // bb) < out_grid[0] * out_grid[1]):
        bb //= 2
    if M_HEADS * (nseq // bb) == out_grid[0] * out_grid[1]:
        nblk = nseq // bb
        as_head_block = lambda i, j: ((i * out_grid[1] + j) // nblk, (i * out_grid[1] + j) % nblk)
        scan_part, _ = _mlstm_sample_part(*sample_args, bb, ids=as_head_block)
        (x1_p, x1g_p, ss1_p), (hm_s, sc, sn, sm) = _call_parts("out_proj_mlstm_sample", out_grid, [out_part, scan_part])
    else:
        (x1_p, x1g_p, ss1_p), = _call_parts("out_proj", out_grid, [out_part])
        scan_part, scan_grid = _mlstm_sample_part(*sample_args, bb)
        (hm_s, sc, sn, sm), = _call_parts("mlstm_sample", scan_grid, [scan_part])
    y_p = _ple(x1_p, x1g_p, ss1_p, p_prompt[0].reshape(bsz * seqlen, PLE_DIM), ple_gate_b, ple_proj_b, final_norm_r)

    ys_s, sh = _ssd_sample(
        proj_s, gates_st, nseq, gate_bias_col, a_col, dsk_row, pad_rows(state_s_conv[0]),
        state_s_ssm[0].reshape(nseq, S_GROUPS, S_GW, S_STATE), s_conv_w[0], s_conv_b_r, s_norm_r)
    out_part_s, out_grid_s = _out_proj_part(xs_, hm_s, ys_s, w_out_b, ple_norm_r)
    (x1_s, x1g_s, ss1_s), = _call_parts("out_proj", out_grid_s, [out_part_s])
    y_s = _ple(x1_s, x1g_s, ss1_s, p_sample[0].reshape(nseq * SEQ4, PLE_DIM), ple_gate_b, ple_proj_b, final_norm_r)

    s_mconv = jnp.swapaxes(_conv_rows_out(proj_s, COL_U, D_M), 0, 1)
    s_sconv = jnp.swapaxes(_conv_rows_out(proj_s, COL_XS, CONV_DIM), 0, 1)
    s_m = jnp.transpose(sm[:, SEQ4 - 1::SEQ4, 0])

    return (
        y_p.reshape(bsz, seqlen, D_MODEL),
        y_s.reshape(nseq, SEQ4, D_MODEL),
        pc[None], pn.reshape(1, bsz, M_HEADS, M_QK), pm[:, :, 0, 0][None], p_mconv[None],
        ph.reshape(1, bsz, S_HEADS, S_HEAD_DIM, S_STATE), p_sconv[None],
        sc[None], sn.reshape(1, nseq, M_HEADS, M_QK), s_m[None], s_mconv[None],
        sh.reshape(1, nseq, S_HEADS, S_HEAD_DIM, S_STATE), s_sconv[None],
    )
```

```python
import functools
from typing import Callable, NamedTuple

import jax
import jax.numpy as jnp
from jax import lax
from jax.experimental import pallas as pl
from jax.experimental.pallas import tpu as pltpu

F32 = jnp.float32
BF16 = jnp.bfloat16

D_MODEL = 4096
D_M = 4096
D_S = 4096
M_HEADS = 8
M_V = 512
M_QK = 256
S_HEADS = 64
S_HEAD_DIM = 64
S_STATE = 128
S_GROUPS = 8
S_HPG = 8
S_GW = S_HPG * S_HEAD_DIM
CONV_DIM = D_S + 2 * S_GROUPS * S_STATE
CONV_K = 4
PLE_DIM = 256
GATE_CAP = 15.0
EPS = 1e-6

COL_U, COL_V, COL_O, COL_ZM, COL_ZS, COL_XS = 0, 4096, 8192, 12288, 16384, 20480
COL_BM = COL_XS + D_S
COL_CM = COL_BM + S_GROUPS * S_STATE
N_MAIN = COL_XS + CONV_DIM
LANES = 128
GL_IG, GL_FG, GL_DT = 0, M_HEADS, 2 * M_HEADS

PROMPT_CHUNK = 256
SAMPLE_BB = 16
VMEM_LIMIT = 62 * 1024 * 1024

NT_DIMS = (((1,), (1,)), ((), ()))


def _dot(a, b):
    return jnp.dot(a, b, preferred_element_type=F32)


def _dot_nt(a, b):
    return lax.dot_general(a, b, NT_DIMS, preferred_element_type=F32)


def _rows_dot_01(a, b01):
    n = a.shape[0]
    hi = a.astype(BF16).astype(F32)
    rest = a - hi
    mid = rest.astype(BF16).astype(F32)
    pieces = jnp.concatenate([hi, mid, rest - mid], axis=0).astype(BF16)
    r = _dot(pieces, b01.astype(BF16))
    return (r[0:n] + r[n:2 * n]) + r[2 * n:3 * n]


def _dot_select(pick, x):
    hi = x.astype(BF16)
    rest = x - hi.astype(F32)
    mid = rest.astype(BF16)
    lo = (rest - mid.astype(F32)).astype(BF16)
    p = pick.astype(BF16)
    return (_dot(p, hi) + _dot(p, mid)) + _dot(p, lo)


def _sigmoid(x):
    return 0.5 * jnp.tanh(0.5 * x) + 0.5


def _silu(x):
    return x * _sigmoid(x)


def _softcap(x):
    return GATE_CAP * jnp.tanh(x / GATE_CAP)


def _softplus(x):
    return jnp.maximum(x, 0.0) + jnp.log1p(jnp.exp(-jnp.abs(x)))


def _log_sigmoid(x):
    return -_softplus(-x)


def _idiv(x, d):
    return lax.shift_right_logical(x, d.bit_length() - 1)


def _imod(x, d):
    return lax.bitwise_and(x, d - 1)


def _lane_pick(x, lane_idx):
    lane = lax.broadcasted_iota(jnp.int32, (1, x.shape[1]), 1)
    return jnp.sum(jnp.where(lane == lane_idx, x, 0.0), axis=1, keepdims=True)


def _col_to_row(col, eye):
    return jnp.sum(jnp.where(eye, col, 0.0), axis=0, keepdims=True)


def _row_to_col(row, eye):
    return jnp.sum(jnp.where(eye, row, 0.0), axis=1, keepdims=True)


class _Part(NamedTuple):
    body: Callable
    ids: Callable
    in_specs: list
    operands: list
    out_specs: list
    out_shape: list
    scratch: list


def _same_ids(*g):
    return g


def _on_grid(ids, block_shape, index_map, **kwargs):
    return pl.BlockSpec(block_shape, lambda *g: index_map(*ids(*g)), **kwargs)


def _call_parts(name, grid, parts):
    n_in = [len(p.in_specs) for p in parts]
    n_out = [len(p.out_specs) for p in parts]
    n_scr = [len(p.scratch) for p in parts]

    def body(*refs):
        gids = tuple(pl.program_id(a) for a in range(len(grid)))
        ins, outs, scr = refs[:sum(n_in)], refs[sum(n_in):sum(n_in) + sum(n_out)], refs[sum(n_in) + sum(n_out):]
        for p, part in enumerate(parts):
            a, b, c = sum(n_in[:p]), sum(n_out[:p]), sum(n_scr[:p])
            part.body(part.ids(*gids), *ins[a:a + n_in[p]], *outs[b:b + n_out[p]], *scr[c:c + n_scr[p]])

    results = pl.pallas_call(
        body,
        grid=grid,
        in_specs=[s for p in parts for s in p.in_specs],
        out_specs=[s for p in parts for s in p.out_specs],
        out_shape=[s for p in parts for s in p.out_shape],
        scratch_shapes=[s for p in parts for s in p.scratch],
        compiler_params=pltpu.CompilerParams(
            dimension_semantics=("arbitrary",) * len(grid), vmem_limit_bytes=VMEM_LIMIT),
        name=name,
    )(*[o for p in parts for o in p.operands])
    return [results[sum(n_out[:p]):sum(n_out[:p]) + n_out[p]] for p in range(len(parts))]


def _linear_step(grid):
    def lin(*g):
        s = g[0]
        for size, idx in zip(grid[1:], g[1:]):
            s = s * size + idx
        return s
    return lin


def _cast_body(ids, src_ref, dst_ref):
    dst_ref[...] = src_ref[...].astype(dst_ref.dtype)


def _cast_part(w, nsteps, step):
    rows = w.shape[0] // nsteps
    assert rows * nsteps == w.shape[0] and rows % 16 == 0
    spec = pl.BlockSpec((rows, w.shape[1]), lambda *g: (step(*g), 0))
    return _Part(body=_cast_body, ids=_same_ids, in_specs=[spec], operands=[w], out_specs=[spec],
                 out_shape=[jax.ShapeDtypeStruct(w.shape, BF16)], scratch=[])


NORM_ROWS = 32


def _rmsnorm_rows_to(x_ref, g, dst_ref):
    def body(r, carry):
        sl = pl.ds(pl.multiple_of(r * NORM_ROWS, NORM_ROWS), NORM_ROWS)
        x = x_ref[sl, :]
        ms = jnp.mean(x * x, axis=-1, keepdims=True)
        dst_ref[sl, :] = (x * lax.rsqrt(ms + EPS) * g).astype(BF16)
        return carry
    lax.fori_loop(0, x_ref.shape[0] // NORM_ROWS, body, 0)


def _in_proj_body(x_ref, g_ref, w_ref, wg_ref, out_ref, gates_ref, gatest_ref, xn_ref):
    @pl.when(pl.program_id(1) == 0)
    def _():
        _rmsnorm_rows_to(x_ref, g_ref[...], xn_ref)
        gates = _dot_nt(xn_ref[...], wg_ref[...])
        gates_ref[...] = gates
        gatest_ref[...] = gates.T
    out_ref[...] = _dot_nt(xn_ref[...], w_ref[...])


def _in_proj(x, norm_w, w_main, w_gate):
    m = x.shape[0]
    tm = min(512, m)
    tn = 1024
    grid = (m // tm, N_MAIN // tn)
    return pl.pallas_call(
        _in_proj_body,
        grid=grid,
        in_specs=[
            pl.BlockSpec((tm, D_MODEL), lambda i, j: (i, 0)),
            pl.BlockSpec((1, D_MODEL), lambda i, j: (0, 0)),
            pl.BlockSpec((tn, D_MODEL), lambda i, j: (j, 0)),
            pl.BlockSpec((LANES, D_MODEL), lambda i, j: (0, 0)),
        ],
        out_specs=[
            pl.BlockSpec((tm, tn), lambda i, j: (i, j)),
            pl.BlockSpec((tm, LANES), lambda i, j: (i, 0)),
            pl.BlockSpec((LANES, tm), lambda i, j: (0, i)),
        ],
        out_shape=[jax.ShapeDtypeStruct((m, N_MAIN), F32), jax.ShapeDtypeStruct((m, LANES), F32),
                   jax.ShapeDtypeStruct((LANES, m), F32)],
        scratch_shapes=[pltpu.VMEM((tm, D_MODEL), BF16)],
        compiler_params=pltpu.CompilerParams(
            dimension_semantics=("parallel", "arbitrary"), vmem_limit_bytes=VMEM_LIMIT),
        name="in_proj",
    )(x, norm_w, w_main, w_gate)


def _rmsnorm_cast_body(x_ref, g_ref, out_ref):
    _rmsnorm_rows_to(x_ref, g_ref[...], out_ref)


def _rmsnorm_cast(x, norm_w):
    m = x.shape[0]
    tm = min(512, m)
    return pl.pallas_call(
        _rmsnorm_cast_body,
        grid=(m // tm,),
        in_specs=[pl.BlockSpec((tm, D_MODEL), lambda i: (i, 0)), pl.BlockSpec((1, D_MODEL), lambda i: (0, 0))],
        out_specs=pl.BlockSpec((tm, D_MODEL), lambda i: (i, 0)),
        out_shape=jax.ShapeDtypeStruct((m, D_MODEL), BF16),
        compiler_params=pltpu.CompilerParams(dimension_semantics=("parallel",), vmem_limit_bytes=VMEM_LIMIT),
        name="rmsnorm_cast",
    )(x, norm_w)


def _in_proj_w32_body(ids, xn_ref, w_ref, ga_ref, gb_ref, out_ref, gates_ref, gatest_ref, wg_ref):
    @pl.when(ids[1] == 0)
    def _():
        wg_ref[0:GL_DT, :] = ga_ref[0].astype(BF16)
        wg_ref[GL_DT:GL_DT + S_HEADS, :] = gb_ref[0].astype(BF16)
        wg_ref[GL_DT + S_HEADS:LANES, :] = jnp.zeros((LANES - GL_DT - S_HEADS, D_MODEL), BF16)
        gates = _dot_nt(xn_ref[...], wg_ref[...])
        gates_ref[...] = gates
        gatest_ref[...] = gates.T
    out_ref[...] = _dot_nt(xn_ref[...], w_ref[0].astype(BF16))


IN_TM, IN_TN = 2048, 512


def _in_proj_part(xn, w_in_t, ids=_same_ids):
    m = xn.shape[0]
    tm = min(IN_TM, m)
    tn = IN_TN if tm * 4 > IN_TM else 2 * IN_TN

    def src_row(j):
        return pl.multiple_of(j * tn + (j // (N_ALIGNED // tn)) * GATE_COLS, GATE_COLS)

    el = pl.Element
    spec = functools.partial(_on_grid, ids)
    return _Part(
        body=_in_proj_w32_body, ids=ids,
        in_specs=[
            spec((tm, D_MODEL), lambda i, j: (i, 0), pipeline_mode=pl.Buffered(1)),
            spec((el(1), el(tn), el(D_MODEL)), lambda i, j: (0, src_row(j), 0)),
            spec((el(1), el(GATE_COLS), el(D_MODEL)), lambda i, j: (0, N_ALIGNED, 0)),
            spec((el(1), el(S_HEADS), el(D_MODEL)), lambda i, j: (0, N_MAIN + GATE_COLS, 0)),
        ],
        operands=[xn, w_in_t, w_in_t, w_in_t],
        out_specs=[
            spec((tm, tn), lambda i, j: (i, j)),
            spec((tm, LANES), lambda i, j: (i, 0)),
            spec((LANES, tm), lambda i, j: (0, i)),
        ],
        out_shape=[jax.ShapeDtypeStruct((m, N_MAIN), F32), jax.ShapeDtypeStruct((m, LANES), F32),
                   jax.ShapeDtypeStruct((LANES, m), F32)],
        scratch=[pltpu.VMEM((LANES, D_MODEL), BF16)],
    ), (m // tm, N_MAIN // tn)


GATE_COLS = 2 * M_HEADS
N_ALIGNED = 4 * D_M


def _w_prep_body(src_ref, ga_ref, gb_ref, out_ref, wg_ref):
    out_ref[...] = src_ref[0].astype(BF16)

    @pl.when(pl.program_id(0) == 0)
    def _():
        wg_ref[0:GL_DT, :] = ga_ref[0].astype(BF16)
        wg_ref[GL_DT:GL_DT + S_HEADS, :] = gb_ref[0].astype(BF16)
        wg_ref[GL_DT + S_HEADS:LANES, :] = jnp.zeros((LANES - GL_DT - S_HEADS, D_MODEL), BF16)


def _w_prep(w_in_t):
    rb = 512

    def src_row(i):
        return pl.multiple_of(i * rb + (i // (N_ALIGNED // rb)) * GATE_COLS, GATE_COLS)

    return pl.pallas_call(
        _w_prep_body,
        grid=(N_MAIN // rb,),
        in_specs=[
            pl.BlockSpec((pl.Element(1), pl.Element(rb), pl.Element(D_MODEL)), lambda i: (0, src_row(i), 0)),
            pl.BlockSpec((pl.Element(1), pl.Element(GATE_COLS), pl.Element(D_MODEL)), lambda i: (0, N_ALIGNED, 0)),
            pl.BlockSpec((pl.Element(1), pl.Element(S_HEADS), pl.Element(D_MODEL)),
                         lambda i: (0, N_MAIN + GATE_COLS, 0)),
        ],
        out_specs=[pl.BlockSpec((rb, D_MODEL), lambda i: (i, 0)), pl.BlockSpec((LANES, D_MODEL), lambda i: (0, 0))],
        out_shape=[jax.ShapeDtypeStruct((N_MAIN, D_MODEL), BF16), jax.ShapeDtypeStruct((LANES, D_MODEL), BF16)],
        compiler_params=pltpu.CompilerParams(dimension_semantics=("arbitrary",), vmem_limit_bytes=VMEM_LIMIT),
        name="w_prep",
    )(w_in_t, w_in_t, w_in_t)


TAIL = 8


def _conv_carry(u, ext_ref, cw, cb):
    c = u.shape[0]
    ext_ref[TAIL:TAIL + c, :] = u
    acc = cb + u * cw[CONV_K - 1:CONV_K, :]
    for j in range(1, CONV_K):
        acc = acc + ext_ref[TAIL - j:TAIL - j + c, :] * cw[CONV_K - 1 - j:CONV_K - j, :]
    ext_ref[0:TAIL, :] = u[c - TAIL:c, :]
    return acc


def _conv_rows(u, buf, cw, cb, t):
    rows = u.shape[0]
    acc = cb + u * cw[CONV_K - 1:CONV_K, :]
    for j in range(1, CONV_K):
        prev = buf if j == CONV_K - 1 else pltpu.roll(buf, rows + j - (CONV_K - 1), 0)
        acc = acc + jnp.where(t >= j, pltpu.roll(u, j, 0), prev) * cw[CONV_K - 1 - j:CONV_K - j, :]
    return acc


def _head_out(hv, o, z, nw):
    og = _sigmoid(o) * hv
    ms = jnp.mean(og * og, axis=-1, keepdims=True)
    return (og * lax.rsqrt(ms + EPS) * nw) * _silu(z)


def _group_out(y, xs, dsk, z, nw):
    gated = (y + dsk * xs) * _silu(z)
    ms = jnp.mean(gated * gated, axis=-1, keepdims=True)
    return gated * lax.rsqrt(ms + EPS) * nw


def _mlstm_prompt_body(ids, u_ref, v_ref, o_ref, z_ref, gt_ref, gb_ref, cw_ref, cb_ref, wq_ref, wkt_ref, nw_ref,
                       hm_ref, c_ref, n_ref, m_ref, ext_ref, mst_ref):
    c = u_ref.shape[0]
    hps = wq_ref.shape[0]
    h0 = ids[1] * hps

    @pl.when(ids[2] == 0)
    def _():
        c_ref[...] = jnp.zeros_like(c_ref)
        n_ref[...] = jnp.zeros_like(n_ref)
        mst_ref[...] = jnp.full_like(mst_ref, -jnp.inf)
        ext_ref[0:TAIL, :] = jnp.zeros((TAIL, ext_ref.shape[1]), F32)

    ucb = _silu(_conv_carry(u_ref[...], ext_ref, cw_ref[...], cb_ref[...])).astype(BF16)
    vb = v_ref[...].astype(BF16)

    rowi = lax.broadcasted_iota(jnp.int32, (c, c), 0)
    coli = lax.broadcasted_iota(jnp.int32, (c, c), 1)
    tri = coli <= rowi
    eye = coli == rowi
    ig8 = _softcap(gt_ref[GL_IG:GL_IG + M_HEADS, :] + gb_ref[GL_IG:GL_IG + M_HEADS, :])
    lf8 = _log_sigmoid(_softcap(gt_ref[GL_FG:GL_FG + M_HEADS, :] + gb_ref[GL_FG:GL_FG + M_HEADS, :]))
    b8 = _rows_dot_01(lf8, jnp.where(rowi <= coli, 1.0, 0.0))
    head_row = lax.broadcasted_iota(jnp.int32, (M_HEADS, 1), 0)

    for hh in range(hps):
        cols = slice(hh * M_V, (hh + 1) * M_V)
        pick = head_row == h0 + hh
        ig_row = jnp.sum(jnp.where(pick, ig8, 0.0), axis=0, keepdims=True)
        b_row = jnp.sum(jnp.where(pick, b8, 0.0), axis=0, keepdims=True)
        b_col = _row_to_col(b_row, eye)
        m_prev = mst_ref[hh:hh + 1, :][:, 0:1]

        q = _dot(ucb[:, cols], wq_ref[hh])
        kt = _dot_nt(wkt_ref[hh], ucb[:, cols])
        qb = q.astype(BF16)
        ktb = kt.astype(BF16)

        r_row = ig_row - b_row
        dmat = jnp.where(tri, b_col + r_row, -jnp.inf)
        inter = b_col + m_prev
        m_t = jnp.maximum(inter, jnp.max(dmat, axis=1, keepdims=True))
        a_inter = jnp.exp(inter - m_t)
        s = _dot(qb, ktb) * jnp.exp(dmat - m_t)
        c_old = c_ref[0, hh]
        n_old = n_ref[0, hh]
        num = _dot(s.astype(BF16), vb[:, cols]) + a_inter * _dot(qb, c_old.astype(BF16))
        qn = _dot_nt(qb, jnp.broadcast_to(n_old, (8, M_QK)).astype(BF16))[:, 0:1]
        den = jnp.sum(s, axis=1, keepdims=True) + a_inter * qn
        hv = num / jnp.maximum(jnp.abs(den), jnp.exp(-m_t))
        hm_ref[:, cols] = _head_out(hv, o_ref[:, cols], z_ref[:, cols], nw_ref[:, cols]).astype(BF16)

        b_end = b_row[:, c - 1:c]
        lw_row = b_end + r_row
        m_new = jnp.maximum(b_end + m_prev, jnp.max(lw_row, axis=1, keepdims=True))
        decay = jnp.exp(b_end + m_prev - m_new)
        wk_row = jnp.exp(lw_row - m_new)
        c_ref[0, hh] = decay * c_old + _dot((kt * wk_row).astype(BF16), vb[:, cols])
        n_ref[0, hh] = decay * n_old + _dot_nt(jnp.broadcast_to(wk_row, (8, c)).astype(BF16), ktb)[0:1, :]
        mst_ref[hh:hh + 1, :] = jnp.broadcast_to(m_new, (1, LANES))
        m_ref[0, hh] = jnp.broadcast_to(m_new, (1, LANES))


MLSTM_HPS = 4


def _mlstm_prompt_part(proj, gates_t, bsz, seqlen, gate_bias_col, conv_w, conv_b, wq, wkt, norm_w):
    c = min(PROMPT_CHUNK, seqlen)
    nch = seqlen // c
    hps = MLSTM_HPS
    width = hps * M_V

    def tok(col0):
        return pl.BlockSpec((c, width), lambda b, h, k: (b * nch + k, col0 // width + h))

    head_vec = lambda rows: pl.BlockSpec((rows, width), lambda b, h, k: (0, h))
    return _Part(
        body=_mlstm_prompt_body, ids=_same_ids,
        in_specs=[
            tok(COL_U), tok(COL_V), tok(COL_O), tok(COL_ZM),
            pl.BlockSpec((LANES, c), lambda b, h, k: (0, b * nch + k)),
            pl.BlockSpec((LANES, 1), lambda b, h, k: (0, 0)),
            head_vec(CONV_K), head_vec(1),
            pl.BlockSpec((hps, M_V, M_QK), lambda b, h, k: (h, 0, 0)),
            pl.BlockSpec((hps, M_QK, M_V), lambda b, h, k: (h, 0, 0)),
            head_vec(1),
        ],
        operands=[proj, proj, proj, proj, gates_t, gate_bias_col, conv_w, conv_b, wq, wkt, norm_w],
        out_specs=[
            pl.BlockSpec((c, width), lambda b, h, k: (b * nch + k, h)),
            pl.BlockSpec((1, hps, M_QK, M_V), lambda b, h, k: (b, h, 0, 0)),
            pl.BlockSpec((1, hps, 1, M_QK), lambda b, h, k: (b, h, 0, 0)),
            pl.BlockSpec((1, hps, 1, LANES), lambda b, h, k: (b, h, 0, 0)),
        ],
        out_shape=[
            jax.ShapeDtypeStruct((bsz * seqlen, D_M), BF16),
            jax.ShapeDtypeStruct((bsz, M_HEADS, M_QK, M_V), F32),
            jax.ShapeDtypeStruct((bsz, M_HEADS, 1, M_QK), F32),
            jax.ShapeDtypeStruct((bsz, M_HEADS, 1, LANES), F32),
        ],
        scratch=[pltpu.VMEM((TAIL + c, width), F32), pltpu.VMEM((8, LANES), F32)],
    ), (bsz, M_HEADS // hps, nch)


LOG2E = 1.4426950408889634
HEAD_ROWS = [slice(r * S_HEAD_DIM, (r + 1) * S_HEAD_DIM) for r in range(S_HPG)]


def _ssd_gate_rows(gt_ref, gbc_ref, avc_ref, lane0):
    rows = pl.ds(lane0, S_HPG)
    dt8 = _softplus(gt_ref[rows, :] + gbc_ref[rows, :])
    return dt8, dt8 * (-jnp.exp(avc_ref[rows, :]))


def _ssd_intra(xst, dt8, cum8, cbt):
    c = xst.shape[1]
    a2r = cum8 * LOG2E
    a2c = jnp.concatenate([a2r, jnp.zeros((LANES - S_HPG, c), F32)], axis=0).T
    ys, xdts = [], []
    for r in range(S_HPG):
        xdt = xst[HEAD_ROWS[r], :] * dt8[r:r + 1, :]
        decay = jnp.exp2(jnp.minimum(a2r[r:r + 1, :] - a2c[:, r:r + 1], 0.0))
        ys.append(_dot(xdt.astype(BF16), (cbt * decay).astype(BF16)))
        xdts.append(xdt)
    return ys, xdts


def _ssd_prompt_body(ids, xs_ref, bm_ref, cm_ref, zs_ref, gt_ref, gbc_ref, avc_ref, dsk_ref,
                     cwx_ref, cbx_ref, cwb_ref, cbb_ref, cwc_ref, cbc_ref, nw_ref,
                     ys_ref, h_ref, extx_ref, extb_ref, extc_ref):
    c = xs_ref.shape[0]
    gps = h_ref.shape[1]
    g0 = ids[1] * gps

    @pl.when(ids[2] == 0)
    def _():
        h_ref[...] = jnp.zeros_like(h_ref)
        extx_ref[0:TAIL, :] = jnp.zeros((TAIL, extx_ref.shape[1]), F32)
        extb_ref[0:TAIL, :] = jnp.zeros((TAIL, extb_ref.shape[1]), F32)
        extc_ref[0:TAIL, :] = jnp.zeros((TAIL, extc_ref.shape[1]), F32)

    xs_all = _silu(_conv_carry(xs_ref[...], extx_ref, cwx_ref[...], cbx_ref[...]))
    bmb_all = _silu(_conv_carry(bm_ref[...], extb_ref, cwb_ref[...], cbb_ref[...])).astype(BF16)
    cmb_all = _silu(_conv_carry(cm_ref[...], extc_ref, cwc_ref[...], cbc_ref[...])).astype(BF16)

    src = lax.broadcasted_iota(jnp.int32, (c, c), 0)
    tgt = lax.broadcasted_iota(jnp.int32, (c, c), 1)
    causal = src <= tgt
    upper = jnp.where(causal, 1.0, 0.0)

    for gg in range(gps):
        cols = slice(gg * S_GW, (gg + 1) * S_GW)
        xs = xs_all[:, cols]
        bmb = bmb_all[:, gg * S_STATE:(gg + 1) * S_STATE]
        cmb = cmb_all[:, gg * S_STATE:(gg + 1) * S_STATE]
        lane0 = pl.multiple_of(GL_DT + (g0 + gg) * S_HPG, S_HPG)
        dt8, la8 = _ssd_gate_rows(gt_ref, gbc_ref, avc_ref, lane0)
        cum8 = _rows_dot_01(la8, upper)
        a_end = cum8[:, c - 1:c]
        w8 = jnp.exp(a_end - cum8)
        ea8 = jnp.exp(cum8)

        cbt = jnp.where(causal, _dot_nt(bmb, cmb), 0.0)
        ys, xdts = _ssd_intra(xs.T, dt8, cum8, cbt)
        h_old = h_ref[0, gg]
        yit = _dot_nt(h_old.astype(BF16), cmb)
        yt = jnp.concatenate([ys[r] + ea8[r:r + 1, :] * yit[HEAD_ROWS[r], :] for r in range(S_HPG)], axis=0)
        ys_ref[:, cols] = _group_out(yt.T, xs, dsk_ref[:, cols], zs_ref[:, cols], nw_ref[:, cols]).astype(BF16)

        xwt = jnp.concatenate([(xdts[r] * w8[r:r + 1, :]).astype(BF16) for r in range(S_HPG)], axis=0)
        upd = _dot(xwt, bmb)
        ea_end = jnp.exp(a_end)
        for r in range(S_HPG):
            h_ref[0, gg, HEAD_ROWS[r], :] = ea_end[r:r + 1, :] * h_old[HEAD_ROWS[r], :] + upd[HEAD_ROWS[r], :]


SSD_GPS = 4


def _ssd_prompt_part(proj, gates_t, bsz, seqlen, gate_bias_col, a_col, dsk_row, conv_w, conv_b, norm_w):
    c = min(PROMPT_CHUNK, seqlen)
    nch = seqlen // c
    gps = SSD_GPS
    tok = lambda width, col0: pl.BlockSpec((c, gps * width), lambda b, g, k: (b * nch + k, col0 // (gps * width) + g))
    vec = lambda rows, width, col0: pl.BlockSpec((rows, gps * width), lambda b, g, k: (0, col0 // (gps * width) + g))
    col = pl.BlockSpec((LANES, 1), lambda b, g, k: (0, 0))
    return _Part(
        body=_ssd_prompt_body, ids=_same_ids,
        in_specs=[
            tok(S_GW, COL_XS), tok(S_STATE, COL_BM), tok(S_STATE, COL_CM), tok(S_GW, COL_ZS),
            pl.BlockSpec((LANES, c), lambda b, g, k: (0, b * nch + k)),
            col, col, vec(1, S_GW, 0),
            vec(CONV_K, S_GW, 0), vec(1, S_GW, 0),
            vec(CONV_K, S_STATE, D_S), vec(1, S_STATE, D_S),
            vec(CONV_K, S_STATE, D_S + S_GROUPS * S_STATE), vec(1, S_STATE, D_S + S_GROUPS * S_STATE),
            vec(1, S_GW, 0),
        ],
        operands=[proj, proj, proj, proj, gates_t, gate_bias_col, a_col, dsk_row,
                  conv_w, conv_b, conv_w, conv_b, conv_w, conv_b, norm_w],
        out_specs=[
            pl.BlockSpec((c, gps * S_GW), lambda b, g, k: (b * nch + k, g)),
            pl.BlockSpec((1, gps, S_GW, S_STATE), lambda b, g, k: (b, g, 0, 0)),
        ],
        out_shape=[
            jax.ShapeDtypeStruct((bsz * seqlen, D_S), BF16),
            jax.ShapeDtypeStruct((bsz, S_GROUPS, S_GW, S_STATE), F32),
        ],
        scratch=[pltpu.VMEM((TAIL + c, gps * S_GW), F32), pltpu.VMEM((TAIL + c, gps * S_STATE), F32),
                 pltpu.VMEM((TAIL + c, gps * S_STATE), F32)],
    ), (bsz, S_GROUPS // gps, nch)


SEQ4 = 4


def _row_masks(rows):
    rowi = lax.broadcasted_iota(jnp.int32, (rows, rows), 0)
    coli = lax.broadcasted_iota(jnp.int32, (rows, rows), 1)
    same = _idiv(rowi, SEQ4) == _idiv(coli, SEQ4)
    return same, same & (coli <= rowi), coli == rowi, same & (_imod(coli, SEQ4) == SEQ4 - 1)


def _seg_cumsum(x, t):
    out = x
    for j in range(1, SEQ4):
        out = out + jnp.where(t >= j, pltpu.roll(x, j, 0), 0.0)
    return out


def _mlstm_sample_body(ids, u_ref, v_ref, o_ref, z_ref, gt_ref, gb_ref, buf_ref, mrow_ref, c0_ref, n0_ref,
                       cw_ref, cb_ref, wq_ref, wkt_ref, nw_ref,
                       hm_ref, c_ref, n_ref, m_ref):
    h = ids[0]
    rows = u_ref.shape[0]
    nseq = rows // SEQ4
    rid = lax.broadcasted_iota(jnp.int32, (rows, 1), 0)
    t = _imod(rid, SEQ4)
    seq = _idiv(rid, SEQ4)
    seq_lane = _idiv(lax.broadcasted_iota(jnp.int32, (1, rows), 1), SEQ4)

    ucb = _silu(_conv_rows(u_ref[...], buf_ref[...], cw_ref[...], cb_ref[...], t)).astype(BF16)
    q = _dot(ucb, wq_ref[0])
    kt = _dot_nt(wkt_ref[0], ucb)
    qb = q.astype(BF16)
    vb = v_ref[...].astype(BF16)

    capped = _softcap(gt_ref[...] + gb_ref[...])
    ig_col = _lane_pick(capped, GL_IG + h)
    b_col = _lane_pick(_seg_cumsum(_log_sigmoid(capped), t), GL_FG + h)
    m_prev = _lane_pick(mrow_ref[...], h)

    same, tri, eye, last = _row_masks(rows)
    b_row = _col_to_row(b_col, eye)
    ig_row = _col_to_row(ig_col, eye)
    dmat = jnp.where(tri, b_col - b_row + ig_row, -jnp.inf)
    inter = b_col + m_prev
    m_t = jnp.maximum(inter, jnp.max(dmat, axis=1, keepdims=True))
    a_inter = jnp.exp(inter - m_t)
    s = _dot(qb, kt.astype(BF16)) * jnp.exp(dmat - m_t)

    b_end = jnp.sum(jnp.where(last, b_row, 0.0), axis=1, keepdims=True)
    lw = b_end - b_col + ig_col
    m_new = jnp.maximum(b_end + m_prev, jnp.max(jnp.where(same, _col_to_row(lw, eye), -jnp.inf), axis=1, keepdims=True))
    decay = jnp.exp(b_end + m_prev - m_new)
    kwt = kt * _col_to_row(jnp.exp(lw - m_new), eye)
    eye_k = (lax.broadcasted_iota(jnp.int32, (M_QK, M_QK), 0) == lax.broadcasted_iota(jnp.int32, (M_QK, M_QK), 1))

    num_inter = jnp.zeros((rows, M_V), F32)
    qn = jnp.zeros((rows, 1), F32)
    for bi in range(nseq):
        own = seq == bi
        c_old = c0_ref[bi, 0]
        n_old = n0_ref[bi, 0]
        num_inter = jnp.where(own, _dot(qb, c_old.astype(BF16)), num_inter)
        qn = jnp.where(own, jnp.sum(q * n_old, axis=1, keepdims=True), qn)
        kw_b = jnp.where(seq_lane == bi, kwt, 0.0)
        d_b = jnp.sum(jnp.where(own & (t == SEQ4 - 1), decay, 0.0), axis=0, keepdims=True)
        c_ref[bi, 0] = d_b * c_old + _dot(kw_b.astype(BF16), vb)
        n_ref[bi, 0] = d_b * n_old + _col_to_row(jnp.sum(kw_b, axis=1, keepdims=True), eye_k)

    num = _dot(s.astype(BF16), vb) + a_inter * num_inter
    den = jnp.sum(s, axis=1, keepdims=True) + a_inter * qn
    hv = num / jnp.maximum(jnp.abs(den), jnp.exp(-m_t))
    hm_ref[...] = _head_out(hv, o_ref[...], z_ref[...], nw_ref[...]).astype(BF16)
    m_ref[0] = jnp.broadcast_to(m_new, (rows, LANES))


def _mlstm_sample_part(proj, gates, nseq_total, gate_bias, buf_rows, m_rows, c0, n0, conv_w, conv_b, wq, wkt, norm_w,
                       bb, ids=_same_ids):
    rows = bb * SEQ4
    nblk = nseq_total // bb
    spec = functools.partial(_on_grid, ids)
    tok = lambda col0: spec((rows, M_V), lambda h, i: (i, col0 // M_V + h))
    head_vec = lambda r: spec((r, M_V), lambda h, i: (0, h))
    c_spec = spec((bb, 1, M_QK, M_V), lambda h, i: (i, h, 0, 0))
    n_spec = spec((bb, 1, 1, M_QK), lambda h, i: (i, h, 0, 0))
    return _Part(
        body=_mlstm_sample_body, ids=ids,
        in_specs=[
            tok(COL_U), tok(COL_V), tok(COL_O), tok(COL_ZM),
            spec((rows, LANES), lambda h, i: (i, 0)),
            spec((1, LANES), lambda h, i: (0, 0)),
            spec((rows, M_V), lambda h, i: (i, h)),
            spec((rows, LANES), lambda h, i: (i, 0)),
            c_spec, n_spec,
            head_vec(CONV_K), head_vec(1),
            spec((1, M_V, M_QK), lambda h, i: (h, 0, 0)),
            spec((1, M_QK, M_V), lambda h, i: (h, 0, 0)),
            head_vec(1),
        ],
        operands=[proj, proj, proj, proj, gates, gate_bias, buf_rows, m_rows, c0, n0, conv_w, conv_b, wq, wkt, norm_w],
        out_specs=[
            spec((rows, M_V), lambda h, i: (i, h)),
            c_spec, n_spec,
            spec((1, rows, LANES), lambda h, i: (h, i, 0)),
        ],
        out_shape=[
            jax.ShapeDtypeStruct((nseq_total * SEQ4, D_M), BF16),
            jax.ShapeDtypeStruct((nseq_total, M_HEADS, M_QK, M_V), F32),
            jax.ShapeDtypeStruct((nseq_total, M_HEADS, 1, M_QK), F32),
            jax.ShapeDtypeStruct((M_HEADS, nseq_total * SEQ4, LANES), F32),
        ],
        scratch=[],
    ), (M_HEADS, nblk)


def _ssd_sample_body(xs_ref, bm_ref, cm_ref, zs_ref, gt_ref, gbc_ref, avc_ref, dsk_ref,
                     bufx_ref, bufb_ref, bufc_ref, h0_ref,
                     cwx_ref, cbx_ref, cwb_ref, cbb_ref, cwc_ref, cbc_ref, nw_ref,
                     ys_ref, h_ref, yit_ref, xwt_ref):
    g = pl.program_id(0)
    rows = xs_ref.shape[0]
    nseq = rows // SEQ4
    rid = lax.broadcasted_iota(jnp.int32, (rows, 1), 0)
    t = _imod(rid, SEQ4)
    seq = _idiv(rid, SEQ4)
    tok_lane = lax.broadcasted_iota(jnp.int32, (1, rows), 1)

    xs = _silu(_conv_rows(xs_ref[...], bufx_ref[...], cwx_ref[...], cbx_ref[...], t))
    bmb = _silu(_conv_rows(bm_ref[...], bufb_ref[...], cwb_ref[...], cbb_ref[...], t)).astype(BF16)
    cmb = _silu(_conv_rows(cm_ref[...], bufc_ref[...], cwc_ref[...], cbc_ref[...], t)).astype(BF16)

    lane0 = pl.multiple_of(GL_DT + g * S_HPG, S_HPG)
    dt8, la8 = _ssd_gate_rows(gt_ref, gbc_ref, avc_ref, lane0)
    src = lax.broadcasted_iota(jnp.int32, (rows, rows), 0)
    tgt = lax.broadcasted_iota(jnp.int32, (rows, rows), 1)
    same = _idiv(src, SEQ4) == _idiv(tgt, SEQ4)
    causal = same & (src <= tgt)
    cum8 = _rows_dot_01(la8, jnp.where(causal, 1.0, 0.0))
    a_end = _rows_dot_01(cum8, jnp.where(same & (_imod(src, SEQ4) == SEQ4 - 1), 1.0, 0.0))
    w8 = jnp.exp(a_end - cum8)
    ea8 = jnp.exp(cum8)
    ea_end = jnp.exp(a_end)

    cbt = jnp.where(causal, _dot_nt(bmb, cmb), 0.0)
    ys, xdts = _ssd_intra(xs.T, dt8, cum8, cbt)
    xwt_ref[...] = jnp.concatenate([(xdts[r] * w8[r:r + 1, :]).astype(BF16) for r in range(S_HPG)], axis=0)
    yit_ref[...] = jnp.zeros_like(yit_ref)

    def per_sequence(bi, carry):
        h_old = h0_ref[bi, 0]
        own = _idiv(tok_lane, SEQ4) == bi
        yit_ref[...] += _dot_nt(h_old.astype(BF16), jnp.where(seq == bi, cmb, jnp.zeros_like(cmb)))
        upd = _dot(jnp.where(own, xwt_ref[...], jnp.zeros_like(xwt_ref)), bmb)
        d8 = jnp.sum(jnp.where(tok_lane == bi * SEQ4 + SEQ4 - 1, ea_end, 0.0), axis=1, keepdims=True)
        for r in range(S_HPG):
            h_ref[bi, 0, HEAD_ROWS[r], :] = d8[r:r + 1, :] * h_old[HEAD_ROWS[r], :] + upd[HEAD_ROWS[r], :]
        return carry

    lax.fori_loop(0, nseq, per_sequence, 0, unroll=8 if nseq % 8 == 0 else 1)
    yit = yit_ref[...]
    yt = jnp.concatenate([ys[r] + ea8[r:r + 1, :] * yit[HEAD_ROWS[r], :] for r in range(S_HPG)], axis=0)
    ys_ref[...] = _group_out(yt.T, xs, dsk_ref[...], zs_ref[...], nw_ref[...]).astype(BF16)


SSD_SAMPLE_BB = 32


def _ssd_sample(proj, gates_t, nseq_total, gate_bias_col, a_col, dsk_row, buf_rows, h0, conv_w, conv_b, norm_w):
    bb = min(SSD_SAMPLE_BB, nseq_total)
    rows = bb * SEQ4
    nblk = nseq_total // bb
    tok = lambda width, col0: pl.BlockSpec((rows, width), lambda g, i: (i, col0 // width + g))
    vec = lambda r, width, col0: pl.BlockSpec((r, width), lambda g, i: (0, col0 // width + g))
    col = pl.BlockSpec((LANES, 1), lambda g, i: (0, 0))
    h_spec = pl.BlockSpec((bb, 1, S_GW, S_STATE), lambda g, i: (i, g, 0, 0))
    off_b = D_S
    off_c = D_S + S_GROUPS * S_STATE
    return pl.pallas_call(
        _ssd_sample_body,
        grid=(S_GROUPS, nblk),
        in_specs=[
            tok(S_GW, COL_XS), tok(S_STATE, COL_BM), tok(S_STATE, COL_CM), tok(S_GW, COL_ZS),
            pl.BlockSpec((LANES, rows), lambda g, i: (0, i)),
            col, col, vec(1, S_GW, 0),
            tok(S_GW, 0), tok(S_STATE, off_b), tok(S_STATE, off_c),
            h_spec,
            vec(CONV_K, S_GW, 0), vec(1, S_GW, 0),
            vec(CONV_K, S_STATE, off_b), vec(1, S_STATE, off_b),
            vec(CONV_K, S_STATE, off_c), vec(1, S_STATE, off_c),
            vec(1, S_GW, 0),
        ],
        out_specs=[pl.BlockSpec((rows, S_GW), lambda g, i: (i, g)), h_spec],
        out_shape=[
            jax.ShapeDtypeStruct((nseq_total * SEQ4, D_S), BF16),
            jax.ShapeDtypeStruct((nseq_total, S_GROUPS, S_GW, S_STATE), F32),
        ],
        scratch_shapes=[pltpu.VMEM((S_GW, rows), F32), pltpu.VMEM((S_GW, rows), BF16)],
        compiler_params=pltpu.CompilerParams(
            dimension_semantics=("parallel", "parallel"), vmem_limit_bytes=VMEM_LIMIT),
        name="ssd_sample",
    )(proj, proj, proj, proj, gates_t, gate_bias_col, a_col, dsk_row, buf_rows, buf_rows, buf_rows, h0,
      conv_w, conv_b, conv_w, conv_b, conv_w, conv_b, norm_w)


OUT_TM, OUT_TN = 512, 512


def _out_proj_body(ids, x_ref, hm_ref, ys_ref, wt_ref, wb_ref, gn_ref, out_ref, x1g_ref, ssp_ref):
    x1 = x_ref[...] + _dot(hm_ref[...], wt_ref[...]) + _dot(ys_ref[...], wb_ref[...])
    out_ref[...] = x1
    x1g_ref[...] = (x1 * gn_ref[...]).astype(BF16)
    ssp_ref[0] = jnp.sum(x1 * x1, axis=1, keepdims=True)


def _out_proj_part(x, hm, ys, w_out, ple_norm, ids=_same_ids):
    m = x.shape[0]
    tm, tn = min(OUT_TM, m), OUT_TN
    nj = D_MODEL // tn
    spec = functools.partial(_on_grid, ids)
    return _Part(
        body=_out_proj_body, ids=ids,
        in_specs=[
            spec((tm, tn), lambda i, j: (i, j)),
            spec((tm, D_M), lambda i, j: (i, 0)),
            spec((tm, D_S), lambda i, j: (i, 0)),
            spec((D_M, tn), lambda i, j: (0, j)),
            spec((D_S, tn), lambda i, j: (1, j)),
            spec((1, tn), lambda i, j: (0, j)),
        ],
        operands=[x, hm, ys, w_out, w_out, ple_norm],
        out_specs=[
            spec((tm, tn), lambda i, j: (i, j)),
            spec((tm, tn), lambda i, j: (i, j)),
            spec((1, tm, 1), lambda i, j: (j, i, 0)),
        ],
        out_shape=[jax.ShapeDtypeStruct((m, D_MODEL), F32), jax.ShapeDtypeStruct((m, D_MODEL), BF16),
                   jax.ShapeDtypeStruct((nj, m, 1), F32)],
        scratch=[],
    ), (m // tm, nj)


def _ple_body(x1g_ref, ss1_ref, x1t_ref, p_ref, wg_ref, wp_ref, fn_ref, out_ref, ss_ref):
    j = pl.program_id(1)
    nj = pl.num_programs(1)
    tn = wg_ref.shape[1]

    @pl.when(j == 0)
    def _():
        ss_ref[...] = jnp.zeros_like(ss_ref)

    rs = lax.rsqrt(jnp.sum(ss1_ref[...], axis=0) * (1.0 / D_MODEL) + EPS)
    gate = _sigmoid(rs * _dot(x1g_ref[...], wg_ref[...]))
    x2 = x1t_ref[...] + gate * _dot(p_ref[...].astype(BF16), wp_ref[...])
    ss_ref[...] += jnp.sum(x2 * x2, axis=1, keepdims=True)
    for jj in range(out_ref.shape[1] // tn):
        @pl.when(j == jj)
        def _(jj=jj):
            out_ref[:, jj * tn:(jj + 1) * tn] = x2

    @pl.when(j == nj - 1)
    def _():
        fn = fn_ref[...]

        def body(r, carry):
            sl = pl.ds(pl.multiple_of(r * NORM_ROWS, NORM_ROWS), NORM_ROWS)
            scale = lax.rsqrt(ss_ref[sl, :] * (1.0 / D_MODEL) + EPS)
            out_ref[sl, :] = out_ref[sl, :] * scale * fn
            return carry
        lax.fori_loop(0, out_ref.shape[0] // NORM_ROWS, body, 0)


def _ple(x1, x1g, ss1, p, w_gate, w_proj, final_norm):
    m = x1.shape[0]
    tm = min(512, m)
    tn = 1024
    return pl.pallas_call(
        _ple_body,
        grid=(m // tm, D_MODEL // tn),
        in_specs=[
            pl.BlockSpec((tm, D_MODEL), lambda i, j: (i, 0)),
            pl.BlockSpec((ss1.shape[0], tm, 1), lambda i, j: (0, i, 0)),
            pl.BlockSpec((tm, tn), lambda i, j: (i, j)),
            pl.BlockSpec((tm, PLE_DIM), lambda i, j: (i, 0)),
            pl.BlockSpec((D_MODEL, tn), lambda i, j: (0, j)),
            pl.BlockSpec((PLE_DIM, tn), lambda i, j: (0, j)),
            pl.BlockSpec((1, D_MODEL), lambda i, j: (0, 0)),
        ],
        out_specs=pl.BlockSpec((tm, D_MODEL), lambda i, j: (i, 0)),
        out_shape=jax.ShapeDtypeStruct((m, D_MODEL), F32),
        scratch_shapes=[pltpu.VMEM((tm, 1), F32)],
        compiler_params=pltpu.CompilerParams(
            dimension_semantics=("parallel", "arbitrary"), vmem_limit_bytes=VMEM_LIMIT),
        name="ple",
    )(x1g, ss1, x1, p, w_gate, w_proj, final_norm)


CONV_TC = 2048


def _slot_row_match(nseq, slot_major_axis, first_token):
    shape = (nseq * SEQ4, (CONV_K - 1) * nseq) if slot_major_axis == 1 else ((CONV_K - 1) * nseq, nseq * SEQ4)
    sm = lax.broadcasted_iota(jnp.int32, shape, slot_major_axis)
    row = lax.broadcasted_iota(jnp.int32, shape, 1 - slot_major_axis)
    hit = None
    for slot in range(CONV_K - 1):
        b = sm - slot * nseq
        this = (b >= 0) & (b < nseq) & (row == b * SEQ4 + first_token + slot)
        hit = this if hit is None else hit | this
    return jnp.where(hit, 1.0, 0.0)


def _conv_rows_in_body(buf_ref, rows_ref):
    nseq = buf_ref.shape[1]
    pick = _slot_row_match(nseq, 1, 0)
    rows_ref[...] = _dot_select(pick, buf_ref[...].reshape((CONV_K - 1) * nseq, buf_ref.shape[2]))


def _conv_rows_in(buf):
    _, nseq, ch = buf.shape
    tc = min(CONV_TC, ch)
    return pl.pallas_call(
        _conv_rows_in_body,
        grid=(ch // tc,),
        in_specs=[pl.BlockSpec((CONV_K - 1, nseq, tc), lambda j: (0, 0, j))],
        out_specs=pl.BlockSpec((nseq * SEQ4, tc), lambda j: (0, j)),
        out_shape=jax.ShapeDtypeStruct((nseq * SEQ4, ch), F32),
        compiler_params=pltpu.CompilerParams(dimension_semantics=("parallel",), vmem_limit_bytes=VMEM_LIMIT),
        name="conv_rows_in",
    )(buf)


def _conv_rows_out_body(rows_ref, buf_ref):
    nseq = buf_ref.shape[1]
    pick = _slot_row_match(nseq, 0, SEQ4 - (CONV_K - 1))
    buf_ref[...] = _dot_select(pick, rows_ref[...]).reshape(buf_ref.shape)


def _conv_rows_out(proj, col0, ch):
    nseq = proj.shape[0] // SEQ4
    tc = min(CONV_TC, ch)
    assert col0 % tc == 0 and ch % tc == 0
    return pl.pallas_call(
        _conv_rows_out_body,
        grid=(ch // tc,),
        in_specs=[pl.BlockSpec((nseq * SEQ4, tc), lambda j: (0, col0 // tc + j))],
        out_specs=pl.BlockSpec((CONV_K - 1, nseq, tc), lambda j: (0, 0, j)),
        out_shape=jax.ShapeDtypeStruct((CONV_K - 1, nseq, ch), F32),
        compiler_params=pltpu.CompilerParams(dimension_semantics=("parallel",), vmem_limit_bytes=VMEM_LIMIT),
        name="conv_rows_out",
    )(proj)


def _pad_lanes(v, offset):
    out = jnp.zeros((1, LANES), F32)
    return lax.dynamic_update_slice(out, v.reshape(1, -1).astype(F32), (0, offset))


def kernel(x_prompt, x_sample, p_prompt, p_sample, state_m_C, state_m_n, state_m_m, state_m_conv, state_s_ssm, state_s_conv, norm_in, w_in, b_ig, b_fg, m_conv_w, m_conv_b, w_q, w_k, m_norm, s_conv_w, s_conv_b, dt_bias, a_log, d_skip, s_norm, w_out, ple_proj, ple_gate, ple_norm, final_norm):
    assert w_in.shape[0] == 1, "single layer"
    bsz, seqlen, _ = x_prompt.shape
    nseq, dec_seq, _ = x_sample.shape
    assert dec_seq == SEQ4

    assert w_in.shape[2] == N_MAIN + GATE_COLS + S_HEADS
    w_in_t = jnp.swapaxes(w_in, 1, 2)
    gate_bias = jnp.concatenate(
        [b_ig[0], b_fg[0], dt_bias[0], jnp.zeros((LANES - 2 * M_HEADS - S_HEADS,), F32)]).reshape(1, LANES)
    gate_bias_col = gate_bias.reshape(LANES, 1)
    a_col = _pad_lanes(a_log[0], GL_DT).reshape(LANES, 1)
    dsk_row = jnp.repeat(d_skip[0], S_HEAD_DIM).reshape(1, D_S)
    wq = w_q[0].astype(BF16)
    wkt = (jnp.swapaxes(w_k[0], 1, 2) * (M_QK ** -0.5)).astype(BF16)
    ple_proj_b = ple_proj[0].astype(BF16)
    norm_in_r = norm_in[0].reshape(1, D_MODEL)
    ple_norm_r = ple_norm[0].reshape(1, D_MODEL)
    final_norm_r = final_norm.reshape(1, D_MODEL)
    m_conv_b_r = m_conv_b[0].reshape(1, D_M)
    s_conv_b_r = s_conv_b[0].reshape(1, CONV_DIM)
    m_norm_r = m_norm[0].reshape(1, D_M)
    s_norm_r = s_norm[0].reshape(1, D_S)

    xp = x_prompt.reshape(bsz * seqlen, D_MODEL)
    xs_ = x_sample.reshape(nseq * SEQ4, D_MODEL)
    in_part, in_grid = _in_proj_part(_rmsnorm_cast(xp, norm_in_r), w_in_t)
    (proj_p, _, gates_pt), = _call_parts("in_proj", in_grid, [in_part])

    mlstm_part, mlstm_grid = _mlstm_prompt_part(
        proj_p, gates_pt, bsz, seqlen, gate_bias_col, m_conv_w[0], m_conv_b_r, wq, wkt, m_norm_r)
    nsteps = mlstm_grid[0] * mlstm_grid[1] * mlstm_grid[2]
    step = _linear_step(mlstm_grid)
    (hm_p, pc, pn, pm), (w_out_b,), (ple_gate_b,) = _call_parts(
        "mlstm_prompt_casts", mlstm_grid,
        [mlstm_part, _cast_part(w_out[0], nsteps, step), _cast_part(ple_gate[0], nsteps, step)])

    ssd_part, ssd_grid = _ssd_prompt_part(
        proj_p, gates_pt, bsz, seqlen, gate_bias_col, a_col, dsk_row, s_conv_w[0], s_conv_b_r, s_norm_r)
    (ys_p, ph), = _call_parts("ssd_prompt", ssd_grid, [ssd_part])
    in_part_s, in_grid_s = _in_proj_part(_rmsnorm_cast(xs_, norm_in_r), w_in_t)
    (proj_s, gates_s, gates_st), = _call_parts("in_proj", in_grid_s, [in_part_s])

    proj_p3 = proj_p.reshape(bsz, seqlen, N_MAIN)
    tail = seqlen - (CONV_K - 1)
    p_mconv = proj_p3[:, tail:, COL_U:COL_U + D_M]
    p_sconv = proj_p3[:, tail:, COL_XS:COL_XS + CONV_DIM]

    pad_rows = lambda st: _conv_rows_in(jnp.swapaxes(st, 0, 1))
    m_rows = jnp.pad(jnp.repeat(state_m_m[0], SEQ4, axis=0), ((0, 0), (0, LANES - M_HEADS)))

    sample_args = (proj_s, gates_s, nseq, gate_bias, pad_rows(state_m_conv[0]), m_rows, state_m_C[0],
                   state_m_n[0].reshape(nseq, M_HEADS, 1, M_QK), m_conv_w[0], m_conv_b_r, wq, wkt, m_norm_r)
    out_part, out_grid = _out_proj_part(xp, hm_p, ys_p, w_out_b, ple_norm_r)
    bb = SAMPLE_BB
    while bb > 1 and (nseq % bb or M_HEADS * (nseq // bb) < out_grid[0] * out_grid[1]):
        bb //= 2
    if M_HEADS * (nseq // bb) == out_grid[0] * out_grid[1]:
        nblk = nseq // bb
        as_head_block = lambda i, j: ((i * out_grid[1] + j) // nblk, (i * out_grid[1] + j) % nblk)
        scan_part, _ = _mlstm_sample_part(*sample_args, bb, ids=as_head_block)
        (x1_p, x1g_p, ss1_p), (hm_s, sc, sn, sm) = _call_parts("out_proj_mlstm_sample", out_grid, [out_part, scan_part])
    else:
        (x1_p, x1g_p, ss1_p), = _call_parts("out_proj", out_grid, [out_part])
        scan_part, scan_grid = _mlstm_sample_part(*sample_args, bb)
        (hm_s, sc, sn, sm), = _call_parts("mlstm_sample", scan_grid, [scan_part])
    y_p = _ple(x1_p, x1g_p, ss1_p, p_prompt[0].reshape(bsz * seqlen, PLE_DIM), ple_gate_b, ple_proj_b, final_norm_r)

    ys_s, sh = _ssd_sample(
        proj_s, gates_st, nseq, gate_bias_col, a_col, dsk_row, pad_rows(state_s_conv[0]),
        state_s_ssm[0].reshape(nseq, S_GROUPS, S_GW, S_STATE), s_conv_w[0], s_conv_b_r, s_norm_r)
    out_part_s, out_grid_s = _out_proj_part(xs_, hm_s, ys_s, w_out_b, ple_norm_r)
    (x1_s, x1g_s, ss1_s), = _call_parts("out_proj", out_grid_s, [out_part_s])
    y_s = _ple(x1_s, x1g_s, ss1_s, p_sample[0].reshape(nseq * SEQ4, PLE_DIM), ple_gate_b, ple_proj_b, final_norm_r)

    s_mconv = jnp.swapaxes(_conv_rows_out(proj_s, COL_U, D_M), 0, 1)
    s_sconv = jnp.swapaxes(_conv_rows_out(proj_s, COL_XS, CONV_DIM), 0, 1)
    s_m = jnp.transpose(sm[:, SEQ4 - 1::SEQ4, 0])

    return (
        y_p.reshape(bsz, seqlen, D_MODEL),
        y_s.reshape(nseq, SEQ4, D_MODEL),
        pc[None], pn.reshape(1, bsz, M_HEADS, M_QK), pm[:, :, 0, 0][None], p_mconv[None],
        ph.reshape(1, bsz, S_HEADS, S_HEAD_DIM, S_STATE), p_sconv[None],
        sc[None], sn.reshape(1, nseq, M_HEADS, M_QK), s_m[None], s_mconv[None],
        sh.reshape(1, nseq, S_HEADS, S_HEAD_DIM, S_STATE), s_sconv[None],
    )
```

```python
import functools
from typing import Callable, NamedTuple

import jax
import jax.numpy as jnp
from jax import lax
from jax.experimental import pallas as pl
from jax.experimental.pallas import tpu as pltpu

F32 = jnp.float32
BF16 = jnp.bfloat16

D_MODEL = 4096
D_M = 4096
D_S = 4096
M_HEADS = 8
M_V = 512
M_QK = 256
S_HEADS = 64
S_HEAD_DIM = 64
S_STATE = 128
S_GROUPS = 8
S_HPG = 8
S_GW = S_HPG * S_HEAD_DIM
CONV_DIM = D_S + 2 * S_GROUPS * S_STATE
CONV_K = 4
PLE_DIM = 256
GATE_CAP = 15.0
EPS = 1e-6

COL_U, COL_V, COL_O, COL_ZM, COL_ZS, COL_XS = 0, 4096, 8192, 12288, 16384, 20480
COL_BM = COL_XS + D_S
COL_CM = COL_BM + S_GROUPS * S_STATE
N_MAIN = COL_XS + CONV_DIM
LANES = 128
GL_IG, GL_FG, GL_DT = 0, M_HEADS, 2 * M_HEADS

PROMPT_CHUNK = 256
SAMPLE_BB = 16
VMEM_LIMIT = 62 * 1024 * 1024

NT_DIMS = (((1,), (1,)), ((), ()))


def _dot(a, b):
    return jnp.dot(a, b, preferred_element_type=F32)


def _dot_nt(a, b):
    return lax.dot_general(a, b, NT_DIMS, preferred_element_type=F32)


def _rows_dot_01(a, b01):
    n = a.shape[0]
    hi = a.astype(BF16).astype(F32)
    rest = a - hi
    mid = rest.astype(BF16).astype(F32)
    pieces = jnp.concatenate([hi, mid, rest - mid], axis=0).astype(BF16)
    r = _dot(pieces, b01.astype(BF16))
    return (r[0:n] + r[n:2 * n]) + r[2 * n:3 * n]


def _dot_select(pick, x):
    hi = x.astype(BF16)
    rest = x - hi.astype(F32)
    mid = rest.astype(BF16)
    lo = (rest - mid.astype(F32)).astype(BF16)
    p = pick.astype(BF16)
    return (_dot(p, hi) + _dot(p, mid)) + _dot(p, lo)


def _sigmoid(x):
    return 0.5 * jnp.tanh(0.5 * x) + 0.5


def _silu(x):
    return x * _sigmoid(x)


def _softcap(x):
    return GATE_CAP * jnp.tanh(x / GATE_CAP)


def _softplus(x):
    return jnp.maximum(x, 0.0) + jnp.log1p(jnp.exp(-jnp.abs(x)))


def _log_sigmoid(x):
    return -_softplus(-x)


def _idiv(x, d):
    return lax.shift_right_logical(x, d.bit_length() - 1)


def _imod(x, d):
    return lax.bitwise_and(x, d - 1)


def _lane_pick(x, lane_idx):
    lane = lax.broadcasted_iota(jnp.int32, (1, x.shape[1]), 1)
    return jnp.sum(jnp.where(lane == lane_idx, x, 0.0), axis=1, keepdims=True)


def _col_to_row(col, eye):
    return jnp.sum(jnp.where(eye, col, 0.0), axis=0, keepdims=True)


def _row_to_col(row, eye):
    return jnp.sum(jnp.where(eye, row, 0.0), axis=1, keepdims=True)


class _Part(NamedTuple):
    body: Callable
    ids: Callable
    in_specs: list
    operands: list
    out_specs: list
    out_shape: list
    scratch: list


def _same_ids(*g):
    return g


def _on_grid(ids, block_shape, index_map, **kwargs):
    return pl.BlockSpec(block_shape, lambda *g: index_map(*ids(*g)), **kwargs)


def _call_parts(name, grid, parts):
    n_in = [len(p.in_specs) for p in parts]
    n_out = [len(p.out_specs) for p in parts]
    n_scr = [len(p.scratch) for p in parts]

    def body(*refs):
        gids = tuple(pl.program_id(a) for a in range(len(grid)))
        ins, outs, scr = refs[:sum(n_in)], refs[sum(n_in):sum(n_in) + sum(n_out)], refs[sum(n_in) + sum(n_out):]
        for p, part in enumerate(parts):
            a, b, c = sum(n_in[:p]), sum(n_out[:p]), sum(n_scr[:p])
            part.body(part.ids(*gids), *ins[a:a + n_in[p]], *outs[b:b + n_out[p]], *scr[c:c + n_scr[p]])

    results = pl.pallas_call(
        body,
        grid=grid,
        in_specs=[s for p in parts for s in p.in_specs],
        out_specs=[s for p in parts for s in p.out_specs],
        out_shape=[s for p in parts for s in p.out_shape],
        scratch_shapes=[s for p in parts for s in p.scratch],
        compiler_params=pltpu.CompilerParams(
            dimension_semantics=("arbitrary",) * len(grid), vmem_limit_bytes=VMEM_LIMIT),
        name=name,
    )(*[o for p in parts for o in p.operands])
    return [results[sum(n_out[:p]):sum(n_out[:p]) + n_out[p]] for p in range(len(parts))]


def _linear_step(grid):
    def lin(*g):
        s = g[0]
        for size, idx in zip(grid[1:], g[1:]):
            s = s * size + idx
        return s
    return lin


def _cast_body(ids, src_ref, dst_ref):
    dst_ref[...] = src_ref[...].astype(dst_ref.dtype)


def _cast_part(w, nsteps, step):
    rows = w.shape[0] // nsteps
    assert rows * nsteps == w.shape[0] and rows % 16 == 0
    spec = pl.BlockSpec((rows, w.shape[1]), lambda *g: (step(*g), 0))
    return _Part(body=_cast_body, ids=_same_ids, in_specs=[spec], operands=[w], out_specs=[spec],
                 out_shape=[jax.ShapeDtypeStruct(w.shape, BF16)], scratch=[])


NORM_ROWS = 32


def _rmsnorm_rows_to(x_ref, g, dst_ref):
    def body(r, carry):
        sl = pl.ds(pl.multiple_of(r * NORM_ROWS, NORM_ROWS), NORM_ROWS)
        x = x_ref[sl, :]
        ms = jnp.mean(x * x, axis=-1, keepdims=True)
        dst_ref[sl, :] = (x * lax.rsqrt(ms + EPS) * g).astype(BF16)
        return carry
    lax.fori_loop(0, x_ref.shape[0] // NORM_ROWS, body, 0)


def _in_proj_body(x_ref, g_ref, w_ref, wg_ref, out_ref, gates_ref, gatest_ref, xn_ref):
    @pl.when(pl.program_id(1) == 0)
    def _():
        _rmsnorm_rows_to(x_ref, g_ref[...], xn_ref)
        gates = _dot_nt(xn_ref[...], wg_ref[...])
        gates_ref[...] = gates
        gatest_ref[...] = gates.T
    out_ref[...] = _dot_nt(xn_ref[...], w_ref[...])


def _in_proj(x, norm_w, w_main, w_gate):
    m = x.shape[0]
    tm = min(512, m)
    tn = 1024
    grid = (m // tm, N_MAIN // tn)
    return pl.pallas_call(
        _in_proj_body,
        grid=grid,
        in_specs=[
            pl.BlockSpec((tm, D_MODEL), lambda i, j: (i, 0)),
            pl.BlockSpec((1, D_MODEL), lambda i, j: (0, 0)),
            pl.BlockSpec((tn, D_MODEL), lambda i, j: (j, 0)),
            pl.BlockSpec((LANES, D_MODEL), lambda i, j: (0, 0)),
        ],
        out_specs=[
            pl.BlockSpec((tm, tn), lambda i, j: (i, j)),
            pl.BlockSpec((tm, LANES), lambda i, j: (i, 0)),
            pl.BlockSpec((LANES, tm), lambda i, j: (0, i)),
        ],
        out_shape=[jax.ShapeDtypeStruct((m, N_MAIN), F32), jax.ShapeDtypeStruct((m, LANES), F32),
                   jax.ShapeDtypeStruct((LANES, m), F32)],
        scratch_shapes=[pltpu.VMEM((tm, D_MODEL), BF16)],
        compiler_params=pltpu.CompilerParams(
            dimension_semantics=("parallel", "arbitrary"), vmem_limit_bytes=VMEM_LIMIT),
        name="in_proj",
    )(x, norm_w, w_main, w_gate)


def _rmsnorm_cast_body(x_ref, g_ref, out_ref):
    _rmsnorm_rows_to(x_ref, g_ref[...], out_ref)


def _rmsnorm_cast(x, norm_w):
    m = x.shape[0]
    tm = min(512, m)
    return pl.pallas_call(
        _rmsnorm_cast_body,
        grid=(m // tm,),
        in_specs=[pl.BlockSpec((tm, D_MODEL), lambda i: (i, 0)), pl.BlockSpec((1, D_MODEL), lambda i: (0, 0))],
        out_specs=pl.BlockSpec((tm, D_MODEL), lambda i: (i, 0)),
        out_shape=jax.ShapeDtypeStruct((m, D_MODEL), BF16),
        compiler_params=pltpu.CompilerParams(dimension_semantics=("parallel",), vmem_limit_bytes=VMEM_LIMIT),
        name="rmsnorm_cast",
    )(x, norm_w)


def _in_proj_w32_body(ids, xn_ref, w_ref, ga_ref, gb_ref, out_ref, gates_ref, gatest_ref, wg_ref):
    @pl.when(ids[1] == 0)
    def _():
        wg_ref[0:GL_DT, :] = ga_ref[0].astype(BF16)
        wg_ref[GL_DT:GL_DT + S_HEADS, :] = gb_ref[0].astype(BF16)
        wg_ref[GL_DT + S_HEADS:LANES, :] = jnp.zeros((LANES - GL_DT - S_HEADS, D_MODEL), BF16)
        gates = _dot_nt(xn_ref[...], wg_ref[...])
        gates_ref[...] = gates
        gatest_ref[...] = gates.T
    out_ref[...] = _dot_nt(xn_ref[...], w_ref[0].astype(BF16))


IN_TM, IN_TN = 2048, 512


def _in_proj_part(xn, w_in_t, ids=_same_ids):
    m = xn.shape[0]
    tm = min(IN_TM, m)
    tn = IN_TN if tm * 4 > IN_TM else 2 * IN_TN

    def src_row(j):
        return pl.multiple_of(j * tn + (j // (N_ALIGNED // tn)) * GATE_COLS, GATE_COLS)

    el = pl.Element
    spec = functools.partial(_on_grid, ids)
    return _Part(
        body=_in_proj_w32_body, ids=ids,
        in_specs=[
            spec((tm, D_MODEL), lambda i, j: (i, 0), pipeline_mode=pl.Buffered(1)),
            spec((el(1), el(tn), el(D_MODEL)), lambda i, j: (0, src_row(j), 0)),
            spec((el(1), el(GATE_COLS), el(D_MODEL)), lambda i, j: (0, N_ALIGNED, 0)),
            spec((el(1), el(S_HEADS), el(D_MODEL)), lambda i, j: (0, N_MAIN + GATE_COLS, 0)),
        ],
        operands=[xn, w_in_t, w_in_t, w_in_t],
        out_specs=[
            spec((tm, tn), lambda i, j: (i, j)),
            spec((tm, LANES), lambda i, j: (i, 0)),
            spec((LANES, tm), lambda i, j: (0, i)),
        ],
        out_shape=[jax.ShapeDtypeStruct((m, N_MAIN), F32), jax.ShapeDtypeStruct((m, LANES), F32),
                   jax.ShapeDtypeStruct((LANES, m), F32)],
        scratch=[pltpu.VMEM((LANES, D_MODEL), BF16)],
    ), (m // tm, N_MAIN // tn)


GATE_COLS = 2 * M_HEADS
N_ALIGNED = 4 * D_M


def _w_prep_body(src_ref, ga_ref, gb_ref, out_ref, wg_ref):
    out_ref[...] = src_ref[0].astype(BF16)

    @pl.when(pl.program_id(0) == 0)
    def _():
        wg_ref[0:GL_DT, :] = ga_ref[0].astype(BF16)
        wg_ref[GL_DT:GL_DT + S_HEADS, :] = gb_ref[0].astype(BF16)
        wg_ref[GL_DT + S_HEADS:LANES, :] = jnp.zeros((LANES - GL_DT - S_HEADS, D_MODEL), BF16)


def _w_prep(w_in_t):
    rb = 512

    def src_row(i):
        return pl.multiple_of(i * rb + (i // (N_ALIGNED // rb)) * GATE_COLS, GATE_COLS)

    return pl.pallas_call(
        _w_prep_body,
        grid=(N_MAIN // rb,),
        in_specs=[
            pl.BlockSpec((pl.Element(1), pl.Element(rb), pl.Element(D_MODEL)), lambda i: (0, src_row(i), 0)),
            pl.BlockSpec((pl.Element(1), pl.Element(GATE_COLS), pl.Element(D_MODEL)), lambda i: (0, N_ALIGNED, 0)),
            pl.BlockSpec((pl.Element(1), pl.Element(S_HEADS), pl.Element(D_MODEL)),
                         lambda i: (0, N_MAIN + GATE_COLS, 0)),
        ],
        out_specs=[pl.BlockSpec((rb, D_MODEL), lambda i: (i, 0)), pl.BlockSpec((LANES, D_MODEL), lambda i: (0, 0))],
        out_shape=[jax.ShapeDtypeStruct((N_MAIN, D_MODEL), BF16), jax.ShapeDtypeStruct((LANES, D_MODEL), BF16)],
        compiler_params=pltpu.CompilerParams(dimension_semantics=("arbitrary",), vmem_limit_bytes=VMEM_LIMIT),
        name="w_prep",
    )(w_in_t, w_in_t, w_in_t)


TAIL = 8


def _conv_carry(u, ext_ref, cw, cb):
    c = u.shape[0]
    ext_ref[TAIL:TAIL + c, :] = u
    acc = cb + u * cw[CONV_K - 1:CONV_K, :]
    for j in range(1, CONV_K):
        acc = acc + ext_ref[TAIL - j:TAIL - j + c, :] * cw[CONV_K - 1 - j:CONV_K - j, :]
    ext_ref[0:TAIL, :] = u[c - TAIL:c, :]
    return acc


def _conv_rows(u, buf, cw, cb, t):
    rows = u.shape[0]
    acc = cb + u * cw[CONV_K - 1:CONV_K, :]
    for j in range(1, CONV_K):
        prev = buf if j == CONV_K - 1 else pltpu.roll(buf, rows + j - (CONV_K - 1), 0)
        acc = acc + jnp.where(t >= j, pltpu.roll(u, j, 0), prev) * cw[CONV_K - 1 - j:CONV_K - j, :]
    return acc


def _head_out(hv, o, z, nw):
    og = _sigmoid(o) * hv
    ms = jnp.mean(og * og, axis=-1, keepdims=True)
    return (og * lax.rsqrt(ms + EPS) * nw) * _silu(z)


def _group_out(y, xs, dsk, z, nw):
    gated = (y + dsk * xs) * _silu(z)
    ms = jnp.mean(gated * gated, axis=-1, keepdims=True)
    return gated * lax.rsqrt(ms + EPS) * nw


def _mlstm_prompt_body(ids, u_ref, v_ref, o_ref, z_ref, gt_ref, gb_ref, cw_ref, cb_ref, wq_ref, wkt_ref, nw_ref,
                       hm_ref, c_ref, n_ref, m_ref, ext_ref, mst_ref):
    c = u_ref.shape[0]
    hps = wq_ref.shape[0]
    h0 = ids[1] * hps

    @pl.when(ids[2] == 0)
    def _():
        c_ref[...] = jnp.zeros_like(c_ref)
        n_ref[...] = jnp.zeros_like(n_ref)
        mst_ref[...] = jnp.full_like(mst_ref, -jnp.inf)
        ext_ref[0:TAIL, :] = jnp.zeros((TAIL, ext_ref.shape[1]), F32)

    ucb = _silu(_conv_carry(u_ref[...], ext_ref, cw_ref[...], cb_ref[...])).astype(BF16)
    vb = v_ref[...].astype(BF16)

    rowi = lax.broadcasted_iota(jnp.int32, (c, c), 0)
    coli = lax.broadcasted_iota(jnp.int32, (c, c), 1)
    tri = coli <= rowi
    eye = coli == rowi
    ig8 = _softcap(gt_ref[GL_IG:GL_IG + M_HEADS, :] + gb_ref[GL_IG:GL_IG + M_HEADS, :])
    lf8 = _log_sigmoid(_softcap(gt_ref[GL_FG:GL_FG + M_HEADS, :] + gb_ref[GL_FG:GL_FG + M_HEADS, :]))
    b8 = _rows_dot_01(lf8, jnp.where(rowi <= coli, 1.0, 0.0))
    head_row = lax.broadcasted_iota(jnp.int32, (M_HEADS, 1), 0)

    for hh in range(hps):
        cols = slice(hh * M_V, (hh + 1) * M_V)
        pick = head_row == h0 + hh
        ig_row = jnp.sum(jnp.where(pick, ig8, 0.0), axis=0, keepdims=True)
        b_row = jnp.sum(jnp.where(pick, b8, 0.0), axis=0, keepdims=True)
        b_col = _row_to_col(b_row, eye)
        m_prev = mst_ref[hh:hh + 1, :][:, 0:1]

        q = _dot(ucb[:, cols], wq_ref[hh])
        kt = _dot_nt(wkt_ref[hh], ucb[:, cols])
        qb = q.astype(BF16)
        ktb = kt.astype(BF16)

        r_row = ig_row - b_row
        dmat = jnp.where(tri, b_col + r_row, -jnp.inf)
        inter = b_col + m_prev
        m_t = jnp.maximum(inter, jnp.max(dmat, axis=1, keepdims=True))
        a_inter = jnp.exp(inter - m_t)
        s = _dot(qb, ktb) * jnp.exp(dmat - m_t)
        c_old = c_ref[0, hh]
        n_old = n_ref[0, hh]
        num = _dot(s.astype(BF16), vb[:, cols]) + a_inter * _dot(qb, c_old.astype(BF16))
        qn = _dot_nt(qb, jnp.broadcast_to(n_old, (8, M_QK)).astype(BF16))[:, 0:1]
        den = jnp.sum(s, axis=1, keepdims=True) + a_inter * qn
        hv = num / jnp.maximum(jnp.abs(den), jnp.exp(-m_t))
        hm_ref[:, cols] = _head_out(hv, o_ref[:, cols], z_ref[:, cols], nw_ref[:, cols]).astype(BF16)

        b_end = b_row[:, c - 1:c]
        lw_row = b_end + r_row
        m_new = jnp.maximum(b_end + m_prev, jnp.max(lw_row, axis=1, keepdims=True))
        decay = jnp.exp(b_end + m_prev - m_new)
        wk_row = jnp.exp(lw_row - m_new)
        c_ref[0, hh] = decay * c_old + _dot((kt * wk_row).astype(BF16), vb[:, cols])
        n_ref[0, hh] = decay * n_old + _dot_nt(jnp.broadcast_to(wk_row, (8, c)).astype(BF16), ktb)[0:1, :]
        mst_ref[hh:hh + 1, :] = jnp.broadcast_to(m_new, (1, LANES))
        m_ref[0, hh] = jnp.broadcast_to(m_new, (1, LANES))


MLSTM_HPS = 4


def _mlstm_prompt_part(proj, gates_t, bsz, seqlen, gate_bias_col, conv_w, conv_b, wq, wkt, norm_w):
    c = min(PROMPT_CHUNK, seqlen)
    nch = seqlen // c
    hps = MLSTM_HPS
    width = hps * M_V

    def tok(col0):
        return pl.BlockSpec((c, width), lambda b, h, k: (b * nch + k, col0 // width + h))

    head_vec = lambda rows: pl.BlockSpec((rows, width), lambda b, h, k: (0, h))
    return _Part(
        body=_mlstm_prompt_body, ids=_same_ids,
        in_specs=[
            tok(COL_U), tok(COL_V), tok(COL_O), tok(COL_ZM),
            pl.BlockSpec((LANES, c), lambda b, h, k: (0, b * nch + k)),
            pl.BlockSpec((LANES, 1), lambda b, h, k: (0, 0)),
            head_vec(CONV_K), head_vec(1),
            pl.BlockSpec((hps, M_V, M_QK), lambda b, h, k: (h, 0, 0)),
            pl.BlockSpec((hps, M_QK, M_V), lambda b, h, k: (h, 0, 0)),
            head_vec(1),
        ],
        operands=[proj, proj, proj, proj, gates_t, gate_bias_col, conv_w, conv_b, wq, wkt, norm_w],
        out_specs=[
            pl.BlockSpec((c, width), lambda b, h, k: (b * nch + k, h)),
            pl.BlockSpec((1, hps, M_QK, M_V), lambda b, h, k: (b, h, 0, 0)),
            pl.BlockSpec((1, hps, 1, M_QK), lambda b, h, k: (b, h, 0, 0)),
            pl.BlockSpec((1, hps, 1, LANES), lambda b, h, k: (b, h, 0, 0)),
        ],
        out_shape=[
            jax.ShapeDtypeStruct((bsz * seqlen, D_M), BF16),
            jax.ShapeDtypeStruct((bsz, M_HEADS, M_QK, M_V), F32),
            jax.ShapeDtypeStruct((bsz, M_HEADS, 1, M_QK), F32),
            jax.ShapeDtypeStruct((bsz, M_HEADS, 1, LANES), F32),
        ],
        scratch=[pltpu.VMEM((TAIL + c, width), F32), pltpu.VMEM((8, LANES), F32)],
    ), (bsz, M_HEADS // hps, nch)


LOG2E = 1.4426950408889634
HEAD_ROWS = [slice(r * S_HEAD_DIM, (r + 1) * S_HEAD_DIM) for r in range(S_HPG)]


def _ssd_gate_rows(gt_ref, gbc_ref, avc_ref, lane0):
    rows = pl.ds(lane0, S_HPG)
    dt8 = _softplus(gt_ref[rows, :] + gbc_ref[rows, :])
    return dt8, dt8 * (-jnp.exp(avc_ref[rows, :]))


def _ssd_intra(xst, dt8, cum8, cbt):
    c = xst.shape[1]
    a2r = cum8 * LOG2E
    a2c = jnp.concatenate([a2r, jnp.zeros((LANES - S_HPG, c), F32)], axis=0).T
    ys, xdts = [], []
    for r in range(S_HPG):
        xdt = xst[HEAD_ROWS[r], :] * dt8[r:r + 1, :]
        decay = jnp.exp2(jnp.minimum(a2r[r:r + 1, :] - a2c[:, r:r + 1], 0.0))
        ys.append(_dot(xdt.astype(BF16), (cbt * decay).astype(BF16)))
        xdts.append(xdt)
    return ys, xdts


def _ssd_prompt_body(ids, xs_ref, bm_ref, cm_ref, zs_ref, gt_ref, gbc_ref, avc_ref, dsk_ref,
                     cwx_ref, cbx_ref, cwb_ref, cbb_ref, cwc_ref, cbc_ref, nw_ref,
                     ys_ref, h_ref, extx_ref, extb_ref, extc_ref):
    c = xs_ref.shape[0]
    gps = h_ref.shape[1]
    g0 = ids[1] * gps

    @pl.when(ids[2] == 0)
    def _():
        h_ref[...] = jnp.zeros_like(h_ref)
        extx_ref[0:TAIL, :] = jnp.zeros((TAIL, extx_ref.shape[1]), F32)
        extb_ref[0:TAIL, :] = jnp.zeros((TAIL, extb_ref.shape[1]), F32)
        extc_ref[0:TAIL, :] = jnp.zeros((TAIL, extc_ref.shape[1]), F32)

    xs_all = _silu(_conv_carry(xs_ref[...], extx_ref, cwx_ref[...], cbx_ref[...]))
    bmb_all = _silu(_conv_carry(bm_ref[...], extb_ref, cwb_ref[...], cbb_ref[...])).astype(BF16)
    cmb_all = _silu(_conv_carry(cm_ref[...], extc_ref, cwc_ref[...], cbc_ref[...])).astype(BF16)

    src = lax.broadcasted_iota(jnp.int32, (c, c), 0)
    tgt = lax.broadcasted_iota(jnp.int32, (c, c), 1)
    causal = src <= tgt
    upper = jnp.where(causal, 1.0, 0.0)

    for gg in range(gps):
        cols = slice(gg * S_GW, (gg + 1) * S_GW)
        xs = xs_all[:, cols]
        bmb = bmb_all[:, gg * S_STATE:(gg + 1) * S_STATE]
        cmb = cmb_all[:, gg * S_STATE:(gg + 1) * S_STATE]
        lane0 = pl.multiple_of(GL_DT + (g0 + gg) * S_HPG, S_HPG)
        dt8, la8 = _ssd_gate_rows(gt_ref, gbc_ref, avc_ref, lane0)
        cum8 = _rows_dot_01(la8, upper)
        a_end = cum8[:, c - 1:c]
        w8 = jnp.exp(a_end - cum8)
        ea8 = jnp.exp(cum8)

        cbt = jnp.where(causal, _dot_nt(bmb, cmb), 0.0)
        ys, xdts = _ssd_intra(xs.T, dt8, cum8, cbt)
        h_old = h_ref[0, gg]
        yit = _dot_nt(h_old.astype(BF16), cmb)
        yt = jnp.concatenate([ys[r] + ea8[r:r + 1, :] * yit[HEAD_ROWS[r], :] for r in range(S_HPG)], axis=0)
        ys_ref[:, cols] = _group_out(yt.T, xs, dsk_ref[:, cols], zs_ref[:, cols], nw_ref[:, cols]).astype(BF16)

        xwt = jnp.concatenate([(xdts[r] * w8[r:r + 1, :]).astype(BF16) for r in range(S_HPG)], axis=0)
        upd = _dot(xwt, bmb)
        ea_end = jnp.exp(a_end)
        for r in range(S_HPG):
            h_ref[0, gg, HEAD_ROWS[r], :] = ea_end[r:r + 1, :] * h_old[HEAD_ROWS[r], :] + upd[HEAD_ROWS[r], :]


SSD_GPS = 8


def _ssd_prompt_part(proj, gates_t, bsz, seqlen, gate_bias_col, a_col, dsk_row, conv_w, conv_b, norm_w):
    c = min(PROMPT_CHUNK, seqlen)
    nch = seqlen // c
    gps = SSD_GPS
    tok = lambda width, col0: pl.BlockSpec((c, gps * width), lambda b, g, k: (b * nch + k, col0 // (gps * width) + g))
    vec = lambda rows, width, col0: pl.BlockSpec((rows, gps * width), lambda b, g, k: (0, col0 // (gps * width) + g))
    col = pl.BlockSpec((LANES, 1), lambda b, g, k: (0, 0))
    return _Part(
        body=_ssd_prompt_body, ids=_same_ids,
        in_specs=[
            tok(S_GW, COL_XS), tok(S_STATE, COL_BM), tok(S_STATE, COL_CM), tok(S_GW, COL_ZS),
            pl.BlockSpec((LANES, c), lambda b, g, k: (0, b * nch + k)),
            col, col, vec(1, S_GW, 0),
            vec(CONV_K, S_GW, 0), vec(1, S_GW, 0),
            vec(CONV_K, S_STATE, D_S), vec(1, S_STATE, D_S),
            vec(CONV_K, S_STATE, D_S + S_GROUPS * S_STATE), vec(1, S_STATE, D_S + S_GROUPS * S_STATE),
            vec(1, S_GW, 0),
        ],
        operands=[proj, proj, proj, proj, gates_t, gate_bias_col, a_col, dsk_row,
                  conv_w, conv_b, conv_w, conv_b, conv_w, conv_b, norm_w],
        out_specs=[
            pl.BlockSpec((c, gps * S_GW), lambda b, g, k: (b * nch + k, g)),
            pl.BlockSpec((1, gps, S_GW, S_STATE), lambda b, g, k: (b, g, 0, 0)),
        ],
        out_shape=[
            jax.ShapeDtypeStruct((bsz * seqlen, D_S), BF16),
            jax.ShapeDtypeStruct((bsz, S_GROUPS, S_GW, S_STATE), F32),
        ],
        scratch=[pltpu.VMEM((TAIL + c, gps * S_GW), F32), pltpu.VMEM((TAIL + c, gps * S_STATE), F32),
                 pltpu.VMEM((TAIL + c, gps * S_STATE), F32)],
    ), (bsz, S_GROUPS // gps, nch)


SEQ4 = 4


def _row_masks(rows):
    rowi = lax.broadcasted_iota(jnp.int32, (rows, rows), 0)
    coli = lax.broadcasted_iota(jnp.int32, (rows, rows), 1)
    same = _idiv(rowi, SEQ4) == _idiv(coli, SEQ4)
    return same, same & (coli <= rowi), coli == rowi, same & (_imod(coli, SEQ4) == SEQ4 - 1)


def _seg_cumsum(x, t):
    out = x
    for j in range(1, SEQ4):
        out = out + jnp.where(t >= j, pltpu.roll(x, j, 0), 0.0)
    return out


def _mlstm_sample_body(ids, u_ref, v_ref, o_ref, z_ref, gt_ref, gb_ref, buf_ref, mrow_ref, c0_ref, n0_ref,
                       cw_ref, cb_ref, wq_ref, wkt_ref, nw_ref,
                       hm_ref, c_ref, n_ref, m_ref):
    h = ids[0]
    rows = u_ref.shape[0]
    nseq = rows // SEQ4
    rid = lax.broadcasted_iota(jnp.int32, (rows, 1), 0)
    t = _imod(rid, SEQ4)
    seq = _idiv(rid, SEQ4)
    seq_lane = _idiv(lax.broadcasted_iota(jnp.int32, (1, rows), 1), SEQ4)

    ucb = _silu(_conv_rows(u_ref[...], buf_ref[...], cw_ref[...], cb_ref[...], t)).astype(BF16)
    q = _dot(ucb, wq_ref[0])
    kt = _dot_nt(wkt_ref[0], ucb)
    qb = q.astype(BF16)
    vb = v_ref[...].astype(BF16)

    capped = _softcap(gt_ref[...] + gb_ref[...])
    ig_col = _lane_pick(capped, GL_IG + h)
    b_col = _lane_pick(_seg_cumsum(_log_sigmoid(capped), t), GL_FG + h)
    m_prev = _lane_pick(mrow_ref[...], h)

    same, tri, eye, last = _row_masks(rows)
    b_row = _col_to_row(b_col, eye)
    ig_row = _col_to_row(ig_col, eye)
    dmat = jnp.where(tri, b_col - b_row + ig_row, -jnp.inf)
    inter = b_col + m_prev
    m_t = jnp.maximum(inter, jnp.max(dmat, axis=1, keepdims=True))
    a_inter = jnp.exp(inter - m_t)
    s = _dot(qb, kt.astype(BF16)) * jnp.exp(dmat - m_t)

    b_end = jnp.sum(jnp.where(last, b_row, 0.0), axis=1, keepdims=True)
    lw = b_end - b_col + ig_col
    m_new = jnp.maximum(b_end + m_prev, jnp.max(jnp.where(same, _col_to_row(lw, eye), -jnp.inf), axis=1, keepdims=True))
    decay = jnp.exp(b_end + m_prev - m_new)
    kwt = kt * _col_to_row(jnp.exp(lw - m_new), eye)
    eye_k = (lax.broadcasted_iota(jnp.int32, (M_QK, M_QK), 0) == lax.broadcasted_iota(jnp.int32, (M_QK, M_QK), 1))

    num_inter = jnp.zeros((rows, M_V), F32)
    qn = jnp.zeros((rows, 1), F32)
    for bi in range(nseq):
        own = seq == bi
        c_old = c0_ref[bi, 0]
        n_old = n0_ref[bi, 0]
        num_inter = jnp.where(own, _dot(qb, c_old.astype(BF16)), num_inter)
        qn = jnp.where(own, jnp.sum(q * n_old, axis=1, keepdims=True), qn)
        kw_b = jnp.where(seq_lane == bi, kwt, 0.0)
        d_b = jnp.sum(jnp.where(own & (t == SEQ4 - 1), decay, 0.0), axis=0, keepdims=True)
        c_ref[bi, 0] = d_b * c_old + _dot(kw_b.astype(BF16), vb)
        n_ref[bi, 0] = d_b * n_old + _col_to_row(jnp.sum(kw_b, axis=1, keepdims=True), eye_k)

    num = _dot(s.astype(BF16), vb) + a_inter * num_inter
    den = jnp.sum(s, axis=1, keepdims=True) + a_inter * qn
    hv = num / jnp.maximum(jnp.abs(den), jnp.exp(-m_t))
    hm_ref[...] = _head_out(hv, o_ref[...], z_ref[...], nw_ref[...]).astype(BF16)
    m_ref[0] = jnp.broadcast_to(m_new, (rows, LANES))


def _mlstm_sample_part(proj, gates, nseq_total, gate_bias, buf_rows, m_rows, c0, n0, conv_w, conv_b, wq, wkt, norm_w,
                       bb, ids=_same_ids):
    rows = bb * SEQ4
    nblk = nseq_total // bb
    spec = functools.partial(_on_grid, ids)
    tok = lambda col0: spec((rows, M_V), lambda h, i: (i, col0 // M_V + h))
    head_vec = lambda r: spec((r, M_V), lambda h, i: (0, h))
    c_spec = spec((bb, 1, M_QK, M_V), lambda h, i: (i, h, 0, 0))
    n_spec = spec((bb, 1, 1, M_QK), lambda h, i: (i, h, 0, 0))
    return _Part(
        body=_mlstm_sample_body, ids=ids,
        in_specs=[
            tok(COL_U), tok(COL_V), tok(COL_O), tok(COL_ZM),
            spec((rows, LANES), lambda h, i: (i, 0)),
            spec((1, LANES), lambda h, i: (0, 0)),
            spec((rows, M_V), lambda h, i: (i, h)),
            spec((rows, LANES), lambda h, i: (i, 0)),
            c_spec, n_spec,
            head_vec(CONV_K), head_vec(1),
            spec((1, M_V, M_QK), lambda h, i: (h, 0, 0)),
            spec((1, M_QK, M_V), lambda h, i: (h, 0, 0)),
            head_vec(1),
        ],
        operands=[proj, proj, proj, proj, gates, gate_bias, buf_rows, m_rows, c0, n0, conv_w, conv_b, wq, wkt, norm_w],
        out_specs=[
            spec((rows, M_V), lambda h, i: (i, h)),
            c_spec, n_spec,
            spec((1, rows, LANES), lambda h, i: (h, i, 0)),
        ],
        out_shape=[
            jax.ShapeDtypeStruct((nseq_total * SEQ4, D_M), BF16),
            jax.ShapeDtypeStruct((nseq_total, M_HEADS, M_QK, M_V), F32),
            jax.ShapeDtypeStruct((nseq_total, M_HEADS, 1, M_QK), F32),
            jax.ShapeDtypeStruct((M_HEADS, nseq_total * SEQ4, LANES), F32),
        ],
        scratch=[],
    ), (M_HEADS, nblk)


def _ssd_sample_body(xs_ref, bm_ref, cm_ref, zs_ref, gt_ref, gbc_ref, avc_ref, dsk_ref,
                     bufx_ref, bufb_ref, bufc_ref, h0_ref,
                     cwx_ref, cbx_ref, cwb_ref, cbb_ref, cwc_ref, cbc_ref, nw_ref,
                     ys_ref, h_ref, yit_ref, xwt_ref):
    g = pl.program_id(0)
    rows = xs_ref.shape[0]
    nseq = rows // SEQ4
    rid = lax.broadcasted_iota(jnp.int32, (rows, 1), 0)
    t = _imod(rid, SEQ4)
    seq = _idiv(rid, SEQ4)
    tok_lane = lax.broadcasted_iota(jnp.int32, (1, rows), 1)

    xs = _silu(_conv_rows(xs_ref[...], bufx_ref[...], cwx_ref[...], cbx_ref[...], t))
    bmb = _silu(_conv_rows(bm_ref[...], bufb_ref[...], cwb_ref[...], cbb_ref[...], t)).astype(BF16)
    cmb = _silu(_conv_rows(cm_ref[...], bufc_ref[...], cwc_ref[...], cbc_ref[...], t)).astype(BF16)

    lane0 = pl.multiple_of(GL_DT + g * S_HPG, S_HPG)
    dt8, la8 = _ssd_gate_rows(gt_ref, gbc_ref, avc_ref, lane0)
    src = lax.broadcasted_iota(jnp.int32, (rows, rows), 0)
    tgt = lax.broadcasted_iota(jnp.int32, (rows, rows), 1)
    same = _idiv(src, SEQ4) == _idiv(tgt, SEQ4)
    causal = same & (src <= tgt)
    cum8 = _rows_dot_01(la8, jnp.where(causal, 1.0, 0.0))
    a_end = _rows_dot_01(cum8, jnp.where(same & (_imod(src, SEQ4) == SEQ4 - 1), 1.0, 0.0))
    w8 = jnp.exp(a_end - cum8)
    ea8 = jnp.exp(cum8)
    ea_end = jnp.exp(a_end)

    cbt = jnp.where(causal, _dot_nt(bmb, cmb), 0.0)
    ys, xdts = _ssd_intra(xs.T, dt8, cum8, cbt)
    xwt_ref[...] = jnp.concatenate([(xdts[r] * w8[r:r + 1, :]).astype(BF16) for r in range(S_HPG)], axis=0)
    yit_ref[...] = jnp.zeros_like(yit_ref)

    def per_sequence(bi, carry):
        h_old = h0_ref[bi, 0]
        own = _idiv(tok_lane, SEQ4) == bi
        yit_ref[...] += _dot_nt(h_old.astype(BF16), jnp.where(seq == bi, cmb, jnp.zeros_like(cmb)))
        upd = _dot(jnp.where(own, xwt_ref[...], jnp.zeros_like(xwt_ref)), bmb)
        d8 = jnp.sum(jnp.where(tok_lane == bi * SEQ4 + SEQ4 - 1, ea_end, 0.0), axis=1, keepdims=True)
        for r in range(S_HPG):
            h_ref[bi, 0, HEAD_ROWS[r], :] = d8[r:r + 1, :] * h_old[HEAD_ROWS[r], :] + upd[HEAD_ROWS[r], :]
        return carry

    lax.fori_loop(0, nseq, per_sequence, 0, unroll=8 if nseq % 8 == 0 else 1)
    yit = yit_ref[...]
    yt = jnp.concatenate([ys[r] + ea8[r:r + 1, :] * yit[HEAD_ROWS[r], :] for r in range(S_HPG)], axis=0)
    ys_ref[...] = _group_out(yt.T, xs, dsk_ref[...], zs_ref[...], nw_ref[...]).astype(BF16)


SSD_SAMPLE_BB = 32


def _ssd_sample(proj, gates_t, nseq_total, gate_bias_col, a_col, dsk_row, buf_rows, h0, conv_w, conv_b, norm_w):
    bb = min(SSD_SAMPLE_BB, nseq_total)
    rows = bb * SEQ4
    nblk = nseq_total // bb
    tok = lambda width, col0: pl.BlockSpec((rows, width), lambda g, i: (i, col0 // width + g))
    vec = lambda r, width, col0: pl.BlockSpec((r, width), lambda g, i: (0, col0 // width + g))
    col = pl.BlockSpec((LANES, 1), lambda g, i: (0, 0))
    h_spec = pl.BlockSpec((bb, 1, S_GW, S_STATE), lambda g, i: (i, g, 0, 0))
    off_b = D_S
    off_c = D_S + S_GROUPS * S_STATE
    return pl.pallas_call(
        _ssd_sample_body,
        grid=(S_GROUPS, nblk),
        in_specs=[
            tok(S_GW, COL_XS), tok(S_STATE, COL_BM), tok(S_STATE, COL_CM), tok(S_GW, COL_ZS),
            pl.BlockSpec((LANES, rows), lambda g, i: (0, i)),
            col, col, vec(1, S_GW, 0),
            tok(S_GW, 0), tok(S_STATE, off_b), tok(S_STATE, off_c),
            h_spec,
            vec(CONV_K, S_GW, 0), vec(1, S_GW, 0),
            vec(CONV_K, S_STATE, off_b), vec(1, S_STATE, off_b),
            vec(CONV_K, S_STATE, off_c), vec(1, S_STATE, off_c),
            vec(1, S_GW, 0),
        ],
        out_specs=[pl.BlockSpec((rows, S_GW), lambda g, i: (i, g)), h_spec],
        out_shape=[
            jax.ShapeDtypeStruct((nseq_total * SEQ4, D_S), BF16),
            jax.ShapeDtypeStruct((nseq_total, S_GROUPS, S_GW, S_STATE), F32),
        ],
        scratch_shapes=[pltpu.VMEM((S_GW, rows), F32), pltpu.VMEM((S_GW, rows), BF16)],
        compiler_params=pltpu.CompilerParams(
            dimension_semantics=("parallel", "parallel"), vmem_limit_bytes=VMEM_LIMIT),
        name="ssd_sample",
    )(proj, proj, proj, proj, gates_t, gate_bias_col, a_col, dsk_row, buf_rows, buf_rows, buf_rows, h0,
      conv_w, conv_b, conv_w, conv_b, conv_w, conv_b, norm_w)


OUT_TM, OUT_TN = 512, 512


def _out_proj_body(ids, x_ref, hm_ref, ys_ref, wt_ref, wb_ref, gn_ref, out_ref, x1g_ref, ssp_ref):
    x1 = x_ref[...] + _dot(hm_ref[...], wt_ref[...]) + _dot(ys_ref[...], wb_ref[...])
    out_ref[...] = x1
    x1g_ref[...] = (x1 * gn_ref[...]).astype(BF16)
    ssp_ref[0] = jnp.sum(x1 * x1, axis=1, keepdims=True)


def _out_proj_part(x, hm, ys, w_out, ple_norm, ids=_same_ids):
    m = x.shape[0]
    tm, tn = min(OUT_TM, m), OUT_TN
    nj = D_MODEL // tn
    spec = functools.partial(_on_grid, ids)
    return _Part(
        body=_out_proj_body, ids=ids,
        in_specs=[
            spec((tm, tn), lambda i, j: (i, j)),
            spec((tm, D_M), lambda i, j: (i, 0)),
            spec((tm, D_S), lambda i, j: (i, 0)),
            spec((D_M, tn), lambda i, j: (0, j)),
            spec((D_S, tn), lambda i, j: (1, j)),
            spec((1, tn), lambda i, j: (0, j)),
        ],
        operands=[x, hm, ys, w_out, w_out, ple_norm],
        out_specs=[
            spec((tm, tn), lambda i, j: (i, j)),
            spec((tm, tn), lambda i, j: (i, j)),
            spec((1, tm, 1), lambda i, j: (j, i, 0)),
        ],
        out_shape=[jax.ShapeDtypeStruct((m, D_MODEL), F32), jax.ShapeDtypeStruct((m, D_MODEL), BF16),
                   jax.ShapeDtypeStruct((nj, m, 1), F32)],
        scratch=[],
    ), (m // tm, nj)


def _ple_body(x1g_ref, ss1_ref, x1t_ref, p_ref, wg_ref, wp_ref, fn_ref, out_ref, ss_ref):
    j = pl.program_id(1)
    nj = pl.num_programs(1)
    tn = wg_ref.shape[1]

    @pl.when(j == 0)
    def _():
        ss_ref[...] = jnp.zeros_like(ss_ref)

    rs = lax.rsqrt(jnp.sum(ss1_ref[...], axis=0) * (1.0 / D_MODEL) + EPS)
    gate = _sigmoid(rs * _dot(x1g_ref[...], wg_ref[...]))
    x2 = x1t_ref[...] + gate * _dot(p_ref[...].astype(BF16), wp_ref[...])
    ss_ref[...] += jnp.sum(x2 * x2, axis=1, keepdims=True)
    for jj in range(out_ref.shape[1] // tn):
        @pl.when(j == jj)
        def _(jj=jj):
            out_ref[:, jj * tn:(jj + 1) * tn] = x2

    @pl.when(j == nj - 1)
    def _():
        fn = fn_ref[...]

        def body(r, carry):
            sl = pl.ds(pl.multiple_of(r * NORM_ROWS, NORM_ROWS), NORM_ROWS)
            scale = lax.rsqrt(ss_ref[sl, :] * (1.0 / D_MODEL) + EPS)
            out_ref[sl, :] = out_ref[sl, :] * scale * fn
            return carry
        lax.fori_loop(0, out_ref.shape[0] // NORM_ROWS, body, 0)


def _ple(x1, x1g, ss1, p, w_gate, w_proj, final_norm):
    m = x1.shape[0]
    tm = min(512, m)
    tn = 1024
    return pl.pallas_call(
        _ple_body,
        grid=(m // tm, D_MODEL // tn),
        in_specs=[
            pl.BlockSpec((tm, D_MODEL), lambda i, j: (i, 0)),
            pl.BlockSpec((ss1.shape[0], tm, 1), lambda i, j: (0, i, 0)),
            pl.BlockSpec((tm, tn), lambda i, j: (i, j)),
            pl.BlockSpec((tm, PLE_DIM), lambda i, j: (i, 0)),
            pl.BlockSpec((D_MODEL, tn), lambda i, j: (0, j)),
            pl.BlockSpec((PLE_DIM, tn), lambda i, j: (0, j)),
            pl.BlockSpec((1, D_MODEL), lambda i, j: (0, 0)),
        ],
        out_specs=pl.BlockSpec((tm, D_MODEL), lambda i, j: (i, 0)),
        out_shape=jax.ShapeDtypeStruct((m, D_MODEL), F32),
        scratch_shapes=[pltpu.VMEM((tm, 1), F32)],
        compiler_params=pltpu.CompilerParams(
            dimension_semantics=("parallel", "arbitrary"), vmem_limit_bytes=VMEM_LIMIT),
        name="ple",
    )(x1g, ss1, x1, p, w_gate, w_proj, final_norm)


CONV_TC = 2048


def _slot_row_match(nseq, slot_major_axis, first_token):
    shape = (nseq * SEQ4, (CONV_K - 1) * nseq) if slot_major_axis == 1 else ((CONV_K - 1) * nseq, nseq * SEQ4)
    sm = lax.broadcasted_iota(jnp.int32, shape, slot_major_axis)
    row = lax.broadcasted_iota(jnp.int32, shape, 1 - slot_major_axis)
    hit = None
    for slot in range(CONV_K - 1):
        b = sm - slot * nseq
        this = (b >= 0) & (b < nseq) & (row == b * SEQ4 + first_token + slot)
        hit = this if hit is None else hit | this
    return jnp.where(hit, 1.0, 0.0)


def _conv_rows_in_body(buf_ref, rows_ref):
    nseq = buf_ref.shape[1]
    pick = _slot_row_match(nseq, 1, 0)
    rows_ref[...] = _dot_select(pick, buf_ref[...].reshape((CONV_K - 1) * nseq, buf_ref.shape[2]))


def _conv_rows_in(buf):
    _, nseq, ch = buf.shape
    tc = min(CONV_TC, ch)
    return pl.pallas_call(
        _conv_rows_in_body,
        grid=(ch // tc,),
        in_specs=[pl.BlockSpec((CONV_K - 1, nseq, tc), lambda j: (0, 0, j))],
        out_specs=pl.BlockSpec((nseq * SEQ4, tc), lambda j: (0, j)),
        out_shape=jax.ShapeDtypeStruct((nseq * SEQ4, ch), F32),
        compiler_params=pltpu.CompilerParams(dimension_semantics=("parallel",), vmem_limit_bytes=VMEM_LIMIT),
        name="conv_rows_in",
    )(buf)


def _conv_rows_out_body(rows_ref, buf_ref):
    nseq = buf_ref.shape[1]
    pick = _slot_row_match(nseq, 0, SEQ4 - (CONV_K - 1))
    buf_ref[...] = _dot_select(pick, rows_ref[...]).reshape(buf_ref.shape)


def _conv_rows_out(proj, col0, ch):
    nseq = proj.shape[0] // SEQ4
    tc = min(CONV_TC, ch)
    assert col0 % tc == 0 and ch % tc == 0
    return pl.pallas_call(
        _conv_rows_out_body,
        grid=(ch // tc,),
        in_specs=[pl.BlockSpec((nseq * SEQ4, tc), lambda j: (0, col0 // tc + j))],
        out_specs=pl.BlockSpec((CONV_K - 1, nseq, tc), lambda j: (0, 0, j)),
        out_shape=jax.ShapeDtypeStruct((CONV_K - 1, nseq, ch), F32),
        compiler_params=pltpu.CompilerParams(dimension_semantics=("parallel",), vmem_limit_bytes=VMEM_LIMIT),
        name="conv_rows_out",
    )(proj)


def _pad_lanes(v, offset):
    out = jnp.zeros((1, LANES), F32)
    return lax.dynamic_update_slice(out, v.reshape(1, -1).astype(F32), (0, offset))


def kernel(x_prompt, x_sample, p_prompt, p_sample, state_m_C, state_m_n, state_m_m, state_m_conv, state_s_ssm, state_s_conv, norm_in, w_in, b_ig, b_fg, m_conv_w, m_conv_b, w_q, w_k, m_norm, s_conv_w, s_conv_b, dt_bias, a_log, d_skip, s_norm, w_out, ple_proj, ple_gate, ple_norm, final_norm):
    assert w_in.shape[0] == 1, "single layer"
    bsz, seqlen, _ = x_prompt.shape
    nseq, dec_seq, _ = x_sample.shape
    assert dec_seq == SEQ4

    assert w_in.shape[2] == N_MAIN + GATE_COLS + S_HEADS
    w_in_t = jnp.swapaxes(w_in, 1, 2)
    gate_bias = jnp.concatenate(
        [b_ig[0], b_fg[0], dt_bias[0], jnp.zeros((LANES - 2 * M_HEADS - S_HEADS,), F32)]).reshape(1, LANES)
    gate_bias_col = gate_bias.reshape(LANES, 1)
    a_col = _pad_lanes(a_log[0], GL_DT).reshape(LANES, 1)
    dsk_row = jnp.repeat(d_skip[0], S_HEAD_DIM).reshape(1, D_S)
    wq = w_q[0].astype(BF16)
    wkt = (jnp.swapaxes(w_k[0], 1, 2) * (M_QK ** -0.5)).astype(BF16)
    ple_proj_b = ple_proj[0].astype(BF16)
    norm_in_r = norm_in[0].reshape(1, D_MODEL)
    ple_norm_r = ple_norm[0].reshape(1, D_MODEL)
    final_norm_r = final_norm.reshape(1, D_MODEL)
    m_conv_b_r = m_conv_b[0].reshape(1, D_M)
    s_conv_b_r = s_conv_b[0].reshape(1, CONV_DIM)
    m_norm_r = m_norm[0].reshape(1, D_M)
    s_norm_r = s_norm[0].reshape(1, D_S)

    xp = x_prompt.reshape(bsz * seqlen, D_MODEL)
    xs_ = x_sample.reshape(nseq * SEQ4, D_MODEL)
    in_part, in_grid = _in_proj_part(_rmsnorm_cast(xp, norm_in_r), w_in_t)
    (proj_p, _, gates_pt), = _call_parts("in_proj", in_grid, [in_part])

    mlstm_part, mlstm_grid = _mlstm_prompt_part(
        proj_p, gates_pt, bsz, seqlen, gate_bias_col, m_conv_w[0], m_conv_b_r, wq, wkt, m_norm_r)
    nsteps = mlstm_grid[0] * mlstm_grid[1] * mlstm_grid[2]
    step = _linear_step(mlstm_grid)
    (hm_p, pc, pn, pm), (w_out_b,), (ple_gate_b,) = _call_parts(
        "mlstm_prompt_casts", mlstm_grid,
        [mlstm_part, _cast_part(w_out[0], nsteps, step), _cast_part(ple_gate[0], nsteps, step)])

    ssd_part, ssd_grid = _ssd_prompt_part(
        proj_p, gates_pt, bsz, seqlen, gate_bias_col, a_col, dsk_row, s_conv_w[0], s_conv_b_r, s_norm_r)
    (ys_p, ph), = _call_parts("ssd_prompt", ssd_grid, [ssd_part])
    in_part_s, in_grid_s = _in_proj_part(_rmsnorm_cast(xs_, norm_in_r), w_in_t)
    (proj_s, gates_s, gates_st), = _call_parts("in_proj", in_grid_s, [in_part_s])

    proj_p3 = proj_p.reshape(bsz, seqlen, N_MAIN)
    tail = seqlen - (CONV_K - 1)
    p_mconv = proj_p3[:, tail:, COL_U:COL_U + D_M]
    p_sconv = proj_p3[:, tail:, COL_XS:COL_XS + CONV_DIM]

    pad_rows = lambda st: _conv_rows_in(jnp.swapaxes(st, 0, 1))
    m_rows = jnp.pad(jnp.repeat(state_m_m[0], SEQ4, axis=0), ((0, 0), (0, LANES - M_HEADS)))

    sample_args = (proj_s, gates_s, nseq, gate_bias, pad_rows(state_m_conv[0]), m_rows, state_m_C[0],
                   state_m_n[0].reshape(nseq, M_HEADS, 1, M_QK), m_conv_w[0], m_conv_b_r, wq, wkt, m_norm_r)
    out_part, out_grid = _out_proj_part(xp, hm_p, ys_p, w_out_b, ple_norm_r)
    bb = SAMPLE_BB
    while bb > 1 and (nseq % bb or M_HEADS * (nseq // bb) < out_grid[0] * out_grid[1]):
        bb //= 2
    if M_HEADS * (nseq // bb) == out_grid[0] * out_grid[1]:
        nblk = nseq // bb
        as_head_block = lambda i, j: ((i * out_grid[1] + j) // nblk, (i * out_grid[1] + j) % nblk)
        scan_part, _ = _mlstm_sample_part(*sample_args, bb, ids=as_head_block)
        (x1_p, x1g_p, ss1_p), (hm_s, sc, sn, sm) = _call_parts("out_proj_mlstm_sample", out_grid, [out_part, scan_part])
    else:
        (x1_p, x1g_p, ss1_p), = _call_parts("out_proj", out_grid, [out_part])
        scan_part, scan_grid = _mlstm_sample_part(*sample_args, bb)
        (hm_s, sc, sn, sm), = _call_parts("mlstm_sample", scan_grid, [scan_part])
    y_p = _ple(x1_p, x1g_p, ss1_p, p_prompt[0].reshape(bsz * seqlen, PLE_DIM), ple_gate_b, ple_proj_b, final_norm_r)

    ys_s, sh = _ssd_sample(
        proj_s, gates_st, nseq, gate_bias_col, a_col, dsk_row, pad_rows(state_s_conv[0]),
        state_s_ssm[0].reshape(nseq, S_GROUPS, S_GW, S_STATE), s_conv_w[0], s_conv_b_r, s_norm_r)
    out_part_s, out_grid_s = _out_proj_part(xs_, hm_s, ys_s, w_out_b, ple_norm_r)
    (x1_s, x1g_s, ss1_s), = _call_parts("out_proj", out_grid_s, [out_part_s])
    y_s = _ple(x1_s, x1g_s, ss1_s, p_sample[0].reshape(nseq * SEQ4, PLE_DIM), ple_gate_b, ple_proj_b, final_norm_r)

    s_mconv = jnp.swapaxes(_conv_rows_out(proj_s, COL_U, D_M), 0, 1)
    s_sconv = jnp.swapaxes(_conv_rows_out(proj_s, COL_XS, CONV_DIM), 0, 1)
    s_m = jnp.transpose(sm[:, SEQ4 - 1::SEQ4, 0])

    return (
        y_p.reshape(bsz, seqlen, D_MODEL),
        y_s.reshape(nseq, SEQ4, D_MODEL),
        pc[None], pn.reshape(1, bsz, M_HEADS, M_QK), pm[:, :, 0, 0][None], p_mconv[None],
        ph.reshape(1, bsz, S_HEADS, S_HEAD_DIM, S_STATE), p_sconv[None],
        sc[None], sn.reshape(1, nseq, M_HEADS, M_QK), s_m[None], s_mconv[None],
        sh.reshape(1, nseq, S_HEADS, S_HEAD_DIM, S_STATE), s_sconv[None],
    )
```

```python
import functools
from typing import Callable, NamedTuple

import jax
import jax.numpy as jnp
from jax import lax
from jax.experimental import pallas as pl
from jax.experimental.pallas import tpu as pltpu

F32 = jnp.float32
BF16 = jnp.bfloat16

D_MODEL = 4096
D_M = 4096
D_S = 4096
M_HEADS = 8
M_V = 512
M_QK = 256
S_HEADS = 64
S_HEAD_DIM = 64
S_STATE = 128
S_GROUPS = 8
S_HPG = 8
S_GW = S_HPG * S_HEAD_DIM
CONV_DIM = D_S + 2 * S_GROUPS * S_STATE
CONV_K = 4
PLE_DIM = 256
GATE_CAP = 15.0
EPS = 1e-6

COL_U, COL_V, COL_O, COL_ZM, COL_ZS, COL_XS = 0, 4096, 8192, 12288, 16384, 20480
COL_BM = COL_XS + D_S
COL_CM = COL_BM + S_GROUPS * S_STATE
N_MAIN = COL_XS + CONV_DIM
LANES = 128
GL_IG, GL_FG, GL_DT = 0, M_HEADS, 2 * M_HEADS

PROMPT_CHUNK = 256
SAMPLE_BB = 16
VMEM_LIMIT = 62 * 1024 * 1024

NT_DIMS = (((1,), (1,)), ((), ()))


def _dot(a, b):
    return jnp.dot(a, b, preferred_element_type=F32)


def _dot_nt(a, b):
    return lax.dot_general(a, b, NT_DIMS, preferred_element_type=F32)


def _rows_dot_01(a, b01):
    n = a.shape[0]
    hi = a.astype(BF16).astype(F32)
    rest = a - hi
    mid = rest.astype(BF16).astype(F32)
    pieces = jnp.concatenate([hi, mid, rest - mid], axis=0).astype(BF16)
    r = _dot(pieces, b01.astype(BF16))
    return (r[0:n] + r[n:2 * n]) + r[2 * n:3 * n]


def _dot_select(pick, x):
    hi = x.astype(BF16)
    rest = x - hi.astype(F32)
    mid = rest.astype(BF16)
    lo = (rest - mid.astype(F32)).astype(BF16)
    p = pick.astype(BF16)
    return (_dot(p, hi) + _dot(p, mid)) + _dot(p, lo)


def _sigmoid(x):
    return 0.5 * jnp.tanh(0.5 * x) + 0.5


def _silu(x):
    return x * _sigmoid(x)


def _softcap(x):
    return GATE_CAP * jnp.tanh(x / GATE_CAP)


def _softplus(x):
    return jnp.maximum(x, 0.0) + jnp.log1p(jnp.exp(-jnp.abs(x)))


def _log_sigmoid(x):
    return -_softplus(-x)


def _idiv(x, d):
    return lax.shift_right_logical(x, d.bit_length() - 1)


def _imod(x, d):
    return lax.bitwise_and(x, d - 1)


def _lane_pick(x, lane_idx):
    lane = lax.broadcasted_iota(jnp.int32, (1, x.shape[1]), 1)
    return jnp.sum(jnp.where(lane == lane_idx, x, 0.0), axis=1, keepdims=True)


def _col_to_row(col, eye):
    return jnp.sum(jnp.where(eye, col, 0.0), axis=0, keepdims=True)


def _row_to_col(row, eye):
    return jnp.sum(jnp.where(eye, row, 0.0), axis=1, keepdims=True)


class _Part(NamedTuple):
    body: Callable
    ids: Callable
    in_specs: list
    operands: list
    out_specs: list
    out_shape: list
    scratch: list


def _same_ids(*g):
    return g


def _on_grid(ids, block_shape, index_map, **kwargs):
    return pl.BlockSpec(block_shape, lambda *g: index_map(*ids(*g)), **kwargs)


def _call_parts(name, grid, parts):
    n_in = [len(p.in_specs) for p in parts]
    n_out = [len(p.out_specs) for p in parts]
    n_scr = [len(p.scratch) for p in parts]

    def body(*refs):
        gids = tuple(pl.program_id(a) for a in range(len(grid)))
        ins, outs, scr = refs[:sum(n_in)], refs[sum(n_in):sum(n_in) + sum(n_out)], refs[sum(n_in) + sum(n_out):]
        for p, part in enumerate(parts):
            a, b, c = sum(n_in[:p]), sum(n_out[:p]), sum(n_scr[:p])
            part.body(part.ids(*gids), *ins[a:a + n_in[p]], *outs[b:b + n_out[p]], *scr[c:c + n_scr[p]])

    results = pl.pallas_call(
        body,
        grid=grid,
        in_specs=[s for p in parts for s in p.in_specs],
        out_specs=[s for p in parts for s in p.out_specs],
        out_shape=[s for p in parts for s in p.out_shape],
        scratch_shapes=[s for p in parts for s in p.scratch],
        compiler_params=pltpu.CompilerParams(
            dimension_semantics=("arbitrary",) * len(grid), vmem_limit_bytes=VMEM_LIMIT),
        name=name,
    )(*[o for p in parts for o in p.operands])
    return [results[sum(n_out[:p]):sum(n_out[:p]) + n_out[p]] for p in range(len(parts))]


def _linear_step(grid):
    def lin(*g):
        s = g[0]
        for size, idx in zip(grid[1:], g[1:]):
            s = s * size + idx
        return s
    return lin


def _cast_body(ids, src_ref, dst_ref):
    dst_ref[...] = src_ref[...].astype(dst_ref.dtype)


def _cast_part(w, nsteps, step):
    rows = w.shape[0] // nsteps
    assert rows * nsteps == w.shape[0] and rows % 16 == 0
    spec = pl.BlockSpec((rows, w.shape[1]), lambda *g: (step(*g), 0))
    return _Part(body=_cast_body, ids=_same_ids, in_specs=[spec], operands=[w], out_specs=[spec],
                 out_shape=[jax.ShapeDtypeStruct(w.shape, BF16)], scratch=[])


NORM_ROWS = 32


def _rmsnorm_rows_to(x_ref, g, dst_ref):
    def body(r, carry):
        sl = pl.ds(pl.multiple_of(r * NORM_ROWS, NORM_ROWS), NORM_ROWS)
        x = x_ref[sl, :]
        ms = jnp.mean(x * x, axis=-1, keepdims=True)
        dst_ref[sl, :] = (x * lax.rsqrt(ms + EPS) * g).astype(BF16)
        return carry
    lax.fori_loop(0, x_ref.shape[0] // NORM_ROWS, body, 0)


def _in_proj_body(x_ref, g_ref, w_ref, wg_ref, out_ref, gates_ref, gatest_ref, xn_ref):
    @pl.when(pl.program_id(1) == 0)
    def _():
        _rmsnorm_rows_to(x_ref, g_ref[...], xn_ref)
        gates = _dot_nt(xn_ref[...], wg_ref[...])
        gates_ref[...] = gates
        gatest_ref[...] = gates.T
    out_ref[...] = _dot_nt(xn_ref[...], w_ref[...])


def _in_proj(x, norm_w, w_main, w_gate):
    m = x.shape[0]
    tm = min(512, m)
    tn = 1024
    grid = (m // tm, N_MAIN // tn)
    return pl.pallas_call(
        _in_proj_body,
        grid=grid,
        in_specs=[
            pl.BlockSpec((tm, D_MODEL), lambda i, j: (i, 0)),
            pl.BlockSpec((1, D_MODEL), lambda i, j: (0, 0)),
            pl.BlockSpec((tn, D_MODEL), lambda i, j: (j, 0)),
            pl.BlockSpec((LANES, D_MODEL), lambda i, j: (0, 0)),
        ],
        out_specs=[
            pl.BlockSpec((tm, tn), lambda i, j: (i, j)),
            pl.BlockSpec((tm, LANES), lambda i, j: (i, 0)),
            pl.BlockSpec((LANES, tm), lambda i, j: (0, i)),
        ],
        out_shape=[jax.ShapeDtypeStruct((m, N_MAIN), F32), jax.ShapeDtypeStruct((m, LANES), F32),
                   jax.ShapeDtypeStruct((LANES, m), F32)],
        scratch_shapes=[pltpu.VMEM((tm, D_MODEL), BF16)],
        compiler_params=pltpu.CompilerParams(
            dimension_semantics=("parallel", "arbitrary"), vmem_limit_bytes=VMEM_LIMIT),
        name="in_proj",
    )(x, norm_w, w_main, w_gate)


def _rmsnorm_cast_body(x_ref, g_ref, out_ref):
    _rmsnorm_rows_to(x_ref, g_ref[...], out_ref)


def _rmsnorm_cast(x, norm_w):
    m = x.shape[0]
    tm = min(512, m)
    return pl.pallas_call(
        _rmsnorm_cast_body,
        grid=(m // tm,),
        in_specs=[pl.BlockSpec((tm, D_MODEL), lambda i: (i, 0)), pl.BlockSpec((1, D_MODEL), lambda i: (0, 0))],
        out_specs=pl.BlockSpec((tm, D_MODEL), lambda i: (i, 0)),
        out_shape=jax.ShapeDtypeStruct((m, D_MODEL), BF16),
        compiler_params=pltpu.CompilerParams(dimension_semantics=("parallel",), vmem_limit_bytes=VMEM_LIMIT),
        name="rmsnorm_cast",
    )(x, norm_w)


def _in_proj_w32_body(ids, xn_ref, w_ref, ga_ref, gb_ref, out_ref, gates_ref, gatest_ref, wg_ref):
    @pl.when(ids[1] == 0)
    def _():
        wg_ref[0:GL_DT, :] = ga_ref[0].astype(BF16)
        wg_ref[GL_DT:GL_DT + S_HEADS, :] = gb_ref[0].astype(BF16)
        wg_ref[GL_DT + S_HEADS:LANES, :] = jnp.zeros((LANES - GL_DT - S_HEADS, D_MODEL), BF16)
        gates = _dot_nt(xn_ref[...], wg_ref[...])
        gates_ref[...] = gates
        gatest_ref[...] = gates.T
    out_ref[...] = _dot_nt(xn_ref[...], w_ref[0].astype(BF16))


IN_TM, IN_TN = 2048, 512


def _in_proj_part(xn, w_in_t, ids=_same_ids):
    m = xn.shape[0]
    tm = min(IN_TM, m)
    tn = IN_TN if tm * 4 > IN_TM else 2 * IN_TN

    def src_row(j):
        return pl.multiple_of(j * tn + (j // (N_ALIGNED // tn)) * GATE_COLS, GATE_COLS)

    el = pl.Element
    spec = functools.partial(_on_grid, ids)
    return _Part(
        body=_in_proj_w32_body, ids=ids,
        in_specs=[
            spec((tm, D_MODEL), lambda i, j: (i, 0), pipeline_mode=pl.Buffered(1)),
            spec((el(1), el(tn), el(D_MODEL)), lambda i, j: (0, src_row(j), 0)),
            spec((el(1), el(GATE_COLS), el(D_MODEL)), lambda i, j: (0, N_ALIGNED, 0)),
            spec((el(1), el(S_HEADS), el(D_MODEL)), lambda i, j: (0, N_MAIN + GATE_COLS, 0)),
        ],
        operands=[xn, w_in_t, w_in_t, w_in_t],
        out_specs=[
            spec((tm, tn), lambda i, j: (i, j)),
            spec((tm, LANES), lambda i, j: (i, 0)),
            spec((LANES, tm), lambda i, j: (0, i)),
        ],
        out_shape=[jax.ShapeDtypeStruct((m, N_MAIN), F32), jax.ShapeDtypeStruct((m, LANES), F32),
                   jax.ShapeDtypeStruct((LANES, m), F32)],
        scratch=[pltpu.VMEM((LANES, D_MODEL), BF16)],
    ), (m // tm, N_MAIN // tn)


GATE_COLS = 2 * M_HEADS
N_ALIGNED = 4 * D_M


def _w_prep_body(src_ref, ga_ref, gb_ref, out_ref, wg_ref):
    out_ref[...] = src_ref[0].astype(BF16)

    @pl.when(pl.program_id(0) == 0)
    def _():
        wg_ref[0:GL_DT, :] = ga_ref[0].astype(BF16)
        wg_ref[GL_DT:GL_DT + S_HEADS, :] = gb_ref[0].astype(BF16)
        wg_ref[GL_DT + S_HEADS:LANES, :] = jnp.zeros((LANES - GL_DT - S_HEADS, D_MODEL), BF16)


def _w_prep(w_in_t):
    rb = 512

    def src_row(i):
        return pl.multiple_of(i * rb + (i // (N_ALIGNED // rb)) * GATE_COLS, GATE_COLS)

    return pl.pallas_call(
        _w_prep_body,
        grid=(N_MAIN // rb,),
        in_specs=[
            pl.BlockSpec((pl.Element(1), pl.Element(rb), pl.Element(D_MODEL)), lambda i: (0, src_row(i), 0)),
            pl.BlockSpec((pl.Element(1), pl.Element(GATE_COLS), pl.Element(D_MODEL)), lambda i: (0, N_ALIGNED, 0)),
            pl.BlockSpec((pl.Element(1), pl.Element(S_HEADS), pl.Element(D_MODEL)),
                         lambda i: (0, N_MAIN + GATE_COLS, 0)),
        ],
        out_specs=[pl.BlockSpec((rb, D_MODEL), lambda i: (i, 0)), pl.BlockSpec((LANES, D_MODEL), lambda i: (0, 0))],
        out_shape=[jax.ShapeDtypeStruct((N_MAIN, D_MODEL), BF16), jax.ShapeDtypeStruct((LANES, D_MODEL), BF16)],
        compiler_params=pltpu.CompilerParams(dimension_semantics=("arbitrary",), vmem_limit_bytes=VMEM_LIMIT),
        name="w_prep",
    )(w_in_t, w_in_t, w_in_t)


TAIL = 8


def _conv_carry(u, ext_ref, cw, cb):
    c = u.shape[0]
    ext_ref[TAIL:TAIL + c, :] = u
    acc = cb + u * cw[CONV_K - 1:CONV_K, :]
    for j in range(1, CONV_K):
        acc = acc + ext_ref[TAIL - j:TAIL - j + c, :] * cw[CONV_K - 1 - j:CONV_K - j, :]
    ext_ref[0:TAIL, :] = u[c - TAIL:c, :]
    return acc


def _conv_rows(u, buf, cw, cb, t):
    rows = u.shape[0]
    acc = cb + u * cw[CONV_K - 1:CONV_K, :]
    for j in range(1, CONV_K):
        prev = buf if j == CONV_K - 1 else pltpu.roll(buf, rows + j - (CONV_K - 1), 0)
        acc = acc + jnp.where(t >= j, pltpu.roll(u, j, 0), prev) * cw[CONV_K - 1 - j:CONV_K - j, :]
    return acc


def _head_out(hv, o, z, nw):
    og = _sigmoid(o) * hv
    ms = jnp.mean(og * og, axis=-1, keepdims=True)
    return (og * lax.rsqrt(ms + EPS) * nw) * _silu(z)


def _group_out(y, xs, dsk, z, nw):
    gated = (y + dsk * xs) * _silu(z)
    ms = jnp.mean(gated * gated, axis=-1, keepdims=True)
    return gated * lax.rsqrt(ms + EPS) * nw


def _mlstm_prompt_body(ids, u_ref, v_ref, o_ref, z_ref, gt_ref, gb_ref, cw_ref, cb_ref, wq_ref, wkt_ref, nw_ref,
                       hm_ref, c_ref, n_ref, m_ref, ext_ref, mst_ref):
    c = u_ref.shape[0]
    hps = wq_ref.shape[0]
    h0 = ids[1] * hps

    @pl.when(ids[2] == 0)
    def _():
        c_ref[...] = jnp.zeros_like(c_ref)
        n_ref[...] = jnp.zeros_like(n_ref)
        mst_ref[...] = jnp.full_like(mst_ref, -jnp.inf)
        ext_ref[0:TAIL, :] = jnp.zeros((TAIL, ext_ref.shape[1]), F32)

    ucb = _silu(_conv_carry(u_ref[...], ext_ref, cw_ref[...], cb_ref[...])).astype(BF16)
    vb = v_ref[...].astype(BF16)

    rowi = lax.broadcasted_iota(jnp.int32, (c, c), 0)
    coli = lax.broadcasted_iota(jnp.int32, (c, c), 1)
    tri = coli <= rowi
    eye = coli == rowi
    ig8 = _softcap(gt_ref[GL_IG:GL_IG + M_HEADS, :] + gb_ref[GL_IG:GL_IG + M_HEADS, :])
    lf8 = _log_sigmoid(_softcap(gt_ref[GL_FG:GL_FG + M_HEADS, :] + gb_ref[GL_FG:GL_FG + M_HEADS, :]))
    b8 = _rows_dot_01(lf8, jnp.where(rowi <= coli, 1.0, 0.0))
    head_row = lax.broadcasted_iota(jnp.int32, (M_HEADS, 1), 0)

    for hh in range(hps):
        cols = slice(hh * M_V, (hh + 1) * M_V)
        pick = head_row == h0 + hh
        ig_row = jnp.sum(jnp.where(pick, ig8, 0.0), axis=0, keepdims=True)
        b_row = jnp.sum(jnp.where(pick, b8, 0.0), axis=0, keepdims=True)
        b_col = _row_to_col(b_row, eye)
        m_prev = mst_ref[hh:hh + 1, :][:, 0:1]

        q = _dot(ucb[:, cols], wq_ref[hh])
        kt = _dot_nt(wkt_ref[hh], ucb[:, cols])
        qb = q.astype(BF16)
        ktb = kt.astype(BF16)

        r_row = ig_row - b_row
        dmat = jnp.where(tri, b_col + r_row, -jnp.inf)
        inter = b_col + m_prev
        m_t = jnp.maximum(inter, jnp.max(dmat, axis=1, keepdims=True))
        a_inter = jnp.exp(inter - m_t)
        s = _dot(qb, ktb) * jnp.exp(dmat - m_t)
        c_old = c_ref[0, hh]
        n_old = n_ref[0, hh]
        num = _dot(s.astype(BF16), vb[:, cols]) + a_inter * _dot(qb, c_old.astype(BF16))
        qn = _dot_nt(qb, jnp.broadcast_to(n_old, (8, M_QK)).astype(BF16))[:, 0:1]
        den = jnp.sum(s, axis=1, keepdims=True) + a_inter * qn
        hv = num / jnp.maximum(jnp.abs(den), jnp.exp(-m_t))
        hm_ref[:, cols] = _head_out(hv, o_ref[:, cols], z_ref[:, cols], nw_ref[:, cols]).astype(BF16)

        b_end = b_row[:, c - 1:c]
        lw_row = b_end + r_row
        m_new = jnp.maximum(b_end + m_prev, jnp.max(lw_row, axis=1, keepdims=True))
        decay = jnp.exp(b_end + m_prev - m_new)
        wk_row = jnp.exp(lw_row - m_new)
        c_ref[0, hh] = decay * c_old + _dot((kt * wk_row).astype(BF16), vb[:, cols])
        n_ref[0, hh] = decay * n_old + _dot_nt(jnp.broadcast_to(wk_row, (8, c)).astype(BF16), ktb)[0:1, :]
        mst_ref[hh:hh + 1, :] = jnp.broadcast_to(m_new, (1, LANES))
        m_ref[0, hh] = jnp.broadcast_to(m_new, (1, LANES))


MLSTM_HPS = 8


def _mlstm_prompt_part(proj, gates_t, bsz, seqlen, gate_bias_col, conv_w, conv_b, wq, wkt, norm_w):
    c = min(PROMPT_CHUNK, seqlen)
    nch = seqlen // c
    hps = MLSTM_HPS
    width = hps * M_V

    def tok(col0):
        return pl.BlockSpec((c, width), lambda b, h, k: (b * nch + k, col0 // width + h))

    head_vec = lambda rows: pl.BlockSpec((rows, width), lambda b, h, k: (0, h))
    return _Part(
        body=_mlstm_prompt_body, ids=_same_ids,
        in_specs=[
            tok(COL_U), tok(COL_V), tok(COL_O), tok(COL_ZM),
            pl.BlockSpec((LANES, c), lambda b, h, k: (0, b * nch + k)),
            pl.BlockSpec((LANES, 1), lambda b, h, k: (0, 0)),
            head_vec(CONV_K), head_vec(1),
            pl.BlockSpec((hps, M_V, M_QK), lambda b, h, k: (h, 0, 0)),
            pl.BlockSpec((hps, M_QK, M_V), lambda b, h, k: (h, 0, 0)),
            head_vec(1),
        ],
        operands=[proj, proj, proj, proj, gates_t, gate_bias_col, conv_w, conv_b, wq, wkt, norm_w],
        out_specs=[
            pl.BlockSpec((c, width), lambda b, h, k: (b * nch + k, h)),
            pl.BlockSpec((1, hps, M_QK, M_V), lambda b, h, k: (b, h, 0, 0)),
            pl.BlockSpec((1, hps, 1, M_QK), lambda b, h, k: (b, h, 0, 0)),
            pl.BlockSpec((1, hps, 1, LANES), lambda b, h, k: (b, h, 0, 0)),
        ],
        out_shape=[
            jax.ShapeDtypeStruct((bsz * seqlen, D_M), BF16),
            jax.ShapeDtypeStruct((bsz, M_HEADS, M_QK, M_V), F32),
            jax.ShapeDtypeStruct((bsz, M_HEADS, 1, M_QK), F32),
            jax.ShapeDtypeStruct((bsz, M_HEADS, 1, LANES), F32),
        ],
        scratch=[pltpu.VMEM((TAIL + c, width), F32), pltpu.VMEM((8, LANES), F32)],
    ), (bsz, M_HEADS // hps, nch)


LOG2E = 1.4426950408889634
HEAD_ROWS = [slice(r * S_HEAD_DIM, (r + 1) * S_HEAD_DIM) for r in range(S_HPG)]


def _ssd_gate_rows(gt_ref, gbc_ref, avc_ref, lane0):
    rows = pl.ds(lane0, S_HPG)
    dt8 = _softplus(gt_ref[rows, :] + gbc_ref[rows, :])
    return dt8, dt8 * (-jnp.exp(avc_ref[rows, :]))


def _ssd_intra(xst, dt8, cum8, cbt):
    c = xst.shape[1]
    a2r = cum8 * LOG2E
    a2c = jnp.concatenate([a2r, jnp.zeros((LANES - S_HPG, c), F32)], axis=0).T
    ys, xdts = [], []
    for r in range(S_HPG):
        xdt = xst[HEAD_ROWS[r], :] * dt8[r:r + 1, :]
        decay = jnp.exp2(jnp.minimum(a2r[r:r + 1, :] - a2c[:, r:r + 1], 0.0))
        ys.append(_dot(xdt.astype(BF16), (cbt * decay).astype(BF16)))
        xdts.append(xdt)
    return ys, xdts


def _ssd_prompt_body(ids, xs_ref, bm_ref, cm_ref, zs_ref, gt_ref, gbc_ref, avc_ref, dsk_ref,
                     cwx_ref, cbx_ref, cwb_ref, cbb_ref, cwc_ref, cbc_ref, nw_ref,
                     ys_ref, h_ref, extx_ref, extb_ref, extc_ref):
    c = xs_ref.shape[0]
    gps = h_ref.shape[1]
    g0 = ids[1] * gps

    @pl.when(ids[2] == 0)
    def _():
        h_ref[...] = jnp.zeros_like(h_ref)
        extx_ref[0:TAIL, :] = jnp.zeros((TAIL, extx_ref.shape[1]), F32)
        extb_ref[0:TAIL, :] = jnp.zeros((TAIL, extb_ref.shape[1]), F32)
        extc_ref[0:TAIL, :] = jnp.zeros((TAIL, extc_ref.shape[1]), F32)

    xs_all = _silu(_conv_carry(xs_ref[...], extx_ref, cwx_ref[...], cbx_ref[...]))
    bmb_all = _silu(_conv_carry(bm_ref[...], extb_ref, cwb_ref[...], cbb_ref[...])).astype(BF16)
    cmb_all = _silu(_conv_carry(cm_ref[...], extc_ref, cwc_ref[...], cbc_ref[...])).astype(BF16)

    src = lax.broadcasted_iota(jnp.int32, (c, c), 0)
    tgt = lax.broadcasted_iota(jnp.int32, (c, c), 1)
    causal = src <= tgt
    upper = jnp.where(causal, 1.0, 0.0)

    for gg in range(gps):
        cols = slice(gg * S_GW, (gg + 1) * S_GW)
        xs = xs_all[:, cols]
        bmb = bmb_all[:, gg * S_STATE:(gg + 1) * S_STATE]
        cmb = cmb_all[:, gg * S_STATE:(gg + 1) * S_STATE]
        lane0 = pl.multiple_of(GL_DT + (g0 + gg) * S_HPG, S_HPG)
        dt8, la8 = _ssd_gate_rows(gt_ref, gbc_ref, avc_ref, lane0)
        cum8 = _rows_dot_01(la8, upper)
        a_end = cum8[:, c - 1:c]
        w8 = jnp.exp(a_end - cum8)
        ea8 = jnp.exp(cum8)

        cbt = jnp.where(causal, _dot_nt(bmb, cmb), 0.0)
        ys, xdts = _ssd_intra(xs.T, dt8, cum8, cbt)
        h_old = h_ref[0, gg]
        yit = _dot_nt(h_old.astype(BF16), cmb)
        yt = jnp.concatenate([ys[r] + ea8[r:r + 1, :] * yit[HEAD_ROWS[r], :] for r in range(S_HPG)], axis=0)
        ys_ref[:, cols] = _group_out(yt.T, xs, dsk_ref[:, cols], zs_ref[:, cols], nw_ref[:, cols]).astype(BF16)

        xwt = jnp.concatenate([(xdts[r] * w8[r:r + 1, :]).astype(BF16) for r in range(S_HPG)], axis=0)
        upd = _dot(xwt, bmb)
        ea_end = jnp.exp(a_end)
        for r in range(S_HPG):
            h_ref[0, gg, HEAD_ROWS[r], :] = ea_end[r:r + 1, :] * h_old[HEAD_ROWS[r], :] + upd[HEAD_ROWS[r], :]


SSD_GPS = 8


def _ssd_prompt_part(proj, gates_t, bsz, seqlen, gate_bias_col, a_col, dsk_row, conv_w, conv_b, norm_w):
    c = min(PROMPT_CHUNK, seqlen)
    nch = seqlen // c
    gps = SSD_GPS
    tok = lambda width, col0: pl.BlockSpec((c, gps * width), lambda b, g, k: (b * nch + k, col0 // (gps * width) + g))
    vec = lambda rows, width, col0: pl.BlockSpec((rows, gps * width), lambda b, g, k: (0, col0 // (gps * width) + g))
    col = pl.BlockSpec((LANES, 1), lambda b, g, k: (0, 0))
    return _Part(
        body=_ssd_prompt_body, ids=_same_ids,
        in_specs=[
            tok(S_GW, COL_XS), tok(S_STATE, COL_BM), tok(S_STATE, COL_CM), tok(S_GW, COL_ZS),
            pl.BlockSpec((LANES, c), lambda b, g, k: (0, b * nch + k)),
            col, col, vec(1, S_GW, 0),
            vec(CONV_K, S_GW, 0), vec(1, S_GW, 0),
            vec(CONV_K, S_STATE, D_S), vec(1, S_STATE, D_S),
            vec(CONV_K, S_STATE, D_S + S_GROUPS * S_STATE), vec(1, S_STATE, D_S + S_GROUPS * S_STATE),
            vec(1, S_GW, 0),
        ],
        operands=[proj, proj, proj, proj, gates_t, gate_bias_col, a_col, dsk_row,
                  conv_w, conv_b, conv_w, conv_b, conv_w, conv_b, norm_w],
        out_specs=[
            pl.BlockSpec((c, gps * S_GW), lambda b, g, k: (b * nch + k, g)),
            pl.BlockSpec((1, gps, S_GW, S_STATE), lambda b, g, k: (b, g, 0, 0)),
        ],
        out_shape=[
            jax.ShapeDtypeStruct((bsz * seqlen, D_S), BF16),
            jax.ShapeDtypeStruct((bsz, S_GROUPS, S_GW, S_STATE), F32),
        ],
        scratch=[pltpu.VMEM((TAIL + c, gps * S_GW), F32), pltpu.VMEM((TAIL + c, gps * S_STATE), F32),
                 pltpu.VMEM((TAIL + c, gps * S_STATE), F32)],
    ), (bsz, S_GROUPS // gps, nch)


SEQ4 = 4


def _row_masks(rows):
    rowi = lax.broadcasted_iota(jnp.int32, (rows, rows), 0)
    coli = lax.broadcasted_iota(jnp.int32, (rows, rows), 1)
    same = _idiv(rowi, SEQ4) == _idiv(coli, SEQ4)
    return same, same & (coli <= rowi), coli == rowi, same & (_imod(coli, SEQ4) == SEQ4 - 1)


def _seg_cumsum(x, t):
    out = x
    for j in range(1, SEQ4):
        out = out + jnp.where(t >= j, pltpu.roll(x, j, 0), 0.0)
    return out


def _mlstm_sample_body(ids, u_ref, v_ref, o_ref, z_ref, gt_ref, gb_ref, buf_ref, mrow_ref, c0_ref, n0_ref,
                       cw_ref, cb_ref, wq_ref, wkt_ref, nw_ref,
                       hm_ref, c_ref, n_ref, m_ref):
    h = ids[0]
    rows = u_ref.shape[0]
    nseq = rows // SEQ4
    rid = lax.broadcasted_iota(jnp.int32, (rows, 1), 0)
    t = _imod(rid, SEQ4)
    seq = _idiv(rid, SEQ4)
    seq_lane = _idiv(lax.broadcasted_iota(jnp.int32, (1, rows), 1), SEQ4)

    ucb = _silu(_conv_rows(u_ref[...], buf_ref[...], cw_ref[...], cb_ref[...], t)).astype(BF16)
    q = _dot(ucb, wq_ref[0])
    kt = _dot_nt(wkt_ref[0], ucb)
    qb = q.astype(BF16)
    vb = v_ref[...].astype(BF16)

    capped = _softcap(gt_ref[...] + gb_ref[...])
    ig_col = _lane_pick(capped, GL_IG + h)
    b_col = _lane_pick(_seg_cumsum(_log_sigmoid(capped), t), GL_FG + h)
    m_prev = _lane_pick(mrow_ref[...], h)

    same, tri, eye, last = _row_masks(rows)
    b_row = _col_to_row(b_col, eye)
    ig_row = _col_to_row(ig_col, eye)
    dmat = jnp.where(tri, b_col - b_row + ig_row, -jnp.inf)
    inter = b_col + m_prev
    m_t = jnp.maximum(inter, jnp.max(dmat, axis=1, keepdims=True))
    a_inter = jnp.exp(inter - m_t)
    s = _dot(qb, kt.astype(BF16)) * jnp.exp(dmat - m_t)

    b_end = jnp.sum(jnp.where(last, b_row, 0.0), axis=1, keepdims=True)
    lw = b_end - b_col + ig_col
    m_new = jnp.maximum(b_end + m_prev, jnp.max(jnp.where(same, _col_to_row(lw, eye), -jnp.inf), axis=1, keepdims=True))
    decay = jnp.exp(b_end + m_prev - m_new)
    kwt = kt * _col_to_row(jnp.exp(lw - m_new), eye)
    eye_k = (lax.broadcasted_iota(jnp.int32, (M_QK, M_QK), 0) == lax.broadcasted_iota(jnp.int32, (M_QK, M_QK), 1))

    num_inter = jnp.zeros((rows, M_V), F32)
    qn = jnp.zeros((rows, 1), F32)
    for bi in range(nseq):
        own = seq == bi
        c_old = c0_ref[bi, 0]
        n_old = n0_ref[bi, 0]
        num_inter = jnp.where(own, _dot(qb, c_old.astype(BF16)), num_inter)
        qn = jnp.where(own, jnp.sum(q * n_old, axis=1, keepdims=True), qn)
        kw_b = jnp.where(seq_lane == bi, kwt, 0.0)
        d_b = jnp.sum(jnp.where(own & (t == SEQ4 - 1), decay, 0.0), axis=0, keepdims=True)
        c_ref[bi, 0] = d_b * c_old + _dot(kw_b.astype(BF16), vb)
        n_ref[bi, 0] = d_b * n_old + _col_to_row(jnp.sum(kw_b, axis=1, keepdims=True), eye_k)

    num = _dot(s.astype(BF16), vb) + a_inter * num_inter
    den = jnp.sum(s, axis=1, keepdims=True) + a_inter * qn
    hv = num / jnp.maximum(jnp.abs(den), jnp.exp(-m_t))
    hm_ref[...] = _head_out(hv, o_ref[...], z_ref[...], nw_ref[...]).astype(BF16)
    m_ref[0] = jnp.broadcast_to(m_new, (rows, LANES))


def _mlstm_sample_part(proj, gates, nseq_total, gate_bias, buf_rows, m_rows, c0, n0, conv_w, conv_b, wq, wkt, norm_w,
                       bb, ids=_same_ids):
    rows = bb * SEQ4
    nblk = nseq_total // bb
    spec = functools.partial(_on_grid, ids)
    tok = lambda col0: spec((rows, M_V), lambda h, i: (i, col0 // M_V + h))
    head_vec = lambda r: spec((r, M_V), lambda h, i: (0, h))
    c_spec = spec((bb, 1, M_QK, M_V), lambda h, i: (i, h, 0, 0))
    n_spec = spec((bb, 1, 1, M_QK), lambda h, i: (i, h, 0, 0))
    return _Part(
        body=_mlstm_sample_body, ids=ids,
        in_specs=[
            tok(COL_U), tok(COL_V), tok(COL_O), tok(COL_ZM),
            spec((rows, LANES), lambda h, i: (i, 0)),
            spec((1, LANES), lambda h, i: (0, 0)),
            spec((rows, M_V), lambda h, i: (i, h)),
            spec((rows, LANES), lambda h, i: (i, 0)),
            c_spec, n_spec,
            head_vec(CONV_K), head_vec(1),
            spec((1, M_V, M_QK), lambda h, i: (h, 0, 0)),
            spec((1, M_QK, M_V), lambda h, i: (h, 0, 0)),
            head_vec(1),
        ],
        operands=[proj, proj, proj, proj, gates, gate_bias, buf_rows, m_rows, c0, n0, conv_w, conv_b, wq, wkt, norm_w],
        out_specs=[
            spec((rows, M_V), lambda h, i: (i, h)),
            c_spec, n_spec,
            spec((1, rows, LANES), lambda h, i: (h, i, 0)),
        ],
        out_shape=[
            jax.ShapeDtypeStruct((nseq_total * SEQ4, D_M), BF16),
            jax.ShapeDtypeStruct((nseq_total, M_HEADS, M_QK, M_V), F32),
            jax.ShapeDtypeStruct((nseq_total, M_HEADS, 1, M_QK), F32),
            jax.ShapeDtypeStruct((M_HEADS, nseq_total * SEQ4, LANES), F32),
        ],
        scratch=[],
    ), (M_HEADS, nblk)


def _ssd_sample_body(xs_ref, bm_ref, cm_ref, zs_ref, gt_ref, gbc_ref, avc_ref, dsk_ref,
                     bufx_ref, bufb_ref, bufc_ref, h0_ref,
                     cwx_ref, cbx_ref, cwb_ref, cbb_ref, cwc_ref, cbc_ref, nw_ref,
                     ys_ref, h_ref, yit_ref, xwt_ref):
    g = pl.program_id(0)
    rows = xs_ref.shape[0]
    nseq = rows // SEQ4
    rid = lax.broadcasted_iota(jnp.int32, (rows, 1), 0)
    t = _imod(rid, SEQ4)
    seq = _idiv(rid, SEQ4)
    tok_lane = lax.broadcasted_iota(jnp.int32, (1, rows), 1)

    xs = _silu(_conv_rows(xs_ref[...], bufx_ref[...], cwx_ref[...], cbx_ref[...], t))
    bmb = _silu(_conv_rows(bm_ref[...], bufb_ref[...], cwb_ref[...], cbb_ref[...], t)).astype(BF16)
    cmb = _silu(_conv_rows(cm_ref[...], bufc_ref[...], cwc_ref[...], cbc_ref[...], t)).astype(BF16)

    lane0 = pl.multiple_of(GL_DT + g * S_HPG, S_HPG)
    dt8, la8 = _ssd_gate_rows(gt_ref, gbc_ref, avc_ref, lane0)
    src = lax.broadcasted_iota(jnp.int32, (rows, rows), 0)
    tgt = lax.broadcasted_iota(jnp.int32, (rows, rows), 1)
    same = _idiv(src, SEQ4) == _idiv(tgt, SEQ4)
    causal = same & (src <= tgt)
    cum8 = _rows_dot_01(la8, jnp.where(causal, 1.0, 0.0))
    a_end = _rows_dot_01(cum8, jnp.where(same & (_imod(src, SEQ4) == SEQ4 - 1), 1.0, 0.0))
    w8 = jnp.exp(a_end - cum8)
    ea8 = jnp.exp(cum8)
    ea_end = jnp.exp(a_end)

    cbt = jnp.where(causal, _dot_nt(bmb, cmb), 0.0)
    ys, xdts = _ssd_intra(xs.T, dt8, cum8, cbt)
    xwt_ref[...] = jnp.concatenate([(xdts[r] * w8[r:r + 1, :]).astype(BF16) for r in range(S_HPG)], axis=0)
    yit_ref[...] = jnp.zeros_like(yit_ref)

    def per_sequence(bi, carry):
        h_old = h0_ref[bi, 0]
        own = _idiv(tok_lane, SEQ4) == bi
        yit_ref[...] += _dot_nt(h_old.astype(BF16), jnp.where(seq == bi, cmb, jnp.zeros_like(cmb)))
        upd = _dot(jnp.where(own, xwt_ref[...], jnp.zeros_like(xwt_ref)), bmb)
        d8 = jnp.sum(jnp.where(tok_lane == bi * SEQ4 + SEQ4 - 1, ea_end, 0.0), axis=1, keepdims=True)
        for r in range(S_HPG):
            h_ref[bi, 0, HEAD_ROWS[r], :] = d8[r:r + 1, :] * h_old[HEAD_ROWS[r], :] + upd[HEAD_ROWS[r], :]
        return carry

    lax.fori_loop(0, nseq, per_sequence, 0, unroll=8 if nseq % 8 == 0 else 1)
    yit = yit_ref[...]
    yt = jnp.concatenate([ys[r] + ea8[r:r + 1, :] * yit[HEAD_ROWS[r], :] for r in range(S_HPG)], axis=0)
    ys_ref[...] = _group_out(yt.T, xs, dsk_ref[...], zs_ref[...], nw_ref[...]).astype(BF16)


SSD_SAMPLE_BB = 32


def _ssd_sample(proj, gates_t, nseq_total, gate_bias_col, a_col, dsk_row, buf_rows, h0, conv_w, conv_b, norm_w):
    bb = min(SSD_SAMPLE_BB, nseq_total)
    rows = bb * SEQ4
    nblk = nseq_total // bb
    tok = lambda width, col0: pl.BlockSpec((rows, width), lambda g, i: (i, col0 // width + g))
    vec = lambda r, width, col0: pl.BlockSpec((r, width), lambda g, i: (0, col0 // width + g))
    col = pl.BlockSpec((LANES, 1), lambda g, i: (0, 0))
    h_spec = pl.BlockSpec((bb, 1, S_GW, S_STATE), lambda g, i: (i, g, 0, 0))
    off_b = D_S
    off_c = D_S + S_GROUPS * S_STATE
    return pl.pallas_call(
        _ssd_sample_body,
        grid=(S_GROUPS, nblk),
        in_specs=[
            tok(S_GW, COL_XS), tok(S_STATE, COL_BM), tok(S_STATE, COL_CM), tok(S_GW, COL_ZS),
            pl.BlockSpec((LANES, rows), lambda g, i: (0, i)),
            col, col, vec(1, S_GW, 0),
            tok(S_GW, 0), tok(S_STATE, off_b), tok(S_STATE, off_c),
            h_spec,
            vec(CONV_K, S_GW, 0), vec(1, S_GW, 0),
            vec(CONV_K, S_STATE, off_b), vec(1, S_STATE, off_b),
            vec(CONV_K, S_STATE, off_c), vec(1, S_STATE, off_c),
            vec(1, S_GW, 0),
        ],
        out_specs=[pl.BlockSpec((rows, S_GW), lambda g, i: (i, g)), h_spec],
        out_shape=[
            jax.ShapeDtypeStruct((nseq_total * SEQ4, D_S), BF16),
            jax.ShapeDtypeStruct((nseq_total, S_GROUPS, S_GW, S_STATE), F32),
        ],
        scratch_shapes=[pltpu.VMEM((S_GW, rows), F32), pltpu.VMEM((S_GW, rows), BF16)],
        compiler_params=pltpu.CompilerParams(
            dimension_semantics=("parallel", "parallel"), vmem_limit_bytes=VMEM_LIMIT),
        name="ssd_sample",
    )(proj, proj, proj, proj, gates_t, gate_bias_col, a_col, dsk_row, buf_rows, buf_rows, buf_rows, h0,
      conv_w, conv_b, conv_w, conv_b, conv_w, conv_b, norm_w)


OUT_TM, OUT_TN = 512, 512


def _out_proj_body(ids, x_ref, hm_ref, ys_ref, wt_ref, wb_ref, gn_ref, out_ref, x1g_ref, ssp_ref):
    x1 = x_ref[...] + _dot(hm_ref[...], wt_ref[...]) + _dot(ys_ref[...], wb_ref[...])
    out_ref[...] = x1
    x1g_ref[...] = (x1 * gn_ref[...]).astype(BF16)
    ssp_ref[0] = jnp.sum(x1 * x1, axis=1, keepdims=True)


def _out_proj_part(x, hm, ys, w_out, ple_norm, ids=_same_ids):
    m = x.shape[0]
    tm, tn = min(OUT_TM, m), OUT_TN
    nj = D_MODEL // tn
    spec = functools.partial(_on_grid, ids)
    return _Part(
        body=_out_proj_body, ids=ids,
        in_specs=[
            spec((tm, tn), lambda i, j: (i, j)),
            spec((tm, D_M), lambda i, j: (i, 0)),
            spec((tm, D_S), lambda i, j: (i, 0)),
            spec((D_M, tn), lambda i, j: (0, j)),
            spec((D_S, tn), lambda i, j: (1, j)),
            spec((1, tn), lambda i, j: (0, j)),
        ],
        operands=[x, hm, ys, w_out, w_out, ple_norm],
        out_specs=[
            spec((tm, tn), lambda i, j: (i, j)),
            spec((tm, tn), lambda i, j: (i, j)),
            spec((1, tm, 1), lambda i, j: (j, i, 0)),
        ],
        out_shape=[jax.ShapeDtypeStruct((m, D_MODEL), F32), jax.ShapeDtypeStruct((m, D_MODEL), BF16),
                   jax.ShapeDtypeStruct((nj, m, 1), F32)],
        scratch=[],
    ), (m // tm, nj)


def _ple_body(x1g_ref, ss1_ref, x1t_ref, p_ref, wg_ref, wp_ref, fn_ref, out_ref, ss_ref):
    j = pl.program_id(1)
    nj = pl.num_programs(1)
    tn = wg_ref.shape[1]

    @pl.when(j == 0)
    def _():
        ss_ref[...] = jnp.zeros_like(ss_ref)

    rs = lax.rsqrt(jnp.sum(ss1_ref[...], axis=0) * (1.0 / D_MODEL) + EPS)
    gate = _sigmoid(rs * _dot(x1g_ref[...], wg_ref[...]))
    x2 = x1t_ref[...] + gate * _dot(p_ref[...].astype(BF16), wp_ref[...])
    ss_ref[...] += jnp.sum(x2 * x2, axis=1, keepdims=True)
    for jj in range(out_ref.shape[1] // tn):
        @pl.when(j == jj)
        def _(jj=jj):
            out_ref[:, jj * tn:(jj + 1) * tn] = x2

    @pl.when(j == nj - 1)
    def _():
        fn = fn_ref[...]

        def body(r, carry):
            sl = pl.ds(pl.multiple_of(r * NORM_ROWS, NORM_ROWS), NORM_ROWS)
            scale = lax.rsqrt(ss_ref[sl, :] * (1.0 / D_MODEL) + EPS)
            out_ref[sl, :] = out_ref[sl, :] * scale * fn
            return carry
        lax.fori_loop(0, out_ref.shape[0] // NORM_ROWS, body, 0)


def _ple(x1, x1g, ss1, p, w_gate, w_proj, final_norm):
    m = x1.shape[0]
    tm = min(512, m)
    tn = 1024
    return pl.pallas_call(
        _ple_body,
        grid=(m // tm, D_MODEL // tn),
        in_specs=[
            pl.BlockSpec((tm, D_MODEL), lambda i, j: (i, 0)),
            pl.BlockSpec((ss1.shape[0], tm, 1), lambda i, j: (0, i, 0)),
            pl.BlockSpec((tm, tn), lambda i, j: (i, j)),
            pl.BlockSpec((tm, PLE_DIM), lambda i, j: (i, 0)),
            pl.BlockSpec((D_MODEL, tn), lambda i, j: (0, j)),
            pl.BlockSpec((PLE_DIM, tn), lambda i, j: (0, j)),
            pl.BlockSpec((1, D_MODEL), lambda i, j: (0, 0)),
        ],
        out_specs=pl.BlockSpec((tm, D_MODEL), lambda i, j: (i, 0)),
        out_shape=jax.ShapeDtypeStruct((m, D_MODEL), F32),
        scratch_shapes=[pltpu.VMEM((tm, 1), F32)],
        compiler_params=pltpu.CompilerParams(
            dimension_semantics=("parallel", "arbitrary"), vmem_limit_bytes=VMEM_LIMIT),
        name="ple",
    )(x1g, ss1, x1, p, w_gate, w_proj, final_norm)


CONV_TC = 2048


def _slot_row_match(nseq, slot_major_axis, first_token):
    shape = (nseq * SEQ4, (CONV_K - 1) * nseq) if slot_major_axis == 1 else ((CONV_K - 1) * nseq, nseq * SEQ4)
    sm = lax.broadcasted_iota(jnp.int32, shape, slot_major_axis)
    row = lax.broadcasted_iota(jnp.int32, shape, 1 - slot_major_axis)
    hit = None
    for slot in range(CONV_K - 1):
        b = sm - slot * nseq
        this = (b >= 0) & (b < nseq) & (row == b * SEQ4 + first_token + slot)
        hit = this if hit is None else hit | this
    return jnp.where(hit, 1.0, 0.0)


def _conv_rows_in_body(buf_ref, rows_ref):
    nseq = buf_ref.shape[1]
    pick = _slot_row_match(nseq, 1, 0)
    rows_ref[...] = _dot_select(pick, buf_ref[...].reshape((CONV_K - 1) * nseq, buf_ref.shape[2]))


def _conv_rows_in(buf):
    _, nseq, ch = buf.shape
    tc = min(CONV_TC, ch)
    return pl.pallas_call(
        _conv_rows_in_body,
        grid=(ch // tc,),
        in_specs=[pl.BlockSpec((CONV_K - 1, nseq, tc), lambda j: (0, 0, j))],
        out_specs=pl.BlockSpec((nseq * SEQ4, tc), lambda j: (0, j)),
        out_shape=jax.ShapeDtypeStruct((nseq * SEQ4, ch), F32),
        compiler_params=pltpu.CompilerParams(dimension_semantics=("parallel",), vmem_limit_bytes=VMEM_LIMIT),
        name="conv_rows_in",
    )(buf)


def _conv_rows_out_body(rows_ref, buf_ref):
    nseq = buf_ref.shape[1]
    pick = _slot_row_match(nseq, 0, SEQ4 - (CONV_K - 1))
    buf_ref[...] = _dot_select(pick, rows_ref[...]).reshape(buf_ref.shape)


def _conv_rows_out(proj, col0, ch):
    nseq = proj.shape[0] // SEQ4
    tc = min(CONV_TC, ch)
    assert col0 % tc == 0 and ch % tc == 0
    return pl.pallas_call(
        _conv_rows_out_body,
        grid=(ch // tc,),
        in_specs=[pl.BlockSpec((nseq * SEQ4, tc), lambda j: (0, col0 // tc + j))],
        out_specs=pl.BlockSpec((CONV_K - 1, nseq, tc), lambda j: (0, 0, j)),
        out_shape=jax.ShapeDtypeStruct((CONV_K - 1, nseq, ch), F32),
        compiler_params=pltpu.CompilerParams(dimension_semantics=("parallel",), vmem_limit_bytes=VMEM_LIMIT),
        name="conv_rows_out",
    )(proj)


def _pad_lanes(v, offset):
    out = jnp.zeros((1, LANES), F32)
    return lax.dynamic_update_slice(out, v.reshape(1, -1).astype(F32), (0, offset))


def kernel(x_prompt, x_sample, p_prompt, p_sample, state_m_C, state_m_n, state_m_m, state_m_conv, state_s_ssm, state_s_conv, norm_in, w_in, b_ig, b_fg, m_conv_w, m_conv_b, w_q, w_k, m_norm, s_conv_w, s_conv_b, dt_bias, a_log, d_skip, s_norm, w_out, ple_proj, ple_gate, ple_norm, final_norm):
    assert w_in.shape[0] == 1, "single layer"
    bsz, seqlen, _ = x_prompt.shape
    nseq, dec_seq, _ = x_sample.shape
    assert dec_seq == SEQ4

    assert w_in.shape[2] == N_MAIN + GATE_COLS + S_HEADS
    w_in_t = jnp.swapaxes(w_in, 1, 2)
    gate_bias = jnp.concatenate(
        [b_ig[0], b_fg[0], dt_bias[0], jnp.zeros((LANES - 2 * M_HEADS - S_HEADS,), F32)]).reshape(1, LANES)
    gate_bias_col = gate_bias.reshape(LANES, 1)
    a_col = _pad_lanes(a_log[0], GL_DT).reshape(LANES, 1)
    dsk_row = jnp.repeat(d_skip[0], S_HEAD_DIM).reshape(1, D_S)
    wq = w_q[0].astype(BF16)
    wkt = (jnp.swapaxes(w_k[0], 1, 2) * (M_QK ** -0.5)).astype(BF16)
    ple_proj_b = ple_proj[0].astype(BF16)
    norm_in_r = norm_in[0].reshape(1, D_MODEL)
    ple_norm_r = ple_norm[0].reshape(1, D_MODEL)
    final_norm_r = final_norm.reshape(1, D_MODEL)
    m_conv_b_r = m_conv_b[0].reshape(1, D_M)
    s_conv_b_r = s_conv_b[0].reshape(1, CONV_DIM)
    m_norm_r = m_norm[0].reshape(1, D_M)
    s_norm_r = s_norm[0].reshape(1, D_S)

    xp = x_prompt.reshape(bsz * seqlen, D_MODEL)
    xs_ = x_sample.reshape(nseq * SEQ4, D_MODEL)
    in_part, in_grid = _in_proj_part(_rmsnorm_cast(xp, norm_in_r), w_in_t)
    (proj_p, _, gates_pt), = _call_parts("in_proj", in_grid, [in_part])

    mlstm_part, mlstm_grid = _mlstm_prompt_part(
        proj_p, gates_pt, bsz, seqlen, gate_bias_col, m_conv_w[0], m_conv_b_r, wq, wkt, m_norm_r)
    (hm_p, pc, pn, pm), = _call_parts("mlstm_prompt", mlstm_grid, [mlstm_part])

    ssd_part, ssd_grid = _ssd_prompt_part(
        proj_p, gates_pt, bsz, seqlen, gate_bias_col, a_col, dsk_row, s_conv_w[0], s_conv_b_r, s_norm_r)
    nsteps = ssd_grid[0] * ssd_grid[1] * ssd_grid[2]
    step = _linear_step(ssd_grid)
    (ys_p, ph), (w_out_b,), (ple_gate_b,) = _call_parts(
        "ssd_prompt_casts", ssd_grid,
        [ssd_part, _cast_part(w_out[0], nsteps, step), _cast_part(ple_gate[0], nsteps, step)])
    in_part_s, in_grid_s = _in_proj_part(_rmsnorm_cast(xs_, norm_in_r), w_in_t)
    (proj_s, gates_s, gates_st), = _call_parts("in_proj", in_grid_s, [in_part_s])

    proj_p3 = proj_p.reshape(bsz, seqlen, N_MAIN)
    tail = seqlen - (CONV_K - 1)
    p_mconv = proj_p3[:, tail:, COL_U:COL_U + D_M]
    p_sconv = proj_p3[:, tail:, COL_XS:COL_XS + CONV_DIM]

    pad_rows = lambda st: _conv_rows_in(jnp.swapaxes(st, 0, 1))
    m_rows = jnp.pad(jnp.repeat(state_m_m[0], SEQ4, axis=0), ((0, 0), (0, LANES - M_HEADS)))

    sample_args = (proj_s, gates_s, nseq, gate_bias, pad_rows(state_m_conv[0]), m_rows, state_m_C[0],
                   state_m_n[0].reshape(nseq, M_HEADS, 1, M_QK), m_conv_w[0], m_conv_b_r, wq, wkt, m_norm_r)
    out_part, out_grid = _out_proj_part(xp, hm_p, ys_p, w_out_b, ple_norm_r)
    bb = SAMPLE_BB
    while bb > 1 and (nseq % bb or M_HEADS * (nseq // bb) < out_grid[0] * out_grid[1]):
        bb //= 2
    if M_HEADS * (nseq // bb) == out_grid[0] * out_grid[1]:
        nblk = nseq // bb
        as_head_block = lambda i, j: ((i * out_grid[1] + j) // nblk, (i * out_grid[1] + j) % nblk)
        scan_part, _ = _mlstm_sample_part(*sample_args, bb, ids=as_head_block)
        (x1_p, x1g_p, ss1_p), (hm_s, sc, sn, sm) = _call_parts("out_proj_mlstm_sample", out_grid, [out_part, scan_part])
    else:
        (x1_p, x1g_p, ss1_p), = _call_parts("out_proj", out_grid, [out_part])
        scan_part, scan_grid = _mlstm_sample_part(*sample_args, bb)
        (hm_s, sc, sn, sm), = _call_parts("mlstm_sample", scan_grid, [scan_part])
    y_p = _ple(x1_p, x1g_p, ss1_p, p_prompt[0].reshape(bsz * seqlen, PLE_DIM), ple_gate_b, ple_proj_b, final_norm_r)

    ys_s, sh = _ssd_sample(
        proj_s, gates_st, nseq, gate_bias_col, a_col, dsk_row, pad_rows(state_s_conv[0]),
        state_s_ssm[0].reshape(nseq, S_GROUPS, S_GW, S_STATE), s_conv_w[0], s_conv_b_r, s_norm_r)
    out_part_s, out_grid_s = _out_proj_part(xs_, hm_s, ys_s, w_out_b, ple_norm_r)
    (x1_s, x1g_s, ss1_s), = _call_parts("out_proj", out_grid_s, [out_part_s])
    y_s = _ple(x1_s, x1g_s, ss1_s, p_sample[0].reshape(nseq * SEQ4, PLE_DIM), ple_gate_b, ple_proj_b, final_norm_r)

    s_mconv = jnp.swapaxes(_conv_rows_out(proj_s, COL_U, D_M), 0, 1)
    s_sconv = jnp.swapaxes(_conv_rows_out(proj_s, COL_XS, CONV_DIM), 0, 1)
    s_m = jnp.transpose(sm[:, SEQ4 - 1::SEQ4, 0])

    return (
        y_p.reshape(bsz, seqlen, D_MODEL),
        y_s.reshape(nseq, SEQ4, D_MODEL),
        pc[None], pn.reshape(1, bsz, M_HEADS, M_QK), pm[:, :, 0, 0][None], p_mconv[None],
        ph.reshape(1, bsz, S_HEADS, S_HEAD_DIM, S_STATE), p_sconv[None],
        sc[None], sn.reshape(1, nseq, M_HEADS, M_QK), s_m[None], s_mconv[None],
        sh.reshape(1, nseq, S_HEADS, S_HEAD_DIM, S_STATE), s_sconv[None],
    )
```
